```python
import math
import jax
import jax.numpy as jnp
from jax import lax
import numpy as np

D_MODEL = 2048
BATCH = 4
SEQ = 8192
DEPTH = 2

GRID_W = 64
CTX_LEN = 256
BRANCH_W = 512
N_BRANCH = 3
SSM_GROUP = 16
SSM_GROUPS = BRANCH_W // SSM_GROUP
SSM_STATE = 64
SSM_DT_MIN = 1e-3
SSM_DT_MAX = 1e-1
NA_HEADS = 8
NA_HEAD_DIM = BRANCH_W // NA_HEADS
NA_WIN_ROWS = 8
NA_WIN_COLS = 16
HY_ORDER = 2
HY_SHORT = 3
HY_BANDS = 16
HY_EMB = 1 + 2 * HY_BANDS
HY_FILTER_W = 64
HY_MIN_DECAY = math.log(1e-2) / 1.5
HY_MAX_DECAY = math.log(1e-2) / 0.3
N_EXPERTS = 16
EXPERT_FF = 2048
EC_CAPACITY = 2
IN_WIDTH = BRANCH_W + 3 * BRANCH_W + (HY_ORDER + 1) * BRANCH_W
NORM_EPS = 1e-6

kernel_name = 'hybrid_s5_natten_hyena_ec_dit_block'


def _rmsnorm(x, g):
    xf = x.astype(jnp.float32)
    y = xf * lax.rsqrt(jnp.mean(xf * xf, axis=-1, keepdims=True) + NORM_EPS)
    return (y * g.astype(jnp.float32)).astype(x.dtype)


def _modulate(x, shift, scale):
    return x * (1.0 + scale) + shift


def _cmul(ar, ai, br, bi):
    return ar * br - ai * bi, ar * bi + ai * br


def _s5_combine(e1, e2):
    a1r, a1i, b1r, b1i = e1
    a2r, a2i, b2r, b2i = e2
    ar, ai = _cmul(a2r, a2i, a1r, a1i)
    br, bi = _cmul(a2r, a2i, b1r, b1i)
    return ar, ai, br + b2r, bi + b2i


def _s5_discretize(lam_re, lam_im, log_step, b_re, b_im):
    lam_re = lam_re.astype(jnp.float32)
    lam_im = lam_im.astype(jnp.float32)
    step = jnp.exp(log_step.astype(jnp.float32))[:, None]
    mag = jnp.exp(lam_re * step)
    a_re = mag * jnp.cos(lam_im * step)
    a_im = mag * jnp.sin(lam_im * step)
    den = lam_re * lam_re + lam_im * lam_im
    num_re = a_re - 1.0
    coef_re = (num_re * lam_re + a_im * lam_im) / den
    coef_im = (a_im * lam_re - num_re * lam_im) / den
    bb_re, bb_im = _cmul(coef_re[..., None], coef_im[..., None],
                         b_re.astype(jnp.float32), b_im.astype(jnp.float32))
    return a_re, a_im, bb_re, bb_im


def _s5_scan(a_re, a_im, bb_re, bb_im, u, h0_re, h0_im, reverse):
    bu_re = jnp.einsum('gpk,blgk->blgp', bb_re, u)
    bu_im = jnp.einsum('gpk,blgk->blgp', bb_im, u)
    first = -1 if reverse else 0
    s_re, s_im = _cmul(a_re, a_im, h0_re, h0_im)
    bu_re = bu_re.at[:, first].add(s_re)
    bu_im = bu_im.at[:, first].add(s_im)
    length = u.shape[1]
    a_re_t = jnp.broadcast_to(a_re, (1, length) + a_re.shape)
    a_im_t = jnp.broadcast_to(a_im, (1, length) + a_im.shape)
    _, _, h_re, h_im = lax.associative_scan(_s5_combine, (a_re_t, a_im_t, bu_re, bu_im),
                                            reverse=reverse, axis=1)
    return h_re, h_im


def _s5_readout(c_re, c_im, h_re, h_im):
    return (jnp.einsum('gkp,blgp->blgk', c_re.astype(jnp.float32), h_re)
            - jnp.einsum('gkp,blgp->blgk', c_im.astype(jnp.float32), h_im))


def _s5_output(y, u, p, dtype):
    b, length = u.shape[0], u.shape[1]
    y = (y + p['ssm_d'].astype(jnp.float32).reshape(SSM_GROUPS, SSM_GROUP) * u).reshape(b, length, BRANCH_W)
    z = jax.nn.gelu(y)
    return (z * jax.nn.sigmoid(z @ p['ssm_w_glu'].astype(jnp.float32))).astype(dtype)


def _s5_branch(u_ctx, u_lat, p, last):
    b = u_lat.shape[0]
    uc = u_ctx.astype(jnp.float32).reshape(b, u_ctx.shape[1], SSM_GROUPS, SSM_GROUP)
    ul = u_lat.astype(jnp.float32).reshape(b, u_lat.shape[1], SSM_GROUPS, SSM_GROUP)
    zero = jnp.zeros((b, SSM_GROUPS, SSM_STATE), jnp.float32)
    ys_lat, ys_ctx = [], []
    for d in range(2):
        rev = d == 1
        fin = 0 if rev else -1
        a_re, a_im, bb_re, bb_im = _s5_discretize(p['ssm_lam_re'][d], p['ssm_lam_im'][d], p['ssm_log_step'][d],
                                                  p['ssm_b_re'][d], p['ssm_b_im'][d])
        hc_re, hc_im = _s5_scan(a_re, a_im, bb_re, bb_im, uc, zero, zero, rev)
        hl_re, hl_im = _s5_scan(a_re, a_im, bb_re, bb_im, ul, hc_re[:, fin], hc_im[:, fin], rev)
        ys_lat.append(_s5_readout(p['ssm_c_re'][d], p['ssm_c_im'][d], hl_re, hl_im))
        if not last:
            ys_ctx.append(_s5_readout(p['ssm_c_re'][d], p['ssm_c_im'][d], hc_re, hc_im))
    out_lat = _s5_output(ys_lat[0] + ys_lat[1], ul, p, u_lat.dtype)
    out_ctx = None if last else _s5_output(ys_ctx[0] + ys_ctx[1], uc, p, u_ctx.dtype)
    return out_lat, out_ctx


def _na_heads(t):
    return t.reshape(t.shape[0], t.shape[1], NA_HEADS, NA_HEAD_DIM)


def _na_context(q, k, v):
    s = jnp.einsum('bqhd,bkhd->bhqk', q, k).astype(jnp.float32) * NA_HEAD_DIM ** -0.5
    pr = jax.nn.softmax(s, axis=-1).astype(v.dtype)
    o = jnp.einsum('bhqk,bkhd->bqhd', pr, v)
    return o.reshape(o.shape[0], o.shape[1], BRANCH_W)


def _na_latent(q, k, v, kc, vc, rpb):
    b, length, nh, dh = q.shape
    rows = length // GRID_W
    kr = min(NA_WIN_ROWS, rows)
    kw = NA_WIN_COLS
    qg = q.reshape(b, rows, GRID_W, nh, dh)
    kg = k.reshape(b, rows, GRID_W, nh, dh)
    vg = v.reshape(b, rows, GRID_W, nh, dh)
    col = np.arange(GRID_W)
    col_idx = np.clip(col - kw // 2, 0, GRID_W - kw)[:, None] + np.arange(kw)[None, :]
    dc = col_idx - col[:, None] + (NA_WIN_COLS - 1)
    rpb = rpb.astype(jnp.float32)
    scale = NA_HEAD_DIM ** -0.5

    def one_row(r):
        start = jnp.clip(r - kr // 2, 0, rows - kr)
        k_rows = lax.dynamic_slice_in_dim(kg, start, kr, axis=1)
        v_rows = lax.dynamic_slice_in_dim(vg, start, kr, axis=1)
        k_win = k_rows[:, :, col_idx]
        v_win = v_rows[:, :, col_idx]
        q_row = lax.dynamic_index_in_dim(qg, r, axis=1, keepdims=False)
        dr = start + jnp.arange(kr) - r + (NA_WIN_ROWS - 1)
        bias = jnp.transpose(rpb[:, dr][:, :, dc], (0, 2, 1, 3))
        s_win = jnp.einsum('bchd,bjcwhd->bhcjw', q_row, k_win).astype(jnp.float32) * scale + bias
        s_ctx = jnp.einsum('bchd,bkhd->bhck', q_row, kc).astype(jnp.float32) * scale
        s = jnp.concatenate([s_win.reshape(b, nh, GRID_W, kr * kw), s_ctx], axis=-1)
        pr = jax.nn.softmax(s, axis=-1).astype(v.dtype)
        p_win = pr[..., :kr * kw].reshape(b, nh, GRID_W, kr, kw)
        p_ctx = pr[..., kr * kw:]
        return (jnp.einsum('bhcjw,bjcwhd->bchd', p_win, v_win)
                + jnp.einsum('bhck,bkhd->bchd', p_ctx, vc))

    out = lax.map(one_row, jnp.arange(rows))
    return jnp.transpose(out, (1, 0, 2, 3, 4)).reshape(b, length, BRANCH_W)


def _hyena_spectrum(length, p):
    f32 = jnp.float32
    t = jnp.linspace(0.0, 1.0, length, dtype=f32)[:, None]
    freqs = jnp.linspace(1e-4, HY_BANDS - 1, HY_BANDS, dtype=f32)
    ang = (2.0 * math.pi / length) * jnp.arange(length, dtype=f32)[:, None] * freqs[None, :]
    z = jnp.concatenate([t, jnp.cos(ang), -jnp.sin(ang)], axis=-1)
    fr = p['hy_freq'].astype(f32)
    h = jnp.sin(fr * (z @ p['hy_w1'].astype(f32) + p['hy_b1'].astype(f32)))
    h = jnp.sin(fr * (h @ p['hy_w2'].astype(f32) + p['hy_b2'].astype(f32)))
    h = jnp.sin(fr * (h @ p['hy_w3'].astype(f32) + p['hy_b3'].astype(f32)))
    h = (h @ p['hy_w4'].astype(f32)).reshape(length, HY_ORDER, 2, BRANCH_W)
    decay = jnp.exp(-t * jnp.abs(jnp.linspace(HY_MIN_DECAY, HY_MAX_DECAY, BRANCH_W, dtype=f32)))
    h = h * decay[:, None, None, :]
    filt = jnp.concatenate([h[:, :, 0], jnp.zeros((1, HY_ORDER, BRANCH_W), f32), h[:0:-1, :, 1]], axis=0)
    filt = filt / jnp.sum(jnp.abs(filt), axis=0, keepdims=True)
    return jnp.fft.rfft(filt, axis=0)


def _hyena_branch(u, p):
    length = u.shape[1]
    half = HY_SHORT // 2
    z = lax.conv_general_dilated(u, p['hy_conv_w'][:, None, :].astype(u.dtype), (1,), ((half, half),),
                                 dimension_numbers=('NWC', 'WIO', 'NWC'),
                                 feature_group_count=u.shape[-1]) + p['hy_conv_b'].astype(u.dtype)
    parts = jnp.split(z, HY_ORDER + 1, axis=-1)
    spec = _hyena_spectrum(length, p)
    y = parts[-1].astype(jnp.float32)
    for n in range(HY_ORDER):
        conv = jnp.fft.irfft(jnp.fft.rfft(y, n=2 * length, axis=1) * spec[None, :, n],
                             n=2 * length, axis=1)[:, :length]
        y = parts[n].astype(jnp.float32) * (conv + y * p['hy_bias'][n].astype(jnp.float32))
    return y.astype(u.dtype)


def _merge(h, branches, p):
    gates = jnp.split(jax.nn.sigmoid(h @ p['w_gate'] + p['b_gate']), N_BRANCH, axis=-1)
    mixed = gates[0] * (branches[0] @ p['w_branch'][0])
    for i in range(1, N_BRANCH):
        mixed = mixed + gates[i] * (branches[i] @ p['w_branch'][i])
    return mixed @ p['w_out']


def _ec_moe(h, p):
    b, length, _ = h.shape
    cap = max(1, EC_CAPACITY * length // N_EXPERTS)
    aff = jax.nn.softmax((h @ p['router']).astype(jnp.float32), axis=-1)
    gate, idx = lax.top_k(jnp.swapaxes(aff, 1, 2), cap)
    bidx = jnp.arange(b)[:, None, None]
    xs = h[bidx, idx]
    a = jnp.einsum('becd,edf->becf', xs, p['exp_w1'])
    g = jnp.einsum('becd,edf->becf', xs, p['exp_w3'])
    y = jnp.einsum('becf,efd->becd', jax.nn.silu(a) * g, p['exp_w2'])
    y = y * gate[..., None].astype(y.dtype)
    return jnp.zeros_like(h).at[bidx, idx].add(y)


def _layer(x, xc, c, c_ctx, p, last):
    w = BRANCH_W
    mod = (jax.nn.silu(c) @ p['w_ada'] + p['b_ada'])[:, None, :]
    mod_c = (jax.nn.silu(c_ctx) @ p['w_ada'] + p['b_ada'])[None, None, :]
    sh1, sc1, g1, sh2, sc2, g2 = jnp.split(mod, 6, axis=-1)
    sh1c, sc1c, g1c, sh2c, sc2c, g2c = jnp.split(mod_c, 6, axis=-1)

    h = _modulate(_rmsnorm(x, p['norm1']), sh1, sc1)
    hc = _modulate(_rmsnorm(xc, p['norm1']), sh1c, sc1c)
    proj = h @ p['w_in']
    projc = hc @ p['w_in']

    ssm_l, ssm_c = _s5_branch(projc[..., :w], proj[..., :w], p, last)

    ql, kl, vl = [_na_heads(t) for t in jnp.split(proj[..., w:4 * w], 3, axis=-1)]
    qc, kc, vc = [_na_heads(t) for t in jnp.split(projc[..., w:4 * w], 3, axis=-1)]
    kc = _rmsnorm(kc, p['na_k_gain'])
    na_l = _na_latent(_rmsnorm(ql, p['na_q_gain']), _rmsnorm(kl, p['na_k_gain']), vl, kc, vc, p['na_rpb'])

    hy_l = _hyena_branch(proj[..., 4 * w:], p)

    x = x + g1 * _merge(h, (ssm_l, na_l, hy_l), p)
    x = x + g2 * _ec_moe(_modulate(_rmsnorm(x, p['norm2']), sh2, sc2), p)
    if last:
        return x, None
    na_c = _na_context(_rmsnorm(qc, p['na_q_gain']), kc, vc)
    hy_c = _hyena_branch(projc[..., 4 * w:], p)
    xc = xc + g1c * _merge(hc, (ssm_c, na_c, hy_c), p)
    xc = xc + g2c * _ec_moe(_modulate(_rmsnorm(xc, p['norm2']), sh2c, sc2c), p)
    return x, xc


def setup_inputs(seed: int = 0) -> dict:
    key = jax.random.key(seed)
    ks = iter(jax.random.split(key, 48))
    f32 = jnp.float32
    D, W, G, P, K, L_ = D_MODEL, BRANCH_W, SSM_GROUPS, SSM_STATE, SSM_GROUP, DEPTH

    def nrm(shape, scale):
        return jax.random.normal(next(ks), shape, f32) * scale

    inp = {}
    inp['x'] = nrm((BATCH, SEQ, D), 1.0)
    inp['c'] = nrm((BATCH, D), 1.0)
    inp['ctx'] = nrm((BATCH, CTX_LEN, D), 1.0)
    inp['c_ctx'] = nrm((D,), 1.0)
    inp['w_ada'] = nrm((L_, D, 6 * D), D ** -0.5)
    inp['b_ada'] = nrm((L_, 6 * D), 0.02)
    inp['norm1'] = 1.0 + nrm((L_, D), 0.02)
    inp['norm2'] = 1.0 + nrm((L_, D), 0.02)
    inp['w_in'] = nrm((L_, D, IN_WIDTH), D ** -0.5)
    inp['ssm_lam_re'] = -0.5 + nrm((L_, 2, G, P), 0.01)
    inp['ssm_lam_im'] = math.pi * jnp.arange(P, dtype=f32) + nrm((L_, 2, G, P), 0.01)
    inp['ssm_log_step'] = jax.random.uniform(next(ks), (L_, 2, G), f32, math.log(SSM_DT_MIN), math.log(SSM_DT_MAX))
    inp['ssm_b_re'] = nrm((L_, 2, G, P, K), (2 * K) ** -0.5)
    inp['ssm_b_im'] = nrm((L_, 2, G, P, K), (2 * K) ** -0.5)
    inp['ssm_c_re'] = nrm((L_, 2, G, K, P), (2 * P) ** -0.5)
    inp['ssm_c_im'] = nrm((L_, 2, G, K, P), (2 * P) ** -0.5)
    inp['ssm_d'] = nrm((L_, W), 1.0)
    inp['ssm_w_glu'] = nrm((L_, W, W), W ** -0.5)
    inp['na_q_gain'] = 1.0 + nrm((L_, NA_HEAD_DIM), 0.02)
    inp['na_k_gain'] = 1.0 + nrm((L_, NA_HEAD_DIM), 0.02)
    inp['na_rpb'] = nrm((L_, NA_HEADS, 2 * NA_WIN_ROWS - 1, 2 * NA_WIN_COLS - 1), 0.1)
    inp['hy_conv_w'] = nrm((L_, HY_SHORT, (HY_ORDER + 1) * W), HY_SHORT ** -0.5)
    inp['hy_conv_b'] = nrm((L_, (HY_ORDER + 1) * W), 0.02)
    inp['hy_w1'] = nrm((L_, HY_EMB, HY_FILTER_W), HY_EMB ** -0.5)
    inp['hy_b1'] = nrm((L_, HY_FILTER_W), 0.1)
    inp['hy_w2'] = nrm((L_, HY_FILTER_W, HY_FILTER_W), HY_FILTER_W ** -0.5)
    inp['hy_b2'] = nrm((L_, HY_FILTER_W), 0.1)
    inp['hy_w3'] = nrm((L_, HY_FILTER_W, HY_FILTER_W), HY_FILTER_W ** -0.5)
    inp['hy_b3'] = nrm((L_, HY_FILTER_W), 0.1)
    inp['hy_w4'] = nrm((L_, HY_FILTER_W, HY_ORDER * 2 * W), HY_FILTER_W ** -0.5)
    inp['hy_freq'] = 1.0 + nrm((L_, HY_FILTER_W), 0.01)
    inp['hy_bias'] = nrm((L_, HY_ORDER, W), 0.5)
    inp['w_gate'] = nrm((L_, D, N_BRANCH * D), D ** -0.5)
    inp['b_gate'] = nrm((L_, N_BRANCH * D), 0.02)
    inp['w_branch'] = nrm((L_, N_BRANCH, W, D), W ** -0.5)
    inp['w_out'] = nrm((L_, D, D), D ** -0.5)
    inp['router'] = nrm((L_, D, N_EXPERTS), D ** -0.5)
    inp['exp_w1'] = nrm((L_, N_EXPERTS, D, EXPERT_FF), D ** -0.5)
    inp['exp_w3'] = nrm((L_, N_EXPERTS, D, EXPERT_FF), D ** -0.5)
    inp['exp_w2'] = nrm((L_, N_EXPERTS, EXPERT_FF, D), EXPERT_FF ** -0.5)
    return inp


def reference(x, c, ctx, c_ctx, w_ada, b_ada, norm1, norm2, w_in,
              ssm_lam_re, ssm_lam_im, ssm_log_step, ssm_b_re, ssm_b_im, ssm_c_re, ssm_c_im, ssm_d, ssm_w_glu,
              na_q_gain, na_k_gain, na_rpb,
              hy_conv_w, hy_conv_b, hy_w1, hy_b1, hy_w2, hy_b2, hy_w3, hy_b3, hy_w4, hy_freq, hy_bias,
              w_gate, b_gate, w_branch, w_out,
              router, exp_w1, exp_w3, exp_w2):
    xc = ctx
    for l in range(DEPTH):
        p = {
            'w_ada': w_ada[l], 'b_ada': b_ada[l], 'norm1': norm1[l], 'norm2': norm2[l], 'w_in': w_in[l],
            'ssm_lam_re': ssm_lam_re[l], 'ssm_lam_im': ssm_lam_im[l], 'ssm_log_step': ssm_log_step[l],
            'ssm_b_re': ssm_b_re[l], 'ssm_b_im': ssm_b_im[l], 'ssm_c_re': ssm_c_re[l], 'ssm_c_im': ssm_c_im[l],
            'ssm_d': ssm_d[l], 'ssm_w_glu': ssm_w_glu[l],
            'na_q_gain': na_q_gain[l], 'na_k_gain': na_k_gain[l], 'na_rpb': na_rpb[l],
            'hy_conv_w': hy_conv_w[l], 'hy_conv_b': hy_conv_b[l], 'hy_w1': hy_w1[l], 'hy_b1': hy_b1[l],
            'hy_w2': hy_w2[l], 'hy_b2': hy_b2[l], 'hy_w3': hy_w3[l], 'hy_b3': hy_b3[l], 'hy_w4': hy_w4[l],
            'hy_freq': hy_freq[l], 'hy_bias': hy_bias[l],
            'w_gate': w_gate[l], 'b_gate': b_gate[l], 'w_branch': w_branch[l], 'w_out': w_out[l],
            'router': router[l], 'exp_w1': exp_w1[l], 'exp_w3': exp_w3[l], 'exp_w2': exp_w2[l],
        }
        x, xc = _layer(x, xc, c, c_ctx, p, l == DEPTH - 1)
    return x
```

```python
import functools
import math

import numpy as np
import jax
import jax.numpy as jnp
from jax import lax
from jax.experimental import pallas as pl
from jax.experimental.pallas import tpu as pltpu

BF16 = jnp.bfloat16
F32 = jnp.float32

NORM_EPS = 1e-6
GRID_W = 64
SSM_GROUP = 16
NA_HEADS = 8
NA_WIN_ROWS = 8
NA_WIN_COLS = 16
HY_ORDER = 2
HY_BANDS = 16
HY_MIN_DECAY = math.log(1e-2) / 1.5
HY_MAX_DECAY = math.log(1e-2) / 0.3
EC_CAPACITY = 2
S5_CHUNK = 16
MASK_VALUE = -1e30

V7X_VMEM_BYTES = 64 * 1024 * 1024


def _cparams(semantics, vmem_mb=48):
    return pltpu.CompilerParams(dimension_semantics=semantics, vmem_limit_bytes=vmem_mb * 1024 * 1024)


def _norm_mod(x, g, sh, sc):
    ms = jnp.mean(x * x, axis=-1, keepdims=True)
    return (x * lax.rsqrt(ms + NORM_EPS) * g) * (1.0 + sc) + sh


def _ada_kernel(c_ref, w_ref, b_ref, o_ref):
    c = c_ref[...]
    a = (c * jax.nn.sigmoid(c)).astype(BF16)
    o_ref[...] = jnp.dot(a, w_ref[...].astype(BF16), preferred_element_type=F32) + b_ref[...]


def _ada(cc, w, b, layer):
    m, k = cc.shape
    depth, _, n = w.shape
    tn = next(t for t in (1024, 512, 256, 128) if n % t == 0)
    return pl.pallas_call(
        _ada_kernel,
        grid=(n // tn,),
        in_specs=[pl.BlockSpec((m, k), lambda j: (0, 0)),
                  pl.BlockSpec((None, k, tn), lambda j: (layer, 0, j)),
                  pl.BlockSpec((None, 1, tn), lambda j: (layer, 0, j))],
        out_specs=pl.BlockSpec((m, tn), lambda j: (0, j)),
        out_shape=jax.ShapeDtypeStruct((m, n), F32),
        compiler_params=_cparams(("arbitrary",)),
        name="ada",
    )(cc, w, b.reshape(depth, 1, n))


def _in_proj_kernel(x_ref, g_ref, sh_ref, sc_ref, w_ref, hg_ref, ones_ref, o_ref, h_scr, *, head_dim):
    j = pl.program_id(2)

    @pl.when(j == 0)
    def _():
        h_scr[...] = _norm_mod(x_ref[...], g_ref[...], sh_ref[...], sc_ref[...]).astype(BF16)

    acc = jnp.dot(h_scr[...], w_ref[...], preferred_element_type=F32)
    is_qk = jnp.logical_or(j == 1, j == 2)

    @pl.when(is_qk)
    def _():
        yy = acc * acc
        hi = yy.astype(BF16)
        lo = (yy - hi.astype(F32)).astype(BF16)
        ss = (jnp.dot(hi, ones_ref[...], preferred_element_type=F32)
              + jnp.dot(lo, ones_ref[...], preferred_element_type=F32))
        o_ref[...] = (acc * lax.rsqrt(ss * (1.0 / head_dim) + NORM_EPS) * hg_ref[...]).astype(o_ref.dtype)

    @pl.when(jnp.logical_not(is_qk))
    def _():
        o_ref[...] = acc.astype(o_ref.dtype)


def _in_proj(x, g, sh, sc, w_bf, head_gain, ones_blk, tm):
    b, l, d = x.shape
    n = w_bf.shape[1]
    tn = ones_blk.shape[0]
    nb = sh.shape[0]
    mod_map = (lambda bi, i, j: (bi, 0, 0)) if nb > 1 else (lambda bi, i, j: (0, 0, 0))
    return pl.pallas_call(
        functools.partial(_in_proj_kernel, head_dim=tn // NA_HEADS),
        grid=(b, l // tm, n // tn),
        in_specs=[pl.BlockSpec((None, tm, d), lambda bi, i, j: (bi, i, 0)),
                  pl.BlockSpec((1, d), lambda bi, i, j: (0, 0)),
                  pl.BlockSpec((None, 1, d), mod_map),
                  pl.BlockSpec((None, 1, d), mod_map),
                  pl.BlockSpec((d, tn), lambda bi, i, j: (0, j)),
                  pl.BlockSpec((None, 1, tn), lambda bi, i, j: (jnp.clip(j - 1, 0, 1), 0, 0)),
                  pl.BlockSpec((tn, tn), lambda bi, i, j: (0, 0))],
        out_specs=pl.BlockSpec((None, tm, tn), lambda bi, i, j: (bi, i, j)),
        out_shape=jax.ShapeDtypeStruct((b, l, n), BF16),
        scratch_shapes=[pltpu.VMEM((tm, d), BF16)],
        compiler_params=_cparams(("parallel", "parallel", "arbitrary")),
        name="in_proj",
    )(x, g.reshape(1, d), sh, sc, w_bf, head_gain, ones_blk)


def _bmm_kernel(a_ref, w_ref, o_ref):
    o_ref[...] = jnp.dot(a_ref[...].astype(BF16), w_ref[...], preferred_element_type=F32).astype(o_ref.dtype)


def _bmm_add_kernel(a_ref, w_ref, y_ref, o_ref):
    o_ref[...] = (y_ref[...] + jnp.dot(a_ref[...].astype(BF16), w_ref[...],
                                       preferred_element_type=F32)).astype(o_ref.dtype)


def _bmm(a, w, add=None, tm=None, name="bmm"):
    g, m, k = a.shape
    n = w.shape[2]
    tm = m if tm is None else tm
    in_specs = [pl.BlockSpec((None, tm, k), lambda gi, i: (gi, i, 0)),
                pl.BlockSpec((None, k, n), lambda gi, i: (gi, 0, 0))]
    args = [a, w]
    kern = _bmm_kernel
    if add is not None:
        in_specs.append(pl.BlockSpec((None, tm, n), lambda gi, i: (gi, i, 0)))
        args.append(add)
        kern = _bmm_add_kernel
    return pl.pallas_call(
        kern,
        grid=(g, m // tm),
        in_specs=in_specs,
        out_specs=pl.BlockSpec((None, tm, n), lambda gi, i: (gi, i, 0)),
        out_shape=jax.ShapeDtypeStruct((g, m, n), F32),
        compiler_params=_cparams(("parallel", "parallel")),
        name=name,
    )(*args)


def _gelu_tanh(x):
    return 0.5 * x * (1.0 + jnp.tanh(math.sqrt(2.0 / math.pi) * (x + 0.044715 * (x * x * x))))


def _s5_out_kernel(y_ref, u_ref, d_ref, w_ref, o_ref):
    y = y_ref[...] + d_ref[...] * u_ref[...].astype(F32)
    z = _gelu_tanh(y)
    gate = jax.nn.sigmoid(jnp.dot(z.astype(BF16), w_ref[...], preferred_element_type=F32))
    o_ref[...] = (z * gate).astype(o_ref.dtype)


def _s5_out(y, u, d, w_glu_bf, tm):
    m, w = y.shape
    return pl.pallas_call(
        _s5_out_kernel,
        grid=(m // tm,),
        in_specs=[pl.BlockSpec((tm, w), lambda i: (i, 0)),
                  pl.BlockSpec((tm, w), lambda i: (i, 0)),
                  pl.BlockSpec((1, w), lambda i: (0, 0)),
                  pl.BlockSpec((w, w), lambda i: (0, 0))],
        out_specs=pl.BlockSpec((tm, w), lambda i: (i, 0)),
        out_shape=jax.ShapeDtypeStruct((m, w), BF16),
        compiler_params=_cparams(("parallel",)),
        name="s5_out",
    )(y, u, d.reshape(1, w), w_glu_bf)


def _s5_tables(p):
    t = S5_CHUNK
    mats, sums, reads, steps = [], [], [], []
    for d in range(2):
        lam_re = p['ssm_lam_re'][d].astype(F32)
        lam_im = p['ssm_lam_im'][d].astype(F32)
        step = jnp.exp(p['ssm_log_step'][d].astype(F32))[:, None]
        mag = jnp.exp(lam_re * step)
        a_re = mag * jnp.cos(lam_im * step)
        a_im = mag * jnp.sin(lam_im * step)
        den = lam_re * lam_re + lam_im * lam_im
        num_re = a_re - 1.0
        coef_re = (num_re * lam_re + a_im * lam_im) / den
        coef_im = (a_im * lam_re - num_re * lam_im) / den
        b_re = p['ssm_b_re'][d].astype(F32)
        b_im = p['ssm_b_im'][d].astype(F32)
        bb_re = coef_re[..., None] * b_re - coef_im[..., None] * b_im
        bb_im = coef_re[..., None] * b_im + coef_im[..., None] * b_re
        c_re = p['ssm_c_re'][d].astype(F32)
        c_im = p['ssm_c_im'][d].astype(F32)
        n = jnp.arange(t + 1, dtype=F32)[:, None, None]
        pm = jnp.exp(lam_re * step * n)
        pw_re = pm * jnp.cos(lam_im * step * n)
        pw_im = pm * jnp.sin(lam_im * step * n)
        ca_re = c_re[None] * pw_re[:, :, None, :] - c_im[None] * pw_im[:, :, None, :]
        ca_im = c_re[None] * pw_im[:, :, None, :] + c_im[None] * pw_re[:, :, None, :]
        ktau = (jnp.einsum('ngop,gpi->ngoi', ca_re[:t], bb_re)
                - jnp.einsum('ngop,gpi->ngoi', ca_im[:t], bb_im))
        s_idx = np.arange(t)[:, None]
        t_idx = np.arange(t)[None, :]
        lag = (t_idx - s_idx) if d == 0 else (s_idx - t_idx)
        valid = jnp.asarray(lag >= 0, F32)
        blk = ktau[np.clip(lag, 0, t - 1)] * valid[:, :, None, None, None]
        mats.append(jnp.transpose(blk, (2, 0, 4, 1, 3)))
        pows = (t - 1 - np.arange(t)) if d == 0 else np.arange(t)
        sw_re = pw_re[pows][:, :, :, None] * bb_re[None] - pw_im[pows][:, :, :, None] * bb_im[None]
        sw_im = pw_re[pows][:, :, :, None] * bb_im[None] + pw_im[pows][:, :, :, None] * bb_re[None]
        sums.append((jnp.transpose(sw_re, (1, 0, 3, 2)), jnp.transpose(sw_im, (1, 0, 3, 2))))
        rp = (np.arange(t) + 1) if d == 0 else (t - np.arange(t))
        reads.append((jnp.transpose(ca_re[rp], (1, 3, 0, 2)), -jnp.transpose(ca_im[rp], (1, 3, 0, 2))))
        steps.append((pw_re[t], pw_im[t]))
    g, pdim = steps[0][0].shape
    k = SSM_GROUP
    mtot = (mats[0] + mats[1]).reshape(g, t * k, t * k)
    ssum = jnp.concatenate([sums[0][0], sums[0][1], sums[1][0], sums[1][1]], axis=-1).reshape(g, t * k, 4 * pdim)
    w1 = jnp.concatenate([mtot, ssum], axis=-1)
    r = jnp.concatenate([reads[0][0], reads[0][1], reads[1][0], reads[1][1]], axis=1).reshape(g, 4 * pdim, t * k)
    return w1.astype(BF16), r.astype(BF16), steps


def _chunk_scan(x_re, x_im, a_re, a_im, reverse):
    def comb(e1, e2):
        a1r, a1i, b1r, b1i = e1
        a2r, a2i, b2r, b2i = e2
        return (a2r * a1r - a2i * a1i, a2r * a1i + a2i * a1r,
                a2r * b1r - a2i * b1i + b2r, a2r * b1i + a2i * b1r + b2i)
    ar = jnp.broadcast_to(a_re[:, None, None, :], x_re.shape)
    ai = jnp.broadcast_to(a_im[:, None, None, :], x_re.shape)
    _, _, h_re, h_im = lax.associative_scan(comb, (ar, ai, x_re, x_im), axis=2, reverse=reverse)
    z = jnp.zeros_like(h_re[:, :, :1])
    if reverse:
        return jnp.concatenate([h_re[:, :, 1:], z], 2), jnp.concatenate([h_im[:, :, 1:], z], 2)
    return jnp.concatenate([z, h_re[:, :, :-1]], 2), jnp.concatenate([z, h_im[:, :, :-1]], 2)


def _s5_branch(u_ctx, u_lat, p, tables, need_ctx):
    w1, r, steps = tables
    b, l, w = u_lat.shape
    lc = u_ctx.shape[1]
    t, k = S5_CHUNK, SSM_GROUP
    g = w // k
    nc_c, nc_l = lc // t, l // t
    pdim = r.shape[1] // 4

    def to_groups(u):
        n = u.shape[1] // t
        return jnp.transpose(u.reshape(b, n, t, g, k), (3, 0, 1, 2, 4)).reshape(g, b, n, t * k)

    ug = jnp.concatenate([to_groups(u_ctx), to_groups(u_lat)], axis=2)
    nc = nc_c + nc_l
    m = b * nc
    yx = _bmm(ug.reshape(g, m, t * k), w1, name="s5_chunk")
    y0 = yx[..., :t * k]
    xs = yx[..., t * k:].reshape(g, b, nc, 4, pdim)
    hf_re, hf_im = _chunk_scan(xs[..., 0, :], xs[..., 1, :], steps[0][0], steps[0][1], reverse=False)
    xb_re = jnp.concatenate([xs[:, :, nc_c:, 2, :], xs[:, :, :nc_c, 2, :]], axis=2)
    xb_im = jnp.concatenate([xs[:, :, nc_c:, 3, :], xs[:, :, :nc_c, 3, :]], axis=2)
    hb_re, hb_im = _chunk_scan(xb_re, xb_im, steps[1][0], steps[1][1], reverse=True)
    hb_re = jnp.concatenate([hb_re[:, :, nc_l:], hb_re[:, :, :nc_l]], axis=2)
    hb_im = jnp.concatenate([hb_im[:, :, nc_l:], hb_im[:, :, :nc_l]], axis=2)
    hin = jnp.concatenate([hf_re, hf_im, hb_re, hb_im], axis=-1).reshape(g, m, 4 * pdim)
    y = _bmm(hin, r, add=y0, name="s5_readout").reshape(g, b, nc, t, k)
    y = jnp.transpose(y, (1, 2, 3, 0, 4)).reshape(b, nc * t, w)
    w_glu = p['ssm_w_glu'].astype(BF16)
    out_l = _s5_out(y[:, lc:].reshape(b * l, w), u_lat.reshape(b * l, w), p['ssm_d'], w_glu, tm=512)
    out_c = None
    if need_ctx:
        out_c = _s5_out(y[:, :lc].reshape(b * lc, w), u_ctx.reshape(b * lc, w), p['ssm_d'], w_glu,
                        tm=min(512, b * lc)).reshape(b, lc, w)
    return out_l.reshape(b, l, w), out_c


def _na_kernel(q_ref, k_ref, v_ref, kc_ref, vc_ref, bias_ref, o_ref, *, rows, head_dim):
    r = pl.program_id(1)
    kr = NA_WIN_ROWS
    start = jnp.clip(r - kr // 2, 0, rows - kr)
    cls = r - start
    base = pl.multiple_of(start * GRID_W, GRID_W)
    kwin = k_ref[pl.ds(base, kr * GRID_W), :]
    vwin = v_ref[pl.ds(base, kr * GRID_W), :]
    q = q_ref[...]
    kc = kc_ref[...]
    vc = vc_ref[...]
    outs = []
    nt = (((1,), (1,)), ((), ()))
    for h in range(NA_HEADS):
        sl = slice(h * head_dim, (h + 1) * head_dim)
        qh = q[:, sl]
        s_win = lax.dot_general(qh, kwin[:, sl], nt, preferred_element_type=F32) + bias_ref[cls, h]
        s_ctx = lax.dot_general(qh, kc[:, sl], nt, preferred_element_type=F32)
        mx = jnp.maximum(jnp.max(s_win, axis=-1, keepdims=True), jnp.max(s_ctx, axis=-1, keepdims=True))
        p_win = jnp.exp(s_win - mx)
        p_ctx = jnp.exp(s_ctx - mx)
        den = jnp.sum(p_win, axis=-1, keepdims=True) + jnp.sum(p_ctx, axis=-1, keepdims=True)
        o = (jnp.dot(p_win.astype(BF16), vwin[:, sl], preferred_element_type=F32)
             + jnp.dot(p_ctx.astype(BF16), vc[:, sl], preferred_element_type=F32))
        outs.append(o / den)
    o_ref[...] = jnp.concatenate(outs, axis=-1).astype(o_ref.dtype)


def _na_bias_table(rpb):
    kr, kw, w = NA_WIN_ROWS, NA_WIN_COLS, GRID_W
    col = np.arange(w)
    cstart = np.clip(col - kw // 2, 0, w - kw)
    kcol = np.arange(w)
    inwin = (kcol[None, :] >= cstart[:, None]) & (kcol[None, :] < cstart[:, None] + kw)
    dc = np.clip(kcol[None, :] - col[:, None] + (kw - 1), 0, 2 * kw - 2)
    rpb = rpb.astype(F32)
    tabs = []
    for o in range(kr):
        dr = np.arange(kr) - o + (kr - 1)
        bias = rpb[:, dr][:, :, dc]
        bias = jnp.where(jnp.asarray(inwin)[None, None], bias, MASK_VALUE)
        tabs.append(jnp.transpose(bias, (0, 2, 1, 3)).reshape(rpb.shape[0], w, kr * w))
    return jnp.stack(tabs, axis=0)


def _na_latent(q, k, v, kc, vc, bias_tab):
    b, l, w = q.shape
    lc = kc.shape[1]
    rows = l // GRID_W
    assert rows >= NA_WIN_ROWS
    nk = NA_WIN_ROWS * GRID_W
    return pl.pallas_call(
        functools.partial(_na_kernel, rows=rows, head_dim=w // NA_HEADS),
        grid=(b, rows),
        in_specs=[pl.BlockSpec((None, GRID_W, w), lambda bi, r: (bi, r, 0)),
                  pl.BlockSpec((None, l, w), lambda bi, r: (bi, 0, 0)),
                  pl.BlockSpec((None, l, w), lambda bi, r: (bi, 0, 0)),
                  pl.BlockSpec((None, lc, w), lambda bi, r: (bi, 0, 0)),
                  pl.BlockSpec((None, lc, w), lambda bi, r: (bi, 0, 0)),
                  pl.BlockSpec((NA_WIN_ROWS, NA_HEADS, GRID_W, nk), lambda bi, r: (0, 0, 0, 0))],
        out_specs=pl.BlockSpec((None, GRID_W, w), lambda bi, r: (bi, r, 0)),
        out_shape=jax.ShapeDtypeStruct((b, l, w), BF16),
        compiler_params=_cparams(("parallel", "arbitrary"), vmem_mb=56),
        name="na_latent",
    )(q, k, v, kc, vc, bias_tab)


def _na_ctx_kernel(q_ref, k_ref, v_ref, o_ref, *, head_dim):
    q, k, v = q_ref[...], k_ref[...], v_ref[...]
    outs = []
    for h in range(NA_HEADS):
        sl = slice(h * head_dim, (h + 1) * head_dim)
        s = lax.dot_general(q[:, sl], k[:, sl], (((1,), (1,)), ((), ())), preferred_element_type=F32)
        pr = jnp.exp(s - jnp.max(s, axis=-1, keepdims=True))
        den = jnp.sum(pr, axis=-1, keepdims=True)
        outs.append(jnp.dot(pr.astype(BF16), v[:, sl], preferred_element_type=F32) / den)
    o_ref[...] = jnp.concatenate(outs, axis=-1).astype(o_ref.dtype)


def _na_context(q, k, v):
    b, lc, w = q.shape
    spec = pl.BlockSpec((None, lc, w), lambda bi: (bi, 0, 0))
    return pl.pallas_call(
        functools.partial(_na_ctx_kernel, head_dim=w // NA_HEADS),
        grid=(b,),
        in_specs=[spec, spec, spec],
        out_specs=spec,
        out_shape=jax.ShapeDtypeStruct((b, lc, w), BF16),
        compiler_params=_cparams(("parallel",)),
        name="na_context",
    )(q, k, v)


def _hyena_spectrum(length, p):
    t = jnp.linspace(0.0, 1.0, length, dtype=F32)[:, None]
    freqs = jnp.linspace(1e-4, HY_BANDS - 1, HY_BANDS, dtype=F32)
    ang = (2.0 * math.pi / length) * jnp.arange(length, dtype=F32)[:, None] * freqs[None, :]
    z = jnp.concatenate([t, jnp.cos(ang), -jnp.sin(ang)], axis=-1)
    fr = p['hy_freq'].astype(F32)
    hp = lax.Precision.HIGHEST
    h = jnp.sin(fr * (jnp.dot(z, p['hy_w1'].astype(F32), precision=hp) + p['hy_b1'].astype(F32)))
    h = jnp.sin(fr * (jnp.dot(h, p['hy_w2'].astype(F32), precision=hp) + p['hy_b2'].astype(F32)))
    h = jnp.sin(fr * (jnp.dot(h, p['hy_w3'].astype(F32), precision=hp) + p['hy_b3'].astype(F32)))
    wdim = p['hy_bias'].shape[-1]
    h = jnp.dot(h, p['hy_w4'].astype(F32), precision=hp).reshape(length, HY_ORDER, 2, wdim)
    decay = jnp.exp(-t * jnp.abs(jnp.linspace(HY_MIN_DECAY, HY_MAX_DECAY, wdim, dtype=F32)))
    h = h * decay[:, None, None, :]
    filt = jnp.concatenate([h[:, :, 0], jnp.zeros((1, HY_ORDER, wdim), F32), h[:0:-1, :, 1]], axis=0)
    filt = filt / jnp.sum(jnp.abs(filt), axis=0, keepdims=True)
    return jnp.fft.rfft(filt, axis=0)


def _hyena_branch(u, p):
    length = u.shape[1]
    uf = u.astype(F32)
    cw = p['hy_conv_w'].astype(F32)
    up = jnp.pad(uf, ((0, 0), (1, 1), (0, 0)))
    z = up[:, :-2] * cw[0] + up[:, 1:-1] * cw[1] + up[:, 2:] * cw[2] + p['hy_conv_b'].astype(F32)
    parts = jnp.split(z, HY_ORDER + 1, axis=-1)
    spec = _hyena_spectrum(length, p)
    y = parts[-1]
    for n in range(HY_ORDER):
        conv = jnp.fft.irfft(jnp.fft.rfft(y, n=2 * length, axis=1) * spec[None, :, n],
                             n=2 * length, axis=1)[:, :length]
        y = parts[n] * (conv + y * p['hy_bias'][n].astype(F32))
    return y.astype(BF16)


def _merge_kernel(x_ref, g_ref, sh_ref, sc_ref, g1_ref, br_ref, wg0_ref, wg1_ref, wg2_ref, bg_ref, wb_ref, wo_ref,
                  o_ref, h_scr, acc_scr):
    j = pl.program_id(2)

    @pl.when(j == 0)
    def _():
        h_scr[...] = _norm_mod(x_ref[...], g_ref[...], sh_ref[...], sc_ref[...]).astype(BF16)
        acc_scr[...] = jnp.zeros_like(acc_scr)

    h = h_scr[...]
    mixed = None
    for i, wg_ref in enumerate((wg0_ref, wg1_ref, wg2_ref)):
        gate = jax.nn.sigmoid(jnp.dot(h, wg_ref[...], preferred_element_type=F32) + bg_ref[i])
        term = gate * jnp.dot(br_ref[i], wb_ref[i], preferred_element_type=F32)
        mixed = term if mixed is None else mixed + term
    acc_scr[...] += jnp.dot(mixed.astype(BF16), wo_ref[...], preferred_element_type=F32)

    @pl.when(j == pl.num_programs(2) - 1)
    def _():
        o_ref[...] = x_ref[...] + g1_ref[...] * acc_scr[...]


def _merge(x, g, sh, sc, g1, branches, w_gate_bf, b_gate, w_branch_bf, w_out_bf, tm, tc):
    b, l, d = x.shape
    nbr, w, _ = w_branch_bf.shape
    nb = sh.shape[0]
    nj = d // tc
    mod_map = (lambda bi, i, j: (bi, 0, 0)) if nb > 1 else (lambda bi, i, j: (0, 0, 0))
    br = jnp.stack(branches, axis=1)
    bg = b_gate.reshape(nbr, nj, 1, tc).transpose(1, 0, 2, 3)
    return pl.pallas_call(
        _merge_kernel,
        grid=(b, l // tm, nj),
        in_specs=[pl.BlockSpec((None, tm, d), lambda bi, i, j: (bi, i, 0)),
                  pl.BlockSpec((1, d), lambda bi, i, j: (0, 0)),
                  pl.BlockSpec((None, 1, d), mod_map),
                  pl.BlockSpec((None, 1, d), mod_map),
                  pl.BlockSpec((None, 1, d), mod_map),
                  pl.BlockSpec((None, nbr, tm, w), lambda bi, i, j: (bi, 0, i, 0)),
                  pl.BlockSpec((d, tc), lambda bi, i, j: (0, j)),
                  pl.BlockSpec((d, tc), lambda bi, i, j: (0, nj + j)),
                  pl.BlockSpec((d, tc), lambda bi, i, j: (0, 2 * nj + j)),
                  pl.BlockSpec((None, nbr, 1, tc), lambda bi, i, j: (j, 0, 0, 0)),
                  pl.BlockSpec((nbr, w, tc), lambda bi, i, j: (0, 0, j)),
                  pl.BlockSpec((tc, d), lambda bi, i, j: (j, 0))],
        out_specs=pl.BlockSpec((None, tm, d), lambda bi, i, j: (bi, i, 0)),
        out_shape=jax.ShapeDtypeStruct((b, l, d), F32),
        scratch_shapes=[pltpu.VMEM((tm, d), BF16), pltpu.VMEM((tm, d), F32)],
        compiler_params=_cparams(("parallel", "parallel", "arbitrary"), vmem_mb=56),
        name="merge",
    )(x, g.reshape(1, d), sh, sc, g1, br, w_gate_bf, w_gate_bf, w_gate_bf, bg, w_branch_bf, w_out_bf)


def _router_kernel(x_ref, g_ref, sh_ref, sc_ref, rt_ref, h_ref, aff_ref):
    h = _norm_mod(x_ref[...], g_ref[...], sh_ref[...], sc_ref[...])
    h_ref[...] = h.astype(h_ref.dtype)
    nt = (((1,), (1,)), ((), ()))
    rt = rt_ref[...]
    r_hi = rt.astype(BF16)
    r_lo = (rt - r_hi.astype(F32)).astype(BF16)
    h_hi = h.astype(BF16)
    h_lo = (h - h_hi.astype(F32)).astype(BF16)
    logits = (lax.dot_general(r_hi, h_hi, nt, preferred_element_type=F32)
              + lax.dot_general(r_hi, h_lo, nt, preferred_element_type=F32)
              + lax.dot_general(r_lo, h_hi, nt, preferred_element_type=F32))
    e = jnp.exp(logits - jnp.max(logits, axis=0, keepdims=True))
    aff_ref[...] = e / jnp.sum(e, axis=0, keepdims=True)


def _router(x, g, sh, sc, router_t, tm):
    b, l, d = x.shape
    e = router_t.shape[0]
    nb = sh.shape[0]
    mod_map = (lambda bi, i: (bi, 0, 0)) if nb > 1 else (lambda bi, i: (0, 0, 0))
    return pl.pallas_call(
        _router_kernel,
        grid=(b, l // tm),
        in_specs=[pl.BlockSpec((None, tm, d), lambda bi, i: (bi, i, 0)),
                  pl.BlockSpec((1, d), lambda bi, i: (0, 0)),
                  pl.BlockSpec((None, 1, d), mod_map),
                  pl.BlockSpec((None, 1, d), mod_map),
                  pl.BlockSpec((e, d), lambda bi, i: (0, 0))],
        out_specs=[pl.BlockSpec((None, tm, d), lambda bi, i: (bi, i, 0)),
                   pl.BlockSpec((None, e, tm), lambda bi, i: (bi, 0, i))],
        out_shape=[jax.ShapeDtypeStruct((b, l, d), BF16), jax.ShapeDtypeStruct((b, e, l), F32)],
        compiler_params=_cparams(("parallel", "parallel")),
        name="router",
    )(x, g.reshape(1, d), sh, sc, router_t)


def _ffn_kernel(xs_ref, gate_ref, w1_ref, w3_ref, w2_ref, o_ref, acc_scr):
    f = pl.program_id(2)

    @pl.when(f == 0)
    def _():
        acc_scr[...] = jnp.zeros_like(acc_scr)

    xs = xs_ref[...]
    a = jnp.dot(xs, w1_ref[...].astype(BF16), preferred_element_type=F32)
    gl = jnp.dot(xs, w3_ref[...].astype(BF16), preferred_element_type=F32)
    mid = (a * jax.nn.sigmoid(a) * gl).astype(BF16)
    acc_scr[...] += jnp.dot(mid, w2_ref[...].astype(BF16), preferred_element_type=F32)

    @pl.when(f == pl.num_programs(2) - 1)
    def _():
        o_ref[...] = (acc_scr[...] * gate_ref[...]).astype(o_ref.dtype)


def _expert_ffn(xs, gate, w1, w3, w2, layer, tr, tf):
    e, r, d = xs.shape
    ff = w1.shape[3]
    return pl.pallas_call(
        _ffn_kernel,
        grid=(e, r // tr, ff // tf),
        in_specs=[pl.BlockSpec((None, tr, d), lambda ei, i, f: (ei, i, 0)),
                  pl.BlockSpec((None, tr, 1), lambda ei, i, f: (ei, i, 0)),
                  pl.BlockSpec((None, None, d, tf), lambda ei, i, f: (layer, ei, 0, f)),
                  pl.BlockSpec((None, None, d, tf), lambda ei, i, f: (layer, ei, 0, f)),
                  pl.BlockSpec((None, None, tf, d), lambda ei, i, f: (layer, ei, f, 0))],
        out_specs=pl.BlockSpec((None, tr, d), lambda ei, i, f: (ei, i, 0)),
        out_shape=jax.ShapeDtypeStruct((e, r, d), F32),
        scratch_shapes=[pltpu.VMEM((tr, d), F32)],
        compiler_params=_cparams(("parallel", "parallel", "arbitrary"), vmem_mb=56),
        name="expert_ffn",
    )(xs, gate, w1, w3, w2)


def _ec_moe_update(x, g, sh, sc, g2, p, tm):
    b, l, d = x.shape
    e = p['router'].shape[1]
    cap = max(1, EC_CAPACITY * l // e)
    h, aff_t = _router(x, g, sh, sc, jnp.transpose(p['router']).astype(F32), tm)
    gate, idx = lax.top_k(aff_t, cap)
    bidx = jnp.arange(b)[:, None, None]
    xs = jnp.transpose(h[bidx, idx], (1, 0, 2, 3)).reshape(e, b * cap, d)
    gt = jnp.transpose(gate, (1, 0, 2)).reshape(e, b * cap, 1)
    tr = min(1024, b * cap)
    ff = p['exp_w1'].shape[3]
    y = _expert_ffn(xs, gt, p['exp_w1'], p['exp_w3'], p['exp_w2'], p['layer'], tr=tr, tf=min(256, ff))
    y = jnp.transpose(y.reshape(e, b, cap, d), (1, 0, 2, 3))
    moe = jnp.zeros((b, l, d), F32).at[bidx, idx].add(y)
    return x + g2 * moe


def _layer(x, xc, c, c_ctx, p, last):
    b, l, d = x.shape
    lc = xc.shape[1]
    w = p['ssm_d'].shape[0]
    hd = w // NA_HEADS

    cc = jnp.zeros((8, d), F32).at[:b].set(c).at[b].set(c_ctx)
    mod = _ada(cc, p['w_ada'], p['b_ada'], p['layer'])
    m_l = [t[:, None, :] for t in jnp.split(mod[:b], 6, axis=-1)]
    m_c = [t[:, None, :] for t in jnp.split(mod[b:b + 1], 6, axis=-1)]

    w_in = p['w_in'].astype(BF16)
    scale = hd ** -0.5
    head_gain = jnp.stack([jnp.tile(p['na_q_gain'].astype(F32) * scale, NA_HEADS),
                           jnp.tile(p['na_k_gain'].astype(F32), NA_HEADS)])[:, None, :]
    ones_blk = jnp.asarray(np.kron(np.eye(NA_HEADS), np.ones((hd, hd))), BF16)
    tm_l = min(512, l)
    tm_c = min(512, lc)
    proj = _in_proj(x, p['norm1'], m_l[0], m_l[1], w_in, head_gain, ones_blk, tm_l)
    projc = _in_proj(xc, p['norm1'], m_c[0], m_c[1], w_in, head_gain, ones_blk, tm_c)

    ssm_l, ssm_c = _s5_branch(projc[..., :w], proj[..., :w], p, _s5_tables(p), need_ctx=not last)

    na_l = _na_latent(proj[..., w:2 * w], proj[..., 2 * w:3 * w], proj[..., 3 * w:4 * w],
                      projc[..., 2 * w:3 * w], projc[..., 3 * w:4 * w], _na_bias_table(p['na_rpb']))
    hy_l = _hyena_branch(proj[..., 4 * w:], p)

    wg = p['w_gate'].astype(BF16)
    wb = p['w_branch'].astype(BF16)
    wo = p['w_out'].astype(BF16)
    x = _merge(x, p['norm1'], m_l[0], m_l[1], m_l[2], (ssm_l, na_l, hy_l), wg, p['b_gate'], wb, wo,
               tm=tm_l, tc=min(512, d))
    x = _ec_moe_update(x, p['norm2'], m_l[3], m_l[4], m_l[5], p, tm_l)
    if last:
        return x, None
    na_c = _na_context(projc[..., w:2 * w], projc[..., 2 * w:3 * w], projc[..., 3 * w:4 * w])
    hy_c = _hyena_branch(projc[..., 4 * w:], p)
    xc = _merge(xc, p['norm1'], m_c[0], m_c[1], m_c[2], (ssm_c, na_c, hy_c), wg, p['b_gate'], wb, wo,
                tm=tm_c, tc=min(512, d))
    xc = _ec_moe_update(xc, p['norm2'], m_c[3], m_c[4], m_c[5], p, tm_c)
    return x, xc


_PARAM_NAMES = ('w_ada', 'b_ada', 'norm1', 'norm2', 'w_in',
                'ssm_lam_re', 'ssm_lam_im', 'ssm_log_step', 'ssm_b_re', 'ssm_b_im', 'ssm_c_re', 'ssm_c_im',
                'ssm_d', 'ssm_w_glu', 'na_q_gain', 'na_k_gain', 'na_rpb',
                'hy_conv_w', 'hy_conv_b', 'hy_w1', 'hy_b1', 'hy_w2', 'hy_b2', 'hy_w3', 'hy_b3', 'hy_w4',
                'hy_freq', 'hy_bias', 'w_gate', 'b_gate', 'w_branch', 'w_out',
                'router', 'exp_w1', 'exp_w3', 'exp_w2')
_STACKED = ('w_ada', 'b_ada', 'exp_w1', 'exp_w3', 'exp_w2')


def kernel(x, c, ctx, c_ctx, w_ada, b_ada, norm1, norm2, w_in, ssm_lam_re, ssm_lam_im, ssm_log_step, ssm_b_re, ssm_b_im, ssm_c_re, ssm_c_im, ssm_d, ssm_w_glu, na_q_gain, na_k_gain, na_rpb, hy_conv_w, hy_conv_b, hy_w1, hy_b1, hy_w2, hy_b2, hy_w3, hy_b3, hy_w4, hy_freq, hy_bias, w_gate, b_gate, w_branch, w_out, router, exp_w1, exp_w3, exp_w2):
    stacked = (w_ada, b_ada, norm1, norm2, w_in, ssm_lam_re, ssm_lam_im, ssm_log_step, ssm_b_re, ssm_b_im,
               ssm_c_re, ssm_c_im, ssm_d, ssm_w_glu, na_q_gain, na_k_gain, na_rpb, hy_conv_w, hy_conv_b,
               hy_w1, hy_b1, hy_w2, hy_b2, hy_w3, hy_b3, hy_w4, hy_freq, hy_bias, w_gate, b_gate, w_branch,
               w_out, router, exp_w1, exp_w3, exp_w2)
    depth = w_ada.shape[0]
    xc = ctx
    for layer in range(depth):
        p = {name: (t if name in _STACKED else t[layer]) for name, t in zip(_PARAM_NAMES, stacked)}
        p['layer'] = layer
        x, xc = _layer(x, xc, c, c_ctx, p, layer == depth - 1)
    return x
```

```python
import functools
import math

import numpy as np
import jax
import jax.numpy as jnp
from jax import lax
from jax.experimental import pallas as pl
from jax.experimental.pallas import tpu as pltpu

BF16 = jnp.bfloat16
F32 = jnp.float32

NORM_EPS = 1e-6
GRID_W = 64
SSM_GROUP = 16
NA_HEADS = 8
NA_WIN_ROWS = 8
NA_WIN_COLS = 16
HY_ORDER = 2
HY_BANDS = 16
HY_MIN_DECAY = math.log(1e-2) / 1.5
HY_MAX_DECAY = math.log(1e-2) / 0.3
EC_CAPACITY = 2
S5_CHUNK = 16
HY_FAST = 128
HY_PAD = 8
MASK_VALUE = -1e30

V7X_VMEM_BYTES = 64 * 1024 * 1024


def _cparams(semantics, vmem_mb=48):
    return pltpu.CompilerParams(dimension_semantics=semantics, vmem_limit_bytes=vmem_mb * 1024 * 1024)


def _norm_mod(x, g, sh, sc):
    ms = jnp.mean(x * x, axis=-1, keepdims=True)
    return (x * lax.rsqrt(ms + NORM_EPS) * g) * (1.0 + sc) + sh


def _ada_kernel(c_ref, w_ref, b_ref, o_ref):
    c = c_ref[...]
    a = (c * jax.nn.sigmoid(c)).astype(BF16)
    o_ref[...] = jnp.dot(a, w_ref[...].astype(BF16), preferred_element_type=F32) + b_ref[...]


def _ada(cc, w, b, layer):
    m, k = cc.shape
    depth, _, n = w.shape
    tn = next(t for t in (1024, 512, 256, 128) if n % t == 0)
    return pl.pallas_call(
        _ada_kernel,
        grid=(n // tn,),
        in_specs=[pl.BlockSpec((m, k), lambda j: (0, 0)),
                  pl.BlockSpec((None, k, tn), lambda j: (layer, 0, j)),
                  pl.BlockSpec((None, 1, tn), lambda j: (layer, 0, j))],
        out_specs=pl.BlockSpec((m, tn), lambda j: (0, j)),
        out_shape=jax.ShapeDtypeStruct((m, n), F32),
        compiler_params=_cparams(("arbitrary",)),
        name="ada",
    )(cc, w, b.reshape(depth, 1, n))


def _in_proj_kernel(x_ref, g_ref, sh_ref, sc_ref, w_ref, hg_ref, ones_ref, o_ref, h_scr, *, head_dim):
    j = pl.program_id(2)

    @pl.when(j == 0)
    def _():
        h_scr[...] = _norm_mod(x_ref[...], g_ref[...], sh_ref[...], sc_ref[...]).astype(BF16)

    acc = jnp.dot(h_scr[...], w_ref[...], preferred_element_type=F32)
    is_qk = jnp.logical_or(j == 1, j == 2)

    @pl.when(is_qk)
    def _():
        yy = acc * acc
        hi = yy.astype(BF16)
        lo = (yy - hi.astype(F32)).astype(BF16)
        ss = (jnp.dot(hi, ones_ref[...], preferred_element_type=F32)
              + jnp.dot(lo, ones_ref[...], preferred_element_type=F32))
        o_ref[...] = (acc * lax.rsqrt(ss * (1.0 / head_dim) + NORM_EPS) * hg_ref[...]).astype(o_ref.dtype)

    @pl.when(jnp.logical_not(is_qk))
    def _():
        o_ref[...] = acc.astype(o_ref.dtype)


def _in_proj(x, g, sh, sc, w_bf, head_gain, ones_blk, tm):
    b, l, d = x.shape
    n = w_bf.shape[1]
    tn = ones_blk.shape[0]
    nb = sh.shape[0]
    mod_map = (lambda bi, i, j: (bi, 0, 0)) if nb > 1 else (lambda bi, i, j: (0, 0, 0))
    return pl.pallas_call(
        functools.partial(_in_proj_kernel, head_dim=tn // NA_HEADS),
        grid=(b, l // tm, n // tn),
        in_specs=[pl.BlockSpec((None, tm, d), lambda bi, i, j: (bi, i, 0)),
                  pl.BlockSpec((1, d), lambda bi, i, j: (0, 0)),
                  pl.BlockSpec((None, 1, d), mod_map),
                  pl.BlockSpec((None, 1, d), mod_map),
                  pl.BlockSpec((d, tn), lambda bi, i, j: (0, j)),
                  pl.BlockSpec((None, 1, tn), lambda bi, i, j: (jnp.clip(j - 1, 0, 1), 0, 0)),
                  pl.BlockSpec((tn, tn), lambda bi, i, j: (0, 0))],
        out_specs=pl.BlockSpec((None, tm, tn), lambda bi, i, j: (bi, i, j)),
        out_shape=jax.ShapeDtypeStruct((b, l, n), BF16),
        scratch_shapes=[pltpu.VMEM((tm, d), BF16)],
        compiler_params=_cparams(("parallel", "parallel", "arbitrary")),
        name="in_proj",
    )(x, g.reshape(1, d), sh, sc, w_bf, head_gain, ones_blk)


def _bmm_kernel(a_ref, w_ref, o_ref):
    o_ref[...] = jnp.dot(a_ref[...].astype(BF16), w_ref[...], preferred_element_type=F32).astype(o_ref.dtype)


def _bmm_add_kernel(a_ref, w_ref, y_ref, o_ref):
    o_ref[...] = (y_ref[...] + jnp.dot(a_ref[...].astype(BF16), w_ref[...],
                                       preferred_element_type=F32)).astype(o_ref.dtype)


def _bmm(a, w, add=None, tm=None, name="bmm"):
    g, m, k = a.shape
    n = w.shape[2]
    tm = m if tm is None else tm
    in_specs = [pl.BlockSpec((None, tm, k), lambda gi, i: (gi, i, 0)),
                pl.BlockSpec((None, k, n), lambda gi, i: (gi, 0, 0))]
    args = [a, w]
    kern = _bmm_kernel
    if add is not None:
        in_specs.append(pl.BlockSpec((None, tm, n), lambda gi, i: (gi, i, 0)))
        args.append(add)
        kern = _bmm_add_kernel
    return pl.pallas_call(
        kern,
        grid=(g, m // tm),
        in_specs=in_specs,
        out_specs=pl.BlockSpec((None, tm, n), lambda gi, i: (gi, i, 0)),
        out_shape=jax.ShapeDtypeStruct((g, m, n), F32),
        compiler_params=_cparams(("parallel", "parallel")),
        name=name,
    )(*args)


def _gelu_tanh(x):
    return 0.5 * x * (1.0 + jnp.tanh(math.sqrt(2.0 / math.pi) * (x + 0.044715 * (x * x * x))))


def _s5_out_kernel(y_ref, u_ref, d_ref, w_ref, o_ref):
    y = y_ref[...] + d_ref[...] * u_ref[...].astype(F32)
    z = _gelu_tanh(y)
    gate = jax.nn.sigmoid(jnp.dot(z.astype(BF16), w_ref[...], preferred_element_type=F32))
    o_ref[...] = (z * gate).astype(o_ref.dtype)


def _s5_out(y, u, d, w_glu_bf, tm):
    m, w = y.shape
    return pl.pallas_call(
        _s5_out_kernel,
        grid=(m // tm,),
        in_specs=[pl.BlockSpec((tm, w), lambda i: (i, 0)),
                  pl.BlockSpec((tm, w), lambda i: (i, 0)),
                  pl.BlockSpec((1, w), lambda i: (0, 0)),
                  pl.BlockSpec((w, w), lambda i: (0, 0))],
        out_specs=pl.BlockSpec((tm, w), lambda i: (i, 0)),
        out_shape=jax.ShapeDtypeStruct((m, w), BF16),
        compiler_params=_cparams(("parallel",)),
        name="s5_out",
    )(y, u, d.reshape(1, w), w_glu_bf)


def _s5_tables(p):
    t = S5_CHUNK
    mats, sums, reads, steps = [], [], [], []
    for d in range(2):
        lam_re = p['ssm_lam_re'][d].astype(F32)
        lam_im = p['ssm_lam_im'][d].astype(F32)
        step = jnp.exp(p['ssm_log_step'][d].astype(F32))[:, None]
        mag = jnp.exp(lam_re * step)
        a_re = mag * jnp.cos(lam_im * step)
        a_im = mag * jnp.sin(lam_im * step)
        den = lam_re * lam_re + lam_im * lam_im
        num_re = a_re - 1.0
        coef_re = (num_re * lam_re + a_im * lam_im) / den
        coef_im = (a_im * lam_re - num_re * lam_im) / den
        b_re = p['ssm_b_re'][d].astype(F32)
        b_im = p['ssm_b_im'][d].astype(F32)
        bb_re = coef_re[..., None] * b_re - coef_im[..., None] * b_im
        bb_im = coef_re[..., None] * b_im + coef_im[..., None] * b_re
        c_re = p['ssm_c_re'][d].astype(F32)
        c_im = p['ssm_c_im'][d].astype(F32)
        n = jnp.arange(t + 1, dtype=F32)[:, None, None]
        pm = jnp.exp(lam_re * step * n)
        pw_re = pm * jnp.cos(lam_im * step * n)
        pw_im = pm * jnp.sin(lam_im * step * n)
        ca_re = c_re[None] * pw_re[:, :, None, :] - c_im[None] * pw_im[:, :, None, :]
        ca_im = c_re[None] * pw_im[:, :, None, :] + c_im[None] * pw_re[:, :, None, :]
        ktau = (jnp.einsum('ngop,gpi->ngoi', ca_re[:t], bb_re)
                - jnp.einsum('ngop,gpi->ngoi', ca_im[:t], bb_im))
        s_idx = np.arange(t)[:, None]
        t_idx = np.arange(t)[None, :]
        lag = (t_idx - s_idx) if d == 0 else (s_idx - t_idx)
        valid = jnp.asarray(lag >= 0, F32)
        blk = ktau[np.clip(lag, 0, t - 1)] * valid[:, :, None, None, None]
        mats.append(jnp.transpose(blk, (2, 0, 4, 1, 3)))
        pows = (t - 1 - np.arange(t)) if d == 0 else np.arange(t)
        sw_re = pw_re[pows][:, :, :, None] * bb_re[None] - pw_im[pows][:, :, :, None] * bb_im[None]
        sw_im = pw_re[pows][:, :, :, None] * bb_im[None] + pw_im[pows][:, :, :, None] * bb_re[None]
        sums.append((jnp.transpose(sw_re, (1, 0, 3, 2)), jnp.transpose(sw_im, (1, 0, 3, 2))))
        rp = (np.arange(t) + 1) if d == 0 else (t - np.arange(t))
        reads.append((jnp.transpose(ca_re[rp], (1, 3, 0, 2)), -jnp.transpose(ca_im[rp], (1, 3, 0, 2))))
        steps.append((pw_re[t], pw_im[t]))
    g, pdim = steps[0][0].shape
    k = SSM_GROUP
    mtot = (mats[0] + mats[1]).reshape(g, t * k, t * k)
    ssum = jnp.concatenate([sums[0][0], sums[0][1], sums[1][0], sums[1][1]], axis=-1).reshape(g, t * k, 4 * pdim)
    r = jnp.concatenate([reads[0][0], reads[0][1], reads[1][0], reads[1][1]], axis=1).reshape(g, 4 * pdim, t * k)
    tk = t * k
    eye = jnp.eye(2, dtype=F32)
    mt = mtot.reshape(g // 2, 2, tk, tk)
    ymat = jnp.einsum('gjts,jk->gjtks', mt, eye).reshape(g // 2, 2 * tk, 2 * tk)
    smat = jnp.einsum('gjtqp,jk->gjtqkp', ssum.reshape(g // 2, 2, tk, 4, pdim), eye).reshape(g // 2, 2 * tk, 8 * pdim)
    w1 = jnp.concatenate([ymat, smat], axis=-1)
    r2 = jnp.einsum('gjqpt,jk->gqjpkt', r.reshape(g // 2, 2, 4, pdim, tk), eye).reshape(g // 2, 8 * pdim, 2 * tk)
    coef = [c.reshape(1, g * pdim) for c in (steps[0][0], steps[0][1], steps[1][0], steps[1][1])]
    return w1.astype(BF16), r2.astype(BF16), coef


def _s5_chunk_kernel(u_ref, w_ref, y_ref, xfr_ref, xfi_ref, xbr_ref, xbi_ref):
    r = jnp.dot(u_ref[...], w_ref[...], preferred_element_type=F32)
    ny = y_ref.shape[-1]
    nx = xfr_ref.shape[-1]
    y_ref[...] = r[:, :ny]
    for i, ref in enumerate((xfr_ref, xfi_ref, xbr_ref, xbi_ref)):
        ref[...] = r[:, ny + i * nx:ny + (i + 1) * nx]


def _s5_chunk(ug, w1):
    gp, m, tk2 = ug.shape
    nx = (w1.shape[2] - tk2) // 4
    xspec = pl.BlockSpec((m, nx), lambda g: (0, g))
    xshape = jax.ShapeDtypeStruct((m, gp * nx), F32)
    return pl.pallas_call(
        _s5_chunk_kernel,
        grid=(gp,),
        in_specs=[pl.BlockSpec((None, m, tk2), lambda g: (g, 0, 0)),
                  pl.BlockSpec((None, tk2, w1.shape[2]), lambda g: (g, 0, 0))],
        out_specs=[pl.BlockSpec((None, m, tk2), lambda g: (g, 0, 0)), xspec, xspec, xspec, xspec],
        out_shape=[jax.ShapeDtypeStruct((gp, m, tk2), F32), xshape, xshape, xshape, xshape],
        compiler_params=_cparams(("parallel",)),
        name="s5_chunk",
    )(ug, w1)


def _s5_scan_kernel(xfr, xfi, xbr, xbi, afr, afi, abr, abi, hfr, hfi, hbr, hbi, *, tiles_ctx, tiles, bsz):
    shape = (2 * bsz, xfr.shape[1])
    low = lax.broadcasted_iota(jnp.int32, shape, 0) < bsz
    zero = jnp.zeros(shape, F32)

    def step(hr, hi, ar, ai, xr, xi):
        return ar * hr - ai * hi + xr, ar * hi + ai * hr + xi

    def fwd_tile(i, carry):
        ar, ai = jnp.broadcast_to(afr[...], shape), jnp.broadcast_to(afi[...], shape)
        r0 = pl.multiple_of(i * 2 * bsz, 2 * bsz)
        xr, xi = xfr[pl.ds(r0, 2 * bsz), :], xfi[pl.ds(r0, 2 * bsz), :]
        sr, si = pltpu.roll(carry[0], bsz, 0), pltpu.roll(carry[1], bsz, 0)
        tr, ti = step(sr, si, ar, ai, xr, xi)
        t2r, t2i = pltpu.roll(tr, bsz, 0), pltpu.roll(ti, bsz, 0)
        hfr[pl.ds(r0, 2 * bsz), :] = jnp.where(low, sr, t2r)
        hfi[pl.ds(r0, 2 * bsz), :] = jnp.where(low, si, t2i)
        return step(t2r, t2i, ar, ai, xr, xi)

    lax.fori_loop(0, tiles, fwd_tile, (zero, zero))

    def bwd_tile(i, carry):
        ar, ai = jnp.broadcast_to(abr[...], shape), jnp.broadcast_to(abi[...], shape)
        r0 = pl.multiple_of(i * 2 * bsz, 2 * bsz)
        xr, xi = xbr[pl.ds(r0, 2 * bsz), :], xbi[pl.ds(r0, 2 * bsz), :]
        sr, si = pltpu.roll(carry[0], bsz, 0), pltpu.roll(carry[1], bsz, 0)
        tr, ti = step(sr, si, ar, ai, xr, xi)
        t2r, t2i = pltpu.roll(tr, bsz, 0), pltpu.roll(ti, bsz, 0)
        hbr[pl.ds(r0, 2 * bsz), :] = jnp.where(low, t2r, sr)
        hbi[pl.ds(r0, 2 * bsz), :] = jnp.where(low, t2i, si)
        return step(t2r, t2i, ar, ai, xr, xi)

    carry = lax.fori_loop(0, tiles_ctx, lambda j, c: bwd_tile(tiles_ctx - 1 - j, c), (zero, zero))
    lax.fori_loop(0, tiles - tiles_ctx, lambda j, c: bwd_tile(tiles - 1 - j, c), carry)


def _s5_scan(xs, coef, chunks_ctx, chunks, bsz):
    m, lanes = xs[0].shape
    assert 2 * bsz == 8 and chunks % 2 == 0 and chunks_ctx % 2 == 0
    lb = min(256, lanes)
    xspec = pl.BlockSpec((m, lb), lambda j: (0, j))
    cspec = pl.BlockSpec((1, lb), lambda j: (0, j))
    shape = jax.ShapeDtypeStruct((m, lanes), F32)
    return pl.pallas_call(
        functools.partial(_s5_scan_kernel, tiles_ctx=chunks_ctx // 2, tiles=chunks // 2, bsz=bsz),
        grid=(lanes // lb,),
        in_specs=[xspec] * 4 + [cspec] * 4,
        out_specs=[xspec] * 4,
        out_shape=[shape] * 4,
        compiler_params=_cparams(("parallel",)),
        name="s5_scan",
    )(*xs, *coef)


def _s5_readout_kernel(y_ref, hfr, hfi, hbr, hbi, r_ref, o_ref):
    h = jnp.concatenate([hfr[...], hfi[...], hbr[...], hbi[...]], axis=1).astype(BF16)
    o_ref[...] = y_ref[...] + jnp.dot(h, r_ref[...], preferred_element_type=F32)


def _s5_readout(y0, hs, r2):
    gp, m, tk2 = y0.shape
    nx = hs[0].shape[1] // gp
    hspec = pl.BlockSpec((m, nx), lambda g: (0, g))
    yspec = pl.BlockSpec((None, m, tk2), lambda g: (g, 0, 0))
    return pl.pallas_call(
        _s5_readout_kernel,
        grid=(gp,),
        in_specs=[yspec, hspec, hspec, hspec, hspec, pl.BlockSpec((None, 4 * nx, tk2), lambda g: (g, 0, 0))],
        out_specs=yspec,
        out_shape=jax.ShapeDtypeStruct((gp, m, tk2), F32),
        compiler_params=_cparams(("parallel",)),
        name="s5_readout",
    )(y0, *hs, r2)


def _s5_branch(u_ctx, u_lat, p, tables, need_ctx):
    w1, r2, coef = tables
    b, l, w = u_lat.shape
    lc = u_ctx.shape[1]
    t, k = S5_CHUNK, SSM_GROUP
    g = w // k
    nc = (lc + l) // t
    m = nc * b
    u_all = jnp.concatenate([u_ctx, u_lat], axis=1)
    ug = jnp.transpose(u_all.reshape(b, nc, t, g // 2, 2, k), (3, 1, 0, 4, 2, 5)).reshape(g // 2, m, 2 * t * k)
    y0, *xs = _s5_chunk(ug, w1)
    hs = _s5_scan(xs, coef, lc // t, nc, b)
    y = _s5_readout(y0, hs, r2).reshape(g // 2, nc, b, 2, t, k)
    y = jnp.transpose(y, (2, 1, 4, 0, 3, 5)).reshape(b, nc * t, w)
    w_glu = p['ssm_w_glu'].astype(BF16)
    out_l = _s5_out(y[:, lc:].reshape(b * l, w), u_lat.reshape(b * l, w), p['ssm_d'], w_glu, tm=512)
    out_c = None
    if need_ctx:
        out_c = _s5_out(y[:, :lc].reshape(b * lc, w), u_ctx.reshape(b * lc, w), p['ssm_d'], w_glu,
                        tm=min(512, b * lc)).reshape(b, lc, w)
    return out_l.reshape(b, l, w), out_c


def _na_kernel(q_ref, k_ref, v_ref, kc_ref, vc_ref, bias_ref, o_ref, *, rows, head_dim):
    r = pl.program_id(1)
    kr = NA_WIN_ROWS
    start = jnp.clip(r - kr // 2, 0, rows - kr)
    cls = r - start
    base = pl.multiple_of(start * GRID_W, GRID_W)
    kwin = k_ref[pl.ds(base, kr * GRID_W), :]
    vwin = v_ref[pl.ds(base, kr * GRID_W), :]
    q = q_ref[...]
    kc = kc_ref[...]
    vc = vc_ref[...]
    outs = []
    nt = (((1,), (1,)), ((), ()))
    for h in range(NA_HEADS):
        sl = slice(h * head_dim, (h + 1) * head_dim)
        qh = q[:, sl]
        s_win = lax.dot_general(qh, kwin[:, sl], nt, preferred_element_type=F32) + bias_ref[cls, h]
        s_ctx = lax.dot_general(qh, kc[:, sl], nt, preferred_element_type=F32)
        mx = jnp.maximum(jnp.max(s_win, axis=-1, keepdims=True), jnp.max(s_ctx, axis=-1, keepdims=True))
        p_win = jnp.exp(s_win - mx)
        p_ctx = jnp.exp(s_ctx - mx)
        den = jnp.sum(p_win, axis=-1, keepdims=True) + jnp.sum(p_ctx, axis=-1, keepdims=True)
        o = (jnp.dot(p_win.astype(BF16), vwin[:, sl], preferred_element_type=F32)
             + jnp.dot(p_ctx.astype(BF16), vc[:, sl], preferred_element_type=F32))
        outs.append(o / den)
    o_ref[...] = jnp.concatenate(outs, axis=-1).astype(o_ref.dtype)


def _na_bias_table(rpb):
    kr, kw, w = NA_WIN_ROWS, NA_WIN_COLS, GRID_W
    col = np.arange(w)
    cstart = np.clip(col - kw // 2, 0, w - kw)
    kcol = np.arange(w)
    inwin = (kcol[None, :] >= cstart[:, None]) & (kcol[None, :] < cstart[:, None] + kw)
    dc = np.clip(kcol[None, :] - col[:, None] + (kw - 1), 0, 2 * kw - 2)
    rpb = rpb.astype(F32)
    tabs = []
    for o in range(kr):
        dr = np.arange(kr) - o + (kr - 1)
        bias = rpb[:, dr][:, :, dc]
        bias = jnp.where(jnp.asarray(inwin)[None, None], bias, MASK_VALUE)
        tabs.append(jnp.transpose(bias, (0, 2, 1, 3)).reshape(rpb.shape[0], w, kr * w))
    return jnp.stack(tabs, axis=0)


def _na_latent(q, k, v, kc, vc, bias_tab):
    b, l, w = q.shape
    lc = kc.shape[1]
    rows = l // GRID_W
    assert rows >= NA_WIN_ROWS
    nk = NA_WIN_ROWS * GRID_W
    return pl.pallas_call(
        functools.partial(_na_kernel, rows=rows, head_dim=w // NA_HEADS),
        grid=(b, rows),
        in_specs=[pl.BlockSpec((None, GRID_W, w), lambda bi, r: (bi, r, 0)),
                  pl.BlockSpec((None, l, w), lambda bi, r: (bi, 0, 0)),
                  pl.BlockSpec((None, l, w), lambda bi, r: (bi, 0, 0)),
                  pl.BlockSpec((None, lc, w), lambda bi, r: (bi, 0, 0)),
                  pl.BlockSpec((None, lc, w), lambda bi, r: (bi, 0, 0)),
                  pl.BlockSpec((NA_WIN_ROWS, NA_HEADS, GRID_W, nk), lambda bi, r: (0, 0, 0, 0))],
        out_specs=pl.BlockSpec((None, GRID_W, w), lambda bi, r: (bi, r, 0)),
        out_shape=jax.ShapeDtypeStruct((b, l, w), BF16),
        compiler_params=_cparams(("parallel", "arbitrary"), vmem_mb=56),
        name="na_latent",
    )(q, k, v, kc, vc, bias_tab)


def _na_ctx_kernel(q_ref, k_ref, v_ref, o_ref, *, head_dim):
    q, k, v = q_ref[...], k_ref[...], v_ref[...]
    outs = []
    for h in range(NA_HEADS):
        sl = slice(h * head_dim, (h + 1) * head_dim)
        s = lax.dot_general(q[:, sl], k[:, sl], (((1,), (1,)), ((), ())), preferred_element_type=F32)
        pr = jnp.exp(s - jnp.max(s, axis=-1, keepdims=True))
        den = jnp.sum(pr, axis=-1, keepdims=True)
        outs.append(jnp.dot(pr.astype(BF16), v[:, sl], preferred_element_type=F32) / den)
    o_ref[...] = jnp.concatenate(outs, axis=-1).astype(o_ref.dtype)


def _na_context(q, k, v):
    b, lc, w = q.shape
    spec = pl.BlockSpec((None, lc, w), lambda bi: (bi, 0, 0))
    return pl.pallas_call(
        functools.partial(_na_ctx_kernel, head_dim=w // NA_HEADS),
        grid=(b,),
        in_specs=[spec, spec, spec],
        out_specs=spec,
        out_shape=jax.ShapeDtypeStruct((b, lc, w), BF16),
        compiler_params=_cparams(("parallel",)),
        name="na_context",
    )(q, k, v)


def _hyena_filter(length, p):
    t = jnp.linspace(0.0, 1.0, length, dtype=F32)[:, None]
    freqs = jnp.linspace(1e-4, HY_BANDS - 1, HY_BANDS, dtype=F32)
    ang = (2.0 * math.pi / length) * jnp.arange(length, dtype=F32)[:, None] * freqs[None, :]
    z = jnp.concatenate([t, jnp.cos(ang), -jnp.sin(ang)], axis=-1)
    fr = p['hy_freq'].astype(F32)
    hp = lax.Precision.HIGHEST
    h = jnp.sin(fr * (jnp.dot(z, p['hy_w1'].astype(F32), precision=hp) + p['hy_b1'].astype(F32)))
    h = jnp.sin(fr * (jnp.dot(h, p['hy_w2'].astype(F32), precision=hp) + p['hy_b2'].astype(F32)))
    h = jnp.sin(fr * (jnp.dot(h, p['hy_w3'].astype(F32), precision=hp) + p['hy_b3'].astype(F32)))
    wdim = p['hy_bias'].shape[-1]
    h = jnp.dot(h, p['hy_w4'].astype(F32), precision=hp).reshape(length, HY_ORDER, 2, wdim)
    decay = jnp.exp(-t * jnp.abs(jnp.linspace(HY_MIN_DECAY, HY_MAX_DECAY, wdim, dtype=F32)))
    h = h * decay[:, None, None, :]
    filt = jnp.concatenate([h[:, :, 0], jnp.zeros((1, HY_ORDER, wdim), F32), h[:0:-1, :, 1]], axis=0)
    filt = filt / jnp.sum(jnp.abs(filt), axis=0, keepdims=True)
    return filt.reshape(2 * length, HY_ORDER * wdim)


def _block2(re, im):
    return np.block([[re, -im], [im, re]])


def _fft_consts(length):
    n = 2 * length
    n2 = HY_FAST
    n1 = n // n2
    h1 = n1 // 2
    a1 = -2.0 * np.pi * np.outer(np.arange(n1), np.arange(n1)) / n1
    a2 = -2.0 * np.pi * np.outer(np.arange(n2), np.arange(n2)) / n2
    f1 = _block2(np.cos(a1[:, :h1]), np.sin(a1[:, :h1]))
    f1_real = np.concatenate([np.cos(a1), np.sin(a1)], axis=0)
    f3 = _block2(np.cos(a2), np.sin(a2))
    g3 = _block2(np.cos(a2), -np.sin(a2))
    g1 = _block2(np.cos(a1[:h1, :]) / n, -np.sin(a1[:h1, :]) / n)
    wa = -2.0 * np.pi * np.arange(n2) / n
    wr = np.broadcast_to(np.cos(wa)[:, None], (n2, 128))
    wi = np.broadcast_to(np.sin(wa)[:, None], (n2, 128))
    cast = lambda m: jnp.asarray(m, BF16)
    return dict(n1=n1, h1=h1, f1=cast(f1), f1_real=cast(f1_real), f3=cast(f3), g3=cast(g3), g1=cast(g1),
                wr=jnp.asarray(wr, F32), wi=jnp.asarray(wi, F32))


def _fft_first_stage(a_r, a_i, load_cols, f1_ref, n1, cb):
    slab = HY_FAST + HY_PAD

    def body(i, carry):
        n2 = 2 * i
        z = jnp.concatenate([load_cols(n2), load_cols(n2 + 1)], axis=1)
        r = jnp.dot(f1_ref[...], z, preferred_element_type=F32)
        a_r[pl.ds(n2, n1, stride=slab), :] = r[:n1, :cb]
        a_i[pl.ds(n2, n1, stride=slab), :] = r[n1:, :cb]
        a_r[pl.ds(n2 + 1, n1, stride=slab), :] = r[:n1, cb:]
        a_i[pl.ds(n2 + 1, n1, stride=slab), :] = r[n1:, cb:]
        return carry

    lax.fori_loop(0, HY_FAST // 2, body, 0)


def _fft_slab_loop(a_r, a_i, f3_ref, wr_ref, wi_ref, n1, cb, finish):
    slab = HY_FAST + HY_PAD
    n2 = HY_FAST
    wr, wi = wr_ref[...], wi_ref[...]
    if cb != wr.shape[1]:
        wr, wi = jnp.tile(wr, (1, cb // wr.shape[1])), jnp.tile(wi, (1, cb // wi.shape[1]))

    def body(i, carry):
        tr, ti = carry
        k1 = 2 * i
        r0 = pl.multiple_of(k1 * slab, 8)
        r1 = pl.multiple_of((k1 + 1) * slab, 8)
        tr1, ti1 = tr * wr - ti * wi, tr * wi + ti * wr
        ar0, ai0 = a_r[pl.ds(r0, n2), :], a_i[pl.ds(r0, n2), :]
        ar1, ai1 = a_r[pl.ds(r1, n2), :], a_i[pl.ds(r1, n2), :]
        xr = jnp.concatenate([ar0 * tr - ai0 * ti, ar1 * tr1 - ai1 * ti1], axis=1)
        xi = jnp.concatenate([ar0 * ti + ai0 * tr, ar1 * ti1 + ai1 * tr1], axis=1)
        bc = jnp.dot(f3_ref[...], jnp.concatenate([xr, xi], axis=0).astype(BF16), preferred_element_type=F32)
        finish(k1, (r0, r1), (tr, ti, tr1, ti1), bc[:n2], bc[n2:])
        return tr1 * wr - ti1 * wi, tr1 * wi + ti1 * wr

    one = jnp.ones((n2, cb), F32)
    lax.fori_loop(0, n1 // 2, body, (one, jnp.zeros_like(one)))


def _fft_conv_kernel(y_ref, sr_ref, si_ref, f1_ref, f3_ref, g3_ref, g1_ref, wr_ref, wi_ref, o_ref, a_r, a_i,
                     *, n1):
    h1 = n1 // 2
    n2 = HY_FAST
    cb = o_ref.shape[-1]
    slab = HY_FAST + HY_PAD

    def load_cols(j):
        r = pl.multiple_of(j * h1, h1)
        return jnp.concatenate([y_ref[0, pl.ds(r, h1), :], y_ref[1, pl.ds(r, h1), :]], axis=0)

    _fft_first_stage(a_r, a_i, load_cols, f1_ref, n1, cb)

    def finish(k1, rows, tw, br, bi):
        s0 = pl.multiple_of(k1 * n2, n2)
        s1 = pl.multiple_of((k1 + 1) * n2, n2)
        sr = jnp.concatenate([sr_ref[pl.ds(s0, n2), :], sr_ref[pl.ds(s1, n2), :]], axis=1).astype(F32)
        si = jnp.concatenate([si_ref[pl.ds(s0, n2), :], si_ref[pl.ds(s1, n2), :]], axis=1).astype(F32)
        y = jnp.concatenate([br * sr - bi * si, br * si + bi * sr], axis=0).astype(BF16)
        ac = jnp.dot(g3_ref[...], y, preferred_element_type=F32)
        pr, pi = ac[:n2], ac[n2:]
        for j in range(2):
            tr, ti = tw[2 * j], tw[2 * j + 1]
            prj, pij = pr[:, j * cb:(j + 1) * cb], pi[:, j * cb:(j + 1) * cb]
            a_r[pl.ds(rows[j], n2), :] = prj * tr + pij * ti
            a_i[pl.ds(rows[j], n2), :] = pij * tr - prj * ti

    _fft_slab_loop(a_r, a_i, f3_ref, wr_ref, wi_ref, n1, cb, finish)

    def out_body(i, carry):
        j0 = 2 * i
        cols = []
        for j in (j0, j0 + 1):
            cols.append(jnp.concatenate([a_r[pl.ds(j, n1, stride=slab), :], a_i[pl.ds(j, n1, stride=slab), :]],
                                        axis=0))
        o = jnp.dot(g1_ref[...], jnp.concatenate(cols, axis=1).astype(BF16), preferred_element_type=F32)
        for jj in range(2):
            r = pl.multiple_of((j0 + jj) * h1, h1)
            o_ref[0, pl.ds(r, h1), :] = o[:h1, jj * cb:(jj + 1) * cb].astype(o_ref.dtype)
            o_ref[1, pl.ds(r, h1), :] = o[h1:, jj * cb:(jj + 1) * cb].astype(o_ref.dtype)
        return carry

    lax.fori_loop(0, n2 // 2, out_body, 0)


def _fft_conv(yp, col0, spec_r, spec_i, scol0, consts, cb=128):
    b, l, _ = yp.shape
    n = spec_r.shape[0]
    n1 = consts['n1']
    slab = HY_FAST + HY_PAD
    nblk = consts['nblk']
    y4 = yp.reshape(b // 2, 2, l, yp.shape[-1])
    const = lambda a: pl.BlockSpec(a.shape, lambda j, pi: (0, 0))
    return pl.pallas_call(
        functools.partial(_fft_conv_kernel, n1=n1),
        grid=(nblk, b // 2),
        in_specs=[pl.BlockSpec((None, 2, l, cb), lambda j, pi: (pi, 0, 0, col0 + j)),
                  pl.BlockSpec((n, cb), lambda j, pi: (0, scol0 + j)),
                  pl.BlockSpec((n, cb), lambda j, pi: (0, scol0 + j)),
                  const(consts['f1']), const(consts['f3']), const(consts['g3']), const(consts['g1']),
                  const(consts['wr']), const(consts['wi'])],
        out_specs=pl.BlockSpec((None, 2, l, cb), lambda j, pi: (pi, 0, 0, j)),
        out_shape=jax.ShapeDtypeStruct((b // 2, 2, l, nblk * cb), BF16),
        scratch_shapes=[pltpu.VMEM((n1 * slab, cb), F32), pltpu.VMEM((n1 * slab, cb), F32)],
        compiler_params=_cparams(("arbitrary", "arbitrary"), vmem_mb=58),
        name="fft_conv",
    )(y4, spec_r, spec_i, consts['f1'], consts['f3'], consts['g3'], consts['g1'],
      consts['wr'], consts['wi']).reshape(b, l, nblk * cb)


def _fft_spec_kernel(f_ref, f1_ref, f3_ref, wr_ref, wi_ref, sr_ref, si_ref, a_r, a_i, *, n1):
    n2 = HY_FAST
    cb = sr_ref.shape[-1]

    def load_cols(j):
        return f_ref[pl.ds(pl.multiple_of(j * n1, n1), n1), :]

    _fft_first_stage(a_r, a_i, load_cols, f1_ref, n1, cb)

    def finish(k1, rows, tw, br, bi):
        for j in range(2):
            s = pl.multiple_of((k1 + j) * n2, n2)
            sr_ref[pl.ds(s, n2), :] = br[:, j * cb:(j + 1) * cb].astype(sr_ref.dtype)
            si_ref[pl.ds(s, n2), :] = bi[:, j * cb:(j + 1) * cb].astype(si_ref.dtype)

    _fft_slab_loop(a_r, a_i, f3_ref, wr_ref, wi_ref, n1, cb, finish)


def _fft_spectrum(filt_p, consts, cb=128):
    n, c = filt_p.shape
    n1 = consts['n1']
    slab = HY_FAST + HY_PAD
    const = lambda a: pl.BlockSpec(a.shape, lambda j: (0, 0))
    spec = pl.BlockSpec((n, cb), lambda j: (0, j))
    return pl.pallas_call(
        functools.partial(_fft_spec_kernel, n1=n1),
        grid=(c // cb,),
        in_specs=[spec, const(consts['f1_real']), const(consts['f3']), const(consts['wr']), const(consts['wi'])],
        out_specs=[spec, spec],
        out_shape=[jax.ShapeDtypeStruct((n, c), BF16)] * 2,
        scratch_shapes=[pltpu.VMEM((n1 * slab, cb), F32), pltpu.VMEM((n1 * slab, cb), F32)],
        compiler_params=_cparams(("arbitrary",), vmem_mb=58),
        name="fft_spectrum",
    )(filt_p, consts['f1_real'], consts['f3'], consts['wr'], consts['wi'])


def _dense_conv_kernel(y_ref, sr_ref, si_ref, f_ref, g_ref, o_ref):
    l = y_ref.shape[1]
    n = sr_ref.shape[0]
    bc = jnp.dot(f_ref[...], jnp.concatenate([y_ref[0], y_ref[1]], axis=0), preferred_element_type=F32)
    br, bi = bc[:n], bc[n:]
    sr, si = sr_ref[...], si_ref[...]
    y = jnp.concatenate([br * sr - bi * si, br * si + bi * sr], axis=0).astype(BF16)
    o = jnp.dot(g_ref[...], y, preferred_element_type=F32)
    o_ref[0] = o[:l].astype(o_ref.dtype)
    o_ref[1] = o[l:].astype(o_ref.dtype)


def _dense_conv(y, col0, spec_r, spec_i, scol0, fmat, gmat, nblk, cb=128):
    b, l, _ = y.shape
    n = spec_r.shape[0]
    y4 = y.reshape(b // 2, 2, l, y.shape[-1])
    const = lambda a: pl.BlockSpec(a.shape, lambda j, pi: (0, 0))
    return pl.pallas_call(
        _dense_conv_kernel,
        grid=(nblk, b // 2),
        in_specs=[pl.BlockSpec((None, 2, l, cb), lambda j, pi: (pi, 0, 0, col0 + j)),
                  pl.BlockSpec((n, cb), lambda j, pi: (0, scol0 + j)),
                  pl.BlockSpec((n, cb), lambda j, pi: (0, scol0 + j)),
                  const(fmat), const(gmat)],
        out_specs=pl.BlockSpec((None, 2, l, cb), lambda j, pi: (pi, 0, 0, j)),
        out_shape=jax.ShapeDtypeStruct((b // 2, 2, l, nblk * cb), BF16),
        compiler_params=_cparams(("parallel", "parallel")),
        name="dense_conv",
    )(y4, spec_r, spec_i, fmat, gmat).reshape(b, l, nblk * cb)


def _sconv_kernel(u_ref, w_ref, b_ref, o_ref):
    u = u_ref[...].astype(F32)
    l = u.shape[0]
    row = lax.broadcasted_iota(jnp.int32, u.shape, 0)
    prev = jnp.where(row == 0, 0.0, pltpu.roll(u, 1, 0))
    nxt = jnp.where(row == l - 1, 0.0, pltpu.roll(u, l - 1, 0))
    o_ref[...] = (prev * w_ref[0:1, :] + u * w_ref[1:2, :] + nxt * w_ref[2:3, :] + b_ref[...]).astype(o_ref.dtype)


def _short_conv(proj, col0, conv_w, conv_b, cb=128):
    b, l, _ = proj.shape
    c = conv_w.shape[1]
    return pl.pallas_call(
        _sconv_kernel,
        grid=(b, c // cb),
        in_specs=[pl.BlockSpec((None, l, cb), lambda bi, j: (bi, 0, col0 + j)),
                  pl.BlockSpec((conv_w.shape[0], cb), lambda bi, j: (0, j)),
                  pl.BlockSpec((1, cb), lambda bi, j: (0, j))],
        out_specs=pl.BlockSpec((None, l, cb), lambda bi, j: (bi, 0, j)),
        out_shape=jax.ShapeDtypeStruct((b, l, c), BF16),
        compiler_params=_cparams(("parallel", "parallel")),
        name="short_conv",
    )(proj, conv_w.astype(F32), conv_b.astype(F32).reshape(1, c))


def _gate_kernel(x_ref, conv_ref, y_ref, bias_ref, o_ref):
    y = y_ref[...].astype(F32)
    o_ref[...] = (x_ref[...].astype(F32) * (conv_ref[...].astype(F32) + y * bias_ref[...])).astype(o_ref.dtype)


def _hy_gate(z, xcol, conv, y, ycol, bias, tl):
    b, l, w = conv.shape
    return pl.pallas_call(
        _gate_kernel,
        grid=(b, l // tl),
        in_specs=[pl.BlockSpec((None, tl, w), lambda bi, i: (bi, i, xcol)),
                  pl.BlockSpec((None, tl, w), lambda bi, i: (bi, i, 0)),
                  pl.BlockSpec((None, tl, w), lambda bi, i: (bi, i, ycol)),
                  pl.BlockSpec((1, w), lambda bi, i: (0, 0))],
        out_specs=pl.BlockSpec((None, tl, w), lambda bi, i: (bi, i, 0)),
        out_shape=jax.ShapeDtypeStruct((b, l, w), BF16),
        compiler_params=_cparams(("parallel", "parallel")),
        name="hy_gate",
    )(z, conv, y, bias.astype(F32).reshape(1, w))


def _hyena_branch(proj, col0, p, cb=128):
    b, l, _ = proj.shape
    w = p['hy_bias'].shape[-1]
    n = 2 * l
    nblk = w // cb
    z = _short_conv(proj, col0, p['hy_conv_w'], p['hy_conv_b'], cb)
    filt = _hyena_filter(l, p)
    tl = min(1024, l)
    if n // HY_FAST >= 16:
        consts = dict(_fft_consts(l), nblk=nblk)
        n1, h1 = consts['n1'], consts['h1']
        perm = lambda a, r: jnp.transpose(a.reshape(a.shape[:-2] + (r, HY_FAST, a.shape[-1])),
                                          tuple(range(a.ndim - 2)) + (a.ndim - 1, a.ndim - 2, a.ndim)
                                          ).reshape(a.shape)
        zp = perm(z, h1)
        spec_r, spec_i = _fft_spectrum(perm(filt.astype(BF16), n1), consts, cb)
        conv = lambda y, ycol, order: _fft_conv(y, ycol * nblk, spec_r, spec_i, order * nblk, consts, cb)
        unperm = lambda a: jnp.transpose(a.reshape(b, HY_FAST, h1, w), (0, 2, 1, 3)).reshape(b, l, w)
    else:
        k = np.arange(n)
        ang = -2.0 * np.pi * np.outer(k, k) / n
        fmat = jnp.asarray(_block2(np.cos(ang[:, :l]), np.sin(ang[:, :l])), BF16)
        gmat = jnp.asarray(_block2(np.cos(ang[:l, :]) / n, -np.sin(ang[:l, :]) / n), BF16)
        f_real = jnp.asarray(np.concatenate([np.cos(ang), np.sin(ang)], axis=0), F32)
        spec = _bmm(f_real[None], filt.astype(BF16)[None], name="dense_spectrum")[0]
        spec_r, spec_i = spec[:n], spec[n:]
        zp = z
        conv = lambda y, ycol, order: _dense_conv(y, ycol * nblk, spec_r, spec_i, order * nblk, fmat, gmat,
                                                  nblk, cb)
        unperm = lambda a: a
    c0 = conv(zp, HY_ORDER, 0)
    y1 = _hy_gate(zp, 0, c0, zp, HY_ORDER, p['hy_bias'][0], tl)
    c1 = conv(y1, 0, 1)
    y2 = _hy_gate(zp, 1, c1, y1, 0, p['hy_bias'][1], tl)
    return unperm(y2)


def _merge_kernel(x_ref, g_ref, sh_ref, sc_ref, g1_ref, br_ref, wg0_ref, wg1_ref, wg2_ref, bg_ref, wb_ref, wo_ref,
                  o_ref, h_scr, acc_scr):
    j = pl.program_id(2)

    @pl.when(j == 0)
    def _():
        h_scr[...] = _norm_mod(x_ref[...], g_ref[...], sh_ref[...], sc_ref[...]).astype(BF16)
        acc_scr[...] = jnp.zeros_like(acc_scr)

    h = h_scr[...]
    mixed = None
    for i, wg_ref in enumerate((wg0_ref, wg1_ref, wg2_ref)):
        gate = jax.nn.sigmoid(jnp.dot(h, wg_ref[...], preferred_element_type=F32) + bg_ref[i])
        term = gate * jnp.dot(br_ref[i], wb_ref[i], preferred_element_type=F32)
        mixed = term if mixed is None else mixed + term
    acc_scr[...] += jnp.dot(mixed.astype(BF16), wo_ref[...], preferred_element_type=F32)

    @pl.when(j == pl.num_programs(2) - 1)
    def _():
        o_ref[...] = x_ref[...] + g1_ref[...] * acc_scr[...]


def _merge(x, g, sh, sc, g1, branches, w_gate_bf, b_gate, w_branch_bf, w_out_bf, tm, tc):
    b, l, d = x.shape
    nbr, w, _ = w_branch_bf.shape
    nb = sh.shape[0]
    nj = d // tc
    mod_map = (lambda bi, i, j: (bi, 0, 0)) if nb > 1 else (lambda bi, i, j: (0, 0, 0))
    br = jnp.stack(branches, axis=1)
    bg = b_gate.reshape(nbr, nj, 1, tc).transpose(1, 0, 2, 3)
    return pl.pallas_call(
        _merge_kernel,
        grid=(b, l // tm, nj),
        in_specs=[pl.BlockSpec((None, tm, d), lambda bi, i, j: (bi, i, 0)),
                  pl.BlockSpec((1, d), lambda bi, i, j: (0, 0)),
                  pl.BlockSpec((None, 1, d), mod_map),
                  pl.BlockSpec((None, 1, d), mod_map),
                  pl.BlockSpec((None, 1, d), mod_map),
                  pl.BlockSpec((None, nbr, tm, w), lambda bi, i, j: (bi, 0, i, 0)),
                  pl.BlockSpec((d, tc), lambda bi, i, j: (0, j)),
                  pl.BlockSpec((d, tc), lambda bi, i, j: (0, nj + j)),
                  pl.BlockSpec((d, tc), lambda bi, i, j: (0, 2 * nj + j)),
                  pl.BlockSpec((None, nbr, 1, tc), lambda bi, i, j: (j, 0, 0, 0)),
                  pl.BlockSpec((nbr, w, tc), lambda bi, i, j: (0, 0, j)),
                  pl.BlockSpec((tc, d), lambda bi, i, j: (j, 0))],
        out_specs=pl.BlockSpec((None, tm, d), lambda bi, i, j: (bi, i, 0)),
        out_shape=jax.ShapeDtypeStruct((b, l, d), F32),
        scratch_shapes=[pltpu.VMEM((tm, d), BF16), pltpu.VMEM((tm, d), F32)],
        compiler_params=_cparams(("parallel", "parallel", "arbitrary"), vmem_mb=56),
        name="merge",
    )(x, g.reshape(1, d), sh, sc, g1, br, w_gate_bf, w_gate_bf, w_gate_bf, bg, w_branch_bf, w_out_bf)


def _router_kernel(x_ref, g_ref, sh_ref, sc_ref, rt_ref, h_ref, aff_ref):
    h = _norm_mod(x_ref[...], g_ref[...], sh_ref[...], sc_ref[...])
    h_ref[...] = h.astype(h_ref.dtype)
    nt = (((1,), (1,)), ((), ()))
    rt = rt_ref[...]
    r_hi = rt.astype(BF16)
    r_lo = (rt - r_hi.astype(F32)).astype(BF16)
    h_hi = h.astype(BF16)
    h_lo = (h - h_hi.astype(F32)).astype(BF16)
    logits = (lax.dot_general(r_hi, h_hi, nt, preferred_element_type=F32)
              + lax.dot_general(r_hi, h_lo, nt, preferred_element_type=F32)
              + lax.dot_general(r_lo, h_hi, nt, preferred_element_type=F32))
    e = jnp.exp(logits - jnp.max(logits, axis=0, keepdims=True))
    aff_ref[...] = e / jnp.sum(e, axis=0, keepdims=True)


def _router(x, g, sh, sc, router_t, tm):
    b, l, d = x.shape
    e = router_t.shape[0]
    nb = sh.shape[0]
    mod_map = (lambda bi, i: (bi, 0, 0)) if nb > 1 else (lambda bi, i: (0, 0, 0))
    return pl.pallas_call(
        _router_kernel,
        grid=(b, l // tm),
        in_specs=[pl.BlockSpec((None, tm, d), lambda bi, i: (bi, i, 0)),
                  pl.BlockSpec((1, d), lambda bi, i: (0, 0)),
                  pl.BlockSpec((None, 1, d), mod_map),
                  pl.BlockSpec((None, 1, d), mod_map),
                  pl.BlockSpec((e, d), lambda bi, i: (0, 0))],
        out_specs=[pl.BlockSpec((None, tm, d), lambda bi, i: (bi, i, 0)),
                   pl.BlockSpec((None, e, tm), lambda bi, i: (bi, 0, i))],
        out_shape=[jax.ShapeDtypeStruct((b, l, d), BF16), jax.ShapeDtypeStruct((b, e, l), F32)],
        compiler_params=_cparams(("parallel", "parallel")),
        name="router",
    )(x, g.reshape(1, d), sh, sc, router_t)


def _ffn_kernel(xs_ref, gate_ref, w1_ref, w3_ref, w2_ref, o_ref, acc_scr):
    f = pl.program_id(2)

    @pl.when(f == 0)
    def _():
        acc_scr[...] = jnp.zeros_like(acc_scr)

    xs = xs_ref[...]
    a = jnp.dot(xs, w1_ref[...].astype(BF16), preferred_element_type=F32)
    gl = jnp.dot(xs, w3_ref[...].astype(BF16), preferred_element_type=F32)
    mid = (a * jax.nn.sigmoid(a) * gl).astype(BF16)
    acc_scr[...] += jnp.dot(mid, w2_ref[...].astype(BF16), preferred_element_type=F32)

    @pl.when(f == pl.num_programs(2) - 1)
    def _():
        o_ref[...] = (acc_scr[...] * gate_ref[...]).astype(o_ref.dtype)


def _expert_ffn(xs, gate, w1, w3, w2, layer, tr, tf):
    e, r, d = xs.shape
    ff = w1.shape[3]
    return pl.pallas_call(
        _ffn_kernel,
        grid=(e, r // tr, ff // tf),
        in_specs=[pl.BlockSpec((None, tr, d), lambda ei, i, f: (ei, i, 0)),
                  pl.BlockSpec((None, tr, 1), lambda ei, i, f: (ei, i, 0)),
                  pl.BlockSpec((None, None, d, tf), lambda ei, i, f: (layer, ei, 0, f)),
                  pl.BlockSpec((None, None, d, tf), lambda ei, i, f: (layer, ei, 0, f)),
                  pl.BlockSpec((None, None, tf, d), lambda ei, i, f: (layer, ei, f, 0))],
        out_specs=pl.BlockSpec((None, tr, d), lambda ei, i, f: (ei, i, 0)),
        out_shape=jax.ShapeDtypeStruct((e, r, d), F32),
        scratch_shapes=[pltpu.VMEM((tr, d), F32)],
        compiler_params=_cparams(("parallel", "parallel", "arbitrary"), vmem_mb=56),
        name="expert_ffn",
    )(xs, gate, w1, w3, w2)


def _ec_moe_update(x, g, sh, sc, g2, p, tm):
    b, l, d = x.shape
    e = p['router'].shape[1]
    cap = max(1, EC_CAPACITY * l // e)
    h, aff_t = _router(x, g, sh, sc, jnp.transpose(p['router']).astype(F32), tm)
    gate, idx = lax.top_k(aff_t, cap)
    bidx = jnp.arange(b)[:, None, None]
    xs = jnp.transpose(h[bidx, idx], (1, 0, 2, 3)).reshape(e, b * cap, d)
    gt = jnp.transpose(gate, (1, 0, 2)).reshape(e, b * cap, 1)
    tr = min(1024, b * cap)
    ff = p['exp_w1'].shape[3]
    y = _expert_ffn(xs, gt, p['exp_w1'], p['exp_w3'], p['exp_w2'], p['layer'], tr=tr, tf=min(256, ff))
    y = jnp.transpose(y.reshape(e, b, cap, d), (1, 0, 2, 3))
    moe = jnp.zeros((b, l, d), F32).at[bidx, idx].add(y)
    return x + g2 * moe


def _layer(x, xc, c, c_ctx, p, last):
    b, l, d = x.shape
    lc = xc.shape[1]
    w = p['ssm_d'].shape[0]
    hd = w // NA_HEADS

    cc = jnp.zeros((8, d), F32).at[:b].set(c).at[b].set(c_ctx)
    mod = _ada(cc, p['w_ada'], p['b_ada'], p['layer'])
    m_l = [t[:, None, :] for t in jnp.split(mod[:b], 6, axis=-1)]
    m_c = [t[:, None, :] for t in jnp.split(mod[b:b + 1], 6, axis=-1)]

    w_in = p['w_in'].astype(BF16)
    scale = hd ** -0.5
    head_gain = jnp.stack([jnp.tile(p['na_q_gain'].astype(F32) * scale, NA_HEADS),
                           jnp.tile(p['na_k_gain'].astype(F32), NA_HEADS)])[:, None, :]
    ones_blk = jnp.asarray(np.kron(np.eye(NA_HEADS), np.ones((hd, hd))), BF16)
    tm_l = min(512, l)
    tm_c = min(512, lc)
    proj = _in_proj(x, p['norm1'], m_l[0], m_l[1], w_in, head_gain, ones_blk, tm_l)
    projc = _in_proj(xc, p['norm1'], m_c[0], m_c[1], w_in, head_gain, ones_blk, tm_c)

    ssm_l, ssm_c = _s5_branch(projc[..., :w], proj[..., :w], p, _s5_tables(p), need_ctx=not last)

    na_l = _na_latent(proj[..., w:2 * w], proj[..., 2 * w:3 * w], proj[..., 3 * w:4 * w],
                      projc[..., 2 * w:3 * w], projc[..., 3 * w:4 * w], _na_bias_table(p['na_rpb']))
    hy_l = _hyena_branch(proj, 4 * w // 128, p)

    wg = p['w_gate'].astype(BF16)
    wb = p['w_branch'].astype(BF16)
    wo = p['w_out'].astype(BF16)
    x = _merge(x, p['norm1'], m_l[0], m_l[1], m_l[2], (ssm_l, na_l, hy_l), wg, p['b_gate'], wb, wo,
               tm=tm_l, tc=min(512, d))
    x = _ec_moe_update(x, p['norm2'], m_l[3], m_l[4], m_l[5], p, tm_l)
    if last:
        return x, None
    na_c = _na_context(projc[..., w:2 * w], projc[..., 2 * w:3 * w], projc[..., 3 * w:4 * w])
    hy_c = _hyena_branch(projc, 4 * w // 128, p)
    xc = _merge(xc, p['norm1'], m_c[0], m_c[1], m_c[2], (ssm_c, na_c, hy_c), wg, p['b_gate'], wb, wo,
                tm=tm_c, tc=min(512, d))
    xc = _ec_moe_update(xc, p['norm2'], m_c[3], m_c[4], m_c[5], p, tm_c)
    return x, xc


_PARAM_NAMES = ('w_ada', 'b_ada', 'norm1', 'norm2', 'w_in',
                'ssm_lam_re', 'ssm_lam_im', 'ssm_log_step', 'ssm_b_re', 'ssm_b_im', 'ssm_c_re', 'ssm_c_im',
                'ssm_d', 'ssm_w_glu', 'na_q_gain', 'na_k_gain', 'na_rpb',
                'hy_conv_w', 'hy_conv_b', 'hy_w1', 'hy_b1', 'hy_w2', 'hy_b2', 'hy_w3', 'hy_b3', 'hy_w4',
                'hy_freq', 'hy_bias', 'w_gate', 'b_gate', 'w_branch', 'w_out',
                'router', 'exp_w1', 'exp_w3', 'exp_w2')
_STACKED = ('w_ada', 'b_ada', 'exp_w1', 'exp_w3', 'exp_w2')


def kernel(x, c, ctx, c_ctx, w_ada, b_ada, norm1, norm2, w_in, ssm_lam_re, ssm_lam_im, ssm_log_step, ssm_b_re, ssm_b_im, ssm_c_re, ssm_c_im, ssm_d, ssm_w_glu, na_q_gain, na_k_gain, na_rpb, hy_conv_w, hy_conv_b, hy_w1, hy_b1, hy_w2, hy_b2, hy_w3, hy_b3, hy_w4, hy_freq, hy_bias, w_gate, b_gate, w_branch, w_out, router, exp_w1, exp_w3, exp_w2):
    stacked = (w_ada, b_ada, norm1, norm2, w_in, ssm_lam_re, ssm_lam_im, ssm_log_step, ssm_b_re, ssm_b_im,
               ssm_c_re, ssm_c_im, ssm_d, ssm_w_glu, na_q_gain, na_k_gain, na_rpb, hy_conv_w, hy_conv_b,
               hy_w1, hy_b1, hy_w2, hy_b2, hy_w3, hy_b3, hy_w4, hy_freq, hy_bias, w_gate, b_gate, w_branch,
               w_out, router, exp_w1, exp_w3, exp_w2)
    depth = w_ada.shape[0]
    xc = ctx
    for layer in range(depth):
        p = {name: (t if name in _STACKED else t[layer]) for name, t in zip(_PARAM_NAMES, stacked)}
        p['layer'] = layer
        x, xc = _layer(x, xc, c, c_ctx, p, layer == depth - 1)
    return x
```

```python
import functools
import math

import numpy as np
import jax
import jax.numpy as jnp
from jax import lax
from jax.experimental import pallas as pl
from jax.experimental.pallas import tpu as pltpu

BF16 = jnp.bfloat16
F32 = jnp.float32

NORM_EPS = 1e-6
GRID_W = 64
SSM_GROUP = 16
NA_HEADS = 8
NA_WIN_ROWS = 8
NA_WIN_COLS = 16
HY_ORDER = 2
HY_BANDS = 16
HY_MIN_DECAY = math.log(1e-2) / 1.5
HY_MAX_DECAY = math.log(1e-2) / 0.3
EC_CAPACITY = 2
S5_CHUNK = 16
HY_FAST = 128
HY_PAD = 8
MASK_VALUE = -1e30
COMBINE_ALIGN = 16

V7X_VMEM_BYTES = 64 * 1024 * 1024


def _cparams(semantics, vmem_mb=48):
    return pltpu.CompilerParams(dimension_semantics=semantics, vmem_limit_bytes=vmem_mb * 1024 * 1024)


def _norm_mod(x, g, sh, sc):
    ms = jnp.mean(x * x, axis=-1, keepdims=True)
    return (x * lax.rsqrt(ms + NORM_EPS) * g) * (1.0 + sc) + sh


def _ada_kernel(c_ref, w_ref, b_ref, o_ref):
    c = c_ref[...]
    a = (c * jax.nn.sigmoid(c)).astype(BF16)
    o_ref[...] = jnp.dot(a, w_ref[...].astype(BF16), preferred_element_type=F32) + b_ref[...]


def _ada(cc, w, b, layer):
    m, k = cc.shape
    depth, _, n = w.shape
    tn = next(t for t in (1024, 512, 256, 128) if n % t == 0)
    return pl.pallas_call(
        _ada_kernel,
        grid=(n // tn,),
        in_specs=[pl.BlockSpec((m, k), lambda j: (0, 0)),
                  pl.BlockSpec((None, k, tn), lambda j: (layer, 0, j)),
                  pl.BlockSpec((None, 1, tn), lambda j: (layer, 0, j))],
        out_specs=pl.BlockSpec((m, tn), lambda j: (0, j)),
        out_shape=jax.ShapeDtypeStruct((m, n), F32),
        compiler_params=_cparams(("arbitrary",)),
        name="ada",
    )(cc, w, b.reshape(depth, 1, n))


def _in_proj_kernel(x_ref, g_ref, sh_ref, sc_ref, w_ref, hg_ref, ones_ref, o_ref, h_scr, *, head_dim):
    j = pl.program_id(2)

    @pl.when(j == 0)
    def _():
        h_scr[...] = _norm_mod(x_ref[...], g_ref[...], sh_ref[...], sc_ref[...]).astype(BF16)

    acc = jnp.dot(h_scr[...], w_ref[...], preferred_element_type=F32)
    is_qk = jnp.logical_or(j == 1, j == 2)

    @pl.when(is_qk)
    def _():
        yy = acc * acc
        hi = yy.astype(BF16)
        lo = (yy - hi.astype(F32)).astype(BF16)
        ss = (jnp.dot(hi, ones_ref[...], preferred_element_type=F32)
              + jnp.dot(lo, ones_ref[...], preferred_element_type=F32))
        o_ref[...] = (acc * lax.rsqrt(ss * (1.0 / head_dim) + NORM_EPS) * hg_ref[...]).astype(o_ref.dtype)

    @pl.when(jnp.logical_not(is_qk))
    def _():
        o_ref[...] = acc.astype(o_ref.dtype)


def _in_proj(x, g, sh, sc, w_bf, head_gain, ones_blk, tm):
    b, l, d = x.shape
    n = w_bf.shape[1]
    tn = ones_blk.shape[0]
    nb = sh.shape[0]
    mod_map = (lambda bi, i, j: (bi, 0, 0)) if nb > 1 else (lambda bi, i, j: (0, 0, 0))
    return pl.pallas_call(
        functools.partial(_in_proj_kernel, head_dim=tn // NA_HEADS),
        grid=(b, l // tm, n // tn),
        in_specs=[pl.BlockSpec((None, tm, d), lambda bi, i, j: (bi, i, 0)),
                  pl.BlockSpec((1, d), lambda bi, i, j: (0, 0)),
                  pl.BlockSpec((None, 1, d), mod_map),
                  pl.BlockSpec((None, 1, d), mod_map),
                  pl.BlockSpec((d, tn), lambda bi, i, j: (0, j)),
                  pl.BlockSpec((None, 1, tn), lambda bi, i, j: (jnp.clip(j - 1, 0, 1), 0, 0)),
                  pl.BlockSpec((tn, tn), lambda bi, i, j: (0, 0))],
        out_specs=pl.BlockSpec((None, tm, tn), lambda bi, i, j: (bi, i, j)),
        out_shape=jax.ShapeDtypeStruct((b, l, n), BF16),
        scratch_shapes=[pltpu.VMEM((tm, d), BF16)],
        compiler_params=_cparams(("parallel", "parallel", "arbitrary")),
        name="in_proj",
    )(x, g.reshape(1, d), sh, sc, w_bf, head_gain, ones_blk)


def _bmm_kernel(a_ref, w_ref, o_ref):
    o_ref[...] = jnp.dot(a_ref[...].astype(BF16), w_ref[...], preferred_element_type=F32).astype(o_ref.dtype)


def _bmm_add_kernel(a_ref, w_ref, y_ref, o_ref):
    o_ref[...] = (y_ref[...] + jnp.dot(a_ref[...].astype(BF16), w_ref[...],
                                       preferred_element_type=F32)).astype(o_ref.dtype)


def _bmm(a, w, add=None, tm=None, name="bmm"):
    g, m, k = a.shape
    n = w.shape[2]
    tm = m if tm is None else tm
    in_specs = [pl.BlockSpec((None, tm, k), lambda gi, i: (gi, i, 0)),
                pl.BlockSpec((None, k, n), lambda gi, i: (gi, 0, 0))]
    args = [a, w]
    kern = _bmm_kernel
    if add is not None:
        in_specs.append(pl.BlockSpec((None, tm, n), lambda gi, i: (gi, i, 0)))
        args.append(add)
        kern = _bmm_add_kernel
    return pl.pallas_call(
        kern,
        grid=(g, m // tm),
        in_specs=in_specs,
        out_specs=pl.BlockSpec((None, tm, n), lambda gi, i: (gi, i, 0)),
        out_shape=jax.ShapeDtypeStruct((g, m, n), F32),
        compiler_params=_cparams(("parallel", "parallel")),
        name=name,
    )(*args)


def _gelu_tanh(x):
    return 0.5 * x * (1.0 + jnp.tanh(math.sqrt(2.0 / math.pi) * (x + 0.044715 * (x * x * x))))


def _s5_out_kernel(y_ref, u_ref, d_ref, w_ref, o_ref):
    y = y_ref[...].astype(F32) + d_ref[...] * u_ref[...].astype(F32)
    z = _gelu_tanh(y)
    gate = jax.nn.sigmoid(jnp.dot(z.astype(BF16), w_ref[...], preferred_element_type=F32))
    o_ref[...] = (z * gate).astype(o_ref.dtype)


def _s5_out(y, u, d, w_glu_bf, tm):
    m, w = y.shape
    return pl.pallas_call(
        _s5_out_kernel,
        grid=(m // tm,),
        in_specs=[pl.BlockSpec((tm, w), lambda i: (i, 0)),
                  pl.BlockSpec((tm, w), lambda i: (i, 0)),
                  pl.BlockSpec((1, w), lambda i: (0, 0)),
                  pl.BlockSpec((w, w), lambda i: (0, 0))],
        out_specs=pl.BlockSpec((tm, w), lambda i: (i, 0)),
        out_shape=jax.ShapeDtypeStruct((m, w), BF16),
        compiler_params=_cparams(("parallel",)),
        name="s5_out",
    )(y, u, d.reshape(1, w), w_glu_bf)


def _s5_tables(p):
    t = S5_CHUNK
    mats, sums, reads, steps = [], [], [], []
    for d in range(2):
        lam_re = p['ssm_lam_re'][d].astype(F32)
        lam_im = p['ssm_lam_im'][d].astype(F32)
        step = jnp.exp(p['ssm_log_step'][d].astype(F32))[:, None]
        mag = jnp.exp(lam_re * step)
        a_re = mag * jnp.cos(lam_im * step)
        a_im = mag * jnp.sin(lam_im * step)
        den = lam_re * lam_re + lam_im * lam_im
        num_re = a_re - 1.0
        coef_re = (num_re * lam_re + a_im * lam_im) / den
        coef_im = (a_im * lam_re - num_re * lam_im) / den
        b_re = p['ssm_b_re'][d].astype(F32)
        b_im = p['ssm_b_im'][d].astype(F32)
        bb_re = coef_re[..., None] * b_re - coef_im[..., None] * b_im
        bb_im = coef_re[..., None] * b_im + coef_im[..., None] * b_re
        c_re = p['ssm_c_re'][d].astype(F32)
        c_im = p['ssm_c_im'][d].astype(F32)
        n = jnp.arange(t + 1, dtype=F32)[:, None, None]
        pm = jnp.exp(lam_re * step * n)
        pw_re = pm * jnp.cos(lam_im * step * n)
        pw_im = pm * jnp.sin(lam_im * step * n)
        ca_re = c_re[None] * pw_re[:, :, None, :] - c_im[None] * pw_im[:, :, None, :]
        ca_im = c_re[None] * pw_im[:, :, None, :] + c_im[None] * pw_re[:, :, None, :]
        ktau = (jnp.einsum('ngop,gpi->ngoi', ca_re[:t], bb_re)
                - jnp.einsum('ngop,gpi->ngoi', ca_im[:t], bb_im))
        s_idx = np.arange(t)[:, None]
        t_idx = np.arange(t)[None, :]
        lag = (t_idx - s_idx) if d == 0 else (s_idx - t_idx)
        valid = jnp.asarray(lag >= 0, F32)
        blk = ktau[np.clip(lag, 0, t - 1)] * valid[:, :, None, None, None]
        mats.append(jnp.transpose(blk, (2, 0, 4, 1, 3)))
        pows = (t - 1 - np.arange(t)) if d == 0 else np.arange(t)
        sw_re = pw_re[pows][:, :, :, None] * bb_re[None] - pw_im[pows][:, :, :, None] * bb_im[None]
        sw_im = pw_re[pows][:, :, :, None] * bb_im[None] + pw_im[pows][:, :, :, None] * bb_re[None]
        sums.append((jnp.transpose(sw_re, (1, 0, 3, 2)), jnp.transpose(sw_im, (1, 0, 3, 2))))
        rp = (np.arange(t) + 1) if d == 0 else (t - np.arange(t))
        reads.append((jnp.transpose(ca_re[rp], (1, 3, 0, 2)), -jnp.transpose(ca_im[rp], (1, 3, 0, 2))))
        steps.append((pw_re[t], pw_im[t]))
    g, pdim = steps[0][0].shape
    k = SSM_GROUP
    mtot = (mats[0] + mats[1]).reshape(g, t * k, t * k)
    ssum = jnp.concatenate([sums[0][0], sums[0][1], sums[1][0], sums[1][1]], axis=-1).reshape(g, t * k, 4 * pdim)
    r = jnp.concatenate([reads[0][0], reads[0][1], reads[1][0], reads[1][1]], axis=1).reshape(g, 4 * pdim, t * k)
    tk = t * k
    eye = jnp.eye(2, dtype=F32)
    mt = mtot.reshape(g // 2, 2, tk, tk)
    ymat = jnp.einsum('gjts,jk->gjtks', mt, eye).reshape(g // 2, 2 * tk, 2 * tk)
    smat = jnp.einsum('gjtqp,jk->gjtqkp', ssum.reshape(g // 2, 2, tk, 4, pdim), eye).reshape(g // 2, 2 * tk, 8 * pdim)
    w1 = jnp.concatenate([ymat, smat], axis=-1)
    r2 = jnp.einsum('gjqpt,jk->gqjpkt', r.reshape(g // 2, 2, 4, pdim, tk), eye).reshape(g // 2, 8 * pdim, 2 * tk)
    coef = [c.reshape(1, g * pdim) for c in (steps[0][0], steps[0][1], steps[1][0], steps[1][1])]
    return w1.astype(BF16), r2.astype(BF16), coef


def _s5_chunk_kernel(u_ref, w_ref, y_ref, xfr_ref, xfi_ref, xbr_ref, xbi_ref):
    r = jnp.dot(u_ref[...], w_ref[...], preferred_element_type=F32)
    ny = y_ref.shape[-1]
    nx = xfr_ref.shape[-1]
    y_ref[...] = r[:, :ny]
    for i, ref in enumerate((xfr_ref, xfi_ref, xbr_ref, xbi_ref)):
        ref[...] = r[:, ny + i * nx:ny + (i + 1) * nx]


def _s5_chunk(ug, w1):
    gp, m, tk2 = ug.shape
    nx = (w1.shape[2] - tk2) // 4
    xspec = pl.BlockSpec((m, nx), lambda g: (0, g))
    xshape = jax.ShapeDtypeStruct((m, gp * nx), F32)
    return pl.pallas_call(
        _s5_chunk_kernel,
        grid=(gp,),
        in_specs=[pl.BlockSpec((None, m, tk2), lambda g: (g, 0, 0)),
                  pl.BlockSpec((None, tk2, w1.shape[2]), lambda g: (g, 0, 0))],
        out_specs=[pl.BlockSpec((None, m, tk2), lambda g: (g, 0, 0)), xspec, xspec, xspec, xspec],
        out_shape=[jax.ShapeDtypeStruct((gp, m, tk2), F32), xshape, xshape, xshape, xshape],
        compiler_params=_cparams(("parallel",)),
        name="s5_chunk",
    )(ug, w1)


def _s5_scan_kernel(xfr, xfi, xbr, xbi, afr, afi, abr, abi, hfr, hfi, hbr, hbi, *, tiles_ctx, tiles, bsz):
    shape = (2 * bsz, xfr.shape[1])
    low = lax.broadcasted_iota(jnp.int32, shape, 0) < bsz
    zero = jnp.zeros(shape, F32)

    def step(hr, hi, ar, ai, xr, xi):
        return ar * hr - ai * hi + xr, ar * hi + ai * hr + xi

    def fwd_tile(i, carry):
        ar, ai = jnp.broadcast_to(afr[...], shape), jnp.broadcast_to(afi[...], shape)
        r0 = pl.multiple_of(i * 2 * bsz, 2 * bsz)
        xr, xi = xfr[pl.ds(r0, 2 * bsz), :], xfi[pl.ds(r0, 2 * bsz), :]
        sr, si = pltpu.roll(carry[0], bsz, 0), pltpu.roll(carry[1], bsz, 0)
        tr, ti = step(sr, si, ar, ai, xr, xi)
        t2r, t2i = pltpu.roll(tr, bsz, 0), pltpu.roll(ti, bsz, 0)
        hfr[pl.ds(r0, 2 * bsz), :] = jnp.where(low, sr, t2r)
        hfi[pl.ds(r0, 2 * bsz), :] = jnp.where(low, si, t2i)
        return step(t2r, t2i, ar, ai, xr, xi)

    lax.fori_loop(0, tiles, fwd_tile, (zero, zero))

    def bwd_tile(i, carry):
        ar, ai = jnp.broadcast_to(abr[...], shape), jnp.broadcast_to(abi[...], shape)
        r0 = pl.multiple_of(i * 2 * bsz, 2 * bsz)
        xr, xi = xbr[pl.ds(r0, 2 * bsz), :], xbi[pl.ds(r0, 2 * bsz), :]
        sr, si = pltpu.roll(carry[0], bsz, 0), pltpu.roll(carry[1], bsz, 0)
        tr, ti = step(sr, si, ar, ai, xr, xi)
        t2r, t2i = pltpu.roll(tr, bsz, 0), pltpu.roll(ti, bsz, 0)
        hbr[pl.ds(r0, 2 * bsz), :] = jnp.where(low, t2r, sr)
        hbi[pl.ds(r0, 2 * bsz), :] = jnp.where(low, t2i, si)
        return step(t2r, t2i, ar, ai, xr, xi)

    carry = lax.fori_loop(0, tiles_ctx, lambda j, c: bwd_tile(tiles_ctx - 1 - j, c), (zero, zero))
    lax.fori_loop(0, tiles - tiles_ctx, lambda j, c: bwd_tile(tiles - 1 - j, c), carry)


def _s5_scan(xs, coef, chunks_ctx, chunks, bsz):
    m, lanes = xs[0].shape
    assert 2 * bsz == 8 and chunks % 2 == 0 and chunks_ctx % 2 == 0
    lb = min(256, lanes)
    xspec = pl.BlockSpec((m, lb), lambda j: (0, j))
    cspec = pl.BlockSpec((1, lb), lambda j: (0, j))
    shape = jax.ShapeDtypeStruct((m, lanes), F32)
    return pl.pallas_call(
        functools.partial(_s5_scan_kernel, tiles_ctx=chunks_ctx // 2, tiles=chunks // 2, bsz=bsz),
        grid=(lanes // lb,),
        in_specs=[xspec] * 4 + [cspec] * 4,
        out_specs=[xspec] * 4,
        out_shape=[shape] * 4,
        compiler_params=_cparams(("parallel",)),
        name="s5_scan",
    )(*xs, *coef)


def _s5_readout_kernel(y_ref, hfr, hfi, hbr, hbi, r_ref, o_ref):
    h = jnp.concatenate([hfr[...], hfi[...], hbr[...], hbi[...]], axis=1).astype(BF16)
    o_ref[...] = (y_ref[...] + jnp.dot(h, r_ref[...], preferred_element_type=F32)).astype(o_ref.dtype)


def _s5_readout(y0, hs, r2):
    gp, m, tk2 = y0.shape
    nx = hs[0].shape[1] // gp
    hspec = pl.BlockSpec((m, nx), lambda g: (0, g))
    yspec = pl.BlockSpec((None, m, tk2), lambda g: (g, 0, 0))
    return pl.pallas_call(
        _s5_readout_kernel,
        grid=(gp,),
        in_specs=[yspec, hspec, hspec, hspec, hspec, pl.BlockSpec((None, 4 * nx, tk2), lambda g: (g, 0, 0))],
        out_specs=yspec,
        out_shape=jax.ShapeDtypeStruct((gp, m, tk2), BF16),
        compiler_params=_cparams(("parallel",)),
        name="s5_readout",
    )(y0, *hs, r2)


def _s5_branch(u_ctx, u_lat, p, tables, need_ctx):
    w1, r2, coef = tables
    b, l, w = u_lat.shape
    lc = u_ctx.shape[1]
    t, k = S5_CHUNK, SSM_GROUP
    g = w // k
    nc = (lc + l) // t
    m = nc * b
    u_all = jnp.concatenate([u_ctx, u_lat], axis=1)
    ug = jnp.transpose(u_all.reshape(b, nc, t, g // 2, 2, k), (3, 1, 0, 4, 2, 5)).reshape(g // 2, m, 2 * t * k)
    y0, *xs = _s5_chunk(ug, w1)
    hs = _s5_scan(xs, coef, lc // t, nc, b)
    y = _s5_readout(y0, hs, r2).reshape(g // 2, nc, b, 2, t, k)
    y = jnp.transpose(y, (2, 1, 4, 0, 3, 5)).reshape(b, nc * t, w)
    w_glu = p['ssm_w_glu'].astype(BF16)
    out_l = _s5_out(y[:, lc:].reshape(b * l, w), u_lat.reshape(b * l, w), p['ssm_d'], w_glu, tm=512)
    out_c = None
    if need_ctx:
        out_c = _s5_out(y[:, :lc].reshape(b * lc, w), u_ctx.reshape(b * lc, w), p['ssm_d'], w_glu,
                        tm=min(512, b * lc)).reshape(b, lc, w)
    return out_l.reshape(b, l, w), out_c


def _na_kernel(q_ref, k_ref, v_ref, kc_ref, vc_ref, bias_ref, o_ref, *, rows, head_dim):
    r = pl.program_id(1)
    kr = NA_WIN_ROWS
    start = jnp.clip(r - kr // 2, 0, rows - kr)
    cls = r - start
    base = pl.multiple_of(start * GRID_W, GRID_W)
    kwin = k_ref[pl.ds(base, kr * GRID_W), :]
    vwin = v_ref[pl.ds(base, kr * GRID_W), :]
    q = q_ref[...]
    kc = kc_ref[...]
    vc = vc_ref[...]
    outs = []
    nt = (((1,), (1,)), ((), ()))
    for h in range(NA_HEADS):
        sl = slice(h * head_dim, (h + 1) * head_dim)
        qh = q[:, sl]
        s_win = lax.dot_general(qh, kwin[:, sl], nt, preferred_element_type=F32) + bias_ref[cls, h]
        s_ctx = lax.dot_general(qh, kc[:, sl], nt, preferred_element_type=F32)
        mx = jnp.maximum(jnp.max(s_win, axis=-1, keepdims=True), jnp.max(s_ctx, axis=-1, keepdims=True))
        p_win = jnp.exp(s_win - mx)
        p_ctx = jnp.exp(s_ctx - mx)
        den = jnp.sum(p_win, axis=-1, keepdims=True) + jnp.sum(p_ctx, axis=-1, keepdims=True)
        o = (jnp.dot(p_win.astype(BF16), vwin[:, sl], preferred_element_type=F32)
             + jnp.dot(p_ctx.astype(BF16), vc[:, sl], preferred_element_type=F32))
        outs.append(o / den)
    o_ref[...] = jnp.concatenate(outs, axis=-1).astype(o_ref.dtype)


def _na_bias_table(rpb):
    kr, kw, w = NA_WIN_ROWS, NA_WIN_COLS, GRID_W
    col = np.arange(w)
    cstart = np.clip(col - kw // 2, 0, w - kw)
    kcol = np.arange(w)
    inwin = (kcol[None, :] >= cstart[:, None]) & (kcol[None, :] < cstart[:, None] + kw)
    dc = np.clip(kcol[None, :] - col[:, None] + (kw - 1), 0, 2 * kw - 2)
    rpb = rpb.astype(F32)
    tabs = []
    for o in range(kr):
        dr = np.arange(kr) - o + (kr - 1)
        bias = rpb[:, dr][:, :, dc]
        bias = jnp.where(jnp.asarray(inwin)[None, None], bias, MASK_VALUE)
        tabs.append(jnp.transpose(bias, (0, 2, 1, 3)).reshape(rpb.shape[0], w, kr * w))
    return jnp.stack(tabs, axis=0)


def _na_latent(q, k, v, kc, vc, bias_tab):
    b, l, w = q.shape
    lc = kc.shape[1]
    rows = l // GRID_W
    assert rows >= NA_WIN_ROWS
    nk = NA_WIN_ROWS * GRID_W
    return pl.pallas_call(
        functools.partial(_na_kernel, rows=rows, head_dim=w // NA_HEADS),
        grid=(b, rows),
        in_specs=[pl.BlockSpec((None, GRID_W, w), lambda bi, r: (bi, r, 0)),
                  pl.BlockSpec((None, l, w), lambda bi, r: (bi, 0, 0)),
                  pl.BlockSpec((None, l, w), lambda bi, r: (bi, 0, 0)),
                  pl.BlockSpec((None, lc, w), lambda bi, r: (bi, 0, 0)),
                  pl.BlockSpec((None, lc, w), lambda bi, r: (bi, 0, 0)),
                  pl.BlockSpec((NA_WIN_ROWS, NA_HEADS, GRID_W, nk), lambda bi, r: (0, 0, 0, 0))],
        out_specs=pl.BlockSpec((None, GRID_W, w), lambda bi, r: (bi, r, 0)),
        out_shape=jax.ShapeDtypeStruct((b, l, w), BF16),
        compiler_params=_cparams(("parallel", "arbitrary"), vmem_mb=56),
        name="na_latent",
    )(q, k, v, kc, vc, bias_tab)


def _na_ctx_kernel(q_ref, k_ref, v_ref, o_ref, *, head_dim):
    q, k, v = q_ref[...], k_ref[...], v_ref[...]
    outs = []
    for h in range(NA_HEADS):
        sl = slice(h * head_dim, (h + 1) * head_dim)
        s = lax.dot_general(q[:, sl], k[:, sl], (((1,), (1,)), ((), ())), preferred_element_type=F32)
        pr = jnp.exp(s - jnp.max(s, axis=-1, keepdims=True))
        den = jnp.sum(pr, axis=-1, keepdims=True)
        outs.append(jnp.dot(pr.astype(BF16), v[:, sl], preferred_element_type=F32) / den)
    o_ref[...] = jnp.concatenate(outs, axis=-1).astype(o_ref.dtype)


def _na_context(q, k, v):
    b, lc, w = q.shape
    spec = pl.BlockSpec((None, lc, w), lambda bi: (bi, 0, 0))
    return pl.pallas_call(
        functools.partial(_na_ctx_kernel, head_dim=w // NA_HEADS),
        grid=(b,),
        in_specs=[spec, spec, spec],
        out_specs=spec,
        out_shape=jax.ShapeDtypeStruct((b, lc, w), BF16),
        compiler_params=_cparams(("parallel",)),
        name="na_context",
    )(q, k, v)


def _hyena_filter(length, p):
    t = jnp.linspace(0.0, 1.0, length, dtype=F32)[:, None]
    freqs = jnp.linspace(1e-4, HY_BANDS - 1, HY_BANDS, dtype=F32)
    ang = (2.0 * math.pi / length) * jnp.arange(length, dtype=F32)[:, None] * freqs[None, :]
    z = jnp.concatenate([t, jnp.cos(ang), -jnp.sin(ang)], axis=-1)
    fr = p['hy_freq'].astype(F32)
    hp = lax.Precision.HIGHEST
    h = jnp.sin(fr * (jnp.dot(z, p['hy_w1'].astype(F32), precision=hp) + p['hy_b1'].astype(F32)))
    h = jnp.sin(fr * (jnp.dot(h, p['hy_w2'].astype(F32), precision=hp) + p['hy_b2'].astype(F32)))
    h = jnp.sin(fr * (jnp.dot(h, p['hy_w3'].astype(F32), precision=hp) + p['hy_b3'].astype(F32)))
    wdim = p['hy_bias'].shape[-1]
    h = jnp.dot(h, p['hy_w4'].astype(F32), precision=hp).reshape(length, HY_ORDER, 2, wdim)
    decay = jnp.exp(-t * jnp.abs(jnp.linspace(HY_MIN_DECAY, HY_MAX_DECAY, wdim, dtype=F32)))
    h = h * decay[:, None, None, :]
    filt = jnp.concatenate([h[:, :, 0], jnp.zeros((1, HY_ORDER, wdim), F32), h[:0:-1, :, 1]], axis=0)
    filt = filt / jnp.sum(jnp.abs(filt), axis=0, keepdims=True)
    return filt.reshape(2 * length, HY_ORDER * wdim)


def _block2(re, im):
    return np.block([[re, -im], [im, re]])


def _fft_consts(length):
    n = 2 * length
    n2 = HY_FAST
    n1 = n // n2
    h1 = n1 // 2
    a1 = -2.0 * np.pi * np.outer(np.arange(n1), np.arange(n1)) / n1
    a2 = -2.0 * np.pi * np.outer(np.arange(n2), np.arange(n2)) / n2
    f1 = _block2(np.cos(a1[:, :h1]), np.sin(a1[:, :h1]))
    f1_real = np.concatenate([np.cos(a1), np.sin(a1)], axis=0)
    f3 = _block2(np.cos(a2), np.sin(a2))
    g3 = _block2(np.cos(a2), -np.sin(a2))
    g1 = _block2(np.cos(a1[:h1, :]) / n, -np.sin(a1[:h1, :]) / n)
    wa = -2.0 * np.pi * np.arange(n2) / n
    wr = np.broadcast_to(np.cos(wa)[:, None], (n2, 128))
    wi = np.broadcast_to(np.sin(wa)[:, None], (n2, 128))
    cast = lambda m: jnp.asarray(m, BF16)
    return dict(n1=n1, h1=h1, f1=cast(f1), f1_real=cast(f1_real), f3=cast(f3), g3=cast(g3), g1=cast(g1),
                wr=jnp.asarray(wr, F32), wi=jnp.asarray(wi, F32))


def _fft_first_stage(a_r, a_i, load_cols, f1_ref, n1, cb):
    slab = HY_FAST + HY_PAD

    def body(i, carry):
        n2 = 2 * i
        z = jnp.concatenate([load_cols(n2), load_cols(n2 + 1)], axis=1)
        r = jnp.dot(f1_ref[...], z, preferred_element_type=F32)
        a_r[pl.ds(n2, n1, stride=slab), :] = r[:n1, :cb]
        a_i[pl.ds(n2, n1, stride=slab), :] = r[n1:, :cb]
        a_r[pl.ds(n2 + 1, n1, stride=slab), :] = r[:n1, cb:]
        a_i[pl.ds(n2 + 1, n1, stride=slab), :] = r[n1:, cb:]
        return carry

    lax.fori_loop(0, HY_FAST // 2, body, 0)


def _fft_slab_loop(a_r, a_i, f3_ref, wr_ref, wi_ref, n1, cb, finish):
    slab = HY_FAST + HY_PAD
    n2 = HY_FAST
    wr, wi = wr_ref[...], wi_ref[...]
    if cb != wr.shape[1]:
        wr, wi = jnp.tile(wr, (1, cb // wr.shape[1])), jnp.tile(wi, (1, cb // wi.shape[1]))

    def body(i, carry):
        tr, ti = carry
        k1 = 2 * i
        r0 = pl.multiple_of(k1 * slab, 8)
        r1 = pl.multiple_of((k1 + 1) * slab, 8)
        tr1, ti1 = tr * wr - ti * wi, tr * wi + ti * wr
        ar0, ai0 = a_r[pl.ds(r0, n2), :], a_i[pl.ds(r0, n2), :]
        ar1, ai1 = a_r[pl.ds(r1, n2), :], a_i[pl.ds(r1, n2), :]
        xr = jnp.concatenate([ar0 * tr - ai0 * ti, ar1 * tr1 - ai1 * ti1], axis=1)
        xi = jnp.concatenate([ar0 * ti + ai0 * tr, ar1 * ti1 + ai1 * tr1], axis=1)
        bc = jnp.dot(f3_ref[...], jnp.concatenate([xr, xi], axis=0).astype(BF16), preferred_element_type=F32)
        finish(k1, (r0, r1), (tr, ti, tr1, ti1), bc[:n2], bc[n2:])
        return tr1 * wr - ti1 * wi, tr1 * wi + ti1 * wr

    one = jnp.ones((n2, cb), F32)
    lax.fori_loop(0, n1 // 2, body, (one, jnp.zeros_like(one)))


def _fft_conv_kernel(y_ref, sr_ref, si_ref, f1_ref, f3_ref, g3_ref, g1_ref, wr_ref, wi_ref, o_ref, a_r, a_i,
                     *, n1):
    h1 = n1 // 2
    n2 = HY_FAST
    cb = o_ref.shape[-1]
    slab = HY_FAST + HY_PAD

    def load_cols(j):
        r = pl.multiple_of(j * h1, h1)
        return jnp.concatenate([y_ref[0, pl.ds(r, h1), :], y_ref[1, pl.ds(r, h1), :]], axis=0)

    _fft_first_stage(a_r, a_i, load_cols, f1_ref, n1, cb)

    def finish(k1, rows, tw, br, bi):
        s0 = pl.multiple_of(k1 * n2, n2)
        s1 = pl.multiple_of((k1 + 1) * n2, n2)
        sr = jnp.concatenate([sr_ref[pl.ds(s0, n2), :], sr_ref[pl.ds(s1, n2), :]], axis=1).astype(F32)
        si = jnp.concatenate([si_ref[pl.ds(s0, n2), :], si_ref[pl.ds(s1, n2), :]], axis=1).astype(F32)
        y = jnp.concatenate([br * sr - bi * si, br * si + bi * sr], axis=0).astype(BF16)
        ac = jnp.dot(g3_ref[...], y, preferred_element_type=F32)
        pr, pi = ac[:n2], ac[n2:]
        for j in range(2):
            tr, ti = tw[2 * j], tw[2 * j + 1]
            prj, pij = pr[:, j * cb:(j + 1) * cb], pi[:, j * cb:(j + 1) * cb]
            a_r[pl.ds(rows[j], n2), :] = prj * tr + pij * ti
            a_i[pl.ds(rows[j], n2), :] = pij * tr - prj * ti

    _fft_slab_loop(a_r, a_i, f3_ref, wr_ref, wi_ref, n1, cb, finish)

    def out_body(i, carry):
        j0 = 2 * i
        cols = []
        for j in (j0, j0 + 1):
            cols.append(jnp.concatenate([a_r[pl.ds(j, n1, stride=slab), :], a_i[pl.ds(j, n1, stride=slab), :]],
                                        axis=0))
        o = jnp.dot(g1_ref[...], jnp.concatenate(cols, axis=1).astype(BF16), preferred_element_type=F32)
        for jj in range(2):
            r = pl.multiple_of((j0 + jj) * h1, h1)
            o_ref[0, pl.ds(r, h1), :] = o[:h1, jj * cb:(jj + 1) * cb].astype(o_ref.dtype)
            o_ref[1, pl.ds(r, h1), :] = o[h1:, jj * cb:(jj + 1) * cb].astype(o_ref.dtype)
        return carry

    lax.fori_loop(0, n2 // 2, out_body, 0)


def _fft_conv(yp, col0, spec_r, spec_i, scol0, consts, cb=128):
    b, l, _ = yp.shape
    n = spec_r.shape[0]
    n1 = consts['n1']
    slab = HY_FAST + HY_PAD
    nblk = consts['nblk']
    y4 = yp.reshape(b // 2, 2, l, yp.shape[-1])
    const = lambda a: pl.BlockSpec(a.shape, lambda j, pi: (0, 0))
    return pl.pallas_call(
        functools.partial(_fft_conv_kernel, n1=n1),
        grid=(nblk, b // 2),
        in_specs=[pl.BlockSpec((None, 2, l, cb), lambda j, pi: (pi, 0, 0, col0 + j)),
                  pl.BlockSpec((n, cb), lambda j, pi: (0, scol0 + j)),
                  pl.BlockSpec((n, cb), lambda j, pi: (0, scol0 + j)),
                  const(consts['f1']), const(consts['f3']), const(consts['g3']), const(consts['g1']),
                  const(consts['wr']), const(consts['wi'])],
        out_specs=pl.BlockSpec((None, 2, l, cb), lambda j, pi: (pi, 0, 0, j)),
        out_shape=jax.ShapeDtypeStruct((b // 2, 2, l, nblk * cb), BF16),
        scratch_shapes=[pltpu.VMEM((n1 * slab, cb), F32), pltpu.VMEM((n1 * slab, cb), F32)],
        compiler_params=_cparams(("arbitrary", "arbitrary"), vmem_mb=58),
        name="fft_conv",
    )(y4, spec_r, spec_i, consts['f1'], consts['f3'], consts['g3'], consts['g1'],
      consts['wr'], consts['wi']).reshape(b, l, nblk * cb)


def _fft_spec_kernel(f_ref, f1_ref, f3_ref, wr_ref, wi_ref, sr_ref, si_ref, a_r, a_i, *, n1):
    n2 = HY_FAST
    cb = sr_ref.shape[-1]

    def load_cols(j):
        return f_ref[pl.ds(pl.multiple_of(j * n1, n1), n1), :]

    _fft_first_stage(a_r, a_i, load_cols, f1_ref, n1, cb)

    def finish(k1, rows, tw, br, bi):
        for j in range(2):
            s = pl.multiple_of((k1 + j) * n2, n2)
            sr_ref[pl.ds(s, n2), :] = br[:, j * cb:(j + 1) * cb].astype(sr_ref.dtype)
            si_ref[pl.ds(s, n2), :] = bi[:, j * cb:(j + 1) * cb].astype(si_ref.dtype)

    _fft_slab_loop(a_r, a_i, f3_ref, wr_ref, wi_ref, n1, cb, finish)


def _fft_spectrum(filt_p, consts, cb=128):
    n, c = filt_p.shape
    n1 = consts['n1']
    slab = HY_FAST + HY_PAD
    const = lambda a: pl.BlockSpec(a.shape, lambda j: (0, 0))
    spec = pl.BlockSpec((n, cb), lambda j: (0, j))
    return pl.pallas_call(
        functools.partial(_fft_spec_kernel, n1=n1),
        grid=(c // cb,),
        in_specs=[spec, const(consts['f1_real']), const(consts['f3']), const(consts['wr']), const(consts['wi'])],
        out_specs=[spec, spec],
        out_shape=[jax.ShapeDtypeStruct((n, c), BF16)] * 2,
        scratch_shapes=[pltpu.VMEM((n1 * slab, cb), F32), pltpu.VMEM((n1 * slab, cb), F32)],
        compiler_params=_cparams(("arbitrary",), vmem_mb=58),
        name="fft_spectrum",
    )(filt_p, consts['f1_real'], consts['f3'], consts['wr'], consts['wi'])


def _dense_conv_kernel(y_ref, sr_ref, si_ref, f_ref, g_ref, o_ref):
    l = y_ref.shape[1]
    n = sr_ref.shape[0]
    bc = jnp.dot(f_ref[...], jnp.concatenate([y_ref[0], y_ref[1]], axis=0), preferred_element_type=F32)
    br, bi = bc[:n], bc[n:]
    sr, si = sr_ref[...], si_ref[...]
    y = jnp.concatenate([br * sr - bi * si, br * si + bi * sr], axis=0).astype(BF16)
    o = jnp.dot(g_ref[...], y, preferred_element_type=F32)
    o_ref[0] = o[:l].astype(o_ref.dtype)
    o_ref[1] = o[l:].astype(o_ref.dtype)


def _dense_conv(y, col0, spec_r, spec_i, scol0, fmat, gmat, nblk, cb=128):
    b, l, _ = y.shape
    n = spec_r.shape[0]
    y4 = y.reshape(b // 2, 2, l, y.shape[-1])
    const = lambda a: pl.BlockSpec(a.shape, lambda j, pi: (0, 0))
    return pl.pallas_call(
        _dense_conv_kernel,
        grid=(nblk, b // 2),
        in_specs=[pl.BlockSpec((None, 2, l, cb), lambda j, pi: (pi, 0, 0, col0 + j)),
                  pl.BlockSpec((n, cb), lambda j, pi: (0, scol0 + j)),
                  pl.BlockSpec((n, cb), lambda j, pi: (0, scol0 + j)),
                  const(fmat), const(gmat)],
        out_specs=pl.BlockSpec((None, 2, l, cb), lambda j, pi: (pi, 0, 0, j)),
        out_shape=jax.ShapeDtypeStruct((b // 2, 2, l, nblk * cb), BF16),
        compiler_params=_cparams(("parallel", "parallel")),
        name="dense_conv",
    )(y4, spec_r, spec_i, fmat, gmat).reshape(b, l, nblk * cb)


def _sconv_kernel(u_ref, w_ref, b_ref, o_ref, *scr, h1):
    u = u_ref[...].astype(F32)
    l = u.shape[0]
    row = lax.broadcasted_iota(jnp.int32, u.shape, 0)
    prev = jnp.where(row == 0, 0.0, pltpu.roll(u, 1, 0))
    nxt = jnp.where(row == l - 1, 0.0, pltpu.roll(u, l - 1, 0))
    z = prev * w_ref[0:1, :] + u * w_ref[1:2, :] + nxt * w_ref[2:3, :] + b_ref[...]
    if h1 is None:
        o_ref[...] = z.astype(o_ref.dtype)
        return
    scr[0][...] = z

    def body(n2, carry):
        o_ref[pl.ds(pl.multiple_of(n2 * h1, h1), h1), :] = scr[0][pl.ds(n2, h1, stride=HY_FAST), :].astype(o_ref.dtype)
        return carry

    lax.fori_loop(0, HY_FAST, body, 0)


def _short_conv(proj, col0, conv_w, conv_b, h1, cb=128):
    b, l, _ = proj.shape
    c = conv_w.shape[1]
    return pl.pallas_call(
        functools.partial(_sconv_kernel, h1=h1),
        scratch_shapes=[] if h1 is None else [pltpu.VMEM((l, cb), F32)],
        grid=(b, c // cb),
        in_specs=[pl.BlockSpec((None, l, cb), lambda bi, j: (bi, 0, col0 + j)),
                  pl.BlockSpec((conv_w.shape[0], cb), lambda bi, j: (0, j)),
                  pl.BlockSpec((1, cb), lambda bi, j: (0, j))],
        out_specs=pl.BlockSpec((None, l, cb), lambda bi, j: (bi, 0, j)),
        out_shape=jax.ShapeDtypeStruct((b, l, c), BF16),
        compiler_params=_cparams(("parallel", "parallel")),
        name="short_conv",
    )(proj, conv_w.astype(F32), conv_b.astype(F32).reshape(1, c))


def _gate_kernel(x_ref, conv_ref, y_ref, bias_ref, o_ref, *scr, h1):
    y = y_ref[...].astype(F32)
    z = x_ref[...].astype(F32) * (conv_ref[...].astype(F32) + y * bias_ref[...])
    if h1 is None:
        o_ref[...] = z.astype(o_ref.dtype)
        return
    scr[0][...] = z

    def body(n1, carry):
        r = pl.multiple_of(n1 * HY_FAST, HY_FAST)
        o_ref[pl.ds(r, HY_FAST), :] = scr[0][pl.ds(n1, HY_FAST, stride=h1), :].astype(o_ref.dtype)
        return carry

    lax.fori_loop(0, h1, body, 0)


def _hy_gate(z, xcol, conv, y, ycol, bias, h1=None, cb=128):
    b, l, w = conv.shape
    return pl.pallas_call(
        functools.partial(_gate_kernel, h1=h1),
        grid=(b, w // cb),
        in_specs=[pl.BlockSpec((None, l, cb), lambda bi, j: (bi, 0, xcol + j)),
                  pl.BlockSpec((None, l, cb), lambda bi, j: (bi, 0, j)),
                  pl.BlockSpec((None, l, cb), lambda bi, j: (bi, 0, ycol + j)),
                  pl.BlockSpec((1, cb), lambda bi, j: (0, j))],
        out_specs=pl.BlockSpec((None, l, cb), lambda bi, j: (bi, 0, j)),
        out_shape=jax.ShapeDtypeStruct((b, l, w), BF16),
        scratch_shapes=[] if h1 is None else [pltpu.VMEM((l, cb), F32)],
        compiler_params=_cparams(("parallel", "parallel")),
        name="hy_gate",
    )(z, conv, y, bias.astype(F32).reshape(1, w))


def _hyena_branch(proj, col0, p, cb=128):
    b, l, _ = proj.shape
    w = p['hy_bias'].shape[-1]
    n = 2 * l
    nblk = w // cb
    filt = _hyena_filter(l, p)
    if n // HY_FAST >= 16:
        consts = dict(_fft_consts(l), nblk=nblk)
        n1, h1 = consts['n1'], consts['h1']
        filt_p = jnp.transpose(filt.astype(BF16).reshape(n1, HY_FAST, -1), (1, 0, 2)).reshape(n, -1)
        spec_r, spec_i = _fft_spectrum(filt_p, consts, cb)
        conv = lambda y, ycol, order: _fft_conv(y, ycol * nblk, spec_r, spec_i, order * nblk, consts, cb)
    else:
        h1 = None
        k = np.arange(n)
        ang = -2.0 * np.pi * np.outer(k, k) / n
        fmat = jnp.asarray(_block2(np.cos(ang[:, :l]), np.sin(ang[:, :l])), BF16)
        gmat = jnp.asarray(_block2(np.cos(ang[:l, :]) / n, -np.sin(ang[:l, :]) / n), BF16)
        f_real = jnp.asarray(np.concatenate([np.cos(ang), np.sin(ang)], axis=0), F32)
        spec = _bmm(f_real[None], filt.astype(BF16)[None], name="dense_spectrum")[0]
        spec_r, spec_i = spec[:n], spec[n:]
        conv = lambda y, ycol, order: _dense_conv(y, ycol * nblk, spec_r, spec_i, order * nblk, fmat, gmat,
                                                  nblk, cb)
    z = _short_conv(proj, col0, p['hy_conv_w'], p['hy_conv_b'], h1, cb)
    c0 = conv(z, HY_ORDER, 0)
    y1 = _hy_gate(z, 0, c0, z, HY_ORDER * nblk, p['hy_bias'][0], None, cb)
    c1 = conv(y1, 0, 1)
    return _hy_gate(z, nblk, c1, y1, 0, p['hy_bias'][1], h1, cb)


def _merge_kernel(x_ref, g_ref, sh_ref, sc_ref, g1_ref, br_ref, wg0_ref, wg1_ref, wg2_ref, bg_ref, wb_ref, wo_ref,
                  o_ref, h_scr, acc_scr):
    j = pl.program_id(2)

    @pl.when(j == 0)
    def _():
        h_scr[...] = _norm_mod(x_ref[...], g_ref[...], sh_ref[...], sc_ref[...]).astype(BF16)
        acc_scr[...] = jnp.zeros_like(acc_scr)

    h = h_scr[...]
    mixed = None
    for i, wg_ref in enumerate((wg0_ref, wg1_ref, wg2_ref)):
        gate = jax.nn.sigmoid(jnp.dot(h, wg_ref[...], preferred_element_type=F32) + bg_ref[i])
        term = gate * jnp.dot(br_ref[i], wb_ref[i], preferred_element_type=F32)
        mixed = term if mixed is None else mixed + term
    acc_scr[...] += jnp.dot(mixed.astype(BF16), wo_ref[...], preferred_element_type=F32)

    @pl.when(j == pl.num_programs(2) - 1)
    def _():
        o_ref[...] = x_ref[...] + g1_ref[...] * acc_scr[...]


def _merge(x, g, sh, sc, g1, branches, w_gate_bf, b_gate, w_branch_bf, w_out_bf, tm, tc):
    b, l, d = x.shape
    nbr, w, _ = w_branch_bf.shape
    nb = sh.shape[0]
    nj = d // tc
    mod_map = (lambda bi, i, j: (bi, 0, 0)) if nb > 1 else (lambda bi, i, j: (0, 0, 0))
    br = jnp.stack(branches, axis=1)
    bg = b_gate.reshape(nbr, nj, 1, tc).transpose(1, 0, 2, 3)
    return pl.pallas_call(
        _merge_kernel,
        grid=(b, l // tm, nj),
        in_specs=[pl.BlockSpec((None, tm, d), lambda bi, i, j: (bi, i, 0)),
                  pl.BlockSpec((1, d), lambda bi, i, j: (0, 0)),
                  pl.BlockSpec((None, 1, d), mod_map),
                  pl.BlockSpec((None, 1, d), mod_map),
                  pl.BlockSpec((None, 1, d), mod_map),
                  pl.BlockSpec((None, nbr, tm, w), lambda bi, i, j: (bi, 0, i, 0)),
                  pl.BlockSpec((d, tc), lambda bi, i, j: (0, j)),
                  pl.BlockSpec((d, tc), lambda bi, i, j: (0, nj + j)),
                  pl.BlockSpec((d, tc), lambda bi, i, j: (0, 2 * nj + j)),
                  pl.BlockSpec((None, nbr, 1, tc), lambda bi, i, j: (j, 0, 0, 0)),
                  pl.BlockSpec((nbr, w, tc), lambda bi, i, j: (0, 0, j)),
                  pl.BlockSpec((tc, d), lambda bi, i, j: (j, 0))],
        out_specs=pl.BlockSpec((None, tm, d), lambda bi, i, j: (bi, i, 0)),
        out_shape=jax.ShapeDtypeStruct((b, l, d), F32),
        scratch_shapes=[pltpu.VMEM((tm, d), BF16), pltpu.VMEM((tm, d), F32)],
        compiler_params=_cparams(("parallel", "parallel", "arbitrary"), vmem_mb=56),
        name="merge",
    )(x, g.reshape(1, d), sh, sc, g1, br, w_gate_bf, w_gate_bf, w_gate_bf, bg, w_branch_bf, w_out_bf)


def _router_kernel(x_ref, g_ref, sh_ref, sc_ref, rt_ref, h_ref, aff_ref):
    h = _norm_mod(x_ref[...], g_ref[...], sh_ref[...], sc_ref[...])
    h_ref[...] = h.astype(h_ref.dtype)
    nt = (((1,), (1,)), ((), ()))
    rt = rt_ref[...]
    r_hi = rt.astype(BF16)
    r_lo = (rt - r_hi.astype(F32)).astype(BF16)
    h_hi = h.astype(BF16)
    h_lo = (h - h_hi.astype(F32)).astype(BF16)
    logits = (lax.dot_general(r_hi, h_hi, nt, preferred_element_type=F32)
              + lax.dot_general(r_hi, h_lo, nt, preferred_element_type=F32)
              + lax.dot_general(r_lo, h_hi, nt, preferred_element_type=F32))
    e = jnp.exp(logits - jnp.max(logits, axis=0, keepdims=True))
    aff_ref[...] = e / jnp.sum(e, axis=0, keepdims=True)


def _router(x, g, sh, sc, router_t, tm):
    b, l, d = x.shape
    e = router_t.shape[0]
    nb = sh.shape[0]
    mod_map = (lambda bi, i: (bi, 0, 0)) if nb > 1 else (lambda bi, i: (0, 0, 0))
    return pl.pallas_call(
        _router_kernel,
        grid=(b, l // tm),
        in_specs=[pl.BlockSpec((None, tm, d), lambda bi, i: (bi, i, 0)),
                  pl.BlockSpec((1, d), lambda bi, i: (0, 0)),
                  pl.BlockSpec((None, 1, d), mod_map),
                  pl.BlockSpec((None, 1, d), mod_map),
                  pl.BlockSpec((e, d), lambda bi, i: (0, 0))],
        out_specs=[pl.BlockSpec((None, tm, d), lambda bi, i: (bi, i, 0)),
                   pl.BlockSpec((None, e, tm), lambda bi, i: (bi, 0, i))],
        out_shape=[jax.ShapeDtypeStruct((b, l, d), BF16), jax.ShapeDtypeStruct((b, e, l), F32)],
        compiler_params=_cparams(("parallel", "parallel")),
        name="router",
    )(x, g.reshape(1, d), sh, sc, router_t)


def _ffn_kernel(xs_ref, gate_ref, w1_ref, w3_ref, w2_ref, o_ref, acc_scr):
    f = pl.program_id(2)

    @pl.when(f == 0)
    def _():
        acc_scr[...] = jnp.zeros_like(acc_scr)

    xs = xs_ref[...]
    a = jnp.dot(xs, w1_ref[...].astype(BF16), preferred_element_type=F32)
    gl = jnp.dot(xs, w3_ref[...].astype(BF16), preferred_element_type=F32)
    mid = (a * jax.nn.sigmoid(a) * gl).astype(BF16)
    acc_scr[...] += jnp.dot(mid, w2_ref[...].astype(BF16), preferred_element_type=F32)

    @pl.when(f == pl.num_programs(2) - 1)
    def _():
        o_ref[...] = (acc_scr[...] * gate_ref[...]).astype(o_ref.dtype)


def _expert_ffn(xs, gate, w1, w3, w2, layer, tr, tf):
    e, r, d = xs.shape
    ff = w1.shape[3]
    return pl.pallas_call(
        _ffn_kernel,
        grid=(e, r // tr, ff // tf),
        in_specs=[pl.BlockSpec((None, tr, d), lambda ei, i, f: (ei, i, 0)),
                  pl.BlockSpec((None, tr, 1), lambda ei, i, f: (ei, i, 0)),
                  pl.BlockSpec((None, None, d, tf), lambda ei, i, f: (layer, ei, 0, f)),
                  pl.BlockSpec((None, None, d, tf), lambda ei, i, f: (layer, ei, 0, f)),
                  pl.BlockSpec((None, None, tf, d), lambda ei, i, f: (layer, ei, f, 0))],
        out_specs=pl.BlockSpec((None, tr, d), lambda ei, i, f: (ei, i, 0)),
        out_shape=jax.ShapeDtypeStruct((e, r, d), BF16),
        scratch_shapes=[pltpu.VMEM((tr, d), F32)],
        compiler_params=_cparams(("parallel", "parallel", "arbitrary"), vmem_mb=56),
        name="expert_ffn",
    )(xs, gate, w1, w3, w2)


def _combine_kernel(st_ref, x_ref, g2_ref, pos_ref, y_ref, o_ref, acc_scr, *, n_exp, tiles):
    bi, j, e = pl.program_id(0), pl.program_id(1), pl.program_id(2)

    @pl.when(e == 0)
    def _():
        acc_scr[...] = jnp.zeros_like(acc_scr)

    win, tt = y_ref.shape[1], pos_ref.shape[1]
    slot = lax.broadcasted_iota(jnp.int32, (win, tt), 0) + st_ref[(bi * n_exp + e) * tiles + j]
    onehot = jnp.where(slot == pos_ref[...], 1.0, 0.0).astype(BF16)
    acc_scr[...] += lax.dot_general(onehot, y_ref[0], (((0,), (0,)), ((), ())), preferred_element_type=F32)

    @pl.when(e == n_exp - 1)
    def _():
        o_ref[...] = x_ref[...] + g2_ref[...] * acc_scr[...]


def _moe_combine(x, g2, pos, y, cap, tt):
    b, l, d = x.shape
    e = y.shape[0]
    tiles = l // tt
    win = min(tt + COMBINE_ALIGN, cap)
    sel = (pos >= 0).reshape(b, e, tiles, tt).sum(-1)
    first = jnp.cumsum(sel, axis=-1) - sel
    start = jnp.minimum(first // COMBINE_ALIGN * COMBINE_ALIGN, cap - win).astype(jnp.int32).reshape(-1)
    nb = g2.shape[0]

    def y_map(bi, j, ei, st):
        return ei, pl.multiple_of(bi * cap + st[(bi * e + ei) * tiles + j], COMBINE_ALIGN), 0

    grid_spec = pltpu.PrefetchScalarGridSpec(
        num_scalar_prefetch=1,
        grid=(b, tiles, e),
        in_specs=[pl.BlockSpec((None, tt, d), lambda bi, j, ei, st: (bi, j, 0)),
                  pl.BlockSpec((None, 1, d), lambda bi, j, ei, st: (bi if nb > 1 else 0, 0, 0)),
                  pl.BlockSpec((None, None, 1, tt), lambda bi, j, ei, st: (bi, ei, 0, j)),
                  pl.BlockSpec((pl.Element(1), pl.Element(win), pl.Element(d)), y_map)],
        out_specs=pl.BlockSpec((None, tt, d), lambda bi, j, ei, st: (bi, j, 0)),
        scratch_shapes=[pltpu.VMEM((tt, d), F32)])
    return pl.pallas_call(
        functools.partial(_combine_kernel, n_exp=e, tiles=tiles),
        grid_spec=grid_spec,
        out_shape=jax.ShapeDtypeStruct((b, l, d), F32),
        compiler_params=_cparams(("parallel", "parallel", "arbitrary")),
        name="moe_combine",
    )(start, x, g2, pos.reshape(b, e, 1, l), y)


def _ec_moe_update(x, g, sh, sc, g2, p, tm):
    b, l, d = x.shape
    e = p['router'].shape[1]
    cap = max(1, EC_CAPACITY * l // e)
    h, aff_t = _router(x, g, sh, sc, jnp.transpose(p['router']).astype(F32), tm)
    gate, idx = lax.top_k(aff_t, cap)
    idx, gate = lax.sort_key_val(idx, gate, dimension=-1)
    bidx = jnp.arange(b)[:, None, None]
    eidx = jnp.arange(e)[None, :, None]
    pos = jnp.full((b, e, l), -1, jnp.int32).at[bidx, eidx, idx].set(
        jnp.broadcast_to(jnp.arange(cap, dtype=jnp.int32), idx.shape), unique_indices=True)
    idx_t = jnp.transpose(idx, (1, 0, 2))
    xs = h[jnp.arange(b)[None, :, None], idx_t].reshape(e, b * cap, d)
    gt = jnp.transpose(gate, (1, 0, 2)).reshape(e, b * cap, 1)
    tr = min(1024, b * cap)
    ff = p['exp_w1'].shape[3]
    y = _expert_ffn(xs, gt, p['exp_w1'], p['exp_w3'], p['exp_w2'], p['layer'], tr=tr, tf=min(512, ff))
    return _moe_combine(x, g2, pos, y, cap, min(256, l))


def _layer(x, xc, c, c_ctx, p, last):
    b, l, d = x.shape
    lc = xc.shape[1]
    w = p['ssm_d'].shape[0]
    hd = w // NA_HEADS

    cc = jnp.zeros((8, d), F32).at[:b].set(c).at[b].set(c_ctx)
    mod = _ada(cc, p['w_ada'], p['b_ada'], p['layer'])
    m_l = [t[:, None, :] for t in jnp.split(mod[:b], 6, axis=-1)]
    m_c = [t[:, None, :] for t in jnp.split(mod[b:b + 1], 6, axis=-1)]

    w_in = p['w_in'].astype(BF16)
    scale = hd ** -0.5
    head_gain = jnp.stack([jnp.tile(p['na_q_gain'].astype(F32) * scale, NA_HEADS),
                           jnp.tile(p['na_k_gain'].astype(F32), NA_HEADS)])[:, None, :]
    ones_blk = jnp.asarray(np.kron(np.eye(NA_HEADS), np.ones((hd, hd))), BF16)
    tm_l = min(512, l)
    tm_c = min(512, lc)
    proj = _in_proj(x, p['norm1'], m_l[0], m_l[1], w_in, head_gain, ones_blk, min(1024, l))
    projc = _in_proj(xc, p['norm1'], m_c[0], m_c[1], w_in, head_gain, ones_blk, tm_c)

    ssm_l, ssm_c = _s5_branch(projc[..., :w], proj[..., :w], p, _s5_tables(p), need_ctx=not last)

    na_l = _na_latent(proj[..., w:2 * w], proj[..., 2 * w:3 * w], proj[..., 3 * w:4 * w],
                      projc[..., 2 * w:3 * w], projc[..., 3 * w:4 * w], _na_bias_table(p['na_rpb']))
    hy_l = _hyena_branch(proj, 4 * w // 128, p)

    wg = p['w_gate'].astype(BF16)
    wb = p['w_branch'].astype(BF16)
    wo = p['w_out'].astype(BF16)
    x = _merge(x, p['norm1'], m_l[0], m_l[1], m_l[2], (ssm_l, na_l, hy_l), wg, p['b_gate'], wb, wo,
               tm=tm_l, tc=min(512, d))
    x = _ec_moe_update(x, p['norm2'], m_l[3], m_l[4], m_l[5], p, tm_l)
    if last:
        return x, None
    na_c = _na_context(projc[..., w:2 * w], projc[..., 2 * w:3 * w], projc[..., 3 * w:4 * w])
    hy_c = _hyena_branch(projc, 4 * w // 128, p)
    xc = _merge(xc, p['norm1'], m_c[0], m_c[1], m_c[2], (ssm_c, na_c, hy_c), wg, p['b_gate'], wb, wo,
                tm=tm_c, tc=min(512, d))
    xc = _ec_moe_update(xc, p['norm2'], m_c[3], m_c[4], m_c[5], p, tm_c)
    return x, xc


_PARAM_NAMES = ('w_ada', 'b_ada', 'norm1', 'norm2', 'w_in',
                'ssm_lam_re', 'ssm_lam_im', 'ssm_log_step', 'ssm_b_re', 'ssm_b_im', 'ssm_c_re', 'ssm_c_im',
                'ssm_d', 'ssm_w_glu', 'na_q_gain', 'na_k_gain', 'na_rpb',
                'hy_conv_w', 'hy_conv_b', 'hy_w1', 'hy_b1', 'hy_w2', 'hy_b2', 'hy_w3', 'hy_b3', 'hy_w4',
                'hy_freq', 'hy_bias', 'w_gate', 'b_gate', 'w_branch', 'w_out',
                'router', 'exp_w1', 'exp_w3', 'exp_w2')
_STACKED = ('w_ada', 'b_ada', 'exp_w1', 'exp_w3', 'exp_w2')


def kernel(x, c, ctx, c_ctx, w_ada, b_ada, norm1, norm2, w_in, ssm_lam_re, ssm_lam_im, ssm_log_step, ssm_b_re, ssm_b_im, ssm_c_re, ssm_c_im, ssm_d, ssm_w_glu, na_q_gain, na_k_gain, na_rpb, hy_conv_w, hy_conv_b, hy_w1, hy_b1, hy_w2, hy_b2, hy_w3, hy_b3, hy_w4, hy_freq, hy_bias, w_gate, b_gate, w_branch, w_out, router, exp_w1, exp_w3, exp_w2):
    stacked = (w_ada, b_ada, norm1, norm2, w_in, ssm_lam_re, ssm_lam_im, ssm_log_step, ssm_b_re, ssm_b_im,
               ssm_c_re, ssm_c_im, ssm_d, ssm_w_glu, na_q_gain, na_k_gain, na_rpb, hy_conv_w, hy_conv_b,
               hy_w1, hy_b1, hy_w2, hy_b2, hy_w3, hy_b3, hy_w4, hy_freq, hy_bias, w_gate, b_gate, w_branch,
               w_out, router, exp_w1, exp_w3, exp_w2)
    depth = w_ada.shape[0]
    xc = ctx
    for layer in range(depth):
        p = {name: (t if name in _STACKED else t[layer]) for name, t in zip(_PARAM_NAMES, stacked)}
        p['layer'] = layer
        x, xc = _layer(x, xc, c, c_ctx, p, layer == depth - 1)
    return x
```

```python
import functools
import math

import numpy as np
import jax
import jax.numpy as jnp
from jax import lax
from jax.experimental import pallas as pl
from jax.experimental.pallas import tpu as pltpu

BF16 = jnp.bfloat16
F32 = jnp.float32

NORM_EPS = 1e-6
GRID_W = 64
SSM_GROUP = 16
NA_HEADS = 8
NA_WIN_ROWS = 8
NA_WIN_COLS = 16
HY_ORDER = 2
HY_BANDS = 16
HY_MIN_DECAY = math.log(1e-2) / 1.5
HY_MAX_DECAY = math.log(1e-2) / 0.3
EC_CAPACITY = 2
S5_CHUNK = 16
HY_FAST = 128
HY_PAD = 8
MASK_VALUE = -1e30
COMBINE_ALIGN = 16

V7X_VMEM_BYTES = 64 * 1024 * 1024


def _cparams(semantics, vmem_mb=48):
    return pltpu.CompilerParams(dimension_semantics=semantics, vmem_limit_bytes=vmem_mb * 1024 * 1024)


def _norm_mod(x, g, sh, sc):
    ms = jnp.mean(x * x, axis=-1, keepdims=True)
    return (x * lax.rsqrt(ms + NORM_EPS) * g) * (1.0 + sc) + sh


def _ada_kernel(c_ref, w_ref, b_ref, o_ref):
    c = c_ref[...]
    a = (c * jax.nn.sigmoid(c)).astype(BF16)
    o_ref[...] = jnp.dot(a, w_ref[...].astype(BF16), preferred_element_type=F32) + b_ref[...]


def _ada(cc, w, b, layer):
    m, k = cc.shape
    depth, _, n = w.shape
    tn = next(t for t in (1024, 512, 256, 128) if n % t == 0)
    return pl.pallas_call(
        _ada_kernel,
        grid=(n // tn,),
        in_specs=[pl.BlockSpec((m, k), lambda j: (0, 0)),
                  pl.BlockSpec((None, k, tn), lambda j: (layer, 0, j)),
                  pl.BlockSpec((None, 1, tn), lambda j: (layer, 0, j))],
        out_specs=pl.BlockSpec((m, tn), lambda j: (0, j)),
        out_shape=jax.ShapeDtypeStruct((m, n), F32),
        compiler_params=_cparams(("arbitrary",)),
        name="ada",
    )(cc, w, b.reshape(depth, 1, n))


def _in_proj_kernel(x_ref, g_ref, sh_ref, sc_ref, w_ref, hg_ref, ones_ref, o_ref, h_scr, *, head_dim):
    j = pl.program_id(2)

    @pl.when(j == 0)
    def _():
        h_scr[...] = _norm_mod(x_ref[...], g_ref[...], sh_ref[...], sc_ref[...]).astype(BF16)

    acc = jnp.dot(h_scr[...], w_ref[...], preferred_element_type=F32)
    is_qk = jnp.logical_or(j == 1, j == 2)

    @pl.when(is_qk)
    def _():
        yy = acc * acc
        hi = yy.astype(BF16)
        lo = (yy - hi.astype(F32)).astype(BF16)
        ss = (jnp.dot(hi, ones_ref[...], preferred_element_type=F32)
              + jnp.dot(lo, ones_ref[...], preferred_element_type=F32))
        o_ref[...] = (acc * lax.rsqrt(ss * (1.0 / head_dim) + NORM_EPS) * hg_ref[...]).astype(o_ref.dtype)

    @pl.when(jnp.logical_not(is_qk))
    def _():
        o_ref[...] = acc.astype(o_ref.dtype)


def _in_proj(x, g, sh, sc, w_bf, head_gain, ones_blk, tm):
    b, l, d = x.shape
    n = w_bf.shape[1]
    tn = ones_blk.shape[0]
    nb = sh.shape[0]
    mod_map = (lambda bi, i, j: (bi, 0, 0)) if nb > 1 else (lambda bi, i, j: (0, 0, 0))
    return pl.pallas_call(
        functools.partial(_in_proj_kernel, head_dim=tn // NA_HEADS),
        grid=(b, l // tm, n // tn),
        in_specs=[pl.BlockSpec((None, tm, d), lambda bi, i, j: (bi, i, 0)),
                  pl.BlockSpec((1, d), lambda bi, i, j: (0, 0)),
                  pl.BlockSpec((None, 1, d), mod_map),
                  pl.BlockSpec((None, 1, d), mod_map),
                  pl.BlockSpec((d, tn), lambda bi, i, j: (0, j)),
                  pl.BlockSpec((None, 1, tn), lambda bi, i, j: (jnp.clip(j - 1, 0, 1), 0, 0)),
                  pl.BlockSpec((tn, tn), lambda bi, i, j: (0, 0))],
        out_specs=pl.BlockSpec((None, tm, tn), lambda bi, i, j: (bi, i, j)),
        out_shape=jax.ShapeDtypeStruct((b, l, n), BF16),
        scratch_shapes=[pltpu.VMEM((tm, d), BF16)],
        compiler_params=_cparams(("parallel", "parallel", "arbitrary")),
        name="in_proj",
    )(x, g.reshape(1, d), sh, sc, w_bf, head_gain, ones_blk)


def _bmm_kernel(a_ref, w_ref, o_ref):
    o_ref[...] = jnp.dot(a_ref[...].astype(BF16), w_ref[...], preferred_element_type=F32).astype(o_ref.dtype)


def _bmm_add_kernel(a_ref, w_ref, y_ref, o_ref):
    o_ref[...] = (y_ref[...] + jnp.dot(a_ref[...].astype(BF16), w_ref[...],
                                       preferred_element_type=F32)).astype(o_ref.dtype)


def _bmm(a, w, add=None, tm=None, name="bmm"):
    g, m, k = a.shape
    n = w.shape[2]
    tm = m if tm is None else tm
    in_specs = [pl.BlockSpec((None, tm, k), lambda gi, i: (gi, i, 0)),
                pl.BlockSpec((None, k, n), lambda gi, i: (gi, 0, 0))]
    args = [a, w]
    kern = _bmm_kernel
    if add is not None:
        in_specs.append(pl.BlockSpec((None, tm, n), lambda gi, i: (gi, i, 0)))
        args.append(add)
        kern = _bmm_add_kernel
    return pl.pallas_call(
        kern,
        grid=(g, m // tm),
        in_specs=in_specs,
        out_specs=pl.BlockSpec((None, tm, n), lambda gi, i: (gi, i, 0)),
        out_shape=jax.ShapeDtypeStruct((g, m, n), F32),
        compiler_params=_cparams(("parallel", "parallel")),
        name=name,
    )(*args)


def _gelu_tanh(x):
    return 0.5 * x * (1.0 + jnp.tanh(math.sqrt(2.0 / math.pi) * (x + 0.044715 * (x * x * x))))


def _s5_out_kernel(y_ref, u_ref, d_ref, w_ref, o_ref):
    y = y_ref[...].astype(F32) + d_ref[...] * u_ref[...].astype(F32)
    z = _gelu_tanh(y)
    gate = jax.nn.sigmoid(jnp.dot(z.astype(BF16), w_ref[...], preferred_element_type=F32))
    o_ref[...] = (z * gate).astype(o_ref.dtype)


def _s5_out(y, u, d, w_glu_bf, tm):
    m, w = y.shape
    return pl.pallas_call(
        _s5_out_kernel,
        grid=(m // tm,),
        in_specs=[pl.BlockSpec((tm, w), lambda i: (i, 0)),
                  pl.BlockSpec((tm, w), lambda i: (i, 0)),
                  pl.BlockSpec((1, w), lambda i: (0, 0)),
                  pl.BlockSpec((w, w), lambda i: (0, 0))],
        out_specs=pl.BlockSpec((tm, w), lambda i: (i, 0)),
        out_shape=jax.ShapeDtypeStruct((m, w), BF16),
        compiler_params=_cparams(("parallel",)),
        name="s5_out",
    )(y, u, d.reshape(1, w), w_glu_bf)


def _s5_tables(p):
    t = S5_CHUNK
    mats, sums, reads, steps = [], [], [], []
    for d in range(2):
        lam_re = p['ssm_lam_re'][d].astype(F32)
        lam_im = p['ssm_lam_im'][d].astype(F32)
        step = jnp.exp(p['ssm_log_step'][d].astype(F32))[:, None]
        mag = jnp.exp(lam_re * step)
        a_re = mag * jnp.cos(lam_im * step)
        a_im = mag * jnp.sin(lam_im * step)
        den = lam_re * lam_re + lam_im * lam_im
        num_re = a_re - 1.0
        coef_re = (num_re * lam_re + a_im * lam_im) / den
        coef_im = (a_im * lam_re - num_re * lam_im) / den
        b_re = p['ssm_b_re'][d].astype(F32)
        b_im = p['ssm_b_im'][d].astype(F32)
        bb_re = coef_re[..., None] * b_re - coef_im[..., None] * b_im
        bb_im = coef_re[..., None] * b_im + coef_im[..., None] * b_re
        c_re = p['ssm_c_re'][d].astype(F32)
        c_im = p['ssm_c_im'][d].astype(F32)
        n = jnp.arange(t + 1, dtype=F32)[:, None, None]
        pm = jnp.exp(lam_re * step * n)
        pw_re = pm * jnp.cos(lam_im * step * n)
        pw_im = pm * jnp.sin(lam_im * step * n)
        ca_re = c_re[None] * pw_re[:, :, None, :] - c_im[None] * pw_im[:, :, None, :]
        ca_im = c_re[None] * pw_im[:, :, None, :] + c_im[None] * pw_re[:, :, None, :]
        ktau = (jnp.einsum('ngop,gpi->ngoi', ca_re[:t], bb_re)
                - jnp.einsum('ngop,gpi->ngoi', ca_im[:t], bb_im))
        s_idx = np.arange(t)[:, None]
        t_idx = np.arange(t)[None, :]
        lag = (t_idx - s_idx) if d == 0 else (s_idx - t_idx)
        valid = jnp.asarray(lag >= 0, F32)
        blk = ktau[np.clip(lag, 0, t - 1)] * valid[:, :, None, None, None]
        mats.append(jnp.transpose(blk, (2, 0, 4, 1, 3)))
        pows = (t - 1 - np.arange(t)) if d == 0 else np.arange(t)
        sw_re = pw_re[pows][:, :, :, None] * bb_re[None] - pw_im[pows][:, :, :, None] * bb_im[None]
        sw_im = pw_re[pows][:, :, :, None] * bb_im[None] + pw_im[pows][:, :, :, None] * bb_re[None]
        sums.append((jnp.transpose(sw_re, (1, 0, 3, 2)), jnp.transpose(sw_im, (1, 0, 3, 2))))
        rp = (np.arange(t) + 1) if d == 0 else (t - np.arange(t))
        reads.append((jnp.transpose(ca_re[rp], (1, 3, 0, 2)), -jnp.transpose(ca_im[rp], (1, 3, 0, 2))))
        steps.append((pw_re[t], pw_im[t]))
    g, pdim = steps[0][0].shape
    k = SSM_GROUP
    mtot = (mats[0] + mats[1]).reshape(g, t * k, t * k)
    ssum = jnp.concatenate([sums[0][0], sums[0][1], sums[1][0], sums[1][1]], axis=-1).reshape(g, t * k, 4 * pdim)
    r = jnp.concatenate([reads[0][0], reads[0][1], reads[1][0], reads[1][1]], axis=1).reshape(g, 4 * pdim, t * k)
    tk = t * k
    eye = jnp.eye(2, dtype=F32)
    mt = mtot.reshape(g // 2, 2, tk, tk)
    ymat = jnp.einsum('gjts,jk->gjtks', mt, eye).reshape(g // 2, 2 * tk, 2 * tk)
    smat = jnp.einsum('gjtqp,jk->gjtqkp', ssum.reshape(g // 2, 2, tk, 4, pdim), eye).reshape(g // 2, 2 * tk, 8 * pdim)
    w1 = jnp.concatenate([ymat, smat], axis=-1)
    r2 = jnp.einsum('gjqpt,jk->gqjpkt', r.reshape(g // 2, 2, 4, pdim, tk), eye).reshape(g // 2, 8 * pdim, 2 * tk)
    coef = [c.reshape(1, g * pdim) for c in (steps[0][0], steps[0][1], steps[1][0], steps[1][1])]
    return w1.astype(BF16), r2.astype(BF16), coef


def _s5_chunk_kernel(u_ref, w_ref, y_ref, xfr_ref, xfi_ref, xbr_ref, xbi_ref):
    r = jnp.dot(u_ref[...], w_ref[...], preferred_element_type=F32)
    ny = y_ref.shape[-1]
    nx = xfr_ref.shape[-1]
    y_ref[...] = r[:, :ny]
    for i, ref in enumerate((xfr_ref, xfi_ref, xbr_ref, xbi_ref)):
        ref[...] = r[:, ny + i * nx:ny + (i + 1) * nx]


def _s5_chunk(ug, w1):
    gp, m, tk2 = ug.shape
    nx = (w1.shape[2] - tk2) // 4
    xspec = pl.BlockSpec((m, nx), lambda g: (0, g))
    xshape = jax.ShapeDtypeStruct((m, gp * nx), F32)
    return pl.pallas_call(
        _s5_chunk_kernel,
        grid=(gp,),
        in_specs=[pl.BlockSpec((None, m, tk2), lambda g: (g, 0, 0)),
                  pl.BlockSpec((None, tk2, w1.shape[2]), lambda g: (g, 0, 0))],
        out_specs=[pl.BlockSpec((None, m, tk2), lambda g: (g, 0, 0)), xspec, xspec, xspec, xspec],
        out_shape=[jax.ShapeDtypeStruct((gp, m, tk2), F32), xshape, xshape, xshape, xshape],
        compiler_params=_cparams(("parallel",)),
        name="s5_chunk",
    )(ug, w1)


def _s5_scan_kernel(xfr, xfi, xbr, xbi, afr, afi, abr, abi, hfr, hfi, hbr, hbi, *, tiles_ctx, tiles, bsz):
    shape = (2 * bsz, xfr.shape[1])
    low = lax.broadcasted_iota(jnp.int32, shape, 0) < bsz
    zero = jnp.zeros(shape, F32)

    def step(hr, hi, ar, ai, xr, xi):
        return ar * hr - ai * hi + xr, ar * hi + ai * hr + xi

    def fwd_tile(i, carry):
        ar, ai = jnp.broadcast_to(afr[...], shape), jnp.broadcast_to(afi[...], shape)
        r0 = pl.multiple_of(i * 2 * bsz, 2 * bsz)
        xr, xi = xfr[pl.ds(r0, 2 * bsz), :], xfi[pl.ds(r0, 2 * bsz), :]
        sr, si = pltpu.roll(carry[0], bsz, 0), pltpu.roll(carry[1], bsz, 0)
        tr, ti = step(sr, si, ar, ai, xr, xi)
        t2r, t2i = pltpu.roll(tr, bsz, 0), pltpu.roll(ti, bsz, 0)
        hfr[pl.ds(r0, 2 * bsz), :] = jnp.where(low, sr, t2r)
        hfi[pl.ds(r0, 2 * bsz), :] = jnp.where(low, si, t2i)
        return step(t2r, t2i, ar, ai, xr, xi)

    lax.fori_loop(0, tiles, fwd_tile, (zero, zero))

    def bwd_tile(i, carry):
        ar, ai = jnp.broadcast_to(abr[...], shape), jnp.broadcast_to(abi[...], shape)
        r0 = pl.multiple_of(i * 2 * bsz, 2 * bsz)
        xr, xi = xbr[pl.ds(r0, 2 * bsz), :], xbi[pl.ds(r0, 2 * bsz), :]
        sr, si = pltpu.roll(carry[0], bsz, 0), pltpu.roll(carry[1], bsz, 0)
        tr, ti = step(sr, si, ar, ai, xr, xi)
        t2r, t2i = pltpu.roll(tr, bsz, 0), pltpu.roll(ti, bsz, 0)
        hbr[pl.ds(r0, 2 * bsz), :] = jnp.where(low, t2r, sr)
        hbi[pl.ds(r0, 2 * bsz), :] = jnp.where(low, t2i, si)
        return step(t2r, t2i, ar, ai, xr, xi)

    carry = lax.fori_loop(0, tiles_ctx, lambda j, c: bwd_tile(tiles_ctx - 1 - j, c), (zero, zero))
    lax.fori_loop(0, tiles - tiles_ctx, lambda j, c: bwd_tile(tiles - 1 - j, c), carry)


def _s5_scan(xs, coef, chunks_ctx, chunks, bsz):
    m, lanes = xs[0].shape
    assert 2 * bsz == 8 and chunks % 2 == 0 and chunks_ctx % 2 == 0
    lb = min(256, lanes)
    xspec = pl.BlockSpec((m, lb), lambda j: (0, j))
    cspec = pl.BlockSpec((1, lb), lambda j: (0, j))
    shape = jax.ShapeDtypeStruct((m, lanes), F32)
    return pl.pallas_call(
        functools.partial(_s5_scan_kernel, tiles_ctx=chunks_ctx // 2, tiles=chunks // 2, bsz=bsz),
        grid=(lanes // lb,),
        in_specs=[xspec] * 4 + [cspec] * 4,
        out_specs=[xspec] * 4,
        out_shape=[shape] * 4,
        compiler_params=_cparams(("parallel",)),
        name="s5_scan",
    )(*xs, *coef)


def _s5_readout_kernel(y_ref, hfr, hfi, hbr, hbi, r_ref, o_ref):
    h = jnp.concatenate([hfr[...], hfi[...], hbr[...], hbi[...]], axis=1).astype(BF16)
    o_ref[...] = (y_ref[...] + jnp.dot(h, r_ref[...], preferred_element_type=F32)).astype(o_ref.dtype)


def _s5_readout(y0, hs, r2):
    gp, m, tk2 = y0.shape
    nx = hs[0].shape[1] // gp
    hspec = pl.BlockSpec((m, nx), lambda g: (0, g))
    yspec = pl.BlockSpec((None, m, tk2), lambda g: (g, 0, 0))
    return pl.pallas_call(
        _s5_readout_kernel,
        grid=(gp,),
        in_specs=[yspec, hspec, hspec, hspec, hspec, pl.BlockSpec((None, 4 * nx, tk2), lambda g: (g, 0, 0))],
        out_specs=yspec,
        out_shape=jax.ShapeDtypeStruct((gp, m, tk2), BF16),
        compiler_params=_cparams(("parallel",)),
        name="s5_readout",
    )(y0, *hs, r2)


def _s5_branch(u_ctx, u_lat, p, tables, need_ctx):
    w1, r2, coef = tables
    b, l, w = u_lat.shape
    lc = u_ctx.shape[1]
    t, k = S5_CHUNK, SSM_GROUP
    g = w // k
    nc = (lc + l) // t
    m = nc * b
    u_all = jnp.concatenate([u_ctx, u_lat], axis=1)
    ug = jnp.transpose(u_all.reshape(b, nc, t, g // 2, 2, k), (3, 1, 0, 4, 2, 5)).reshape(g // 2, m, 2 * t * k)
    y0, *xs = _s5_chunk(ug, w1)
    hs = _s5_scan(xs, coef, lc // t, nc, b)
    y = _s5_readout(y0, hs, r2).reshape(g // 2, nc, b, 2, t, k)
    y = jnp.transpose(y, (2, 1, 4, 0, 3, 5)).reshape(b, nc * t, w)
    w_glu = p['ssm_w_glu'].astype(BF16)
    out_l = _s5_out(y[:, lc:].reshape(b * l, w), u_lat.reshape(b * l, w), p['ssm_d'], w_glu, tm=512)
    out_c = None
    if need_ctx:
        out_c = _s5_out(y[:, :lc].reshape(b * lc, w), u_ctx.reshape(b * lc, w), p['ssm_d'], w_glu,
                        tm=min(512, b * lc)).reshape(b, lc, w)
    return out_l.reshape(b, l, w), out_c


def _na_kernel(q_ref, k_ref, v_ref, kc_ref, vc_ref, bias_ref, o_ref, s_scr, p_scr, *, rows, head_dim):
    r = pl.program_id(1)
    kr = NA_WIN_ROWS
    start = jnp.clip(r - kr // 2, 0, rows - kr)
    cls = r - start
    base = pl.multiple_of(start * GRID_W, GRID_W)
    kwin = k_ref[pl.ds(base, kr * GRID_W), :]
    vwin = v_ref[pl.ds(base, kr * GRID_W), :]
    q = q_ref[...]
    kc = kc_ref[...]
    vc = vc_ref[...]
    outs = []
    nt = (((1,), (1,)), ((), ()))
    lane_tile = 2 * head_dim
    nwin = kwin.shape[0]
    first_head = lax.broadcasted_iota(jnp.int32, (q.shape[0], lane_tile), 1) < head_dim
    for h in range(NA_HEADS):
        sl = slice(h // 2 * lane_tile, (h // 2 + 1) * lane_tile)
        qh = jnp.where(first_head == (h % 2 == 0), q[:, sl], jnp.zeros_like(q[:, sl]))
        s_scr[h, :, :nwin] = lax.dot_general(qh, kwin[:, sl], nt, preferred_element_type=F32) + bias_ref[cls, h]
        s_scr[h, :, nwin:] = lax.dot_general(qh, kc[:, sl], nt, preferred_element_type=F32)
    inv = []
    for h in range(NA_HEADS):
        s = s_scr[h]
        pr = jnp.exp(s - jnp.max(s, axis=-1, keepdims=True))
        inv.append(1.0 / jnp.sum(pr, axis=-1, keepdims=True))
        p_scr[h] = pr.astype(BF16)
    for pair in range(NA_HEADS // 2):
        sl = slice(pair * lane_tile, (pair + 1) * lane_tile)
        halves = []
        for half in range(2):
            h = 2 * pair + half
            o = (jnp.dot(p_scr[h, :, :nwin], vwin[:, sl], preferred_element_type=F32)
                 + jnp.dot(p_scr[h, :, nwin:], vc[:, sl], preferred_element_type=F32))
            halves.append(o * inv[h])
        outs.append(jnp.where(first_head, halves[0], halves[1]))
    o_ref[...] = jnp.concatenate(outs, axis=-1).astype(o_ref.dtype)


def _na_bias_table(rpb):
    kr, kw, w = NA_WIN_ROWS, NA_WIN_COLS, GRID_W
    col = np.arange(w)
    cstart = np.clip(col - kw // 2, 0, w - kw)
    kcol = np.arange(w)
    inwin = (kcol[None, :] >= cstart[:, None]) & (kcol[None, :] < cstart[:, None] + kw)
    dc = np.clip(kcol[None, :] - col[:, None] + (kw - 1), 0, 2 * kw - 2)
    rpb = rpb.astype(F32)
    tabs = []
    for o in range(kr):
        dr = np.arange(kr) - o + (kr - 1)
        bias = rpb[:, dr][:, :, dc]
        bias = jnp.where(jnp.asarray(inwin)[None, None], bias, MASK_VALUE)
        tabs.append(jnp.transpose(bias, (0, 2, 1, 3)).reshape(rpb.shape[0], w, kr * w))
    return jnp.stack(tabs, axis=0)


def _na_latent(q, k, v, kc, vc, bias_tab):
    b, l, w = q.shape
    lc = kc.shape[1]
    rows = l // GRID_W
    assert rows >= NA_WIN_ROWS
    nk = NA_WIN_ROWS * GRID_W
    return pl.pallas_call(
        functools.partial(_na_kernel, rows=rows, head_dim=w // NA_HEADS),
        grid=(b, rows),
        in_specs=[pl.BlockSpec((None, GRID_W, w), lambda bi, r: (bi, r, 0)),
                  pl.BlockSpec((None, l, w), lambda bi, r: (bi, 0, 0)),
                  pl.BlockSpec((None, l, w), lambda bi, r: (bi, 0, 0)),
                  pl.BlockSpec((None, lc, w), lambda bi, r: (bi, 0, 0)),
                  pl.BlockSpec((None, lc, w), lambda bi, r: (bi, 0, 0)),
                  pl.BlockSpec((NA_WIN_ROWS, NA_HEADS, GRID_W, nk), lambda bi, r: (0, 0, 0, 0))],
        out_specs=pl.BlockSpec((None, GRID_W, w), lambda bi, r: (bi, r, 0)),
        out_shape=jax.ShapeDtypeStruct((b, l, w), BF16),
        scratch_shapes=[pltpu.VMEM((NA_HEADS, GRID_W, nk + lc), F32), pltpu.VMEM((NA_HEADS, GRID_W, nk + lc), BF16)],
        compiler_params=_cparams(("parallel", "arbitrary"), vmem_mb=56),
        name="na_latent",
    )(q, k, v, kc, vc, bias_tab)


def _na_ctx_kernel(q_ref, k_ref, v_ref, o_ref, *, head_dim):
    q, k, v = q_ref[...], k_ref[...], v_ref[...]
    outs = []
    for h in range(NA_HEADS):
        sl = slice(h * head_dim, (h + 1) * head_dim)
        s = lax.dot_general(q[:, sl], k[:, sl], (((1,), (1,)), ((), ())), preferred_element_type=F32)
        pr = jnp.exp(s - jnp.max(s, axis=-1, keepdims=True))
        den = jnp.sum(pr, axis=-1, keepdims=True)
        outs.append(jnp.dot(pr.astype(BF16), v[:, sl], preferred_element_type=F32) / den)
    o_ref[...] = jnp.concatenate(outs, axis=-1).astype(o_ref.dtype)


def _na_context(q, k, v):
    b, lc, w = q.shape
    spec = pl.BlockSpec((None, lc, w), lambda bi: (bi, 0, 0))
    return pl.pallas_call(
        functools.partial(_na_ctx_kernel, head_dim=w // NA_HEADS),
        grid=(b,),
        in_specs=[spec, spec, spec],
        out_specs=spec,
        out_shape=jax.ShapeDtypeStruct((b, lc, w), BF16),
        compiler_params=_cparams(("parallel",)),
        name="na_context",
    )(q, k, v)


def _hyena_filter(length, p):
    t = jnp.linspace(0.0, 1.0, length, dtype=F32)[:, None]
    freqs = jnp.linspace(1e-4, HY_BANDS - 1, HY_BANDS, dtype=F32)
    ang = (2.0 * math.pi / length) * jnp.arange(length, dtype=F32)[:, None] * freqs[None, :]
    z = jnp.concatenate([t, jnp.cos(ang), -jnp.sin(ang)], axis=-1)
    fr = p['hy_freq'].astype(F32)
    hp = lax.Precision.HIGHEST
    h = jnp.sin(fr * (jnp.dot(z, p['hy_w1'].astype(F32), precision=hp) + p['hy_b1'].astype(F32)))
    h = jnp.sin(fr * (jnp.dot(h, p['hy_w2'].astype(F32), precision=hp) + p['hy_b2'].astype(F32)))
    h = jnp.sin(fr * (jnp.dot(h, p['hy_w3'].astype(F32), precision=hp) + p['hy_b3'].astype(F32)))
    wdim = p['hy_bias'].shape[-1]
    h = jnp.dot(h, p['hy_w4'].astype(F32), precision=hp).reshape(length, HY_ORDER, 2, wdim)
    decay = jnp.exp(-t * jnp.abs(jnp.linspace(HY_MIN_DECAY, HY_MAX_DECAY, wdim, dtype=F32)))
    h = h * decay[:, None, None, :]
    filt = jnp.concatenate([h[:, :, 0], jnp.zeros((1, HY_ORDER, wdim), F32), h[:0:-1, :, 1]], axis=0)
    filt = filt / jnp.sum(jnp.abs(filt), axis=0, keepdims=True)
    return filt.reshape(2 * length, HY_ORDER * wdim)


def _block2(re, im):
    return np.block([[re, -im], [im, re]])


def _fft_consts(length):
    n = 2 * length
    n2 = HY_FAST
    n1 = n // n2
    h1 = n1 // 2
    a1 = -2.0 * np.pi * np.outer(np.arange(n1), np.arange(n1)) / n1
    a2 = -2.0 * np.pi * np.outer(np.arange(n2), np.arange(n2)) / n2
    f1 = _block2(np.cos(a1[:, :h1]), np.sin(a1[:, :h1]))
    f1_real = np.concatenate([np.cos(a1), np.sin(a1)], axis=0)
    f3 = _block2(np.cos(a2), np.sin(a2))
    g3 = _block2(np.cos(a2), -np.sin(a2))
    g1 = _block2(np.cos(a1[:h1, :]) / n, -np.sin(a1[:h1, :]) / n)
    wa = -2.0 * np.pi * np.arange(n2) / n
    wr = np.broadcast_to(np.cos(wa)[:, None], (n2, 128))
    wi = np.broadcast_to(np.sin(wa)[:, None], (n2, 128))
    cast = lambda m: jnp.asarray(m, BF16)
    return dict(n1=n1, h1=h1, f1=cast(f1), f1_real=cast(f1_real), f3=cast(f3), g3=cast(g3), g1=cast(g1),
                wr=jnp.asarray(wr, F32), wi=jnp.asarray(wi, F32))


def _fft_first_stage(a_r, a_i, load_cols, f1_ref, n1, cb):
    slab = HY_FAST + HY_PAD

    def body(i, carry):
        n2 = 2 * i
        z = jnp.concatenate([load_cols(n2), load_cols(n2 + 1)], axis=1)
        r = jnp.dot(f1_ref[...], z, preferred_element_type=F32)
        a_r[pl.ds(n2, n1, stride=slab), :] = r[:n1, :cb]
        a_i[pl.ds(n2, n1, stride=slab), :] = r[n1:, :cb]
        a_r[pl.ds(n2 + 1, n1, stride=slab), :] = r[:n1, cb:]
        a_i[pl.ds(n2 + 1, n1, stride=slab), :] = r[n1:, cb:]
        return carry

    lax.fori_loop(0, HY_FAST // 2, body, 0)


def _fft_slab_loop(a_r, a_i, f3_ref, wr_ref, wi_ref, n1, cb, finish):
    slab = HY_FAST + HY_PAD
    n2 = HY_FAST
    wr, wi = wr_ref[...], wi_ref[...]
    if cb != wr.shape[1]:
        wr, wi = jnp.tile(wr, (1, cb // wr.shape[1])), jnp.tile(wi, (1, cb // wi.shape[1]))

    def body(i, carry):
        tr, ti = carry
        k1 = 2 * i
        r0 = pl.multiple_of(k1 * slab, 8)
        r1 = pl.multiple_of((k1 + 1) * slab, 8)
        tr1, ti1 = tr * wr - ti * wi, tr * wi + ti * wr
        ar0, ai0 = a_r[pl.ds(r0, n2), :], a_i[pl.ds(r0, n2), :]
        ar1, ai1 = a_r[pl.ds(r1, n2), :], a_i[pl.ds(r1, n2), :]
        xr = jnp.concatenate([ar0 * tr - ai0 * ti, ar1 * tr1 - ai1 * ti1], axis=1)
        xi = jnp.concatenate([ar0 * ti + ai0 * tr, ar1 * ti1 + ai1 * tr1], axis=1)
        bc = jnp.dot(f3_ref[...], jnp.concatenate([xr, xi], axis=0).astype(BF16), preferred_element_type=F32)
        finish(k1, (r0, r1), (tr, ti, tr1, ti1), bc[:n2], bc[n2:])
        return tr1 * wr - ti1 * wi, tr1 * wi + ti1 * wr

    one = jnp.ones((n2, cb), F32)
    lax.fori_loop(0, n1 // 2, body, (one, jnp.zeros_like(one)))


def _fft_conv_kernel(y_ref, sr_ref, si_ref, f1_ref, f3_ref, g3_ref, g1_ref, wr_ref, wi_ref, o_ref, a_r, a_i,
                     *, n1):
    h1 = n1 // 2
    n2 = HY_FAST
    cb = o_ref.shape[-1]
    slab = HY_FAST + HY_PAD

    def load_cols(j):
        r = pl.multiple_of(j * h1, h1)
        return jnp.concatenate([y_ref[0, pl.ds(r, h1), :], y_ref[1, pl.ds(r, h1), :]], axis=0)

    _fft_first_stage(a_r, a_i, load_cols, f1_ref, n1, cb)

    def finish(k1, rows, tw, br, bi):
        s0 = pl.multiple_of(k1 * n2, n2)
        s1 = pl.multiple_of((k1 + 1) * n2, n2)
        sr = jnp.concatenate([sr_ref[pl.ds(s0, n2), :], sr_ref[pl.ds(s1, n2), :]], axis=1).astype(F32)
        si = jnp.concatenate([si_ref[pl.ds(s0, n2), :], si_ref[pl.ds(s1, n2), :]], axis=1).astype(F32)
        y = jnp.concatenate([br * sr - bi * si, br * si + bi * sr], axis=0).astype(BF16)
        ac = jnp.dot(g3_ref[...], y, preferred_element_type=F32)
        pr, pi = ac[:n2], ac[n2:]
        for j in range(2):
            tr, ti = tw[2 * j], tw[2 * j + 1]
            prj, pij = pr[:, j * cb:(j + 1) * cb], pi[:, j * cb:(j + 1) * cb]
            a_r[pl.ds(rows[j], n2), :] = prj * tr + pij * ti
            a_i[pl.ds(rows[j], n2), :] = pij * tr - prj * ti

    _fft_slab_loop(a_r, a_i, f3_ref, wr_ref, wi_ref, n1, cb, finish)

    def out_body(i, carry):
        j0 = 2 * i
        cols = []
        for j in (j0, j0 + 1):
            cols.append(jnp.concatenate([a_r[pl.ds(j, n1, stride=slab), :], a_i[pl.ds(j, n1, stride=slab), :]],
                                        axis=0))
        o = jnp.dot(g1_ref[...], jnp.concatenate(cols, axis=1).astype(BF16), preferred_element_type=F32)
        for jj in range(2):
            r = pl.multiple_of((j0 + jj) * h1, h1)
            o_ref[0, pl.ds(r, h1), :] = o[:h1, jj * cb:(jj + 1) * cb].astype(o_ref.dtype)
            o_ref[1, pl.ds(r, h1), :] = o[h1:, jj * cb:(jj + 1) * cb].astype(o_ref.dtype)
        return carry

    lax.fori_loop(0, n2 // 2, out_body, 0)


def _fft_conv(yp, col0, spec_r, spec_i, scol0, consts, cb=128):
    b, l, _ = yp.shape
    n = spec_r.shape[0]
    n1 = consts['n1']
    slab = HY_FAST + HY_PAD
    nblk = consts['nblk']
    y4 = yp.reshape(b // 2, 2, l, yp.shape[-1])
    const = lambda a: pl.BlockSpec(a.shape, lambda j, pi: (0, 0))
    return pl.pallas_call(
        functools.partial(_fft_conv_kernel, n1=n1),
        grid=(nblk, b // 2),
        in_specs=[pl.BlockSpec((None, 2, l, cb), lambda j, pi: (pi, 0, 0, col0 + j)),
                  pl.BlockSpec((n, cb), lambda j, pi: (0, scol0 + j)),
                  pl.BlockSpec((n, cb), lambda j, pi: (0, scol0 + j)),
                  const(consts['f1']), const(consts['f3']), const(consts['g3']), const(consts['g1']),
                  const(consts['wr']), const(consts['wi'])],
        out_specs=pl.BlockSpec((None, 2, l, cb), lambda j, pi: (pi, 0, 0, j)),
        out_shape=jax.ShapeDtypeStruct((b // 2, 2, l, nblk * cb), BF16),
        scratch_shapes=[pltpu.VMEM((n1 * slab, cb), F32), pltpu.VMEM((n1 * slab, cb), F32)],
        compiler_params=_cparams(("arbitrary", "arbitrary"), vmem_mb=58),
        name="fft_conv",
    )(y4, spec_r, spec_i, consts['f1'], consts['f3'], consts['g3'], consts['g1'],
      consts['wr'], consts['wi']).reshape(b, l, nblk * cb)


def _fft_spec_kernel(f_ref, f1_ref, f3_ref, wr_ref, wi_ref, sr_ref, si_ref, a_r, a_i, *, n1):
    n2 = HY_FAST
    cb = sr_ref.shape[-1]

    def load_cols(j):
        return f_ref[pl.ds(pl.multiple_of(j * n1, n1), n1), :]

    _fft_first_stage(a_r, a_i, load_cols, f1_ref, n1, cb)

    def finish(k1, rows, tw, br, bi):
        for j in range(2):
            s = pl.multiple_of((k1 + j) * n2, n2)
            sr_ref[pl.ds(s, n2), :] = br[:, j * cb:(j + 1) * cb].astype(sr_ref.dtype)
            si_ref[pl.ds(s, n2), :] = bi[:, j * cb:(j + 1) * cb].astype(si_ref.dtype)

    _fft_slab_loop(a_r, a_i, f3_ref, wr_ref, wi_ref, n1, cb, finish)


def _fft_spectrum(filt_p, consts, cb=128):
    n, c = filt_p.shape
    n1 = consts['n1']
    slab = HY_FAST + HY_PAD
    const = lambda a: pl.BlockSpec(a.shape, lambda j: (0, 0))
    spec = pl.BlockSpec((n, cb), lambda j: (0, j))
    return pl.pallas_call(
        functools.partial(_fft_spec_kernel, n1=n1),
        grid=(c // cb,),
        in_specs=[spec, const(consts['f1_real']), const(consts['f3']), const(consts['wr']), const(consts['wi'])],
        out_specs=[spec, spec],
        out_shape=[jax.ShapeDtypeStruct((n, c), BF16)] * 2,
        scratch_shapes=[pltpu.VMEM((n1 * slab, cb), F32), pltpu.VMEM((n1 * slab, cb), F32)],
        compiler_params=_cparams(("arbitrary",), vmem_mb=58),
        name="fft_spectrum",
    )(filt_p, consts['f1_real'], consts['f3'], consts['wr'], consts['wi'])


def _dense_conv_kernel(y_ref, sr_ref, si_ref, f_ref, g_ref, o_ref):
    l = y_ref.shape[1]
    n = sr_ref.shape[0]
    bc = jnp.dot(f_ref[...], jnp.concatenate([y_ref[0], y_ref[1]], axis=0), preferred_element_type=F32)
    br, bi = bc[:n], bc[n:]
    sr, si = sr_ref[...], si_ref[...]
    y = jnp.concatenate([br * sr - bi * si, br * si + bi * sr], axis=0).astype(BF16)
    o = jnp.dot(g_ref[...], y, preferred_element_type=F32)
    o_ref[0] = o[:l].astype(o_ref.dtype)
    o_ref[1] = o[l:].astype(o_ref.dtype)


def _dense_conv(y, col0, spec_r, spec_i, scol0, fmat, gmat, nblk, cb=128):
    b, l, _ = y.shape
    n = spec_r.shape[0]
    y4 = y.reshape(b // 2, 2, l, y.shape[-1])
    const = lambda a: pl.BlockSpec(a.shape, lambda j, pi: (0, 0))
    return pl.pallas_call(
        _dense_conv_kernel,
        grid=(nblk, b // 2),
        in_specs=[pl.BlockSpec((None, 2, l, cb), lambda j, pi: (pi, 0, 0, col0 + j)),
                  pl.BlockSpec((n, cb), lambda j, pi: (0, scol0 + j)),
                  pl.BlockSpec((n, cb), lambda j, pi: (0, scol0 + j)),
                  const(fmat), const(gmat)],
        out_specs=pl.BlockSpec((None, 2, l, cb), lambda j, pi: (pi, 0, 0, j)),
        out_shape=jax.ShapeDtypeStruct((b // 2, 2, l, nblk * cb), BF16),
        compiler_params=_cparams(("parallel", "parallel")),
        name="dense_conv",
    )(y4, spec_r, spec_i, fmat, gmat).reshape(b, l, nblk * cb)


def _sconv_kernel(u_ref, w_ref, b_ref, o_ref, *scr, h1):
    u = u_ref[...].astype(F32)
    l = u.shape[0]
    row = lax.broadcasted_iota(jnp.int32, u.shape, 0)
    prev = jnp.where(row == 0, 0.0, pltpu.roll(u, 1, 0))
    nxt = jnp.where(row == l - 1, 0.0, pltpu.roll(u, l - 1, 0))
    z = prev * w_ref[0:1, :] + u * w_ref[1:2, :] + nxt * w_ref[2:3, :] + b_ref[...]
    if h1 is None:
        o_ref[...] = z.astype(o_ref.dtype)
        return
    scr[0][...] = z

    def body(n2, carry):
        o_ref[pl.ds(pl.multiple_of(n2 * h1, h1), h1), :] = scr[0][pl.ds(n2, h1, stride=HY_FAST), :].astype(o_ref.dtype)
        return carry

    lax.fori_loop(0, HY_FAST, body, 0)


def _short_conv(proj, col0, conv_w, conv_b, h1, cb=128):
    b, l, _ = proj.shape
    c = conv_w.shape[1]
    return pl.pallas_call(
        functools.partial(_sconv_kernel, h1=h1),
        scratch_shapes=[] if h1 is None else [pltpu.VMEM((l, cb), F32)],
        grid=(b, c // cb),
        in_specs=[pl.BlockSpec((None, l, cb), lambda bi, j: (bi, 0, col0 + j)),
                  pl.BlockSpec((conv_w.shape[0], cb), lambda bi, j: (0, j)),
                  pl.BlockSpec((1, cb), lambda bi, j: (0, j))],
        out_specs=pl.BlockSpec((None, l, cb), lambda bi, j: (bi, 0, j)),
        out_shape=jax.ShapeDtypeStruct((b, l, c), BF16),
        compiler_params=_cparams(("parallel", "parallel")),
        name="short_conv",
    )(proj, conv_w.astype(F32), conv_b.astype(F32).reshape(1, c))


def _gate_kernel(x_ref, conv_ref, y_ref, bias_ref, o_ref, *scr, h1):
    y = y_ref[...].astype(F32)
    z = x_ref[...].astype(F32) * (conv_ref[...].astype(F32) + y * bias_ref[...])
    if h1 is None:
        o_ref[...] = z.astype(o_ref.dtype)
        return
    scr[0][...] = z

    def body(n1, carry):
        r = pl.multiple_of(n1 * HY_FAST, HY_FAST)
        o_ref[pl.ds(r, HY_FAST), :] = scr[0][pl.ds(n1, HY_FAST, stride=h1), :].astype(o_ref.dtype)
        return carry

    lax.fori_loop(0, h1, body, 0)


def _hy_gate(z, xcol, conv, y, ycol, bias, h1=None, cb=128):
    b, l, w = conv.shape
    return pl.pallas_call(
        functools.partial(_gate_kernel, h1=h1),
        grid=(b, w // cb),
        in_specs=[pl.BlockSpec((None, l, cb), lambda bi, j: (bi, 0, xcol + j)),
                  pl.BlockSpec((None, l, cb), lambda bi, j: (bi, 0, j)),
                  pl.BlockSpec((None, l, cb), lambda bi, j: (bi, 0, ycol + j)),
                  pl.BlockSpec((1, cb), lambda bi, j: (0, j))],
        out_specs=pl.BlockSpec((None, l, cb), lambda bi, j: (bi, 0, j)),
        out_shape=jax.ShapeDtypeStruct((b, l, w), BF16),
        scratch_shapes=[] if h1 is None else [pltpu.VMEM((l, cb), F32)],
        compiler_params=_cparams(("parallel", "parallel")),
        name="hy_gate",
    )(z, conv, y, bias.astype(F32).reshape(1, w))


def _hyena_branch(proj, col0, p, cb=128):
    b, l, _ = proj.shape
    w = p['hy_bias'].shape[-1]
    n = 2 * l
    nblk = w // cb
    filt = _hyena_filter(l, p)
    if n // HY_FAST >= 16:
        consts = dict(_fft_consts(l), nblk=nblk)
        n1, h1 = consts['n1'], consts['h1']
        filt_p = jnp.transpose(filt.astype(BF16).reshape(n1, HY_FAST, -1), (1, 0, 2)).reshape(n, -1)
        spec_r, spec_i = _fft_spectrum(filt_p, consts, cb)
        conv = lambda y, ycol, order: _fft_conv(y, ycol * nblk, spec_r, spec_i, order * nblk, consts, cb)
    else:
        h1 = None
        k = np.arange(n)
        ang = -2.0 * np.pi * np.outer(k, k) / n
        fmat = jnp.asarray(_block2(np.cos(ang[:, :l]), np.sin(ang[:, :l])), BF16)
        gmat = jnp.asarray(_block2(np.cos(ang[:l, :]) / n, -np.sin(ang[:l, :]) / n), BF16)
        f_real = jnp.asarray(np.concatenate([np.cos(ang), np.sin(ang)], axis=0), F32)
        spec = _bmm(f_real[None], filt.astype(BF16)[None], name="dense_spectrum")[0]
        spec_r, spec_i = spec[:n], spec[n:]
        conv = lambda y, ycol, order: _dense_conv(y, ycol * nblk, spec_r, spec_i, order * nblk, fmat, gmat,
                                                  nblk, cb)
    z = _short_conv(proj, col0, p['hy_conv_w'], p['hy_conv_b'], h1, cb)
    c0 = conv(z, HY_ORDER, 0)
    y1 = _hy_gate(z, 0, c0, z, HY_ORDER * nblk, p['hy_bias'][0], None, cb)
    c1 = conv(y1, 0, 1)
    return _hy_gate(z, nblk, c1, y1, 0, p['hy_bias'][1], h1, cb)


def _merge_kernel(x_ref, g_ref, sh_ref, sc_ref, g1_ref, br_ref, wg0_ref, wg1_ref, wg2_ref, bg_ref, wb_ref, wo_ref,
                  o_ref, h_scr, acc_scr):
    j = pl.program_id(2)

    @pl.when(j == 0)
    def _():
        h_scr[...] = _norm_mod(x_ref[...], g_ref[...], sh_ref[...], sc_ref[...]).astype(BF16)
        acc_scr[...] = jnp.zeros_like(acc_scr)

    h = h_scr[...]
    mixed = None
    for i, wg_ref in enumerate((wg0_ref, wg1_ref, wg2_ref)):
        gate = jax.nn.sigmoid(jnp.dot(h, wg_ref[...], preferred_element_type=F32) + bg_ref[i])
        term = gate * jnp.dot(br_ref[i], wb_ref[i], preferred_element_type=F32)
        mixed = term if mixed is None else mixed + term
    acc_scr[...] += jnp.dot(mixed.astype(BF16), wo_ref[...], preferred_element_type=F32)

    @pl.when(j == pl.num_programs(2) - 1)
    def _():
        o_ref[...] = x_ref[...] + g1_ref[...] * acc_scr[...]


def _merge(x, g, sh, sc, g1, branches, w_gate_bf, b_gate, w_branch_bf, w_out_bf, tm, tc):
    b, l, d = x.shape
    nbr, w, _ = w_branch_bf.shape
    nb = sh.shape[0]
    nj = d // tc
    mod_map = (lambda bi, i, j: (bi, 0, 0)) if nb > 1 else (lambda bi, i, j: (0, 0, 0))
    br = jnp.stack(branches, axis=1)
    bg = b_gate.reshape(nbr, nj, 1, tc).transpose(1, 0, 2, 3)
    return pl.pallas_call(
        _merge_kernel,
        grid=(b, l // tm, nj),
        in_specs=[pl.BlockSpec((None, tm, d), lambda bi, i, j: (bi, i, 0)),
                  pl.BlockSpec((1, d), lambda bi, i, j: (0, 0)),
                  pl.BlockSpec((None, 1, d), mod_map),
                  pl.BlockSpec((None, 1, d), mod_map),
                  pl.BlockSpec((None, 1, d), mod_map),
                  pl.BlockSpec((None, nbr, tm, w), lambda bi, i, j: (bi, 0, i, 0)),
                  pl.BlockSpec((d, tc), lambda bi, i, j: (0, j)),
                  pl.BlockSpec((d, tc), lambda bi, i, j: (0, nj + j)),
                  pl.BlockSpec((d, tc), lambda bi, i, j: (0, 2 * nj + j)),
                  pl.BlockSpec((None, nbr, 1, tc), lambda bi, i, j: (j, 0, 0, 0)),
                  pl.BlockSpec((nbr, w, tc), lambda bi, i, j: (0, 0, j)),
                  pl.BlockSpec((tc, d), lambda bi, i, j: (j, 0))],
        out_specs=pl.BlockSpec((None, tm, d), lambda bi, i, j: (bi, i, 0)),
        out_shape=jax.ShapeDtypeStruct((b, l, d), F32),
        scratch_shapes=[pltpu.VMEM((tm, d), BF16), pltpu.VMEM((tm, d), F32)],
        compiler_params=_cparams(("parallel", "parallel", "arbitrary"), vmem_mb=56),
        name="merge",
    )(x, g.reshape(1, d), sh, sc, g1, br, w_gate_bf, w_gate_bf, w_gate_bf, bg, w_branch_bf, w_out_bf)


def _router_kernel(x_ref, g_ref, sh_ref, sc_ref, rt_ref, h_ref, aff_ref):
    h = _norm_mod(x_ref[...], g_ref[...], sh_ref[...], sc_ref[...])
    h_ref[...] = h.astype(h_ref.dtype)
    nt = (((1,), (1,)), ((), ()))
    rt = rt_ref[...]
    r_hi = rt.astype(BF16)
    r_lo = (rt - r_hi.astype(F32)).astype(BF16)
    h_hi = h.astype(BF16)
    h_lo = (h - h_hi.astype(F32)).astype(BF16)
    logits = (lax.dot_general(r_hi, h_hi, nt, preferred_element_type=F32)
              + lax.dot_general(r_hi, h_lo, nt, preferred_element_type=F32)
              + lax.dot_general(r_lo, h_hi, nt, preferred_element_type=F32))
    e = jnp.exp(logits - jnp.max(logits, axis=0, keepdims=True))
    aff_ref[...] = e / jnp.sum(e, axis=0, keepdims=True)


def _router(x, g, sh, sc, router_t, tm):
    b, l, d = x.shape
    e = router_t.shape[0]
    nb = sh.shape[0]
    mod_map = (lambda bi, i: (bi, 0, 0)) if nb > 1 else (lambda bi, i: (0, 0, 0))
    return pl.pallas_call(
        _router_kernel,
        grid=(b, l // tm),
        in_specs=[pl.BlockSpec((None, tm, d), lambda bi, i: (bi, i, 0)),
                  pl.BlockSpec((1, d), lambda bi, i: (0, 0)),
                  pl.BlockSpec((None, 1, d), mod_map),
                  pl.BlockSpec((None, 1, d), mod_map),
                  pl.BlockSpec((e, d), lambda bi, i: (0, 0))],
        out_specs=[pl.BlockSpec((None, tm, d), lambda bi, i: (bi, i, 0)),
                   pl.BlockSpec((None, e, tm), lambda bi, i: (bi, 0, i))],
        out_shape=[jax.ShapeDtypeStruct((b, l, d), BF16), jax.ShapeDtypeStruct((b, e, l), F32)],
        compiler_params=_cparams(("parallel", "parallel")),
        name="router",
    )(x, g.reshape(1, d), sh, sc, router_t)


def _ffn_kernel(xs_ref, gate_ref, w1_ref, w3_ref, w2_ref, o_ref, acc_scr):
    f = pl.program_id(2)

    @pl.when(f == 0)
    def _():
        acc_scr[...] = jnp.zeros_like(acc_scr)

    xs = xs_ref[...]
    a = jnp.dot(xs, w1_ref[...].astype(BF16), preferred_element_type=F32)
    gl = jnp.dot(xs, w3_ref[...].astype(BF16), preferred_element_type=F32)
    mid = (a * jax.nn.sigmoid(a) * gl).astype(BF16)
    acc_scr[...] += jnp.dot(mid, w2_ref[...].astype(BF16), preferred_element_type=F32)

    @pl.when(f == pl.num_programs(2) - 1)
    def _():
        o_ref[...] = (acc_scr[...] * gate_ref[...]).astype(o_ref.dtype)


def _expert_ffn(xs, gate, w1, w3, w2, layer, tr, tf):
    e, r, d = xs.shape
    ff = w1.shape[3]
    return pl.pallas_call(
        _ffn_kernel,
        grid=(e, r // tr, ff // tf),
        in_specs=[pl.BlockSpec((None, tr, d), lambda ei, i, f: (ei, i, 0)),
                  pl.BlockSpec((None, tr, 1), lambda ei, i, f: (ei, i, 0)),
                  pl.BlockSpec((None, None, d, tf), lambda ei, i, f: (layer, ei, 0, f)),
                  pl.BlockSpec((None, None, d, tf), lambda ei, i, f: (layer, ei, 0, f)),
                  pl.BlockSpec((None, None, tf, d), lambda ei, i, f: (layer, ei, f, 0))],
        out_specs=pl.BlockSpec((None, tr, d), lambda ei, i, f: (ei, i, 0)),
        out_shape=jax.ShapeDtypeStruct((e, r, d), BF16),
        scratch_shapes=[pltpu.VMEM((tr, d), F32)],
        compiler_params=_cparams(("parallel", "parallel", "arbitrary"), vmem_mb=56),
        name="expert_ffn",
    )(xs, gate, w1, w3, w2)


def _combine_kernel(st_ref, x_ref, g2_ref, pos_ref, *refs, n_exp, tiles, tn):
    y_refs, o_ref, oh_scr = refs[:n_exp], refs[n_exp], refs[n_exp + 1]
    bi, j = pl.program_id(0), pl.program_id(1)
    tt, win = oh_scr.shape[1], y_refs[0].shape[1]
    lane = lax.broadcasted_iota(jnp.int32, (tt, win), 1)
    for e in range(n_exp):
        slot = lane + st_ref[(bi * n_exp + e) * tiles + j]
        oh_scr[e] = jnp.where(slot == pos_ref[:, e:e + 1], 1.0, 0.0).astype(BF16)
    for c in range(o_ref.shape[1] // tn):
        cols = slice(c * tn, (c + 1) * tn)
        acc = jnp.dot(oh_scr[0], y_refs[0][0, :, cols], preferred_element_type=F32)
        for e in range(1, n_exp):
            acc = acc + jnp.dot(oh_scr[e], y_refs[e][0, :, cols], preferred_element_type=F32)
        o_ref[:, cols] = x_ref[:, cols] + g2_ref[:, cols] * acc


def _moe_combine(x, g2, pos, y, cap, tt):
    b, l, d = x.shape
    e = y.shape[0]
    tiles = l // tt
    win = min(tt + COMBINE_ALIGN, cap)
    sel = (pos >= 0).reshape(b, tiles, tt, e).sum(2)
    first = jnp.cumsum(sel, axis=1) - sel
    start = jnp.minimum(first // COMBINE_ALIGN * COMBINE_ALIGN, cap - win).astype(jnp.int32)
    start = jnp.transpose(start, (0, 2, 1)).reshape(-1)
    nb = g2.shape[0]

    def y_spec(ei):
        def y_map(bi, j, st):
            return ei, pl.multiple_of(bi * cap + st[(bi * e + ei) * tiles + j], COMBINE_ALIGN), 0
        return pl.BlockSpec((pl.Element(1), pl.Element(win), pl.Element(d)), y_map)

    grid_spec = pltpu.PrefetchScalarGridSpec(
        num_scalar_prefetch=1,
        grid=(b, tiles),
        in_specs=[pl.BlockSpec((None, tt, d), lambda bi, j, st: (bi, j, 0)),
                  pl.BlockSpec((None, 1, d), lambda bi, j, st: (bi if nb > 1 else 0, 0, 0)),
                  pl.BlockSpec((None, tt, e), lambda bi, j, st: (bi, j, 0))] + [y_spec(ei) for ei in range(e)],
        out_specs=pl.BlockSpec((None, tt, d), lambda bi, j, st: (bi, j, 0)),
        scratch_shapes=[pltpu.VMEM((e, tt, win), BF16)])
    return pl.pallas_call(
        functools.partial(_combine_kernel, n_exp=e, tiles=tiles, tn=min(256, d)),
        grid_spec=grid_spec,
        out_shape=jax.ShapeDtypeStruct((b, l, d), F32),
        compiler_params=_cparams(("parallel", "parallel"), vmem_mb=56),
        name="moe_combine",
    )(start, x, g2, pos, *([y] * e))


def _ec_moe_update(x, g, sh, sc, g2, p, tm):
    b, l, d = x.shape
    e = p['router'].shape[1]
    cap = max(1, EC_CAPACITY * l // e)
    h, aff_t = _router(x, g, sh, sc, jnp.transpose(p['router']).astype(F32), tm)
    gate, idx = lax.top_k(aff_t, cap)
    idx, gate = lax.sort_key_val(idx, gate, dimension=-1)
    bidx = jnp.arange(b)[:, None, None]
    eidx = jnp.arange(e)[None, :, None]
    pos = jnp.full((b, l, e), -1, jnp.int32).at[bidx, idx, eidx].set(
        jnp.broadcast_to(jnp.arange(cap, dtype=jnp.int32), idx.shape), unique_indices=True)
    idx_t = jnp.transpose(idx, (1, 0, 2))
    xs = h[jnp.arange(b)[None, :, None], idx_t].reshape(e, b * cap, d)
    gt = jnp.transpose(gate, (1, 0, 2)).reshape(e, b * cap, 1)
    tr = min(1024, b * cap)
    ff = p['exp_w1'].shape[3]
    y = _expert_ffn(xs, gt, p['exp_w1'], p['exp_w3'], p['exp_w2'], p['layer'], tr=tr, tf=min(512, ff))
    return _moe_combine(x, g2, pos, y, cap, min(256, l))


def _layer(x, xc, c, c_ctx, p, last):
    b, l, d = x.shape
    lc = xc.shape[1]
    w = p['ssm_d'].shape[0]
    hd = w // NA_HEADS

    cc = jnp.zeros((8, d), F32).at[:b].set(c).at[b].set(c_ctx)
    mod = _ada(cc, p['w_ada'], p['b_ada'], p['layer'])
    m_l = [t[:, None, :] for t in jnp.split(mod[:b], 6, axis=-1)]
    m_c = [t[:, None, :] for t in jnp.split(mod[b:b + 1], 6, axis=-1)]

    w_in = p['w_in'].astype(BF16)
    scale = hd ** -0.5
    head_gain = jnp.stack([jnp.tile(p['na_q_gain'].astype(F32) * scale, NA_HEADS),
                           jnp.tile(p['na_k_gain'].astype(F32), NA_HEADS)])[:, None, :]
    ones_blk = jnp.asarray(np.kron(np.eye(NA_HEADS), np.ones((hd, hd))), BF16)
    tm_l = min(512, l)
    tm_c = min(512, lc)
    proj = _in_proj(x, p['norm1'], m_l[0], m_l[1], w_in, head_gain, ones_blk, min(1024, l))
    projc = _in_proj(xc, p['norm1'], m_c[0], m_c[1], w_in, head_gain, ones_blk, tm_c)

    ssm_l, ssm_c = _s5_branch(projc[..., :w], proj[..., :w], p, _s5_tables(p), need_ctx=not last)

    na_l = _na_latent(proj[..., w:2 * w], proj[..., 2 * w:3 * w], proj[..., 3 * w:4 * w],
                      projc[..., 2 * w:3 * w], projc[..., 3 * w:4 * w], _na_bias_table(p['na_rpb']))
    hy_l = _hyena_branch(proj, 4 * w // 128, p)

    wg = p['w_gate'].astype(BF16)
    wb = p['w_branch'].astype(BF16)
    wo = p['w_out'].astype(BF16)
    x = _merge(x, p['norm1'], m_l[0], m_l[1], m_l[2], (ssm_l, na_l, hy_l), wg, p['b_gate'], wb, wo,
               tm=tm_l, tc=min(512, d))
    x = _ec_moe_update(x, p['norm2'], m_l[3], m_l[4], m_l[5], p, tm_l)
    if last:
        return x, None
    na_c = _na_context(projc[..., w:2 * w], projc[..., 2 * w:3 * w], projc[..., 3 * w:4 * w])
    hy_c = _hyena_branch(projc, 4 * w // 128, p)
    xc = _merge(xc, p['norm1'], m_c[0], m_c[1], m_c[2], (ssm_c, na_c, hy_c), wg, p['b_gate'], wb, wo,
                tm=tm_c, tc=min(512, d))
    xc = _ec_moe_update(xc, p['norm2'], m_c[3], m_c[4], m_c[5], p, tm_c)
    return x, xc


_PARAM_NAMES = ('w_ada', 'b_ada', 'norm1', 'norm2', 'w_in',
                'ssm_lam_re', 'ssm_lam_im', 'ssm_log_step', 'ssm_b_re', 'ssm_b_im', 'ssm_c_re', 'ssm_c_im',
                'ssm_d', 'ssm_w_glu', 'na_q_gain', 'na_k_gain', 'na_rpb',
                'hy_conv_w', 'hy_conv_b', 'hy_w1', 'hy_b1', 'hy_w2', 'hy_b2', 'hy_w3', 'hy_b3', 'hy_w4',
                'hy_freq', 'hy_bias', 'w_gate', 'b_gate', 'w_branch', 'w_out',
                'router', 'exp_w1', 'exp_w3', 'exp_w2')
_STACKED = ('w_ada', 'b_ada', 'exp_w1', 'exp_w3', 'exp_w2')


def kernel(x, c, ctx, c_ctx, w_ada, b_ada, norm1, norm2, w_in, ssm_lam_re, ssm_lam_im, ssm_log_step, ssm_b_re, ssm_b_im, ssm_c_re, ssm_c_im, ssm_d, ssm_w_glu, na_q_gain, na_k_gain, na_rpb, hy_conv_w, hy_conv_b, hy_w1, hy_b1, hy_w2, hy_b2, hy_w3, hy_b3, hy_w4, hy_freq, hy_bias, w_gate, b_gate, w_branch, w_out, router, exp_w1, exp_w3, exp_w2):
    stacked = (w_ada, b_ada, norm1, norm2, w_in, ssm_lam_re, ssm_lam_im, ssm_log_step, ssm_b_re, ssm_b_im,
               ssm_c_re, ssm_c_im, ssm_d, ssm_w_glu, na_q_gain, na_k_gain, na_rpb, hy_conv_w, hy_conv_b,
               hy_w1, hy_b1, hy_w2, hy_b2, hy_w3, hy_b3, hy_w4, hy_freq, hy_bias, w_gate, b_gate, w_branch,
               w_out, router, exp_w1, exp_w3, exp_w2)
    depth = w_ada.shape[0]
    xc = ctx
    for layer in range(depth):
        p = {name: (t if name in _STACKED else t[layer]) for name, t in zip(_PARAM_NAMES, stacked)}
        p['layer'] = layer
        x, xc = _layer(x, xc, c, c_ctx, p, layer == depth - 1)
    return x
```

```python
import functools
import math

import numpy as np
import jax
import jax.numpy as jnp
from jax import lax
from jax.experimental import pallas as pl
from jax.experimental.pallas import tpu as pltpu

BF16 = jnp.bfloat16
F32 = jnp.float32

NORM_EPS = 1e-6
GRID_W = 64
SSM_GROUP = 16
NA_HEADS = 8
NA_WIN_ROWS = 8
NA_WIN_COLS = 16
HY_ORDER = 2
HY_BANDS = 16
HY_MIN_DECAY = math.log(1e-2) / 1.5
HY_MAX_DECAY = math.log(1e-2) / 0.3
EC_CAPACITY = 2
S5_CHUNK = 16
HY_FAST = 128
HY_PAD = 8
MASK_VALUE = -1e30
COMBINE_ALIGN = 16

V7X_VMEM_BYTES = 64 * 1024 * 1024


def _cparams(semantics, vmem_mb=48):
    return pltpu.CompilerParams(dimension_semantics=semantics, vmem_limit_bytes=vmem_mb * 1024 * 1024)


def _norm_mod(x, g, sh, sc):
    ms = jnp.mean(x * x, axis=-1, keepdims=True)
    return (x * lax.rsqrt(ms + NORM_EPS) * g) * (1.0 + sc) + sh


def _ada_kernel(c_ref, w_ref, b_ref, o_ref):
    c = c_ref[...]
    a = (c * jax.nn.sigmoid(c)).astype(BF16)
    o_ref[...] = jnp.dot(a, w_ref[...].astype(BF16), preferred_element_type=F32) + b_ref[...]


def _ada(cc, w, b, layer):
    m, k = cc.shape
    depth, _, n = w.shape
    tn = next(t for t in (1024, 512, 256, 128) if n % t == 0)
    return pl.pallas_call(
        _ada_kernel,
        grid=(n // tn,),
        in_specs=[pl.BlockSpec((m, k), lambda j: (0, 0)),
                  pl.BlockSpec((None, k, tn), lambda j: (layer, 0, j)),
                  pl.BlockSpec((None, 1, tn), lambda j: (layer, 0, j))],
        out_specs=pl.BlockSpec((m, tn), lambda j: (0, j)),
        out_shape=jax.ShapeDtypeStruct((m, n), F32),
        compiler_params=_cparams(("arbitrary",)),
        name="ada",
    )(cc, w, b.reshape(depth, 1, n))


def _in_proj_kernel(x_ref, g_ref, sh_ref, sc_ref, w_ref, hg_ref, ones_ref, o_ref, h_scr, *, head_dim):
    j = pl.program_id(2)

    @pl.when(j == 0)
    def _():
        h_scr[...] = _norm_mod(x_ref[...], g_ref[...], sh_ref[...], sc_ref[...]).astype(BF16)

    acc = jnp.dot(h_scr[...], w_ref[...], preferred_element_type=F32)
    is_qk = jnp.logical_or(j == 1, j == 2)

    @pl.when(is_qk)
    def _():
        yy = acc * acc
        hi = yy.astype(BF16)
        lo = (yy - hi.astype(F32)).astype(BF16)
        ss = (jnp.dot(hi, ones_ref[...], preferred_element_type=F32)
              + jnp.dot(lo, ones_ref[...], preferred_element_type=F32))
        o_ref[...] = (acc * lax.rsqrt(ss * (1.0 / head_dim) + NORM_EPS) * hg_ref[...]).astype(o_ref.dtype)

    @pl.when(jnp.logical_not(is_qk))
    def _():
        o_ref[...] = acc.astype(o_ref.dtype)


def _in_proj(x, g, sh, sc, w_bf, head_gain, ones_blk, tm):
    b, l, d = x.shape
    n = w_bf.shape[1]
    tn = ones_blk.shape[0]
    nb = sh.shape[0]
    mod_map = (lambda bi, i, j: (bi, 0, 0)) if nb > 1 else (lambda bi, i, j: (0, 0, 0))
    return pl.pallas_call(
        functools.partial(_in_proj_kernel, head_dim=tn // NA_HEADS),
        grid=(b, l // tm, n // tn),
        in_specs=[pl.BlockSpec((None, tm, d), lambda bi, i, j: (bi, i, 0)),
                  pl.BlockSpec((1, d), lambda bi, i, j: (0, 0)),
                  pl.BlockSpec((None, 1, d), mod_map),
                  pl.BlockSpec((None, 1, d), mod_map),
                  pl.BlockSpec((d, tn), lambda bi, i, j: (0, j)),
                  pl.BlockSpec((None, 1, tn), lambda bi, i, j: (jnp.clip(j - 1, 0, 1), 0, 0)),
                  pl.BlockSpec((tn, tn), lambda bi, i, j: (0, 0))],
        out_specs=pl.BlockSpec((None, tm, tn), lambda bi, i, j: (bi, i, j)),
        out_shape=jax.ShapeDtypeStruct((b, l, n), BF16),
        scratch_shapes=[pltpu.VMEM((tm, d), BF16)],
        compiler_params=_cparams(("parallel", "parallel", "arbitrary")),
        name="in_proj",
    )(x, g.reshape(1, d), sh, sc, w_bf, head_gain, ones_blk)


def _bmm_kernel(a_ref, w_ref, o_ref):
    o_ref[...] = jnp.dot(a_ref[...].astype(BF16), w_ref[...], preferred_element_type=F32).astype(o_ref.dtype)


def _bmm_add_kernel(a_ref, w_ref, y_ref, o_ref):
    o_ref[...] = (y_ref[...] + jnp.dot(a_ref[...].astype(BF16), w_ref[...],
                                       preferred_element_type=F32)).astype(o_ref.dtype)


def _bmm(a, w, add=None, tm=None, name="bmm"):
    g, m, k = a.shape
    n = w.shape[2]
    tm = m if tm is None else tm
    in_specs = [pl.BlockSpec((None, tm, k), lambda gi, i: (gi, i, 0)),
                pl.BlockSpec((None, k, n), lambda gi, i: (gi, 0, 0))]
    args = [a, w]
    kern = _bmm_kernel
    if add is not None:
        in_specs.append(pl.BlockSpec((None, tm, n), lambda gi, i: (gi, i, 0)))
        args.append(add)
        kern = _bmm_add_kernel
    return pl.pallas_call(
        kern,
        grid=(g, m // tm),
        in_specs=in_specs,
        out_specs=pl.BlockSpec((None, tm, n), lambda gi, i: (gi, i, 0)),
        out_shape=jax.ShapeDtypeStruct((g, m, n), F32),
        compiler_params=_cparams(("parallel", "parallel")),
        name=name,
    )(*args)


def _gelu_tanh(x):
    return 0.5 * x * (1.0 + jnp.tanh(math.sqrt(2.0 / math.pi) * (x + 0.044715 * (x * x * x))))


def _s5_out_kernel(y_ref, u_ref, d_ref, w_ref, o_ref):
    y = y_ref[...].astype(F32) + d_ref[...] * u_ref[...].astype(F32)
    z = _gelu_tanh(y)
    gate = jax.nn.sigmoid(jnp.dot(z.astype(BF16), w_ref[...], preferred_element_type=F32))
    o_ref[...] = (z * gate).astype(o_ref.dtype)


def _s5_out(y, u, d, w_glu_bf, tm):
    m, w = y.shape
    return pl.pallas_call(
        _s5_out_kernel,
        grid=(m // tm,),
        in_specs=[pl.BlockSpec((tm, w), lambda i: (i, 0)),
                  pl.BlockSpec((tm, w), lambda i: (i, 0)),
                  pl.BlockSpec((1, w), lambda i: (0, 0)),
                  pl.BlockSpec((w, w), lambda i: (0, 0))],
        out_specs=pl.BlockSpec((tm, w), lambda i: (i, 0)),
        out_shape=jax.ShapeDtypeStruct((m, w), BF16),
        compiler_params=_cparams(("parallel",)),
        name="s5_out",
    )(y, u, d.reshape(1, w), w_glu_bf)


def _s5_tables(p):
    t = S5_CHUNK
    mats, sums, reads, steps = [], [], [], []
    for d in range(2):
        lam_re = p['ssm_lam_re'][d].astype(F32)
        lam_im = p['ssm_lam_im'][d].astype(F32)
        step = jnp.exp(p['ssm_log_step'][d].astype(F32))[:, None]
        mag = jnp.exp(lam_re * step)
        a_re = mag * jnp.cos(lam_im * step)
        a_im = mag * jnp.sin(lam_im * step)
        den = lam_re * lam_re + lam_im * lam_im
        num_re = a_re - 1.0
        coef_re = (num_re * lam_re + a_im * lam_im) / den
        coef_im = (a_im * lam_re - num_re * lam_im) / den
        b_re = p['ssm_b_re'][d].astype(F32)
        b_im = p['ssm_b_im'][d].astype(F32)
        bb_re = coef_re[..., None] * b_re - coef_im[..., None] * b_im
        bb_im = coef_re[..., None] * b_im + coef_im[..., None] * b_re
        c_re = p['ssm_c_re'][d].astype(F32)
        c_im = p['ssm_c_im'][d].astype(F32)
        n = jnp.arange(t + 1, dtype=F32)[:, None, None]
        pm = jnp.exp(lam_re * step * n)
        pw_re = pm * jnp.cos(lam_im * step * n)
        pw_im = pm * jnp.sin(lam_im * step * n)
        ca_re = c_re[None] * pw_re[:, :, None, :] - c_im[None] * pw_im[:, :, None, :]
        ca_im = c_re[None] * pw_im[:, :, None, :] + c_im[None] * pw_re[:, :, None, :]
        ktau = (jnp.einsum('ngop,gpi->ngoi', ca_re[:t], bb_re)
                - jnp.einsum('ngop,gpi->ngoi', ca_im[:t], bb_im))
        s_idx = np.arange(t)[:, None]
        t_idx = np.arange(t)[None, :]
        lag = (t_idx - s_idx) if d == 0 else (s_idx - t_idx)
        valid = jnp.asarray(lag >= 0, F32)
        blk = ktau[np.clip(lag, 0, t - 1)] * valid[:, :, None, None, None]
        mats.append(jnp.transpose(blk, (2, 0, 4, 1, 3)))
        pows = (t - 1 - np.arange(t)) if d == 0 else np.arange(t)
        sw_re = pw_re[pows][:, :, :, None] * bb_re[None] - pw_im[pows][:, :, :, None] * bb_im[None]
        sw_im = pw_re[pows][:, :, :, None] * bb_im[None] + pw_im[pows][:, :, :, None] * bb_re[None]
        sums.append((jnp.transpose(sw_re, (1, 0, 3, 2)), jnp.transpose(sw_im, (1, 0, 3, 2))))
        rp = (np.arange(t) + 1) if d == 0 else (t - np.arange(t))
        reads.append((jnp.transpose(ca_re[rp], (1, 3, 0, 2)), -jnp.transpose(ca_im[rp], (1, 3, 0, 2))))
        steps.append((pw_re[t], pw_im[t]))
    g, pdim = steps[0][0].shape
    k = SSM_GROUP
    mtot = (mats[0] + mats[1]).reshape(g, t * k, t * k)
    ssum = jnp.concatenate([sums[0][0], sums[0][1], sums[1][0], sums[1][1]], axis=-1).reshape(g, t * k, 4 * pdim)
    r = jnp.concatenate([reads[0][0], reads[0][1], reads[1][0], reads[1][1]], axis=1).reshape(g, 4 * pdim, t * k)
    tk = t * k
    eye = jnp.eye(2, dtype=F32)
    mt = mtot.reshape(g // 2, 2, tk, tk)
    ymat = jnp.einsum('gjts,jk->gjtks', mt, eye).reshape(g // 2, 2 * tk, 2 * tk)
    smat = jnp.einsum('gjtqp,jk->gjtqkp', ssum.reshape(g // 2, 2, tk, 4, pdim), eye).reshape(g // 2, 2 * tk, 8 * pdim)
    w1 = jnp.concatenate([ymat, smat], axis=-1)
    r2 = jnp.einsum('gjqpt,jk->gqjpkt', r.reshape(g // 2, 2, 4, pdim, tk), eye).reshape(g // 2, 8 * pdim, 2 * tk)
    coef = [c.reshape(1, g * pdim) for c in (steps[0][0], steps[0][1], steps[1][0], steps[1][1])]
    return w1.astype(BF16), r2.astype(BF16), coef


def _s5_chunk_kernel(u_ref, w_ref, y_ref, xfr_ref, xfi_ref, xbr_ref, xbi_ref):
    r = jnp.dot(u_ref[...], w_ref[...], preferred_element_type=F32)
    ny = y_ref.shape[-1]
    nx = xfr_ref.shape[-1]
    y_ref[...] = r[:, :ny]
    for i, ref in enumerate((xfr_ref, xfi_ref, xbr_ref, xbi_ref)):
        ref[...] = r[:, ny + i * nx:ny + (i + 1) * nx]


def _s5_chunk(ug, w1):
    gp, m, tk2 = ug.shape
    nx = (w1.shape[2] - tk2) // 4
    xspec = pl.BlockSpec((m, nx), lambda g: (0, g))
    xshape = jax.ShapeDtypeStruct((m, gp * nx), F32)
    return pl.pallas_call(
        _s5_chunk_kernel,
        grid=(gp,),
        in_specs=[pl.BlockSpec((None, m, tk2), lambda g: (g, 0, 0)),
                  pl.BlockSpec((None, tk2, w1.shape[2]), lambda g: (g, 0, 0))],
        out_specs=[pl.BlockSpec((None, m, tk2), lambda g: (g, 0, 0)), xspec, xspec, xspec, xspec],
        out_shape=[jax.ShapeDtypeStruct((gp, m, tk2), F32), xshape, xshape, xshape, xshape],
        compiler_params=_cparams(("parallel",)),
        name="s5_chunk",
    )(ug, w1)


def _s5_scan_kernel(xfr, xfi, xbr, xbi, afr, afi, abr, abi, hfr, hfi, hbr, hbi, *, tiles_ctx, tiles, bsz):
    shape = (2 * bsz, xfr.shape[1])
    low = lax.broadcasted_iota(jnp.int32, shape, 0) < bsz
    zero = jnp.zeros(shape, F32)

    def step(hr, hi, ar, ai, xr, xi):
        return ar * hr - ai * hi + xr, ar * hi + ai * hr + xi

    def fwd_tile(i, carry):
        ar, ai = jnp.broadcast_to(afr[...], shape), jnp.broadcast_to(afi[...], shape)
        r0 = pl.multiple_of(i * 2 * bsz, 2 * bsz)
        xr, xi = xfr[pl.ds(r0, 2 * bsz), :], xfi[pl.ds(r0, 2 * bsz), :]
        sr, si = pltpu.roll(carry[0], bsz, 0), pltpu.roll(carry[1], bsz, 0)
        tr, ti = step(sr, si, ar, ai, xr, xi)
        t2r, t2i = pltpu.roll(tr, bsz, 0), pltpu.roll(ti, bsz, 0)
        hfr[pl.ds(r0, 2 * bsz), :] = jnp.where(low, sr, t2r)
        hfi[pl.ds(r0, 2 * bsz), :] = jnp.where(low, si, t2i)
        return step(t2r, t2i, ar, ai, xr, xi)

    lax.fori_loop(0, tiles, fwd_tile, (zero, zero))

    def bwd_tile(i, carry):
        ar, ai = jnp.broadcast_to(abr[...], shape), jnp.broadcast_to(abi[...], shape)
        r0 = pl.multiple_of(i * 2 * bsz, 2 * bsz)
        xr, xi = xbr[pl.ds(r0, 2 * bsz), :], xbi[pl.ds(r0, 2 * bsz), :]
        sr, si = pltpu.roll(carry[0], bsz, 0), pltpu.roll(carry[1], bsz, 0)
        tr, ti = step(sr, si, ar, ai, xr, xi)
        t2r, t2i = pltpu.roll(tr, bsz, 0), pltpu.roll(ti, bsz, 0)
        hbr[pl.ds(r0, 2 * bsz), :] = jnp.where(low, t2r, sr)
        hbi[pl.ds(r0, 2 * bsz), :] = jnp.where(low, t2i, si)
        return step(t2r, t2i, ar, ai, xr, xi)

    carry = lax.fori_loop(0, tiles_ctx, lambda j, c: bwd_tile(tiles_ctx - 1 - j, c), (zero, zero))
    lax.fori_loop(0, tiles - tiles_ctx, lambda j, c: bwd_tile(tiles - 1 - j, c), carry)


def _s5_scan(xs, coef, chunks_ctx, chunks, bsz):
    m, lanes = xs[0].shape
    assert 2 * bsz == 8 and chunks % 2 == 0 and chunks_ctx % 2 == 0
    lb = min(256, lanes)
    xspec = pl.BlockSpec((m, lb), lambda j: (0, j))
    cspec = pl.BlockSpec((1, lb), lambda j: (0, j))
    shape = jax.ShapeDtypeStruct((m, lanes), F32)
    return pl.pallas_call(
        functools.partial(_s5_scan_kernel, tiles_ctx=chunks_ctx // 2, tiles=chunks // 2, bsz=bsz),
        grid=(lanes // lb,),
        in_specs=[xspec] * 4 + [cspec] * 4,
        out_specs=[xspec] * 4,
        out_shape=[shape] * 4,
        compiler_params=_cparams(("parallel",)),
        name="s5_scan",
    )(*xs, *coef)


def _s5_readout_kernel(y_ref, hfr, hfi, hbr, hbi, r_ref, o_ref):
    h = jnp.concatenate([hfr[...], hfi[...], hbr[...], hbi[...]], axis=1).astype(BF16)
    o_ref[...] = (y_ref[...] + jnp.dot(h, r_ref[...], preferred_element_type=F32)).astype(o_ref.dtype)


def _s5_readout(y0, hs, r2):
    gp, m, tk2 = y0.shape
    nx = hs[0].shape[1] // gp
    hspec = pl.BlockSpec((m, nx), lambda g: (0, g))
    yspec = pl.BlockSpec((None, m, tk2), lambda g: (g, 0, 0))
    return pl.pallas_call(
        _s5_readout_kernel,
        grid=(gp,),
        in_specs=[yspec, hspec, hspec, hspec, hspec, pl.BlockSpec((None, 4 * nx, tk2), lambda g: (g, 0, 0))],
        out_specs=yspec,
        out_shape=jax.ShapeDtypeStruct((gp, m, tk2), BF16),
        compiler_params=_cparams(("parallel",)),
        name="s5_readout",
    )(y0, *hs, r2)


def _s5_branch(u_ctx, u_lat, p, tables, need_ctx):
    w1, r2, coef = tables
    b, l, w = u_lat.shape
    lc = u_ctx.shape[1]
    t, k = S5_CHUNK, SSM_GROUP
    g = w // k
    nc = (lc + l) // t
    m = nc * b
    u_all = jnp.concatenate([u_ctx, u_lat], axis=1)
    ug = jnp.transpose(u_all.reshape(b, nc, t, g // 2, 2, k), (3, 1, 0, 4, 2, 5)).reshape(g // 2, m, 2 * t * k)
    y0, *xs = _s5_chunk(ug, w1)
    hs = _s5_scan(xs, coef, lc // t, nc, b)
    y = _s5_readout(y0, hs, r2).reshape(g // 2, nc, b, 2, t, k)
    y = jnp.transpose(y, (2, 1, 4, 0, 3, 5)).reshape(b, nc * t, w)
    w_glu = p['ssm_w_glu'].astype(BF16)
    out_l = _s5_out(y[:, lc:].reshape(b * l, w), u_lat.reshape(b * l, w), p['ssm_d'], w_glu, tm=512)
    out_c = None
    if need_ctx:
        out_c = _s5_out(y[:, :lc].reshape(b * lc, w), u_ctx.reshape(b * lc, w), p['ssm_d'], w_glu,
                        tm=min(512, b * lc)).reshape(b, lc, w)
    return out_l.reshape(b, l, w), out_c


def _na_kernel(q_ref, k_ref, v_ref, kc_ref, vc_ref, bias_ref, o_ref, s_scr, p_scr, *, rows, head_dim):
    r = pl.program_id(1)
    kr = NA_WIN_ROWS
    start = jnp.clip(r - kr // 2, 0, rows - kr)
    cls = r - start
    base = pl.multiple_of(start * GRID_W, GRID_W)
    kwin = k_ref[pl.ds(base, kr * GRID_W), :]
    vwin = v_ref[pl.ds(base, kr * GRID_W), :]
    q = q_ref[...]
    kc = kc_ref[...]
    vc = vc_ref[...]
    outs = []
    nt = (((1,), (1,)), ((), ()))
    lane_tile = 2 * head_dim
    nwin = kwin.shape[0]
    first_head = lax.broadcasted_iota(jnp.int32, (q.shape[0], lane_tile), 1) < head_dim
    for h in range(NA_HEADS):
        sl = slice(h // 2 * lane_tile, (h // 2 + 1) * lane_tile)
        qh = jnp.where(first_head == (h % 2 == 0), q[:, sl], jnp.zeros_like(q[:, sl]))
        s_scr[h, :, :nwin] = lax.dot_general(qh, kwin[:, sl], nt, preferred_element_type=F32) + bias_ref[cls, h]
        s_scr[h, :, nwin:] = lax.dot_general(qh, kc[:, sl], nt, preferred_element_type=F32)
    inv = []
    for h in range(NA_HEADS):
        s = s_scr[h]
        pr = jnp.exp(s - jnp.max(s, axis=-1, keepdims=True))
        inv.append(1.0 / jnp.sum(pr, axis=-1, keepdims=True))
        p_scr[h] = pr.astype(BF16)
    for pair in range(NA_HEADS // 2):
        sl = slice(pair * lane_tile, (pair + 1) * lane_tile)
        halves = []
        for half in range(2):
            h = 2 * pair + half
            o = (jnp.dot(p_scr[h, :, :nwin], vwin[:, sl], preferred_element_type=F32)
                 + jnp.dot(p_scr[h, :, nwin:], vc[:, sl], preferred_element_type=F32))
            halves.append(o * inv[h])
        outs.append(jnp.where(first_head, halves[0], halves[1]))
    o_ref[...] = jnp.concatenate(outs, axis=-1).astype(o_ref.dtype)


def _na_bias_table(rpb):
    kr, kw, w = NA_WIN_ROWS, NA_WIN_COLS, GRID_W
    col = np.arange(w)
    cstart = np.clip(col - kw // 2, 0, w - kw)
    kcol = np.arange(w)
    inwin = (kcol[None, :] >= cstart[:, None]) & (kcol[None, :] < cstart[:, None] + kw)
    dc = np.clip(kcol[None, :] - col[:, None] + (kw - 1), 0, 2 * kw - 2)
    rpb = rpb.astype(F32)
    tabs = []
    for o in range(kr):
        dr = np.arange(kr) - o + (kr - 1)
        bias = rpb[:, dr][:, :, dc]
        bias = jnp.where(jnp.asarray(inwin)[None, None], bias, MASK_VALUE)
        tabs.append(jnp.transpose(bias, (0, 2, 1, 3)).reshape(rpb.shape[0], w, kr * w))
    return jnp.stack(tabs, axis=0)


def _na_latent(q, k, v, kc, vc, bias_tab):
    b, l, w = q.shape
    lc = kc.shape[1]
    rows = l // GRID_W
    assert rows >= NA_WIN_ROWS
    nk = NA_WIN_ROWS * GRID_W
    return pl.pallas_call(
        functools.partial(_na_kernel, rows=rows, head_dim=w // NA_HEADS),
        grid=(b, rows),
        in_specs=[pl.BlockSpec((None, GRID_W, w), lambda bi, r: (bi, r, 0)),
                  pl.BlockSpec((None, l, w), lambda bi, r: (bi, 0, 0)),
                  pl.BlockSpec((None, l, w), lambda bi, r: (bi, 0, 0)),
                  pl.BlockSpec((None, lc, w), lambda bi, r: (bi, 0, 0)),
                  pl.BlockSpec((None, lc, w), lambda bi, r: (bi, 0, 0)),
                  pl.BlockSpec((NA_WIN_ROWS, NA_HEADS, GRID_W, nk), lambda bi, r: (0, 0, 0, 0))],
        out_specs=pl.BlockSpec((None, GRID_W, w), lambda bi, r: (bi, r, 0)),
        out_shape=jax.ShapeDtypeStruct((b, l, w), BF16),
        scratch_shapes=[pltpu.VMEM((NA_HEADS, GRID_W, nk + lc), F32), pltpu.VMEM((NA_HEADS, GRID_W, nk + lc), BF16)],
        compiler_params=_cparams(("parallel", "arbitrary"), vmem_mb=56),
        name="na_latent",
    )(q, k, v, kc, vc, bias_tab)


def _na_ctx_kernel(q_ref, k_ref, v_ref, o_ref, *, head_dim):
    q, k, v = q_ref[...], k_ref[...], v_ref[...]
    outs = []
    for h in range(NA_HEADS):
        sl = slice(h * head_dim, (h + 1) * head_dim)
        s = lax.dot_general(q[:, sl], k[:, sl], (((1,), (1,)), ((), ())), preferred_element_type=F32)
        pr = jnp.exp(s - jnp.max(s, axis=-1, keepdims=True))
        den = jnp.sum(pr, axis=-1, keepdims=True)
        outs.append(jnp.dot(pr.astype(BF16), v[:, sl], preferred_element_type=F32) / den)
    o_ref[...] = jnp.concatenate(outs, axis=-1).astype(o_ref.dtype)


def _na_context(q, k, v):
    b, lc, w = q.shape
    spec = pl.BlockSpec((None, lc, w), lambda bi: (bi, 0, 0))
    return pl.pallas_call(
        functools.partial(_na_ctx_kernel, head_dim=w // NA_HEADS),
        grid=(b,),
        in_specs=[spec, spec, spec],
        out_specs=spec,
        out_shape=jax.ShapeDtypeStruct((b, lc, w), BF16),
        compiler_params=_cparams(("parallel",)),
        name="na_context",
    )(q, k, v)


def _hyena_filter(length, p, slow=None):
    n = 2 * length
    pos = np.arange(n)
    if slow is not None:
        pos = pos.reshape(slow, n // slow).T.reshape(-1)
    off = np.where(pos < length, pos, n - pos)
    backward = jnp.asarray(pos > length)[:, None, None]
    live = jnp.asarray(pos != length, F32)[:, None, None]
    tau = jnp.asarray(np.minimum(off, length - 1), F32)[:, None]
    t = tau / (length - 1)
    freqs = jnp.linspace(1e-4, HY_BANDS - 1, HY_BANDS, dtype=F32)
    ang = (2.0 * math.pi / length) * tau * freqs[None, :]
    z = jnp.concatenate([t, jnp.cos(ang), -jnp.sin(ang)], axis=-1)
    fr = p['hy_freq'].astype(F32)
    hp = lax.Precision.HIGHEST
    h = jnp.sin(fr * (jnp.dot(z, p['hy_w1'].astype(F32), precision=hp) + p['hy_b1'].astype(F32)))
    h = jnp.sin(fr * (jnp.dot(h, p['hy_w2'].astype(F32), precision=hp) + p['hy_b2'].astype(F32)))
    h = jnp.sin(fr * (jnp.dot(h, p['hy_w3'].astype(F32), precision=hp) + p['hy_b3'].astype(F32)))
    wdim = p['hy_bias'].shape[-1]
    h = jnp.dot(h, p['hy_w4'].astype(F32), precision=hp).reshape(n, HY_ORDER, 2, wdim)
    decay = jnp.exp(-t * jnp.abs(jnp.linspace(HY_MIN_DECAY, HY_MAX_DECAY, wdim, dtype=F32)))
    filt = jnp.where(backward, h[:, :, 1], h[:, :, 0]) * (decay[:, None, :] * live)
    filt = filt / jnp.sum(jnp.abs(filt), axis=0, keepdims=True)
    return filt.reshape(n, HY_ORDER * wdim)


def _block2(re, im):
    return np.block([[re, -im], [im, re]])


def _fft_consts(length):
    n = 2 * length
    n2 = HY_FAST
    n1 = n // n2
    h1 = n1 // 2
    a1 = -2.0 * np.pi * np.outer(np.arange(n1), np.arange(n1)) / n1
    a2 = -2.0 * np.pi * np.outer(np.arange(n2), np.arange(n2)) / n2
    f1 = _block2(np.cos(a1[:, :h1]), np.sin(a1[:, :h1]))
    f1_real = np.concatenate([np.cos(a1), np.sin(a1)], axis=0)
    f3 = _block2(np.cos(a2), np.sin(a2))
    g3 = _block2(np.cos(a2), -np.sin(a2))
    g1 = _block2(np.cos(a1[:h1, :]) / n, -np.sin(a1[:h1, :]) / n)
    wa = -2.0 * np.pi * np.arange(n2) / n
    wr = np.broadcast_to(np.cos(wa)[:, None], (n2, 128))
    wi = np.broadcast_to(np.sin(wa)[:, None], (n2, 128))
    cast = lambda m: jnp.asarray(m, BF16)
    return dict(n1=n1, h1=h1, f1=cast(f1), f1_real=cast(f1_real), f3=cast(f3), g3=cast(g3), g1=cast(g1),
                wr=jnp.asarray(wr, F32), wi=jnp.asarray(wi, F32))


def _fft_first_stage(a_r, a_i, load_cols, f1_ref, n1, cb):
    slab = HY_FAST + HY_PAD

    def body(i, carry):
        n2 = 2 * i
        z = jnp.concatenate([load_cols(n2), load_cols(n2 + 1)], axis=1)
        r = jnp.dot(f1_ref[...], z, preferred_element_type=F32)
        a_r[pl.ds(n2, n1, stride=slab), :] = r[:n1, :cb]
        a_i[pl.ds(n2, n1, stride=slab), :] = r[n1:, :cb]
        a_r[pl.ds(n2 + 1, n1, stride=slab), :] = r[:n1, cb:]
        a_i[pl.ds(n2 + 1, n1, stride=slab), :] = r[n1:, cb:]
        return carry

    lax.fori_loop(0, HY_FAST // 2, body, 0, unroll=4)


def _fft_slab_loop(a_r, a_i, f3_ref, wr_ref, wi_ref, n1, cb, finish):
    slab = HY_FAST + HY_PAD
    n2 = HY_FAST
    wr, wi = wr_ref[...], wi_ref[...]
    if cb != wr.shape[1]:
        wr, wi = jnp.tile(wr, (1, cb // wr.shape[1])), jnp.tile(wi, (1, cb // wi.shape[1]))

    def body(i, carry):
        tr, ti = carry
        k1 = 2 * i
        r0 = pl.multiple_of(k1 * slab, 8)
        r1 = pl.multiple_of((k1 + 1) * slab, 8)
        tr1, ti1 = tr * wr - ti * wi, tr * wi + ti * wr
        ar0, ai0 = a_r[pl.ds(r0, n2), :], a_i[pl.ds(r0, n2), :]
        ar1, ai1 = a_r[pl.ds(r1, n2), :], a_i[pl.ds(r1, n2), :]
        xr = jnp.concatenate([ar0 * tr - ai0 * ti, ar1 * tr1 - ai1 * ti1], axis=1)
        xi = jnp.concatenate([ar0 * ti + ai0 * tr, ar1 * ti1 + ai1 * tr1], axis=1)
        bc = jnp.dot(f3_ref[...], jnp.concatenate([xr, xi], axis=0).astype(BF16), preferred_element_type=F32)
        finish(k1, (r0, r1), (tr, ti, tr1, ti1), bc[:n2], bc[n2:])
        return tr1 * wr - ti1 * wi, tr1 * wi + ti1 * wr

    one = jnp.ones((n2, cb), F32)
    lax.fori_loop(0, n1 // 2, body, (one, jnp.zeros_like(one)), unroll=2)


def _fft_conv_kernel(y_ref, sr_ref, si_ref, f1_ref, f3_ref, g3_ref, g1_ref, wr_ref, wi_ref, o_ref, a_r, a_i,
                     *, n1):
    h1 = n1 // 2
    n2 = HY_FAST
    cb = o_ref.shape[-1]
    slab = HY_FAST + HY_PAD

    def load_cols(j):
        r = pl.multiple_of(j * h1, h1)
        return jnp.concatenate([y_ref[0, pl.ds(r, h1), :], y_ref[1, pl.ds(r, h1), :]], axis=0)

    _fft_first_stage(a_r, a_i, load_cols, f1_ref, n1, cb)

    def finish(k1, rows, tw, br, bi):
        s0 = pl.multiple_of(k1 * n2, n2)
        s1 = pl.multiple_of((k1 + 1) * n2, n2)
        sr = jnp.concatenate([sr_ref[pl.ds(s0, n2), :], sr_ref[pl.ds(s1, n2), :]], axis=1).astype(F32)
        si = jnp.concatenate([si_ref[pl.ds(s0, n2), :], si_ref[pl.ds(s1, n2), :]], axis=1).astype(F32)
        y = jnp.concatenate([br * sr - bi * si, br * si + bi * sr], axis=0).astype(BF16)
        ac = jnp.dot(g3_ref[...], y, preferred_element_type=F32)
        pr, pi = ac[:n2], ac[n2:]
        for j in range(2):
            tr, ti = tw[2 * j], tw[2 * j + 1]
            prj, pij = pr[:, j * cb:(j + 1) * cb], pi[:, j * cb:(j + 1) * cb]
            a_r[pl.ds(rows[j], n2), :] = prj * tr + pij * ti
            a_i[pl.ds(rows[j], n2), :] = pij * tr - prj * ti

    _fft_slab_loop(a_r, a_i, f3_ref, wr_ref, wi_ref, n1, cb, finish)

    def out_body(i, carry):
        j0 = 2 * i
        cols = []
        for j in (j0, j0 + 1):
            cols.append(jnp.concatenate([a_r[pl.ds(j, n1, stride=slab), :], a_i[pl.ds(j, n1, stride=slab), :]],
                                        axis=0))
        o = jnp.dot(g1_ref[...], jnp.concatenate(cols, axis=1).astype(BF16), preferred_element_type=F32)
        for jj in range(2):
            r = pl.multiple_of((j0 + jj) * h1, h1)
            o_ref[0, pl.ds(r, h1), :] = o[:h1, jj * cb:(jj + 1) * cb].astype(o_ref.dtype)
            o_ref[1, pl.ds(r, h1), :] = o[h1:, jj * cb:(jj + 1) * cb].astype(o_ref.dtype)
        return carry

    lax.fori_loop(0, n2 // 2, out_body, 0, unroll=4)


def _fft_conv(yp, col0, spec_r, spec_i, scol0, consts, cb=128):
    b, l, _ = yp.shape
    n = spec_r.shape[0]
    n1 = consts['n1']
    slab = HY_FAST + HY_PAD
    nblk = consts['nblk']
    y4 = yp.reshape(b // 2, 2, l, yp.shape[-1])
    const = lambda a: pl.BlockSpec(a.shape, lambda j, pi: (0, 0))
    return pl.pallas_call(
        functools.partial(_fft_conv_kernel, n1=n1),
        grid=(nblk, b // 2),
        in_specs=[pl.BlockSpec((None, 2, l, cb), lambda j, pi: (pi, 0, 0, col0 + j)),
                  pl.BlockSpec((n, cb), lambda j, pi: (0, scol0 + j)),
                  pl.BlockSpec((n, cb), lambda j, pi: (0, scol0 + j)),
                  const(consts['f1']), const(consts['f3']), const(consts['g3']), const(consts['g1']),
                  const(consts['wr']), const(consts['wi'])],
        out_specs=pl.BlockSpec((None, 2, l, cb), lambda j, pi: (pi, 0, 0, j)),
        out_shape=jax.ShapeDtypeStruct((b // 2, 2, l, nblk * cb), BF16),
        scratch_shapes=[pltpu.VMEM((n1 * slab, cb), F32), pltpu.VMEM((n1 * slab, cb), F32)],
        compiler_params=_cparams(("arbitrary", "arbitrary"), vmem_mb=58),
        name="fft_conv",
    )(y4, spec_r, spec_i, consts['f1'], consts['f3'], consts['g3'], consts['g1'],
      consts['wr'], consts['wi']).reshape(b, l, nblk * cb)


def _fft_spec_kernel(f_ref, f1_ref, f3_ref, wr_ref, wi_ref, sr_ref, si_ref, a_r, a_i, *, n1):
    n2 = HY_FAST
    cb = sr_ref.shape[-1]

    def load_cols(j):
        return f_ref[pl.ds(pl.multiple_of(j * n1, n1), n1), :]

    _fft_first_stage(a_r, a_i, load_cols, f1_ref, n1, cb)

    def finish(k1, rows, tw, br, bi):
        for j in range(2):
            s = pl.multiple_of((k1 + j) * n2, n2)
            sr_ref[pl.ds(s, n2), :] = br[:, j * cb:(j + 1) * cb].astype(sr_ref.dtype)
            si_ref[pl.ds(s, n2), :] = bi[:, j * cb:(j + 1) * cb].astype(si_ref.dtype)

    _fft_slab_loop(a_r, a_i, f3_ref, wr_ref, wi_ref, n1, cb, finish)


def _fft_spectrum(filt_p, consts, cb=128):
    n, c = filt_p.shape
    n1 = consts['n1']
    slab = HY_FAST + HY_PAD
    const = lambda a: pl.BlockSpec(a.shape, lambda j: (0, 0))
    spec = pl.BlockSpec((n, cb), lambda j: (0, j))
    return pl.pallas_call(
        functools.partial(_fft_spec_kernel, n1=n1),
        grid=(c // cb,),
        in_specs=[spec, const(consts['f1_real']), const(consts['f3']), const(consts['wr']), const(consts['wi'])],
        out_specs=[spec, spec],
        out_shape=[jax.ShapeDtypeStruct((n, c), BF16)] * 2,
        scratch_shapes=[pltpu.VMEM((n1 * slab, cb), F32), pltpu.VMEM((n1 * slab, cb), F32)],
        compiler_params=_cparams(("arbitrary",), vmem_mb=58),
        name="fft_spectrum",
    )(filt_p, consts['f1_real'], consts['f3'], consts['wr'], consts['wi'])


def _dense_conv_kernel(y_ref, sr_ref, si_ref, f_ref, g_ref, o_ref):
    l = y_ref.shape[1]
    n = sr_ref.shape[0]
    bc = jnp.dot(f_ref[...], jnp.concatenate([y_ref[0], y_ref[1]], axis=0), preferred_element_type=F32)
    br, bi = bc[:n], bc[n:]
    sr, si = sr_ref[...], si_ref[...]
    y = jnp.concatenate([br * sr - bi * si, br * si + bi * sr], axis=0).astype(BF16)
    o = jnp.dot(g_ref[...], y, preferred_element_type=F32)
    o_ref[0] = o[:l].astype(o_ref.dtype)
    o_ref[1] = o[l:].astype(o_ref.dtype)


def _dense_conv(y, col0, spec_r, spec_i, scol0, fmat, gmat, nblk, cb=128):
    b, l, _ = y.shape
    n = spec_r.shape[0]
    y4 = y.reshape(b // 2, 2, l, y.shape[-1])
    const = lambda a: pl.BlockSpec(a.shape, lambda j, pi: (0, 0))
    return pl.pallas_call(
        _dense_conv_kernel,
        grid=(nblk, b // 2),
        in_specs=[pl.BlockSpec((None, 2, l, cb), lambda j, pi: (pi, 0, 0, col0 + j)),
                  pl.BlockSpec((n, cb), lambda j, pi: (0, scol0 + j)),
                  pl.BlockSpec((n, cb), lambda j, pi: (0, scol0 + j)),
                  const(fmat), const(gmat)],
        out_specs=pl.BlockSpec((None, 2, l, cb), lambda j, pi: (pi, 0, 0, j)),
        out_shape=jax.ShapeDtypeStruct((b // 2, 2, l, nblk * cb), BF16),
        compiler_params=_cparams(("parallel", "parallel")),
        name="dense_conv",
    )(y4, spec_r, spec_i, fmat, gmat).reshape(b, l, nblk * cb)


def _sconv_kernel(u_ref, w_ref, b_ref, o_ref, *scr, h1):
    u = u_ref[...].astype(F32)
    l = u.shape[0]
    row = lax.broadcasted_iota(jnp.int32, u.shape, 0)
    prev = jnp.where(row == 0, 0.0, pltpu.roll(u, 1, 0))
    nxt = jnp.where(row == l - 1, 0.0, pltpu.roll(u, l - 1, 0))
    z = prev * w_ref[0:1, :] + u * w_ref[1:2, :] + nxt * w_ref[2:3, :] + b_ref[...]
    if h1 is None:
        o_ref[...] = z.astype(o_ref.dtype)
        return
    scr[0][...] = z

    def body(n2, carry):
        o_ref[pl.ds(pl.multiple_of(n2 * h1, h1), h1), :] = scr[0][pl.ds(n2, h1, stride=HY_FAST), :].astype(o_ref.dtype)
        return carry

    lax.fori_loop(0, HY_FAST, body, 0)


def _short_conv(proj, col0, conv_w, conv_b, h1, cb=128):
    b, l, _ = proj.shape
    c = conv_w.shape[1]
    return pl.pallas_call(
        functools.partial(_sconv_kernel, h1=h1),
        scratch_shapes=[] if h1 is None else [pltpu.VMEM((l, cb), F32)],
        grid=(b, c // cb),
        in_specs=[pl.BlockSpec((None, l, cb), lambda bi, j: (bi, 0, col0 + j)),
                  pl.BlockSpec((conv_w.shape[0], cb), lambda bi, j: (0, j)),
                  pl.BlockSpec((1, cb), lambda bi, j: (0, j))],
        out_specs=pl.BlockSpec((None, l, cb), lambda bi, j: (bi, 0, j)),
        out_shape=jax.ShapeDtypeStruct((b, l, c), BF16),
        compiler_params=_cparams(("parallel", "parallel")),
        name="short_conv",
    )(proj, conv_w.astype(F32), conv_b.astype(F32).reshape(1, c))


def _gate_kernel(x_ref, conv_ref, y_ref, bias_ref, o_ref, *scr, h1):
    y = y_ref[...].astype(F32)
    z = x_ref[...].astype(F32) * (conv_ref[...].astype(F32) + y * bias_ref[...])
    if h1 is None:
        o_ref[...] = z.astype(o_ref.dtype)
        return
    scr[0][...] = z

    def body(n1, carry):
        r = pl.multiple_of(n1 * HY_FAST, HY_FAST)
        o_ref[pl.ds(r, HY_FAST), :] = scr[0][pl.ds(n1, HY_FAST, stride=h1), :].astype(o_ref.dtype)
        return carry

    lax.fori_loop(0, h1, body, 0)


def _hy_gate(z, xcol, conv, y, ycol, bias, h1=None, cb=128):
    b, l, w = conv.shape
    return pl.pallas_call(
        functools.partial(_gate_kernel, h1=h1),
        grid=(b, w // cb),
        in_specs=[pl.BlockSpec((None, l, cb), lambda bi, j: (bi, 0, xcol + j)),
                  pl.BlockSpec((None, l, cb), lambda bi, j: (bi, 0, j)),
                  pl.BlockSpec((None, l, cb), lambda bi, j: (bi, 0, ycol + j)),
                  pl.BlockSpec((1, cb), lambda bi, j: (0, j))],
        out_specs=pl.BlockSpec((None, l, cb), lambda bi, j: (bi, 0, j)),
        out_shape=jax.ShapeDtypeStruct((b, l, w), BF16),
        scratch_shapes=[] if h1 is None else [pltpu.VMEM((l, cb), F32)],
        compiler_params=_cparams(("parallel", "parallel")),
        name="hy_gate",
    )(z, conv, y, bias.astype(F32).reshape(1, w))


def _hyena_branch(proj, col0, p, cb=128):
    b, l, _ = proj.shape
    w = p['hy_bias'].shape[-1]
    n = 2 * l
    nblk = w // cb
    if n // HY_FAST >= 16:
        consts = dict(_fft_consts(l), nblk=nblk)
        n1, h1 = consts['n1'], consts['h1']
        spec_r, spec_i = _fft_spectrum(_hyena_filter(l, p, slow=n1).astype(BF16), consts, cb)
        conv = lambda y, ycol, order: _fft_conv(y, ycol * nblk, spec_r, spec_i, order * nblk, consts, cb)
    else:
        h1 = None
        k = np.arange(n)
        ang = -2.0 * np.pi * np.outer(k, k) / n
        fmat = jnp.asarray(_block2(np.cos(ang[:, :l]), np.sin(ang[:, :l])), BF16)
        gmat = jnp.asarray(_block2(np.cos(ang[:l, :]) / n, -np.sin(ang[:l, :]) / n), BF16)
        f_real = jnp.asarray(np.concatenate([np.cos(ang), np.sin(ang)], axis=0), F32)
        spec = _bmm(f_real[None], _hyena_filter(l, p).astype(BF16)[None], name="dense_spectrum")[0]
        spec_r, spec_i = spec[:n], spec[n:]
        conv = lambda y, ycol, order: _dense_conv(y, ycol * nblk, spec_r, spec_i, order * nblk, fmat, gmat,
                                                  nblk, cb)
    z = _short_conv(proj, col0, p['hy_conv_w'], p['hy_conv_b'], h1, cb)
    c0 = conv(z, HY_ORDER, 0)
    y1 = _hy_gate(z, 0, c0, z, HY_ORDER * nblk, p['hy_bias'][0], None, cb)
    c1 = conv(y1, 0, 1)
    return _hy_gate(z, nblk, c1, y1, 0, p['hy_bias'][1], h1, cb)


def _merge_kernel(x_ref, g_ref, sh_ref, sc_ref, g1_ref, br_ref, wg0_ref, wg1_ref, wg2_ref, bg_ref, wb_ref, wo_ref,
                  o_ref, h_scr, acc_scr):
    j = pl.program_id(2)

    @pl.when(j == 0)
    def _():
        h_scr[...] = _norm_mod(x_ref[...], g_ref[...], sh_ref[...], sc_ref[...]).astype(BF16)
        acc_scr[...] = jnp.zeros_like(acc_scr)

    h = h_scr[...]
    mixed = None
    for i, wg_ref in enumerate((wg0_ref, wg1_ref, wg2_ref)):
        gate = jax.nn.sigmoid(jnp.dot(h, wg_ref[...], preferred_element_type=F32) + bg_ref[i])
        term = gate * jnp.dot(br_ref[i], wb_ref[i], preferred_element_type=F32)
        mixed = term if mixed is None else mixed + term
    acc_scr[...] += jnp.dot(mixed.astype(BF16), wo_ref[...], preferred_element_type=F32)

    @pl.when(j == pl.num_programs(2) - 1)
    def _():
        o_ref[...] = x_ref[...] + g1_ref[...] * acc_scr[...]


def _merge(x, g, sh, sc, g1, branches, w_gate_bf, b_gate, w_branch_bf, w_out_bf, tm, tc):
    b, l, d = x.shape
    nbr, w, _ = w_branch_bf.shape
    nb = sh.shape[0]
    nj = d // tc
    mod_map = (lambda bi, i, j: (bi, 0, 0)) if nb > 1 else (lambda bi, i, j: (0, 0, 0))
    br = jnp.stack(branches, axis=1)
    bg = b_gate.reshape(nbr, nj, 1, tc).transpose(1, 0, 2, 3)
    return pl.pallas_call(
        _merge_kernel,
        grid=(b, l // tm, nj),
        in_specs=[pl.BlockSpec((None, tm, d), lambda bi, i, j: (bi, i, 0)),
                  pl.BlockSpec((1, d), lambda bi, i, j: (0, 0)),
                  pl.BlockSpec((None, 1, d), mod_map),
                  pl.BlockSpec((None, 1, d), mod_map),
                  pl.BlockSpec((None, 1, d), mod_map),
                  pl.BlockSpec((None, nbr, tm, w), lambda bi, i, j: (bi, 0, i, 0)),
                  pl.BlockSpec((d, tc), lambda bi, i, j: (0, j)),
                  pl.BlockSpec((d, tc), lambda bi, i, j: (0, nj + j)),
                  pl.BlockSpec((d, tc), lambda bi, i, j: (0, 2 * nj + j)),
                  pl.BlockSpec((None, nbr, 1, tc), lambda bi, i, j: (j, 0, 0, 0)),
                  pl.BlockSpec((nbr, w, tc), lambda bi, i, j: (0, 0, j)),
                  pl.BlockSpec((tc, d), lambda bi, i, j: (j, 0))],
        out_specs=pl.BlockSpec((None, tm, d), lambda bi, i, j: (bi, i, 0)),
        out_shape=jax.ShapeDtypeStruct((b, l, d), F32),
        scratch_shapes=[pltpu.VMEM((tm, d), BF16), pltpu.VMEM((tm, d), F32)],
        compiler_params=_cparams(("parallel", "parallel", "arbitrary"), vmem_mb=56),
        name="merge",
    )(x, g.reshape(1, d), sh, sc, g1, br, w_gate_bf, w_gate_bf, w_gate_bf, bg, w_branch_bf, w_out_bf)


def _router_kernel(x_ref, g_ref, sh_ref, sc_ref, rt_ref, h_ref, aff_ref):
    h = _norm_mod(x_ref[...], g_ref[...], sh_ref[...], sc_ref[...])
    h_ref[...] = h.astype(h_ref.dtype)
    nt = (((1,), (1,)), ((), ()))
    rt = rt_ref[...]
    r_hi = rt.astype(BF16)
    r_lo = (rt - r_hi.astype(F32)).astype(BF16)
    h_hi = h.astype(BF16)
    h_lo = (h - h_hi.astype(F32)).astype(BF16)
    logits = (lax.dot_general(r_hi, h_hi, nt, preferred_element_type=F32)
              + lax.dot_general(r_hi, h_lo, nt, preferred_element_type=F32)
              + lax.dot_general(r_lo, h_hi, nt, preferred_element_type=F32))
    e = jnp.exp(logits - jnp.max(logits, axis=0, keepdims=True))
    aff_ref[...] = e / jnp.sum(e, axis=0, keepdims=True)


def _router(x, g, sh, sc, router_t, tm):
    b, l, d = x.shape
    e = router_t.shape[0]
    nb = sh.shape[0]
    mod_map = (lambda bi, i: (bi, 0, 0)) if nb > 1 else (lambda bi, i: (0, 0, 0))
    return pl.pallas_call(
        _router_kernel,
        grid=(b, l // tm),
        in_specs=[pl.BlockSpec((None, tm, d), lambda bi, i: (bi, i, 0)),
                  pl.BlockSpec((1, d), lambda bi, i: (0, 0)),
                  pl.BlockSpec((None, 1, d), mod_map),
                  pl.BlockSpec((None, 1, d), mod_map),
                  pl.BlockSpec((e, d), lambda bi, i: (0, 0))],
        out_specs=[pl.BlockSpec((None, tm, d), lambda bi, i: (bi, i, 0)),
                   pl.BlockSpec((None, e, tm), lambda bi, i: (bi, 0, i))],
        out_shape=[jax.ShapeDtypeStruct((b, l, d), BF16), jax.ShapeDtypeStruct((b, e, l), F32)],
        compiler_params=_cparams(("parallel", "parallel")),
        name="router",
    )(x, g.reshape(1, d), sh, sc, router_t)


def _ffn_kernel(xs_ref, gate_ref, w1_ref, w3_ref, w2_ref, o_ref, acc_scr):
    f = pl.program_id(2)

    @pl.when(f == 0)
    def _():
        acc_scr[...] = jnp.zeros_like(acc_scr)

    xs = xs_ref[...]
    a = jnp.dot(xs, w1_ref[...].astype(BF16), preferred_element_type=F32)
    gl = jnp.dot(xs, w3_ref[...].astype(BF16), preferred_element_type=F32)
    mid = (a * jax.nn.sigmoid(a) * gl).astype(BF16)
    acc_scr[...] += jnp.dot(mid, w2_ref[...].astype(BF16), preferred_element_type=F32)

    @pl.when(f == pl.num_programs(2) - 1)
    def _():
        o_ref[...] = (acc_scr[...] * gate_ref[...]).astype(o_ref.dtype)


def _expert_ffn(xs, gate, w1, w3, w2, layer, tr, tf):
    e, r, d = xs.shape
    ff = w1.shape[3]
    return pl.pallas_call(
        _ffn_kernel,
        grid=(e, r // tr, ff // tf),
        in_specs=[pl.BlockSpec((None, tr, d), lambda ei, i, f: (ei, i, 0)),
                  pl.BlockSpec((None, tr, 1), lambda ei, i, f: (ei, i, 0)),
                  pl.BlockSpec((None, None, d, tf), lambda ei, i, f: (layer, ei, 0, f)),
                  pl.BlockSpec((None, None, d, tf), lambda ei, i, f: (layer, ei, 0, f)),
                  pl.BlockSpec((None, None, tf, d), lambda ei, i, f: (layer, ei, f, 0))],
        out_specs=pl.BlockSpec((None, tr, d), lambda ei, i, f: (ei, i, 0)),
        out_shape=jax.ShapeDtypeStruct((e, r, d), BF16),
        scratch_shapes=[pltpu.VMEM((tr, d), F32)],
        compiler_params=_cparams(("parallel", "parallel", "arbitrary"), vmem_mb=56),
        name="expert_ffn",
    )(xs, gate, w1, w3, w2)


def _combine_kernel(st_ref, x_ref, g2_ref, pos_ref, *refs, n_exp, tiles, tn):
    y_refs, o_ref, oh_scr = refs[:n_exp], refs[n_exp], refs[n_exp + 1]
    bi, j = pl.program_id(0), pl.program_id(1)
    tt, win = oh_scr.shape[1], y_refs[0].shape[1]
    lane = lax.broadcasted_iota(jnp.int32, (tt, win), 1)
    main = min(win, tt)
    spill = []
    for e in range(n_exp):
        rel = pos_ref[:, e:e + 1] - st_ref[(bi * n_exp + e) * tiles + j]
        oh_scr[e] = jnp.where(lane == rel, 1.0, 0.0).astype(BF16)
        spill.append(jnp.max(rel) >= main)
    for c in range(o_ref.shape[1] // tn):
        cols = slice(c * tn, (c + 1) * tn)
        acc = jnp.dot(oh_scr[0, :, :main], y_refs[0][0, :main, cols], preferred_element_type=F32)
        for e in range(1, n_exp):
            acc = acc + jnp.dot(oh_scr[e, :, :main], y_refs[e][0, :main, cols], preferred_element_type=F32)
        o_ref[:, cols] = x_ref[:, cols] + g2_ref[:, cols] * acc
    if win > main:
        for e in range(n_exp):
            @pl.when(spill[e])
            def _(e=e):
                tail = jnp.dot(oh_scr[e, :, main:], y_refs[e][0, main:, :], preferred_element_type=F32)
                o_ref[...] += g2_ref[...] * tail


def _moe_combine(x, g2, pos, y, cap, tt):
    b, l, d = x.shape
    e = y.shape[0]
    tiles = l // tt
    win = min(tt + COMBINE_ALIGN, cap)
    sel = (pos >= 0).reshape(b, tiles, tt, e).sum(2)
    first = jnp.cumsum(sel, axis=1) - sel
    start = jnp.minimum(first // COMBINE_ALIGN * COMBINE_ALIGN, cap - win).astype(jnp.int32)
    start = jnp.transpose(start, (0, 2, 1)).reshape(-1)
    nb = g2.shape[0]

    def y_spec(ei):
        def y_map(bi, j, st):
            return ei, pl.multiple_of(bi * cap + st[(bi * e + ei) * tiles + j], COMBINE_ALIGN), 0
        return pl.BlockSpec((pl.Element(1), pl.Element(win), pl.Element(d)), y_map)

    grid_spec = pltpu.PrefetchScalarGridSpec(
        num_scalar_prefetch=1,
        grid=(b, tiles),
        in_specs=[pl.BlockSpec((None, tt, d), lambda bi, j, st: (bi, j, 0)),
                  pl.BlockSpec((None, 1, d), lambda bi, j, st: (bi if nb > 1 else 0, 0, 0)),
                  pl.BlockSpec((None, tt, e), lambda bi, j, st: (bi, j, 0))] + [y_spec(ei) for ei in range(e)],
        out_specs=pl.BlockSpec((None, tt, d), lambda bi, j, st: (bi, j, 0)),
        scratch_shapes=[pltpu.VMEM((e, tt, win), BF16)])
    return pl.pallas_call(
        functools.partial(_combine_kernel, n_exp=e, tiles=tiles, tn=min(256, d)),
        grid_spec=grid_spec,
        out_shape=jax.ShapeDtypeStruct((b, l, d), F32),
        compiler_params=_cparams(("parallel", "parallel"), vmem_mb=56),
        name="moe_combine",
    )(start, x, g2, pos, *([y] * e))


def _ec_moe_update(x, g, sh, sc, g2, p, tm):
    b, l, d = x.shape
    e = p['router'].shape[1]
    cap = max(1, EC_CAPACITY * l // e)
    h, aff_t = _router(x, g, sh, sc, jnp.transpose(p['router']).astype(F32), tm)
    gate, idx = lax.top_k(aff_t, cap)
    idx, gate = lax.sort_key_val(idx, gate, dimension=-1)
    bidx = jnp.arange(b)[:, None, None]
    eidx = jnp.arange(e)[None, :, None]
    pos = jnp.full((b, l, e), -1, jnp.int32).at[bidx, idx, eidx].set(
        jnp.broadcast_to(jnp.arange(cap, dtype=jnp.int32), idx.shape), unique_indices=True)
    idx_t = jnp.transpose(idx, (1, 0, 2))
    xs = h[jnp.arange(b)[None, :, None], idx_t].reshape(e, b * cap, d)
    gt = jnp.transpose(gate, (1, 0, 2)).reshape(e, b * cap, 1)
    tr = min(1024, b * cap)
    ff = p['exp_w1'].shape[3]
    y = _expert_ffn(xs, gt, p['exp_w1'], p['exp_w3'], p['exp_w2'], p['layer'], tr=tr, tf=min(512, ff))
    return _moe_combine(x, g2, pos, y, cap, min(256, l))


def _layer(x, xc, c, c_ctx, p, last):
    b, l, d = x.shape
    lc = xc.shape[1]
    w = p['ssm_d'].shape[0]
    hd = w // NA_HEADS

    cc = jnp.zeros((8, d), F32).at[:b].set(c).at[b].set(c_ctx)
    mod = _ada(cc, p['w_ada'], p['b_ada'], p['layer'])
    m_l = [t[:, None, :] for t in jnp.split(mod[:b], 6, axis=-1)]
    m_c = [t[:, None, :] for t in jnp.split(mod[b:b + 1], 6, axis=-1)]

    w_in = p['w_in'].astype(BF16)
    scale = hd ** -0.5
    head_gain = jnp.stack([jnp.tile(p['na_q_gain'].astype(F32) * scale, NA_HEADS),
                           jnp.tile(p['na_k_gain'].astype(F32), NA_HEADS)])[:, None, :]
    ones_blk = jnp.asarray(np.kron(np.eye(NA_HEADS), np.ones((hd, hd))), BF16)
    tm_l = min(512, l)
    tm_c = min(512, lc)
    proj = _in_proj(x, p['norm1'], m_l[0], m_l[1], w_in, head_gain, ones_blk, min(1024, l))
    projc = _in_proj(xc, p['norm1'], m_c[0], m_c[1], w_in, head_gain, ones_blk, tm_c)

    ssm_l, ssm_c = _s5_branch(projc[..., :w], proj[..., :w], p, _s5_tables(p), need_ctx=not last)

    na_l = _na_latent(proj[..., w:2 * w], proj[..., 2 * w:3 * w], proj[..., 3 * w:4 * w],
                      projc[..., 2 * w:3 * w], projc[..., 3 * w:4 * w], _na_bias_table(p['na_rpb']))
    hy_l = _hyena_branch(proj, 4 * w // 128, p)

    wg = p['w_gate'].astype(BF16)
    wb = p['w_branch'].astype(BF16)
    wo = p['w_out'].astype(BF16)
    x = _merge(x, p['norm1'], m_l[0], m_l[1], m_l[2], (ssm_l, na_l, hy_l), wg, p['b_gate'], wb, wo,
               tm=tm_l, tc=min(512, d))
    x = _ec_moe_update(x, p['norm2'], m_l[3], m_l[4], m_l[5], p, tm_l)
    if last:
        return x, None
    na_c = _na_context(projc[..., w:2 * w], projc[..., 2 * w:3 * w], projc[..., 3 * w:4 * w])
    hy_c = _hyena_branch(projc, 4 * w // 128, p)
    xc = _merge(xc, p['norm1'], m_c[0], m_c[1], m_c[2], (ssm_c, na_c, hy_c), wg, p['b_gate'], wb, wo,
                tm=tm_c, tc=min(512, d))
    xc = _ec_moe_update(xc, p['norm2'], m_c[3], m_c[4], m_c[5], p, tm_c)
    return x, xc


_PARAM_NAMES = ('w_ada', 'b_ada', 'norm1', 'norm2', 'w_in',
                'ssm_lam_re', 'ssm_lam_im', 'ssm_log_step', 'ssm_b_re', 'ssm_b_im', 'ssm_c_re', 'ssm_c_im',
                'ssm_d', 'ssm_w_glu', 'na_q_gain', 'na_k_gain', 'na_rpb',
                'hy_conv_w', 'hy_conv_b', 'hy_w1', 'hy_b1', 'hy_w2', 'hy_b2', 'hy_w3', 'hy_b3', 'hy_w4',
                'hy_freq', 'hy_bias', 'w_gate', 'b_gate', 'w_branch', 'w_out',
                'router', 'exp_w1', 'exp_w3', 'exp_w2')
_STACKED = ('w_ada', 'b_ada', 'exp_w1', 'exp_w3', 'exp_w2')


def kernel(x, c, ctx, c_ctx, w_ada, b_ada, norm1, norm2, w_in, ssm_lam_re, ssm_lam_im, ssm_log_step, ssm_b_re, ssm_b_im, ssm_c_re, ssm_c_im, ssm_d, ssm_w_glu, na_q_gain, na_k_gain, na_rpb, hy_conv_w, hy_conv_b, hy_w1, hy_b1, hy_w2, hy_b2, hy_w3, hy_b3, hy_w4, hy_freq, hy_bias, w_gate, b_gate, w_branch, w_out, router, exp_w1, exp_w3, exp_w2):
    stacked = (w_ada, b_ada, norm1, norm2, w_in, ssm_lam_re, ssm_lam_im, ssm_log_step, ssm_b_re, ssm_b_im,
               ssm_c_re, ssm_c_im, ssm_d, ssm_w_glu, na_q_gain, na_k_gain, na_rpb, hy_conv_w, hy_conv_b,
               hy_w1, hy_b1, hy_w2, hy_b2, hy_w3, hy_b3, hy_w4, hy_freq, hy_bias, w_gate, b_gate, w_branch,
               w_out, router, exp_w1, exp_w3, exp_w2)
    depth = w_ada.shape[0]
    xc = ctx
    for layer in range(depth):
        p = {name: (t if name in _STACKED else t[layer]) for name, t in zip(_PARAM_NAMES, stacked)}
        p['layer'] = layer
        x, xc = _layer(x, xc, c, c_ctx, p, layer == depth - 1)
    return x
```

```python
import functools
import math

import numpy as np
import jax
import jax.numpy as jnp
from jax import lax
from jax.experimental import pallas as pl
from jax.experimental.pallas import tpu as pltpu

BF16 = jnp.bfloat16
F32 = jnp.float32

NORM_EPS = 1e-6
GRID_W = 64
SSM_GROUP = 16
NA_HEADS = 8
NA_WIN_ROWS = 8
NA_WIN_COLS = 16
HY_ORDER = 2
HY_BANDS = 16
HY_MIN_DECAY = math.log(1e-2) / 1.5
HY_MAX_DECAY = math.log(1e-2) / 0.3
EC_CAPACITY = 2
S5_CHUNK = 16
HY_FAST = 128
HY_PAD = 8
MASK_VALUE = -1e30
COMBINE_ALIGN = 16

V7X_VMEM_BYTES = 64 * 1024 * 1024


def _cparams(semantics, vmem_mb=48):
    return pltpu.CompilerParams(dimension_semantics=semantics, vmem_limit_bytes=vmem_mb * 1024 * 1024)


def _norm_mod(x, g, sh, sc):
    ms = jnp.mean(x * x, axis=-1, keepdims=True)
    return (x * lax.rsqrt(ms + NORM_EPS) * g) * (1.0 + sc) + sh


def _ada_kernel(c_ref, w_ref, b_ref, o_ref):
    c = c_ref[...]
    a = (c * jax.nn.sigmoid(c)).astype(BF16)
    o_ref[...] = jnp.dot(a, w_ref[...].astype(BF16), preferred_element_type=F32) + b_ref[...]


def _ada(cc, w, b, layer):
    m, k = cc.shape
    depth, _, n = w.shape
    tn = next(t for t in (1024, 512, 256, 128) if n % t == 0)
    return pl.pallas_call(
        _ada_kernel,
        grid=(n // tn,),
        in_specs=[pl.BlockSpec((m, k), lambda j: (0, 0)),
                  pl.BlockSpec((None, k, tn), lambda j: (layer, 0, j)),
                  pl.BlockSpec((None, 1, tn), lambda j: (layer, 0, j))],
        out_specs=pl.BlockSpec((m, tn), lambda j: (0, j)),
        out_shape=jax.ShapeDtypeStruct((m, n), F32),
        compiler_params=_cparams(("arbitrary",)),
        name="ada",
    )(cc, w, b.reshape(depth, 1, n))


def _in_proj_kernel(x_ref, g_ref, sh_ref, sc_ref, w_ref, hg_ref, ones_ref, o_ref, h_scr, *, head_dim):
    j = pl.program_id(2)

    @pl.when(j == 0)
    def _():
        h_scr[...] = _norm_mod(x_ref[...], g_ref[...], sh_ref[...], sc_ref[...]).astype(BF16)

    acc = jnp.dot(h_scr[...], w_ref[...], preferred_element_type=F32)
    is_qk = jnp.logical_or(j == 1, j == 2)

    @pl.when(is_qk)
    def _():
        yy = acc * acc
        hi = yy.astype(BF16)
        lo = (yy - hi.astype(F32)).astype(BF16)
        ss = (jnp.dot(hi, ones_ref[...], preferred_element_type=F32)
              + jnp.dot(lo, ones_ref[...], preferred_element_type=F32))
        o_ref[...] = (acc * lax.rsqrt(ss * (1.0 / head_dim) + NORM_EPS) * hg_ref[...]).astype(o_ref.dtype)

    @pl.when(jnp.logical_not(is_qk))
    def _():
        o_ref[...] = acc.astype(o_ref.dtype)


def _in_proj(x, g, sh, sc, w_bf, head_gain, ones_blk, tm):
    b, l, d = x.shape
    n = w_bf.shape[1]
    tn = ones_blk.shape[0]
    nb = sh.shape[0]
    mod_map = (lambda bi, i, j: (bi, 0, 0)) if nb > 1 else (lambda bi, i, j: (0, 0, 0))
    return pl.pallas_call(
        functools.partial(_in_proj_kernel, head_dim=tn // NA_HEADS),
        grid=(b, l // tm, n // tn),
        in_specs=[pl.BlockSpec((None, tm, d), lambda bi, i, j: (bi, i, 0)),
                  pl.BlockSpec((1, d), lambda bi, i, j: (0, 0)),
                  pl.BlockSpec((None, 1, d), mod_map),
                  pl.BlockSpec((None, 1, d), mod_map),
                  pl.BlockSpec((d, tn), lambda bi, i, j: (0, j)),
                  pl.BlockSpec((None, 1, tn), lambda bi, i, j: (jnp.clip(j - 1, 0, 1), 0, 0)),
                  pl.BlockSpec((tn, tn), lambda bi, i, j: (0, 0))],
        out_specs=pl.BlockSpec((None, tm, tn), lambda bi, i, j: (bi, i, j)),
        out_shape=jax.ShapeDtypeStruct((b, l, n), BF16),
        scratch_shapes=[pltpu.VMEM((tm, d), BF16)],
        compiler_params=_cparams(("parallel", "parallel", "arbitrary")),
        name="in_proj",
    )(x, g.reshape(1, d), sh, sc, w_bf, head_gain, ones_blk)


def _bmm_kernel(a_ref, w_ref, o_ref):
    o_ref[...] = jnp.dot(a_ref[...].astype(BF16), w_ref[...], preferred_element_type=F32).astype(o_ref.dtype)


def _bmm_add_kernel(a_ref, w_ref, y_ref, o_ref):
    o_ref[...] = (y_ref[...] + jnp.dot(a_ref[...].astype(BF16), w_ref[...],
                                       preferred_element_type=F32)).astype(o_ref.dtype)


def _bmm(a, w, add=None, tm=None, name="bmm"):
    g, m, k = a.shape
    n = w.shape[2]
    tm = m if tm is None else tm
    in_specs = [pl.BlockSpec((None, tm, k), lambda gi, i: (gi, i, 0)),
                pl.BlockSpec((None, k, n), lambda gi, i: (gi, 0, 0))]
    args = [a, w]
    kern = _bmm_kernel
    if add is not None:
        in_specs.append(pl.BlockSpec((None, tm, n), lambda gi, i: (gi, i, 0)))
        args.append(add)
        kern = _bmm_add_kernel
    return pl.pallas_call(
        kern,
        grid=(g, m // tm),
        in_specs=in_specs,
        out_specs=pl.BlockSpec((None, tm, n), lambda gi, i: (gi, i, 0)),
        out_shape=jax.ShapeDtypeStruct((g, m, n), F32),
        compiler_params=_cparams(("parallel", "parallel")),
        name=name,
    )(*args)


def _gelu_tanh(x):
    return 0.5 * x * (1.0 + jnp.tanh(math.sqrt(2.0 / math.pi) * (x + 0.044715 * (x * x * x))))


def _s5_out_kernel(y_ref, u_ref, d_ref, w_ref, o_ref):
    y = y_ref[...].astype(F32) + d_ref[...] * u_ref[...].astype(F32)
    z = _gelu_tanh(y)
    gate = jax.nn.sigmoid(jnp.dot(z.astype(BF16), w_ref[...], preferred_element_type=F32))
    o_ref[...] = (z * gate).astype(o_ref.dtype)


def _s5_out(y, row0, u, d, w_glu_bf):
    b, l, w = u.shape
    tm = math.gcd(math.gcd(l, row0), 512) if row0 else math.gcd(l, 512)
    off = row0 // tm
    return pl.pallas_call(
        _s5_out_kernel,
        grid=(b, l // tm),
        in_specs=[pl.BlockSpec((None, tm, w), lambda bi, i: (bi, off + i, 0)),
                  pl.BlockSpec((None, tm, w), lambda bi, i: (bi, i, 0)),
                  pl.BlockSpec((1, w), lambda bi, i: (0, 0)),
                  pl.BlockSpec((w, w), lambda bi, i: (0, 0))],
        out_specs=pl.BlockSpec((None, tm, w), lambda bi, i: (bi, i, 0)),
        out_shape=jax.ShapeDtypeStruct((b, l, w), BF16),
        compiler_params=_cparams(("parallel", "parallel")),
        name="s5_out",
    )(y, u, d.reshape(1, w), w_glu_bf)


def _s5_tables(p):
    t = S5_CHUNK
    mats, sums, reads, steps = [], [], [], []
    for d in range(2):
        lam_re = p['ssm_lam_re'][d].astype(F32)
        lam_im = p['ssm_lam_im'][d].astype(F32)
        step = jnp.exp(p['ssm_log_step'][d].astype(F32))[:, None]
        mag = jnp.exp(lam_re * step)
        a_re = mag * jnp.cos(lam_im * step)
        a_im = mag * jnp.sin(lam_im * step)
        den = lam_re * lam_re + lam_im * lam_im
        num_re = a_re - 1.0
        coef_re = (num_re * lam_re + a_im * lam_im) / den
        coef_im = (a_im * lam_re - num_re * lam_im) / den
        b_re = p['ssm_b_re'][d].astype(F32)
        b_im = p['ssm_b_im'][d].astype(F32)
        bb_re = coef_re[..., None] * b_re - coef_im[..., None] * b_im
        bb_im = coef_re[..., None] * b_im + coef_im[..., None] * b_re
        c_re = p['ssm_c_re'][d].astype(F32)
        c_im = p['ssm_c_im'][d].astype(F32)
        n = jnp.arange(t + 1, dtype=F32)[:, None, None]
        pm = jnp.exp(lam_re * step * n)
        pw_re = pm * jnp.cos(lam_im * step * n)
        pw_im = pm * jnp.sin(lam_im * step * n)
        ca_re = c_re[None] * pw_re[:, :, None, :] - c_im[None] * pw_im[:, :, None, :]
        ca_im = c_re[None] * pw_im[:, :, None, :] + c_im[None] * pw_re[:, :, None, :]
        ktau = (jnp.einsum('ngop,gpi->ngoi', ca_re[:t], bb_re)
                - jnp.einsum('ngop,gpi->ngoi', ca_im[:t], bb_im))
        s_idx = np.arange(t)[:, None]
        t_idx = np.arange(t)[None, :]
        lag = (t_idx - s_idx) if d == 0 else (s_idx - t_idx)
        lag_sel = jnp.asarray(lag[:, :, None] == np.arange(t), F32)
        blk = jnp.einsum('stn,ngoi->gsito', lag_sel, ktau, precision=lax.Precision.HIGHEST)
        mats.append(blk)
        flip = (lambda a: a[::-1]) if d == 0 else (lambda a: a)
        unflip = (lambda a: a) if d == 0 else (lambda a: a[::-1])
        sp_re, sp_im = flip(pw_re[:t]), flip(pw_im[:t])
        sw_re = sp_re[:, :, :, None] * bb_re[None] - sp_im[:, :, :, None] * bb_im[None]
        sw_im = sp_re[:, :, :, None] * bb_im[None] + sp_im[:, :, :, None] * bb_re[None]
        sums.append((jnp.transpose(sw_re, (1, 0, 3, 2)), jnp.transpose(sw_im, (1, 0, 3, 2))))
        rd_re, rd_im = unflip(ca_re[1:t + 1]), unflip(ca_im[1:t + 1])
        reads.append((jnp.transpose(rd_re, (1, 3, 0, 2)), -jnp.transpose(rd_im, (1, 3, 0, 2))))
        steps.append((pw_re[t], pw_im[t]))
    g, pdim = steps[0][0].shape
    k = SSM_GROUP
    mtot = (mats[0] + mats[1]).reshape(g, t * k, t * k)
    ssum = jnp.concatenate([sums[0][0], sums[0][1], sums[1][0], sums[1][1]], axis=-1).reshape(g, t * k, 4 * pdim)
    r = jnp.concatenate([reads[0][0], reads[0][1], reads[1][0], reads[1][1]], axis=1).reshape(g, 4 * pdim, t * k)
    tk = t * k
    eye = jnp.eye(2, dtype=F32)
    ymat = jnp.einsum('gjsito,jk->gsjitko', mtot.reshape(g // 2, 2, t, k, t, k), eye).reshape(g // 2, 2 * tk, 2 * tk)
    smat = jnp.einsum('gjsiqp,jk->gsjiqkp', ssum.reshape(g // 2, 2, t, k, 4, pdim), eye
                      ).reshape(g // 2, 2 * tk, 8 * pdim)
    w1 = jnp.concatenate([ymat, smat], axis=-1)
    r2 = jnp.einsum('gjqpto,jk->gqjptko', r.reshape(g // 2, 2, 4, pdim, t, k), eye).reshape(g // 2, 8 * pdim, 2 * tk)
    coef = [c.reshape(1, g * pdim) for c in (steps[0][0], steps[0][1], steps[1][0], steps[1][1])]
    return w1.astype(BF16), r2.astype(BF16), coef


LANES = 128


def _to_groups_kernel(u_ref, o_ref, u_scr, o_scr, *, cbk, bsz):
    t, k2 = S5_CHUNK, 2 * SSM_GROUP
    per = LANES // k2
    for b in range(bsz):
        for lt in range(u_scr.shape[0]):
            u_scr[lt] = u_ref[b, :, lt * LANES:(lt + 1) * LANES].astype(F32)
        for lt in range(u_scr.shape[0]):
            xs = [u_scr[lt, pl.ds(s, cbk, stride=t), :] for s in range(t)]
            for q in range(per):
                v = jnp.concatenate([x[:, q * k2:(q + 1) * k2] for x in xs], axis=1)
                for ot in range(o_scr.shape[1]):
                    o_scr[lt * per + q, ot, pl.ds(b, cbk, stride=bsz), :] = v[:, ot * LANES:(ot + 1) * LANES]
    for gp in range(o_scr.shape[0]):
        for ot in range(o_scr.shape[1]):
            o_ref[gp, :, ot * LANES:(ot + 1) * LANES] = o_scr[gp, ot].astype(o_ref.dtype)


def _from_groups_kernel(y_ref, o_ref, y_scr, o_scr, *, cbk, bsz):
    t, k2 = S5_CHUNK, 2 * SSM_GROUP
    per = LANES // k2
    ngp, nlt = y_scr.shape[0], y_scr.shape[1]
    for gp in range(ngp):
        for ot in range(nlt):
            y_scr[gp, ot] = y_ref[gp, :, ot * LANES:(ot + 1) * LANES].astype(F32)
    for b in range(bsz):
        for ot in range(nlt):
            zs = [y_scr[gp, ot, pl.ds(b, cbk, stride=bsz), :] for gp in range(ngp)]
            for q in range(per):
                s = ot * per + q
                v = jnp.concatenate([z[:, q * k2:(q + 1) * k2] for z in zs], axis=1)
                for lt in range(o_scr.shape[0]):
                    o_scr[lt, pl.ds(s, cbk, stride=t), :] = v[:, lt * LANES:(lt + 1) * LANES]
        for lt in range(o_scr.shape[0]):
            o_ref[b, :, lt * LANES:(lt + 1) * LANES] = o_scr[lt].astype(o_ref.dtype)


def _chunk_block(nc, bsz):
    return next(c for c in range(min(64, nc), 0, -1) if nc % c == 0 and (c * bsz) % 16 == 0)


def _to_groups(u):
    bsz, ltot, w = u.shape
    t, k2 = S5_CHUNK, 2 * SSM_GROUP
    nc = ltot // t
    cbk = _chunk_block(nc, bsz)
    ngp = w // k2
    return pl.pallas_call(
        functools.partial(_to_groups_kernel, cbk=cbk, bsz=bsz),
        grid=(nc // cbk,),
        in_specs=[pl.BlockSpec((bsz, cbk * t, w), lambda i: (0, i, 0))],
        out_specs=pl.BlockSpec((ngp, cbk * bsz, t * k2), lambda i: (0, i, 0)),
        out_shape=jax.ShapeDtypeStruct((ngp, nc * bsz, t * k2), BF16),
        scratch_shapes=[pltpu.VMEM((w // LANES, cbk * t, LANES), F32),
                        pltpu.VMEM((ngp, t * k2 // LANES, cbk * bsz, LANES), F32)],
        compiler_params=_cparams(("parallel",)),
        name="s5_to_groups",
    )(u)


def _from_groups(y, bsz):
    ngp, m, tk2 = y.shape
    t, k2 = S5_CHUNK, 2 * SSM_GROUP
    nc = m // bsz
    cbk = _chunk_block(nc, bsz)
    w = ngp * k2
    return pl.pallas_call(
        functools.partial(_from_groups_kernel, cbk=cbk, bsz=bsz),
        grid=(nc // cbk,),
        in_specs=[pl.BlockSpec((ngp, cbk * bsz, tk2), lambda i: (0, i, 0))],
        out_specs=pl.BlockSpec((bsz, cbk * t, w), lambda i: (0, i, 0)),
        out_shape=jax.ShapeDtypeStruct((bsz, nc * t, w), BF16),
        scratch_shapes=[pltpu.VMEM((ngp, tk2 // LANES, cbk * bsz, LANES), F32),
                        pltpu.VMEM((w // LANES, cbk * t, LANES), F32)],
        compiler_params=_cparams(("parallel",)),
        name="s5_from_groups",
    )(y)


def _s5_chunk_kernel(u_ref, w_ref, y_ref, xfr_ref, xfi_ref, xbr_ref, xbi_ref):
    r = jnp.dot(u_ref[...], w_ref[...], preferred_element_type=F32)
    ny = y_ref.shape[-1]
    nx = xfr_ref.shape[-1]
    y_ref[...] = r[:, :ny]
    for i, ref in enumerate((xfr_ref, xfi_ref, xbr_ref, xbi_ref)):
        ref[...] = r[:, ny + i * nx:ny + (i + 1) * nx]


def _s5_chunk(ug, w1):
    gp, m, tk2 = ug.shape
    nx = (w1.shape[2] - tk2) // 4
    xspec = pl.BlockSpec((m, nx), lambda g: (0, g))
    xshape = jax.ShapeDtypeStruct((m, gp * nx), F32)
    return pl.pallas_call(
        _s5_chunk_kernel,
        grid=(gp,),
        in_specs=[pl.BlockSpec((None, m, tk2), lambda g: (g, 0, 0)),
                  pl.BlockSpec((None, tk2, w1.shape[2]), lambda g: (g, 0, 0))],
        out_specs=[pl.BlockSpec((None, m, tk2), lambda g: (g, 0, 0)), xspec, xspec, xspec, xspec],
        out_shape=[jax.ShapeDtypeStruct((gp, m, tk2), F32), xshape, xshape, xshape, xshape],
        compiler_params=_cparams(("parallel",)),
        name="s5_chunk",
    )(ug, w1)


def _s5_scan_kernel(xfr, xfi, xbr, xbi, afr, afi, abr, abi, hfr, hfi, hbr, hbi, *, tiles_ctx, tiles, bsz):
    shape = (2 * bsz, xfr.shape[1])
    low = lax.broadcasted_iota(jnp.int32, shape, 0) < bsz
    zero = jnp.zeros(shape, F32)

    def step(hr, hi, ar, ai, xr, xi):
        return ar * hr - ai * hi + xr, ar * hi + ai * hr + xi

    def fwd_tile(i, carry):
        ar, ai = jnp.broadcast_to(afr[...], shape), jnp.broadcast_to(afi[...], shape)
        r0 = pl.multiple_of(i * 2 * bsz, 2 * bsz)
        xr, xi = xfr[pl.ds(r0, 2 * bsz), :], xfi[pl.ds(r0, 2 * bsz), :]
        sr, si = pltpu.roll(carry[0], bsz, 0), pltpu.roll(carry[1], bsz, 0)
        tr, ti = step(sr, si, ar, ai, xr, xi)
        t2r, t2i = pltpu.roll(tr, bsz, 0), pltpu.roll(ti, bsz, 0)
        hfr[pl.ds(r0, 2 * bsz), :] = jnp.where(low, sr, t2r)
        hfi[pl.ds(r0, 2 * bsz), :] = jnp.where(low, si, t2i)
        return step(t2r, t2i, ar, ai, xr, xi)

    lax.fori_loop(0, tiles, fwd_tile, (zero, zero))

    def bwd_tile(i, carry):
        ar, ai = jnp.broadcast_to(abr[...], shape), jnp.broadcast_to(abi[...], shape)
        r0 = pl.multiple_of(i * 2 * bsz, 2 * bsz)
        xr, xi = xbr[pl.ds(r0, 2 * bsz), :], xbi[pl.ds(r0, 2 * bsz), :]
        sr, si = pltpu.roll(carry[0], bsz, 0), pltpu.roll(carry[1], bsz, 0)
        tr, ti = step(sr, si, ar, ai, xr, xi)
        t2r, t2i = pltpu.roll(tr, bsz, 0), pltpu.roll(ti, bsz, 0)
        hbr[pl.ds(r0, 2 * bsz), :] = jnp.where(low, t2r, sr)
        hbi[pl.ds(r0, 2 * bsz), :] = jnp.where(low, t2i, si)
        return step(t2r, t2i, ar, ai, xr, xi)

    carry = lax.fori_loop(0, tiles_ctx, lambda j, c: bwd_tile(tiles_ctx - 1 - j, c), (zero, zero))
    lax.fori_loop(0, tiles - tiles_ctx, lambda j, c: bwd_tile(tiles - 1 - j, c), carry)


def _s5_scan(xs, coef, chunks_ctx, chunks, bsz):
    m, lanes = xs[0].shape
    assert 2 * bsz == 8 and chunks % 2 == 0 and chunks_ctx % 2 == 0
    lb = min(256, lanes)
    xspec = pl.BlockSpec((m, lb), lambda j: (0, j))
    cspec = pl.BlockSpec((1, lb), lambda j: (0, j))
    shape = jax.ShapeDtypeStruct((m, lanes), F32)
    return pl.pallas_call(
        functools.partial(_s5_scan_kernel, tiles_ctx=chunks_ctx // 2, tiles=chunks // 2, bsz=bsz),
        grid=(lanes // lb,),
        in_specs=[xspec] * 4 + [cspec] * 4,
        out_specs=[xspec] * 4,
        out_shape=[shape] * 4,
        compiler_params=_cparams(("parallel",)),
        name="s5_scan",
    )(*xs, *coef)


def _s5_readout_kernel(y_ref, hfr, hfi, hbr, hbi, r_ref, o_ref):
    h = jnp.concatenate([hfr[...], hfi[...], hbr[...], hbi[...]], axis=1).astype(BF16)
    o_ref[...] = (y_ref[...] + jnp.dot(h, r_ref[...], preferred_element_type=F32)).astype(o_ref.dtype)


def _s5_readout(y0, hs, r2):
    gp, m, tk2 = y0.shape
    nx = hs[0].shape[1] // gp
    hspec = pl.BlockSpec((m, nx), lambda g: (0, g))
    yspec = pl.BlockSpec((None, m, tk2), lambda g: (g, 0, 0))
    return pl.pallas_call(
        _s5_readout_kernel,
        grid=(gp,),
        in_specs=[yspec, hspec, hspec, hspec, hspec, pl.BlockSpec((None, 4 * nx, tk2), lambda g: (g, 0, 0))],
        out_specs=yspec,
        out_shape=jax.ShapeDtypeStruct((gp, m, tk2), BF16),
        compiler_params=_cparams(("parallel",)),
        name="s5_readout",
    )(y0, *hs, r2)


def _s5_branch(u_ctx, u_lat, p, tables, need_ctx):
    w1, r2, coef = tables
    b, l, w = u_lat.shape
    lc = u_ctx.shape[1]
    t, k = S5_CHUNK, SSM_GROUP
    g = w // k
    nc = (lc + l) // t
    m = nc * b
    ug = _to_groups(jnp.concatenate([u_ctx, u_lat], axis=1))
    y0, *xs = _s5_chunk(ug, w1)
    hs = _s5_scan(xs, coef, lc // t, nc, b)
    y = _from_groups(_s5_readout(y0, hs, r2), b)
    w_glu = p['ssm_w_glu'].astype(BF16)
    out_l = _s5_out(y, lc, u_lat, p['ssm_d'], w_glu)
    out_c = _s5_out(y, 0, u_ctx, p['ssm_d'], w_glu) if need_ctx else None
    return out_l, out_c


def _na_kernel(q_ref, k_ref, v_ref, kc_ref, vc_ref, bias_ref, o_ref, s_scr, p_scr, *, rows, head_dim):
    r = pl.program_id(1)
    kr = NA_WIN_ROWS
    start = jnp.clip(r - kr // 2, 0, rows - kr)
    cls = r - start
    base = pl.multiple_of(start * GRID_W, GRID_W)
    kwin = k_ref[pl.ds(base, kr * GRID_W), :]
    vwin = v_ref[pl.ds(base, kr * GRID_W), :]
    q = q_ref[...]
    kc = kc_ref[...]
    vc = vc_ref[...]
    outs = []
    nt = (((1,), (1,)), ((), ()))
    lane_tile = 2 * head_dim
    nwin = kwin.shape[0]
    first_head = lax.broadcasted_iota(jnp.int32, (q.shape[0], lane_tile), 1) < head_dim
    for h in range(NA_HEADS):
        sl = slice(h // 2 * lane_tile, (h // 2 + 1) * lane_tile)
        qh = jnp.where(first_head == (h % 2 == 0), q[:, sl], jnp.zeros_like(q[:, sl]))
        s_scr[h, :, :nwin] = lax.dot_general(qh, kwin[:, sl], nt, preferred_element_type=F32) + bias_ref[cls, h]
        s_scr[h, :, nwin:] = lax.dot_general(qh, kc[:, sl], nt, preferred_element_type=F32)
    inv = []
    for h in range(NA_HEADS):
        s = s_scr[h]
        pr = jnp.exp(s - jnp.max(s, axis=-1, keepdims=True))
        inv.append(1.0 / jnp.sum(pr, axis=-1, keepdims=True))
        p_scr[h] = pr.astype(BF16)
    for pair in range(NA_HEADS // 2):
        sl = slice(pair * lane_tile, (pair + 1) * lane_tile)
        halves = []
        for half in range(2):
            h = 2 * pair + half
            o = (jnp.dot(p_scr[h, :, :nwin], vwin[:, sl], preferred_element_type=F32)
                 + jnp.dot(p_scr[h, :, nwin:], vc[:, sl], preferred_element_type=F32))
            halves.append(o * inv[h])
        outs.append(jnp.where(first_head, halves[0], halves[1]))
    o_ref[...] = jnp.concatenate(outs, axis=-1).astype(o_ref.dtype)


def _na_bias_table(rpb):
    kr, kw, w = NA_WIN_ROWS, NA_WIN_COLS, GRID_W
    col = np.arange(w)
    cstart = np.clip(col - kw // 2, 0, w - kw)
    kcol = np.arange(w)
    inwin = (kcol[None, :] >= cstart[:, None]) & (kcol[None, :] < cstart[:, None] + kw)
    dc = np.clip(kcol[None, :] - col[:, None] + (kw - 1), 0, 2 * kw - 2)
    rpb = rpb.astype(F32)
    dr = np.arange(kr)[None, :] - np.arange(kr)[:, None] + (kr - 1)
    row_sel = jnp.asarray(dr[:, :, None] == np.arange(2 * kr - 1), F32)
    col_sel = jnp.asarray((dc[:, :, None] == np.arange(2 * kw - 1)) & inwin[:, :, None], F32)
    hp = lax.Precision.HIGHEST
    rows = jnp.einsum('hab,oja->ohjb', rpb, row_sel, precision=hp)
    bias = jnp.einsum('ohjb,cdb->ohcjd', rows, col_sel, precision=hp)
    bias = jnp.where(jnp.asarray(inwin)[None, None, :, None, :], bias, MASK_VALUE)
    return bias.reshape(kr, rpb.shape[0], w, kr * w)


def _na_latent(q, k, v, kc, vc, bias_tab):
    b, l, w = q.shape
    lc = kc.shape[1]
    rows = l // GRID_W
    assert rows >= NA_WIN_ROWS
    nk = NA_WIN_ROWS * GRID_W
    return pl.pallas_call(
        functools.partial(_na_kernel, rows=rows, head_dim=w // NA_HEADS),
        grid=(b, rows),
        in_specs=[pl.BlockSpec((None, GRID_W, w), lambda bi, r: (bi, r, 0)),
                  pl.BlockSpec((None, l, w), lambda bi, r: (bi, 0, 0)),
                  pl.BlockSpec((None, l, w), lambda bi, r: (bi, 0, 0)),
                  pl.BlockSpec((None, lc, w), lambda bi, r: (bi, 0, 0)),
                  pl.BlockSpec((None, lc, w), lambda bi, r: (bi, 0, 0)),
                  pl.BlockSpec((NA_WIN_ROWS, NA_HEADS, GRID_W, nk), lambda bi, r: (0, 0, 0, 0))],
        out_specs=pl.BlockSpec((None, GRID_W, w), lambda bi, r: (bi, r, 0)),
        out_shape=jax.ShapeDtypeStruct((b, l, w), BF16),
        scratch_shapes=[pltpu.VMEM((NA_HEADS, GRID_W, nk + lc), F32), pltpu.VMEM((NA_HEADS, GRID_W, nk + lc), BF16)],
        compiler_params=_cparams(("parallel", "arbitrary"), vmem_mb=56),
        name="na_latent",
    )(q, k, v, kc, vc, bias_tab)


def _na_ctx_kernel(q_ref, k_ref, v_ref, o_ref, *, head_dim):
    q, k, v = q_ref[...], k_ref[...], v_ref[...]
    outs = []
    for h in range(NA_HEADS):
        sl = slice(h * head_dim, (h + 1) * head_dim)
        s = lax.dot_general(q[:, sl], k[:, sl], (((1,), (1,)), ((), ())), preferred_element_type=F32)
        pr = jnp.exp(s - jnp.max(s, axis=-1, keepdims=True))
        den = jnp.sum(pr, axis=-1, keepdims=True)
        outs.append(jnp.dot(pr.astype(BF16), v[:, sl], preferred_element_type=F32) / den)
    o_ref[...] = jnp.concatenate(outs, axis=-1).astype(o_ref.dtype)


def _na_context(q, k, v):
    b, lc, w = q.shape
    spec = pl.BlockSpec((None, lc, w), lambda bi: (bi, 0, 0))
    return pl.pallas_call(
        functools.partial(_na_ctx_kernel, head_dim=w // NA_HEADS),
        grid=(b,),
        in_specs=[spec, spec, spec],
        out_specs=spec,
        out_shape=jax.ShapeDtypeStruct((b, lc, w), BF16),
        compiler_params=_cparams(("parallel",)),
        name="na_context",
    )(q, k, v)


def _hyena_filter(length, p, slow=None):
    n = 2 * length
    pos = np.arange(n)
    if slow is not None:
        pos = pos.reshape(slow, n // slow).T.reshape(-1)
    off = np.where(pos < length, pos, n - pos)
    backward = jnp.asarray(pos > length)[:, None, None]
    live = jnp.asarray(pos != length, F32)[:, None, None]
    tau = jnp.asarray(np.minimum(off, length - 1), F32)[:, None]
    t = tau / (length - 1)
    freqs = jnp.linspace(1e-4, HY_BANDS - 1, HY_BANDS, dtype=F32)
    ang = (2.0 * math.pi / length) * tau * freqs[None, :]
    z = jnp.concatenate([t, jnp.cos(ang), -jnp.sin(ang)], axis=-1)
    fr = p['hy_freq'].astype(F32)
    hp = lax.Precision.HIGHEST
    h = jnp.sin(fr * (jnp.dot(z, p['hy_w1'].astype(F32), precision=hp) + p['hy_b1'].astype(F32)))
    h = jnp.sin(fr * (jnp.dot(h, p['hy_w2'].astype(F32), precision=hp) + p['hy_b2'].astype(F32)))
    h = jnp.sin(fr * (jnp.dot(h, p['hy_w3'].astype(F32), precision=hp) + p['hy_b3'].astype(F32)))
    wdim = p['hy_bias'].shape[-1]
    h = jnp.dot(h, p['hy_w4'].astype(F32), precision=hp).reshape(n, HY_ORDER, 2, wdim)
    decay = jnp.exp(-t * jnp.abs(jnp.linspace(HY_MIN_DECAY, HY_MAX_DECAY, wdim, dtype=F32)))
    filt = jnp.where(backward, h[:, :, 1], h[:, :, 0]) * (decay[:, None, :] * live)
    filt = filt / jnp.sum(jnp.abs(filt), axis=0, keepdims=True)
    return filt.reshape(n, HY_ORDER * wdim)


def _block2(re, im):
    return np.block([[re, -im], [im, re]])


def _fft_consts(length):
    n = 2 * length
    n2 = HY_FAST
    n1 = n // n2
    h1 = n1 // 2
    a1 = -2.0 * np.pi * np.outer(np.arange(n1), np.arange(n1)) / n1
    a2 = -2.0 * np.pi * np.outer(np.arange(n2), np.arange(n2)) / n2
    f1 = _block2(np.cos(a1[:, :h1]), np.sin(a1[:, :h1]))
    f1_real = np.concatenate([np.cos(a1), np.sin(a1)], axis=0)
    f3 = _block2(np.cos(a2), np.sin(a2))
    g3 = _block2(np.cos(a2), -np.sin(a2))
    g1 = _block2(np.cos(a1[:h1, :]) / n, -np.sin(a1[:h1, :]) / n)
    wa = -2.0 * np.pi * np.arange(n2) / n
    wr = np.broadcast_to(np.cos(wa)[:, None], (n2, 128))
    wi = np.broadcast_to(np.sin(wa)[:, None], (n2, 128))
    cast = lambda m: jnp.asarray(m, BF16)
    return dict(n1=n1, h1=h1, f1=cast(f1), f1_real=cast(f1_real), f3=cast(f3), g3=cast(g3), g1=cast(g1),
                wr=jnp.asarray(wr, F32), wi=jnp.asarray(wi, F32))


def _fft_first_stage(a_r, a_i, load_cols, f1_ref, n1, cb):
    slab = HY_FAST + HY_PAD

    def body(i, carry):
        n2 = 2 * i
        z = jnp.concatenate([load_cols(n2), load_cols(n2 + 1)], axis=1)
        r = jnp.dot(f1_ref[...], z, preferred_element_type=F32)
        a_r[pl.ds(n2, n1, stride=slab), :] = r[:n1, :cb]
        a_i[pl.ds(n2, n1, stride=slab), :] = r[n1:, :cb]
        a_r[pl.ds(n2 + 1, n1, stride=slab), :] = r[:n1, cb:]
        a_i[pl.ds(n2 + 1, n1, stride=slab), :] = r[n1:, cb:]
        return carry

    lax.fori_loop(0, HY_FAST // 2, body, 0, unroll=4)


def _fft_slab_loop(a_r, a_i, f3_ref, wr_ref, wi_ref, n1, cb, finish):
    slab = HY_FAST + HY_PAD
    n2 = HY_FAST
    wr, wi = wr_ref[...], wi_ref[...]
    if cb != wr.shape[1]:
        wr, wi = jnp.tile(wr, (1, cb // wr.shape[1])), jnp.tile(wi, (1, cb // wi.shape[1]))

    def body(i, carry):
        tr, ti = carry
        k1 = 2 * i
        r0 = pl.multiple_of(k1 * slab, 8)
        r1 = pl.multiple_of((k1 + 1) * slab, 8)
        tr1, ti1 = tr * wr - ti * wi, tr * wi + ti * wr
        ar0, ai0 = a_r[pl.ds(r0, n2), :], a_i[pl.ds(r0, n2), :]
        ar1, ai1 = a_r[pl.ds(r1, n2), :], a_i[pl.ds(r1, n2), :]
        xr = jnp.concatenate([ar0 * tr - ai0 * ti, ar1 * tr1 - ai1 * ti1], axis=1)
        xi = jnp.concatenate([ar0 * ti + ai0 * tr, ar1 * ti1 + ai1 * tr1], axis=1)
        bc = jnp.dot(f3_ref[...], jnp.concatenate([xr, xi], axis=0).astype(BF16), preferred_element_type=F32)
        finish(k1, (r0, r1), (tr, ti, tr1, ti1), bc[:n2], bc[n2:])
        return tr1 * wr - ti1 * wi, tr1 * wi + ti1 * wr

    one = jnp.ones((n2, cb), F32)
    lax.fori_loop(0, n1 // 2, body, (one, jnp.zeros_like(one)), unroll=2)


def _fft_conv_kernel(y_ref, sr_ref, si_ref, f1_ref, f3_ref, g3_ref, g1_ref, wr_ref, wi_ref, o_ref, a_r, a_i,
                     *, n1):
    h1 = n1 // 2
    n2 = HY_FAST
    cb = o_ref.shape[-1]
    slab = HY_FAST + HY_PAD

    def load_cols(j):
        r = pl.multiple_of(j * h1, h1)
        return jnp.concatenate([y_ref[0, pl.ds(r, h1), :], y_ref[1, pl.ds(r, h1), :]], axis=0)

    _fft_first_stage(a_r, a_i, load_cols, f1_ref, n1, cb)

    def finish(k1, rows, tw, br, bi):
        s0 = pl.multiple_of(k1 * n2, n2)
        s1 = pl.multiple_of((k1 + 1) * n2, n2)
        sr = jnp.concatenate([sr_ref[pl.ds(s0, n2), :], sr_ref[pl.ds(s1, n2), :]], axis=1).astype(F32)
        si = jnp.concatenate([si_ref[pl.ds(s0, n2), :], si_ref[pl.ds(s1, n2), :]], axis=1).astype(F32)
        y = jnp.concatenate([br * sr - bi * si, br * si + bi * sr], axis=0).astype(BF16)
        ac = jnp.dot(g3_ref[...], y, preferred_element_type=F32)
        pr, pi = ac[:n2], ac[n2:]
        for j in range(2):
            tr, ti = tw[2 * j], tw[2 * j + 1]
            prj, pij = pr[:, j * cb:(j + 1) * cb], pi[:, j * cb:(j + 1) * cb]
            a_r[pl.ds(rows[j], n2), :] = prj * tr + pij * ti
            a_i[pl.ds(rows[j], n2), :] = pij * tr - prj * ti

    _fft_slab_loop(a_r, a_i, f3_ref, wr_ref, wi_ref, n1, cb, finish)

    def out_body(i, carry):
        j0 = 2 * i
        cols = []
        for j in (j0, j0 + 1):
            cols.append(jnp.concatenate([a_r[pl.ds(j, n1, stride=slab), :], a_i[pl.ds(j, n1, stride=slab), :]],
                                        axis=0))
        o = jnp.dot(g1_ref[...], jnp.concatenate(cols, axis=1).astype(BF16), preferred_element_type=F32)
        for jj in range(2):
            r = pl.multiple_of((j0 + jj) * h1, h1)
            o_ref[0, pl.ds(r, h1), :] = o[:h1, jj * cb:(jj + 1) * cb].astype(o_ref.dtype)
            o_ref[1, pl.ds(r, h1), :] = o[h1:, jj * cb:(jj + 1) * cb].astype(o_ref.dtype)
        return carry

    lax.fori_loop(0, n2 // 2, out_body, 0, unroll=4)


def _fft_conv(yp, col0, spec_r, spec_i, scol0, consts, cb=128):
    b, l, _ = yp.shape
    n = spec_r.shape[0]
    n1 = consts['n1']
    slab = HY_FAST + HY_PAD
    nblk = consts['nblk']
    y4 = yp.reshape(b // 2, 2, l, yp.shape[-1])
    const = lambda a: pl.BlockSpec(a.shape, lambda j, pi: (0, 0))
    return pl.pallas_call(
        functools.partial(_fft_conv_kernel, n1=n1),
        grid=(nblk, b // 2),
        in_specs=[pl.BlockSpec((None, 2, l, cb), lambda j, pi: (pi, 0, 0, col0 + j)),
                  pl.BlockSpec((n, cb), lambda j, pi: (0, scol0 + j)),
                  pl.BlockSpec((n, cb), lambda j, pi: (0, scol0 + j)),
                  const(consts['f1']), const(consts['f3']), const(consts['g3']), const(consts['g1']),
                  const(consts['wr']), const(consts['wi'])],
        out_specs=pl.BlockSpec((None, 2, l, cb), lambda j, pi: (pi, 0, 0, j)),
        out_shape=jax.ShapeDtypeStruct((b // 2, 2, l, nblk * cb), BF16),
        scratch_shapes=[pltpu.VMEM((n1 * slab, cb), F32), pltpu.VMEM((n1 * slab, cb), F32)],
        compiler_params=_cparams(("arbitrary", "arbitrary"), vmem_mb=58),
        name="fft_conv",
    )(y4, spec_r, spec_i, consts['f1'], consts['f3'], consts['g3'], consts['g1'],
      consts['wr'], consts['wi']).reshape(b, l, nblk * cb)


def _fft_spec_kernel(f_ref, f1_ref, f3_ref, wr_ref, wi_ref, sr_ref, si_ref, a_r, a_i, *, n1):
    n2 = HY_FAST
    cb = sr_ref.shape[-1]

    def load_cols(j):
        return f_ref[pl.ds(pl.multiple_of(j * n1, n1), n1), :]

    _fft_first_stage(a_r, a_i, load_cols, f1_ref, n1, cb)

    def finish(k1, rows, tw, br, bi):
        for j in range(2):
            s = pl.multiple_of((k1 + j) * n2, n2)
            sr_ref[pl.ds(s, n2), :] = br[:, j * cb:(j + 1) * cb].astype(sr_ref.dtype)
            si_ref[pl.ds(s, n2), :] = bi[:, j * cb:(j + 1) * cb].astype(si_ref.dtype)

    _fft_slab_loop(a_r, a_i, f3_ref, wr_ref, wi_ref, n1, cb, finish)


def _fft_spectrum(filt_p, consts, cb=128):
    n, c = filt_p.shape
    n1 = consts['n1']
    slab = HY_FAST + HY_PAD
    const = lambda a: pl.BlockSpec(a.shape, lambda j: (0, 0))
    spec = pl.BlockSpec((n, cb), lambda j: (0, j))
    return pl.pallas_call(
        functools.partial(_fft_spec_kernel, n1=n1),
        grid=(c // cb,),
        in_specs=[spec, const(consts['f1_real']), const(consts['f3']), const(consts['wr']), const(consts['wi'])],
        out_specs=[spec, spec],
        out_shape=[jax.ShapeDtypeStruct((n, c), BF16)] * 2,
        scratch_shapes=[pltpu.VMEM((n1 * slab, cb), F32), pltpu.VMEM((n1 * slab, cb), F32)],
        compiler_params=_cparams(("arbitrary",), vmem_mb=58),
        name="fft_spectrum",
    )(filt_p, consts['f1_real'], consts['f3'], consts['wr'], consts['wi'])


def _dense_conv_kernel(y_ref, sr_ref, si_ref, f_ref, g_ref, o_ref):
    l = y_ref.shape[1]
    n = sr_ref.shape[0]
    bc = jnp.dot(f_ref[...], jnp.concatenate([y_ref[0], y_ref[1]], axis=0), preferred_element_type=F32)
    br, bi = bc[:n], bc[n:]
    sr, si = sr_ref[...], si_ref[...]
    y = jnp.concatenate([br * sr - bi * si, br * si + bi * sr], axis=0).astype(BF16)
    o = jnp.dot(g_ref[...], y, preferred_element_type=F32)
    o_ref[0] = o[:l].astype(o_ref.dtype)
    o_ref[1] = o[l:].astype(o_ref.dtype)


def _dense_conv(y, col0, spec_r, spec_i, scol0, fmat, gmat, nblk, cb=128):
    b, l, _ = y.shape
    n = spec_r.shape[0]
    y4 = y.reshape(b // 2, 2, l, y.shape[-1])
    const = lambda a: pl.BlockSpec(a.shape, lambda j, pi: (0, 0))
    return pl.pallas_call(
        _dense_conv_kernel,
        grid=(nblk, b // 2),
        in_specs=[pl.BlockSpec((None, 2, l, cb), lambda j, pi: (pi, 0, 0, col0 + j)),
                  pl.BlockSpec((n, cb), lambda j, pi: (0, scol0 + j)),
                  pl.BlockSpec((n, cb), lambda j, pi: (0, scol0 + j)),
                  const(fmat), const(gmat)],
        out_specs=pl.BlockSpec((None, 2, l, cb), lambda j, pi: (pi, 0, 0, j)),
        out_shape=jax.ShapeDtypeStruct((b // 2, 2, l, nblk * cb), BF16),
        compiler_params=_cparams(("parallel", "parallel")),
        name="dense_conv",
    )(y4, spec_r, spec_i, fmat, gmat).reshape(b, l, nblk * cb)


def _sconv_kernel(u_ref, w_ref, b_ref, o_ref, *scr, h1):
    u = u_ref[...].astype(F32)
    l = u.shape[0]
    row = lax.broadcasted_iota(jnp.int32, u.shape, 0)
    prev = jnp.where(row == 0, 0.0, pltpu.roll(u, 1, 0))
    nxt = jnp.where(row == l - 1, 0.0, pltpu.roll(u, l - 1, 0))
    z = prev * w_ref[0:1, :] + u * w_ref[1:2, :] + nxt * w_ref[2:3, :] + b_ref[...]
    if h1 is None:
        o_ref[...] = z.astype(o_ref.dtype)
        return
    scr[0][...] = z

    def body(n2, carry):
        o_ref[pl.ds(pl.multiple_of(n2 * h1, h1), h1), :] = scr[0][pl.ds(n2, h1, stride=HY_FAST), :].astype(o_ref.dtype)
        return carry

    lax.fori_loop(0, HY_FAST, body, 0)


def _short_conv(proj, col0, conv_w, conv_b, h1, cb=128):
    b, l, _ = proj.shape
    c = conv_w.shape[1]
    return pl.pallas_call(
        functools.partial(_sconv_kernel, h1=h1),
        scratch_shapes=[] if h1 is None else [pltpu.VMEM((l, cb), F32)],
        grid=(b, c // cb),
        in_specs=[pl.BlockSpec((None, l, cb), lambda bi, j: (bi, 0, col0 + j)),
                  pl.BlockSpec((conv_w.shape[0], cb), lambda bi, j: (0, j)),
                  pl.BlockSpec((1, cb), lambda bi, j: (0, j))],
        out_specs=pl.BlockSpec((None, l, cb), lambda bi, j: (bi, 0, j)),
        out_shape=jax.ShapeDtypeStruct((b, l, c), BF16),
        compiler_params=_cparams(("parallel", "parallel")),
        name="short_conv",
    )(proj, conv_w.astype(F32), conv_b.astype(F32).reshape(1, c))


def _gate_kernel(x_ref, conv_ref, y_ref, bias_ref, o_ref, *scr, h1):
    y = y_ref[...].astype(F32)
    z = x_ref[...].astype(F32) * (conv_ref[...].astype(F32) + y * bias_ref[...])
    if h1 is None:
        o_ref[...] = z.astype(o_ref.dtype)
        return
    scr[0][...] = z

    def body(n1, carry):
        r = pl.multiple_of(n1 * HY_FAST, HY_FAST)
        o_ref[pl.ds(r, HY_FAST), :] = scr[0][pl.ds(n1, HY_FAST, stride=h1), :].astype(o_ref.dtype)
        return carry

    lax.fori_loop(0, h1, body, 0)


def _hy_gate(z, xcol, conv, y, ycol, bias, h1=None, cb=128):
    b, l, w = conv.shape
    return pl.pallas_call(
        functools.partial(_gate_kernel, h1=h1),
        grid=(b, w // cb),
        in_specs=[pl.BlockSpec((None, l, cb), lambda bi, j: (bi, 0, xcol + j)),
                  pl.BlockSpec((None, l, cb), lambda bi, j: (bi, 0, j)),
                  pl.BlockSpec((None, l, cb), lambda bi, j: (bi, 0, ycol + j)),
                  pl.BlockSpec((1, cb), lambda bi, j: (0, j))],
        out_specs=pl.BlockSpec((None, l, cb), lambda bi, j: (bi, 0, j)),
        out_shape=jax.ShapeDtypeStruct((b, l, w), BF16),
        scratch_shapes=[] if h1 is None else [pltpu.VMEM((l, cb), F32)],
        compiler_params=_cparams(("parallel", "parallel")),
        name="hy_gate",
    )(z, conv, y, bias.astype(F32).reshape(1, w))


def _hyena_branch(proj, col0, p, cb=128):
    b, l, _ = proj.shape
    w = p['hy_bias'].shape[-1]
    n = 2 * l
    nblk = w // cb
    if n // HY_FAST >= 16:
        consts = dict(_fft_consts(l), nblk=nblk)
        n1, h1 = consts['n1'], consts['h1']
        spec_r, spec_i = _fft_spectrum(_hyena_filter(l, p, slow=n1).astype(BF16), consts, cb)
        conv = lambda y, ycol, order: _fft_conv(y, ycol * nblk, spec_r, spec_i, order * nblk, consts, cb)
    else:
        h1 = None
        k = np.arange(n)
        ang = -2.0 * np.pi * np.outer(k, k) / n
        fmat = jnp.asarray(_block2(np.cos(ang[:, :l]), np.sin(ang[:, :l])), BF16)
        gmat = jnp.asarray(_block2(np.cos(ang[:l, :]) / n, -np.sin(ang[:l, :]) / n), BF16)
        f_real = jnp.asarray(np.concatenate([np.cos(ang), np.sin(ang)], axis=0), F32)
        spec = _bmm(f_real[None], _hyena_filter(l, p).astype(BF16)[None], name="dense_spectrum")[0]
        spec_r, spec_i = spec[:n], spec[n:]
        conv = lambda y, ycol, order: _dense_conv(y, ycol * nblk, spec_r, spec_i, order * nblk, fmat, gmat,
                                                  nblk, cb)
    z = _short_conv(proj, col0, p['hy_conv_w'], p['hy_conv_b'], h1, cb)
    c0 = conv(z, HY_ORDER, 0)
    y1 = _hy_gate(z, 0, c0, z, HY_ORDER * nblk, p['hy_bias'][0], None, cb)
    c1 = conv(y1, 0, 1)
    return _hy_gate(z, nblk, c1, y1, 0, p['hy_bias'][1], h1, cb)


def _merge_kernel(x_ref, g_ref, sh_ref, sc_ref, g1_ref, br_ref, wg0_ref, wg1_ref, wg2_ref, bg_ref, wb_ref, wo_ref,
                  o_ref, h_scr, acc_scr):
    j = pl.program_id(2)

    @pl.when(j == 0)
    def _():
        h_scr[...] = _norm_mod(x_ref[...], g_ref[...], sh_ref[...], sc_ref[...]).astype(BF16)
        acc_scr[...] = jnp.zeros_like(acc_scr)

    h = h_scr[...]
    mixed = None
    for i, wg_ref in enumerate((wg0_ref, wg1_ref, wg2_ref)):
        gate = jax.nn.sigmoid(jnp.dot(h, wg_ref[...], preferred_element_type=F32) + bg_ref[i])
        term = gate * jnp.dot(br_ref[i], wb_ref[i], preferred_element_type=F32)
        mixed = term if mixed is None else mixed + term
    acc_scr[...] += jnp.dot(mixed.astype(BF16), wo_ref[...], preferred_element_type=F32)

    @pl.when(j == pl.num_programs(2) - 1)
    def _():
        o_ref[...] = x_ref[...] + g1_ref[...] * acc_scr[...]


def _merge(x, g, sh, sc, g1, branches, w_gate_bf, b_gate, w_branch_bf, w_out_bf, tm, tc):
    b, l, d = x.shape
    nbr, w, _ = w_branch_bf.shape
    nb = sh.shape[0]
    nj = d // tc
    mod_map = (lambda bi, i, j: (bi, 0, 0)) if nb > 1 else (lambda bi, i, j: (0, 0, 0))
    br = jnp.stack(branches, axis=1)
    bg = b_gate.reshape(nbr, nj, 1, tc).transpose(1, 0, 2, 3)
    return pl.pallas_call(
        _merge_kernel,
        grid=(b, l // tm, nj),
        in_specs=[pl.BlockSpec((None, tm, d), lambda bi, i, j: (bi, i, 0)),
                  pl.BlockSpec((1, d), lambda bi, i, j: (0, 0)),
                  pl.BlockSpec((None, 1, d), mod_map),
                  pl.BlockSpec((None, 1, d), mod_map),
                  pl.BlockSpec((None, 1, d), mod_map),
                  pl.BlockSpec((None, nbr, tm, w), lambda bi, i, j: (bi, 0, i, 0)),
                  pl.BlockSpec((d, tc), lambda bi, i, j: (0, j)),
                  pl.BlockSpec((d, tc), lambda bi, i, j: (0, nj + j)),
                  pl.BlockSpec((d, tc), lambda bi, i, j: (0, 2 * nj + j)),
                  pl.BlockSpec((None, nbr, 1, tc), lambda bi, i, j: (j, 0, 0, 0)),
                  pl.BlockSpec((nbr, w, tc), lambda bi, i, j: (0, 0, j)),
                  pl.BlockSpec((tc, d), lambda bi, i, j: (j, 0))],
        out_specs=pl.BlockSpec((None, tm, d), lambda bi, i, j: (bi, i, 0)),
        out_shape=jax.ShapeDtypeStruct((b, l, d), F32),
        scratch_shapes=[pltpu.VMEM((tm, d), BF16), pltpu.VMEM((tm, d), F32)],
        compiler_params=_cparams(("parallel", "parallel", "arbitrary"), vmem_mb=56),
        name="merge",
    )(x, g.reshape(1, d), sh, sc, g1, br, w_gate_bf, w_gate_bf, w_gate_bf, bg, w_branch_bf, w_out_bf)


def _router_kernel(x_ref, g_ref, sh_ref, sc_ref, rt_ref, h_ref, aff_ref):
    h = _norm_mod(x_ref[...], g_ref[...], sh_ref[...], sc_ref[...])
    h_ref[...] = h.astype(h_ref.dtype)
    nt = (((1,), (1,)), ((), ()))
    rt = rt_ref[...]
    r_hi = rt.astype(BF16)
    r_lo = (rt - r_hi.astype(F32)).astype(BF16)
    h_hi = h.astype(BF16)
    h_lo = (h - h_hi.astype(F32)).astype(BF16)
    logits = (lax.dot_general(r_hi, h_hi, nt, preferred_element_type=F32)
              + lax.dot_general(r_hi, h_lo, nt, preferred_element_type=F32)
              + lax.dot_general(r_lo, h_hi, nt, preferred_element_type=F32))
    e = jnp.exp(logits - jnp.max(logits, axis=0, keepdims=True))
    aff_ref[...] = e / jnp.sum(e, axis=0, keepdims=True)


def _router(x, g, sh, sc, router_t, tm):
    b, l, d = x.shape
    e = router_t.shape[0]
    nb = sh.shape[0]
    mod_map = (lambda bi, i: (bi, 0, 0)) if nb > 1 else (lambda bi, i: (0, 0, 0))
    return pl.pallas_call(
        _router_kernel,
        grid=(b, l // tm),
        in_specs=[pl.BlockSpec((None, tm, d), lambda bi, i: (bi, i, 0)),
                  pl.BlockSpec((1, d), lambda bi, i: (0, 0)),
                  pl.BlockSpec((None, 1, d), mod_map),
                  pl.BlockSpec((None, 1, d), mod_map),
                  pl.BlockSpec((e, d), lambda bi, i: (0, 0))],
        out_specs=[pl.BlockSpec((None, tm, d), lambda bi, i: (bi, i, 0)),
                   pl.BlockSpec((None, e, tm), lambda bi, i: (bi, 0, i))],
        out_shape=[jax.ShapeDtypeStruct((b, l, d), BF16), jax.ShapeDtypeStruct((b, e, l), F32)],
        compiler_params=_cparams(("parallel", "parallel")),
        name="router",
    )(x, g.reshape(1, d), sh, sc, router_t)


def _ffn_kernel(xs_ref, gate_ref, w1_ref, w3_ref, w2_ref, o_ref, acc_scr):
    f = pl.program_id(2)

    @pl.when(f == 0)
    def _():
        acc_scr[...] = jnp.zeros_like(acc_scr)

    xs = xs_ref[...]
    a = jnp.dot(xs, w1_ref[...].astype(BF16), preferred_element_type=F32)
    gl = jnp.dot(xs, w3_ref[...].astype(BF16), preferred_element_type=F32)
    mid = (a * jax.nn.sigmoid(a) * gl).astype(BF16)
    acc_scr[...] += jnp.dot(mid, w2_ref[...].astype(BF16), preferred_element_type=F32)

    @pl.when(f == pl.num_programs(2) - 1)
    def _():
        o_ref[...] = (acc_scr[...] * gate_ref[...]).astype(o_ref.dtype)


def _expert_ffn(xs, gate, w1, w3, w2, layer, tr, tf):
    e, r, d = xs.shape
    ff = w1.shape[3]
    return pl.pallas_call(
        _ffn_kernel,
        grid=(e, r // tr, ff // tf),
        in_specs=[pl.BlockSpec((None, tr, d), lambda ei, i, f: (ei, i, 0)),
                  pl.BlockSpec((None, tr, 1), lambda ei, i, f: (ei, i, 0)),
                  pl.BlockSpec((None, None, d, tf), lambda ei, i, f: (layer, ei, 0, f)),
                  pl.BlockSpec((None, None, d, tf), lambda ei, i, f: (layer, ei, 0, f)),
                  pl.BlockSpec((None, None, tf, d), lambda ei, i, f: (layer, ei, f, 0))],
        out_specs=pl.BlockSpec((None, tr, d), lambda ei, i, f: (ei, i, 0)),
        out_shape=jax.ShapeDtypeStruct((e, r, d), BF16),
        scratch_shapes=[pltpu.VMEM((tr, d), F32)],
        compiler_params=_cparams(("parallel", "parallel", "arbitrary"), vmem_mb=56),
        name="expert_ffn",
    )(xs, gate, w1, w3, w2)


def _combine_kernel(st_ref, x_ref, g2_ref, pos_ref, *refs, n_exp, tiles, tn):
    y_refs, o_ref, oh_scr = refs[:n_exp], refs[n_exp], refs[n_exp + 1]
    bi, j = pl.program_id(0), pl.program_id(1)
    tt, win = oh_scr.shape[1], y_refs[0].shape[1]
    lane = lax.broadcasted_iota(jnp.int32, (tt, win), 1)
    main = min(win, tt)
    spill = []
    for e in range(n_exp):
        rel = pos_ref[:, e:e + 1] - st_ref[(bi * n_exp + e) * tiles + j]
        oh_scr[e] = jnp.where(lane == rel, 1.0, 0.0).astype(BF16)
        spill.append(jnp.max(rel) >= main)
    for c in range(o_ref.shape[1] // tn):
        cols = slice(c * tn, (c + 1) * tn)
        acc = jnp.dot(oh_scr[0, :, :main], y_refs[0][0, :main, cols], preferred_element_type=F32)
        for e in range(1, n_exp):
            acc = acc + jnp.dot(oh_scr[e, :, :main], y_refs[e][0, :main, cols], preferred_element_type=F32)
        o_ref[:, cols] = x_ref[:, cols] + g2_ref[:, cols] * acc
    if win > main:
        for e in range(n_exp):
            @pl.when(spill[e])
            def _(e=e):
                tail = jnp.dot(oh_scr[e, :, main:], y_refs[e][0, main:, :], preferred_element_type=F32)
                o_ref[...] += g2_ref[...] * tail


def _moe_combine(x, g2, pos, y, cap, tt):
    b, l, d = x.shape
    e = y.shape[0]
    tiles = l // tt
    win = min(tt + COMBINE_ALIGN, cap)
    sel = (pos >= 0).reshape(b, tiles, tt, e).sum(2)
    first = jnp.cumsum(sel, axis=1) - sel
    start = jnp.minimum(first // COMBINE_ALIGN * COMBINE_ALIGN, cap - win).astype(jnp.int32)
    start = jnp.transpose(start, (0, 2, 1)).reshape(-1)
    nb = g2.shape[0]

    def y_spec(ei):
        def y_map(bi, j, st):
            return ei, pl.multiple_of(bi * cap + st[(bi * e + ei) * tiles + j], COMBINE_ALIGN), 0
        return pl.BlockSpec((pl.Element(1), pl.Element(win), pl.Element(d)), y_map)

    grid_spec = pltpu.PrefetchScalarGridSpec(
        num_scalar_prefetch=1,
        grid=(b, tiles),
        in_specs=[pl.BlockSpec((None, tt, d), lambda bi, j, st: (bi, j, 0)),
                  pl.BlockSpec((None, 1, d), lambda bi, j, st: (bi if nb > 1 else 0, 0, 0)),
                  pl.BlockSpec((None, tt, e), lambda bi, j, st: (bi, j, 0))] + [y_spec(ei) for ei in range(e)],
        out_specs=pl.BlockSpec((None, tt, d), lambda bi, j, st: (bi, j, 0)),
        scratch_shapes=[pltpu.VMEM((e, tt, win), BF16)])
    return pl.pallas_call(
        functools.partial(_combine_kernel, n_exp=e, tiles=tiles, tn=min(256, d)),
        grid_spec=grid_spec,
        out_shape=jax.ShapeDtypeStruct((b, l, d), F32),
        compiler_params=_cparams(("parallel", "parallel"), vmem_mb=56),
        name="moe_combine",
    )(start, x, g2, pos, *([y] * e))


def _ec_moe_update(x, g, sh, sc, g2, p, tm):
    b, l, d = x.shape
    e = p['router'].shape[1]
    cap = max(1, EC_CAPACITY * l // e)
    h, aff_t = _router(x, g, sh, sc, jnp.transpose(p['router']).astype(F32), tm)
    gate, idx = lax.top_k(aff_t, cap)
    idx, gate = lax.sort_key_val(idx, gate, dimension=-1)
    bidx = jnp.arange(b)[:, None, None]
    eidx = jnp.arange(e)[None, :, None]
    pos = jnp.full((b, l, e), -1, jnp.int32).at[bidx, idx, eidx].set(
        jnp.broadcast_to(jnp.arange(cap, dtype=jnp.int32), idx.shape), unique_indices=True)
    idx_t = jnp.transpose(idx, (1, 0, 2))
    xs = h[jnp.arange(b)[None, :, None], idx_t].reshape(e, b * cap, d)
    gt = jnp.transpose(gate, (1, 0, 2)).reshape(e, b * cap, 1)
    tr = min(1024, b * cap)
    ff = p['exp_w1'].shape[3]
    y = _expert_ffn(xs, gt, p['exp_w1'], p['exp_w3'], p['exp_w2'], p['layer'], tr=tr, tf=min(512, ff))
    return _moe_combine(x, g2, pos, y, cap, min(256, l))


def _layer(x, xc, c, c_ctx, p, last):
    b, l, d = x.shape
    lc = xc.shape[1]
    w = p['ssm_d'].shape[0]
    hd = w // NA_HEADS

    cc = jnp.zeros((8, d), F32).at[:b].set(c).at[b].set(c_ctx)
    mod = _ada(cc, p['w_ada'], p['b_ada'], p['layer'])
    m_l = [t[:, None, :] for t in jnp.split(mod[:b], 6, axis=-1)]
    m_c = [t[:, None, :] for t in jnp.split(mod[b:b + 1], 6, axis=-1)]

    w_in = p['w_in'].astype(BF16)
    scale = hd ** -0.5
    head_gain = jnp.stack([jnp.tile(p['na_q_gain'].astype(F32) * scale, NA_HEADS),
                           jnp.tile(p['na_k_gain'].astype(F32), NA_HEADS)])[:, None, :]
    ones_blk = jnp.asarray(np.kron(np.eye(NA_HEADS), np.ones((hd, hd))), BF16)
    tm_l = min(512, l)
    tm_c = min(512, lc)
    proj = _in_proj(x, p['norm1'], m_l[0], m_l[1], w_in, head_gain, ones_blk, min(1024, l))
    projc = _in_proj(xc, p['norm1'], m_c[0], m_c[1], w_in, head_gain, ones_blk, tm_c)

    ssm_l, ssm_c = _s5_branch(projc[..., :w], proj[..., :w], p, _s5_tables(p), need_ctx=not last)

    na_l = _na_latent(proj[..., w:2 * w], proj[..., 2 * w:3 * w], proj[..., 3 * w:4 * w],
                      projc[..., 2 * w:3 * w], projc[..., 3 * w:4 * w], _na_bias_table(p['na_rpb']))
    hy_l = _hyena_branch(proj, 4 * w // 128, p)

    wg = p['w_gate'].astype(BF16)
    wb = p['w_branch'].astype(BF16)
    wo = p['w_out'].astype(BF16)
    x = _merge(x, p['norm1'], m_l[0], m_l[1], m_l[2], (ssm_l, na_l, hy_l), wg, p['b_gate'], wb, wo,
               tm=tm_l, tc=min(512, d))
    x = _ec_moe_update(x, p['norm2'], m_l[3], m_l[4], m_l[5], p, tm_l)
    if last:
        return x, None
    na_c = _na_context(projc[..., w:2 * w], projc[..., 2 * w:3 * w], projc[..., 3 * w:4 * w])
    hy_c = _hyena_branch(projc, 4 * w // 128, p)
    xc = _merge(xc, p['norm1'], m_c[0], m_c[1], m_c[2], (ssm_c, na_c, hy_c), wg, p['b_gate'], wb, wo,
                tm=tm_c, tc=min(512, d))
    xc = _ec_moe_update(xc, p['norm2'], m_c[3], m_c[4], m_c[5], p, tm_c)
    return x, xc


_PARAM_NAMES = ('w_ada', 'b_ada', 'norm1', 'norm2', 'w_in',
                'ssm_lam_re', 'ssm_lam_im', 'ssm_log_step', 'ssm_b_re', 'ssm_b_im', 'ssm_c_re', 'ssm_c_im',
                'ssm_d', 'ssm_w_glu', 'na_q_gain', 'na_k_gain', 'na_rpb',
                'hy_conv_w', 'hy_conv_b', 'hy_w1', 'hy_b1', 'hy_w2', 'hy_b2', 'hy_w3', 'hy_b3', 'hy_w4',
                'hy_freq', 'hy_bias', 'w_gate', 'b_gate', 'w_branch', 'w_out',
                'router', 'exp_w1', 'exp_w3', 'exp_w2')
_STACKED = ('w_ada', 'b_ada', 'exp_w1', 'exp_w3', 'exp_w2')


def kernel(x, c, ctx, c_ctx, w_ada, b_ada, norm1, norm2, w_in, ssm_lam_re, ssm_lam_im, ssm_log_step, ssm_b_re, ssm_b_im, ssm_c_re, ssm_c_im, ssm_d, ssm_w_glu, na_q_gain, na_k_gain, na_rpb, hy_conv_w, hy_conv_b, hy_w1, hy_b1, hy_w2, hy_b2, hy_w3, hy_b3, hy_w4, hy_freq, hy_bias, w_gate, b_gate, w_branch, w_out, router, exp_w1, exp_w3, exp_w2):
    stacked = (w_ada, b_ada, norm1, norm2, w_in, ssm_lam_re, ssm_lam_im, ssm_log_step, ssm_b_re, ssm_b_im,
               ssm_c_re, ssm_c_im, ssm_d, ssm_w_glu, na_q_gain, na_k_gain, na_rpb, hy_conv_w, hy_conv_b,
               hy_w1, hy_b1, hy_w2, hy_b2, hy_w3, hy_b3, hy_w4, hy_freq, hy_bias, w_gate, b_gate, w_branch,
               w_out, router, exp_w1, exp_w3, exp_w2)
    depth = w_ada.shape[0]
    xc = ctx
    for layer in range(depth):
        p = {name: (t if name in _STACKED else t[layer]) for name, t in zip(_PARAM_NAMES, stacked)}
        p['layer'] = layer
        x, xc = _layer(x, xc, c, c_ctx, p, layer == depth - 1)
    return x
```

```python
import functools
import math

import numpy as np
import jax
import jax.numpy as jnp
from jax import lax
from jax.experimental import pallas as pl
from jax.experimental.pallas import tpu as pltpu

BF16 = jnp.bfloat16
F32 = jnp.float32

NORM_EPS = 1e-6
GRID_W = 64
SSM_GROUP = 16
NA_HEADS = 8
NA_WIN_ROWS = 8
NA_WIN_COLS = 16
HY_ORDER = 2
HY_BANDS = 16
HY_MIN_DECAY = math.log(1e-2) / 1.5
HY_MAX_DECAY = math.log(1e-2) / 0.3
EC_CAPACITY = 2
S5_CHUNK = 16
HY_FAST = 128
HY_PAD = 8
MASK_VALUE = -1e30
COMBINE_ALIGN = 16

V7X_VMEM_BYTES = 64 * 1024 * 1024


def _cparams(semantics, vmem_mb=48):
    return pltpu.CompilerParams(dimension_semantics=semantics, vmem_limit_bytes=vmem_mb * 1024 * 1024)


def _norm_mod(x, g, sh, sc):
    ms = jnp.mean(x * x, axis=-1, keepdims=True)
    return (x * lax.rsqrt(ms + NORM_EPS) * g) * (1.0 + sc) + sh


def _ada_kernel(c_ref, w_ref, b_ref, o_ref):
    c = c_ref[...]
    a = (c * jax.nn.sigmoid(c)).astype(BF16)
    o_ref[...] = jnp.dot(a, w_ref[...].astype(BF16), preferred_element_type=F32) + b_ref[...]


def _ada(cc, w, b, layer):
    m, k = cc.shape
    depth, _, n = w.shape
    tn = next(t for t in (1024, 512, 256, 128) if n % t == 0)
    return pl.pallas_call(
        _ada_kernel,
        grid=(n // tn,),
        in_specs=[pl.BlockSpec((m, k), lambda j: (0, 0)),
                  pl.BlockSpec((None, k, tn), lambda j: (layer, 0, j)),
                  pl.BlockSpec((None, 1, tn), lambda j: (layer, 0, j))],
        out_specs=pl.BlockSpec((m, tn), lambda j: (0, j)),
        out_shape=jax.ShapeDtypeStruct((m, n), F32),
        compiler_params=_cparams(("arbitrary",)),
        name="ada",
    )(cc, w, b.reshape(depth, 1, n))


def _in_proj_kernel(x_ref, g_ref, sh_ref, sc_ref, w_ref, hg_ref, ones_ref, o_ref, h_scr, *, head_dim):
    j = pl.program_id(2)

    @pl.when(j == 0)
    def _():
        h_scr[...] = _norm_mod(x_ref[...], g_ref[...], sh_ref[...], sc_ref[...]).astype(BF16)

    acc = jnp.dot(h_scr[...], w_ref[...], preferred_element_type=F32)
    is_qk = jnp.logical_or(j == 1, j == 2)

    @pl.when(is_qk)
    def _():
        yy = acc * acc
        hi = yy.astype(BF16)
        lo = (yy - hi.astype(F32)).astype(BF16)
        ss = (jnp.dot(hi, ones_ref[...], preferred_element_type=F32)
              + jnp.dot(lo, ones_ref[...], preferred_element_type=F32))
        o_ref[...] = (acc * lax.rsqrt(ss * (1.0 / head_dim) + NORM_EPS) * hg_ref[...]).astype(o_ref.dtype)

    @pl.when(jnp.logical_not(is_qk))
    def _():
        o_ref[...] = acc.astype(o_ref.dtype)


def _in_proj(x, g, sh, sc, w_bf, head_gain, ones_blk, tm):
    b, l, d = x.shape
    n = w_bf.shape[1]
    tn = ones_blk.shape[0]
    nb = sh.shape[0]
    mod_map = (lambda bi, i, j: (bi, 0, 0)) if nb > 1 else (lambda bi, i, j: (0, 0, 0))
    return pl.pallas_call(
        functools.partial(_in_proj_kernel, head_dim=tn // NA_HEADS),
        grid=(b, l // tm, n // tn),
        in_specs=[pl.BlockSpec((None, tm, d), lambda bi, i, j: (bi, i, 0)),
                  pl.BlockSpec((1, d), lambda bi, i, j: (0, 0)),
                  pl.BlockSpec((None, 1, d), mod_map),
                  pl.BlockSpec((None, 1, d), mod_map),
                  pl.BlockSpec((d, tn), lambda bi, i, j: (0, j)),
                  pl.BlockSpec((None, 1, tn), lambda bi, i, j: (jnp.clip(j - 1, 0, 1), 0, 0)),
                  pl.BlockSpec((tn, tn), lambda bi, i, j: (0, 0))],
        out_specs=pl.BlockSpec((None, tm, tn), lambda bi, i, j: (bi, i, j)),
        out_shape=jax.ShapeDtypeStruct((b, l, n), BF16),
        scratch_shapes=[pltpu.VMEM((tm, d), BF16)],
        compiler_params=_cparams(("parallel", "parallel", "arbitrary")),
        name="in_proj",
    )(x, g.reshape(1, d), sh, sc, w_bf, head_gain, ones_blk)


def _bmm_kernel(a_ref, w_ref, o_ref):
    o_ref[...] = jnp.dot(a_ref[...].astype(BF16), w_ref[...], preferred_element_type=F32).astype(o_ref.dtype)


def _bmm_add_kernel(a_ref, w_ref, y_ref, o_ref):
    o_ref[...] = (y_ref[...] + jnp.dot(a_ref[...].astype(BF16), w_ref[...],
                                       preferred_element_type=F32)).astype(o_ref.dtype)


def _bmm(a, w, add=None, tm=None, name="bmm"):
    g, m, k = a.shape
    n = w.shape[2]
    tm = m if tm is None else tm
    in_specs = [pl.BlockSpec((None, tm, k), lambda gi, i: (gi, i, 0)),
                pl.BlockSpec((None, k, n), lambda gi, i: (gi, 0, 0))]
    args = [a, w]
    kern = _bmm_kernel
    if add is not None:
        in_specs.append(pl.BlockSpec((None, tm, n), lambda gi, i: (gi, i, 0)))
        args.append(add)
        kern = _bmm_add_kernel
    return pl.pallas_call(
        kern,
        grid=(g, m // tm),
        in_specs=in_specs,
        out_specs=pl.BlockSpec((None, tm, n), lambda gi, i: (gi, i, 0)),
        out_shape=jax.ShapeDtypeStruct((g, m, n), F32),
        compiler_params=_cparams(("parallel", "parallel")),
        name=name,
    )(*args)


def _gelu_tanh(x):
    return 0.5 * x * (1.0 + jnp.tanh(math.sqrt(2.0 / math.pi) * (x + 0.044715 * (x * x * x))))


def _s5_out_kernel(y_ref, u_ref, d_ref, w_ref, o_ref):
    y = y_ref[...].astype(F32) + d_ref[...] * u_ref[...].astype(F32)
    z = _gelu_tanh(y)
    gate = jax.nn.sigmoid(jnp.dot(z.astype(BF16), w_ref[...], preferred_element_type=F32))
    o_ref[...] = (z * gate).astype(o_ref.dtype)


def _s5_out(y, row0, u, d, w_glu_bf):
    b, l, w = u.shape
    tm = math.gcd(math.gcd(l, row0), 512) if row0 else math.gcd(l, 512)
    off = row0 // tm
    return pl.pallas_call(
        _s5_out_kernel,
        grid=(b, l // tm),
        in_specs=[pl.BlockSpec((None, tm, w), lambda bi, i: (bi, off + i, 0)),
                  pl.BlockSpec((None, tm, w), lambda bi, i: (bi, i, 0)),
                  pl.BlockSpec((1, w), lambda bi, i: (0, 0)),
                  pl.BlockSpec((w, w), lambda bi, i: (0, 0))],
        out_specs=pl.BlockSpec((None, tm, w), lambda bi, i: (bi, i, 0)),
        out_shape=jax.ShapeDtypeStruct((b, l, w), BF16),
        compiler_params=_cparams(("parallel", "parallel")),
        name="s5_out",
    )(y, u, d.reshape(1, w), w_glu_bf)


def _s5_tables(p):
    t = S5_CHUNK
    mats, sums, reads, steps = [], [], [], []
    for d in range(2):
        lam_re = p['ssm_lam_re'][d].astype(F32)
        lam_im = p['ssm_lam_im'][d].astype(F32)
        step = jnp.exp(p['ssm_log_step'][d].astype(F32))[:, None]
        mag = jnp.exp(lam_re * step)
        a_re = mag * jnp.cos(lam_im * step)
        a_im = mag * jnp.sin(lam_im * step)
        den = lam_re * lam_re + lam_im * lam_im
        num_re = a_re - 1.0
        coef_re = (num_re * lam_re + a_im * lam_im) / den
        coef_im = (a_im * lam_re - num_re * lam_im) / den
        b_re = p['ssm_b_re'][d].astype(F32)
        b_im = p['ssm_b_im'][d].astype(F32)
        bb_re = coef_re[..., None] * b_re - coef_im[..., None] * b_im
        bb_im = coef_re[..., None] * b_im + coef_im[..., None] * b_re
        c_re = p['ssm_c_re'][d].astype(F32)
        c_im = p['ssm_c_im'][d].astype(F32)
        n = jnp.arange(t + 1, dtype=F32)[:, None, None]
        pm = jnp.exp(lam_re * step * n)
        pw_re = pm * jnp.cos(lam_im * step * n)
        pw_im = pm * jnp.sin(lam_im * step * n)
        ca_re = c_re[None] * pw_re[:, :, None, :] - c_im[None] * pw_im[:, :, None, :]
        ca_im = c_re[None] * pw_im[:, :, None, :] + c_im[None] * pw_re[:, :, None, :]
        ktau = (jnp.einsum('ngop,gpi->ngoi', ca_re[:t], bb_re)
                - jnp.einsum('ngop,gpi->ngoi', ca_im[:t], bb_im))
        s_idx = np.arange(t)[:, None]
        t_idx = np.arange(t)[None, :]
        lag = (t_idx - s_idx) if d == 0 else (s_idx - t_idx)
        lag_sel = jnp.asarray(lag[:, :, None] == np.arange(t), F32)
        blk = jnp.einsum('stn,ngoi->gsito', lag_sel, ktau, precision=lax.Precision.HIGHEST)
        mats.append(blk)
        flip = (lambda a: a[::-1]) if d == 0 else (lambda a: a)
        unflip = (lambda a: a) if d == 0 else (lambda a: a[::-1])
        sp_re, sp_im = flip(pw_re[:t]), flip(pw_im[:t])
        sw_re = sp_re[:, :, :, None] * bb_re[None] - sp_im[:, :, :, None] * bb_im[None]
        sw_im = sp_re[:, :, :, None] * bb_im[None] + sp_im[:, :, :, None] * bb_re[None]
        sums.append((jnp.transpose(sw_re, (1, 0, 3, 2)), jnp.transpose(sw_im, (1, 0, 3, 2))))
        rd_re, rd_im = unflip(ca_re[1:t + 1]), unflip(ca_im[1:t + 1])
        reads.append((jnp.transpose(rd_re, (1, 3, 0, 2)), -jnp.transpose(rd_im, (1, 3, 0, 2))))
        steps.append((pw_re[t], pw_im[t]))
    g, pdim = steps[0][0].shape
    k = SSM_GROUP
    mtot = (mats[0] + mats[1]).reshape(g, t * k, t * k)
    ssum = jnp.concatenate([sums[0][0], sums[0][1], sums[1][0], sums[1][1]], axis=-1).reshape(g, t * k, 4 * pdim)
    r = jnp.concatenate([reads[0][0], reads[0][1], reads[1][0], reads[1][1]], axis=1).reshape(g, 4 * pdim, t * k)
    tk = t * k
    eye = jnp.eye(2, dtype=F32)
    ymat = jnp.einsum('gjsito,jk->gsjitko', mtot.reshape(g // 2, 2, t, k, t, k), eye).reshape(g // 2, 2 * tk, 2 * tk)
    smat = jnp.einsum('gjsiqp,jk->gsjiqkp', ssum.reshape(g // 2, 2, t, k, 4, pdim), eye
                      ).reshape(g // 2, 2 * tk, 8 * pdim)
    w1 = jnp.concatenate([ymat, smat], axis=-1)
    r2 = jnp.einsum('gjqpto,jk->gqjptko', r.reshape(g // 2, 2, 4, pdim, t, k), eye).reshape(g // 2, 8 * pdim, 2 * tk)
    coef = [c.reshape(1, g * pdim) for c in (steps[0][0], steps[0][1], steps[1][0], steps[1][1])]
    return w1.astype(BF16), r2.astype(BF16), coef


LANES = 128


def _to_groups_kernel(u_ref, o_ref, u_scr, o_scr, *, cbk, bsz):
    t, k2 = S5_CHUNK, 2 * SSM_GROUP
    per = LANES // k2
    for b in range(bsz):
        for lt in range(u_scr.shape[0]):
            u_scr[lt] = u_ref[b, :, lt * LANES:(lt + 1) * LANES].astype(F32)
        for lt in range(u_scr.shape[0]):
            xs = [u_scr[lt, pl.ds(s, cbk, stride=t), :] for s in range(t)]
            for q in range(per):
                v = jnp.concatenate([x[:, q * k2:(q + 1) * k2] for x in xs], axis=1)
                for ot in range(o_scr.shape[1]):
                    o_scr[lt * per + q, ot, pl.ds(b, cbk, stride=bsz), :] = v[:, ot * LANES:(ot + 1) * LANES]
    for gp in range(o_scr.shape[0]):
        for ot in range(o_scr.shape[1]):
            o_ref[gp, :, ot * LANES:(ot + 1) * LANES] = o_scr[gp, ot].astype(o_ref.dtype)


def _from_groups_kernel(y_ref, o_ref, y_scr, o_scr, *, cbk, bsz):
    t, k2 = S5_CHUNK, 2 * SSM_GROUP
    per = LANES // k2
    ngp, nlt = y_scr.shape[0], y_scr.shape[1]
    for gp in range(ngp):
        for ot in range(nlt):
            y_scr[gp, ot] = y_ref[gp, :, ot * LANES:(ot + 1) * LANES].astype(F32)
    for b in range(bsz):
        for ot in range(nlt):
            zs = [y_scr[gp, ot, pl.ds(b, cbk, stride=bsz), :] for gp in range(ngp)]
            for q in range(per):
                s = ot * per + q
                v = jnp.concatenate([z[:, q * k2:(q + 1) * k2] for z in zs], axis=1)
                for lt in range(o_scr.shape[0]):
                    o_scr[lt, pl.ds(s, cbk, stride=t), :] = v[:, lt * LANES:(lt + 1) * LANES]
        for lt in range(o_scr.shape[0]):
            o_ref[b, :, lt * LANES:(lt + 1) * LANES] = o_scr[lt].astype(o_ref.dtype)


def _chunk_block(nc, bsz):
    return next(c for c in range(min(64, nc), 0, -1) if nc % c == 0 and (c * bsz) % 16 == 0)


def _to_groups(u):
    bsz, ltot, w = u.shape
    t, k2 = S5_CHUNK, 2 * SSM_GROUP
    nc = ltot // t
    cbk = _chunk_block(nc, bsz)
    ngp = w // k2
    return pl.pallas_call(
        functools.partial(_to_groups_kernel, cbk=cbk, bsz=bsz),
        grid=(nc // cbk,),
        in_specs=[pl.BlockSpec((bsz, cbk * t, w), lambda i: (0, i, 0))],
        out_specs=pl.BlockSpec((ngp, cbk * bsz, t * k2), lambda i: (0, i, 0)),
        out_shape=jax.ShapeDtypeStruct((ngp, nc * bsz, t * k2), BF16),
        scratch_shapes=[pltpu.VMEM((w // LANES, cbk * t, LANES), F32),
                        pltpu.VMEM((ngp, t * k2 // LANES, cbk * bsz, LANES), F32)],
        compiler_params=_cparams(("parallel",)),
        name="s5_to_groups",
    )(u)


def _from_groups(y, bsz):
    ngp, m, tk2 = y.shape
    t, k2 = S5_CHUNK, 2 * SSM_GROUP
    nc = m // bsz
    cbk = _chunk_block(nc, bsz)
    w = ngp * k2
    return pl.pallas_call(
        functools.partial(_from_groups_kernel, cbk=cbk, bsz=bsz),
        grid=(nc // cbk,),
        in_specs=[pl.BlockSpec((ngp, cbk * bsz, tk2), lambda i: (0, i, 0))],
        out_specs=pl.BlockSpec((bsz, cbk * t, w), lambda i: (0, i, 0)),
        out_shape=jax.ShapeDtypeStruct((bsz, nc * t, w), BF16),
        scratch_shapes=[pltpu.VMEM((ngp, tk2 // LANES, cbk * bsz, LANES), F32),
                        pltpu.VMEM((w // LANES, cbk * t, LANES), F32)],
        compiler_params=_cparams(("parallel",)),
        name="s5_from_groups",
    )(y)


def _s5_chunk_kernel(u_ref, w_ref, y_ref, xfr_ref, xfi_ref, xbr_ref, xbi_ref):
    r = jnp.dot(u_ref[...], w_ref[...], preferred_element_type=F32)
    ny = y_ref.shape[-1]
    nx = xfr_ref.shape[-1]
    y_ref[...] = r[:, :ny]
    for i, ref in enumerate((xfr_ref, xfi_ref, xbr_ref, xbi_ref)):
        ref[...] = r[:, ny + i * nx:ny + (i + 1) * nx]


def _s5_chunk(ug, w1):
    gp, m, tk2 = ug.shape
    nx = (w1.shape[2] - tk2) // 4
    xspec = pl.BlockSpec((m, nx), lambda g: (0, g))
    xshape = jax.ShapeDtypeStruct((m, gp * nx), F32)
    return pl.pallas_call(
        _s5_chunk_kernel,
        grid=(gp,),
        in_specs=[pl.BlockSpec((None, m, tk2), lambda g: (g, 0, 0)),
                  pl.BlockSpec((None, tk2, w1.shape[2]), lambda g: (g, 0, 0))],
        out_specs=[pl.BlockSpec((None, m, tk2), lambda g: (g, 0, 0)), xspec, xspec, xspec, xspec],
        out_shape=[jax.ShapeDtypeStruct((gp, m, tk2), F32), xshape, xshape, xshape, xshape],
        compiler_params=_cparams(("parallel",)),
        name="s5_chunk",
    )(ug, w1)


def _s5_scan_kernel(xfr, xfi, xbr, xbi, afr, afi, abr, abi, hfr, hfi, hbr, hbi, *, tiles_ctx, tiles, bsz):
    shape = (2 * bsz, xfr.shape[1])
    low = lax.broadcasted_iota(jnp.int32, shape, 0) < bsz
    zero = jnp.zeros(shape, F32)

    def step(hr, hi, ar, ai, xr, xi):
        return ar * hr - ai * hi + xr, ar * hi + ai * hr + xi

    def fwd_tile(i, carry):
        ar, ai = jnp.broadcast_to(afr[...], shape), jnp.broadcast_to(afi[...], shape)
        r0 = pl.multiple_of(i * 2 * bsz, 2 * bsz)
        xr, xi = xfr[pl.ds(r0, 2 * bsz), :], xfi[pl.ds(r0, 2 * bsz), :]
        sr, si = pltpu.roll(carry[0], bsz, 0), pltpu.roll(carry[1], bsz, 0)
        tr, ti = step(sr, si, ar, ai, xr, xi)
        t2r, t2i = pltpu.roll(tr, bsz, 0), pltpu.roll(ti, bsz, 0)
        hfr[pl.ds(r0, 2 * bsz), :] = jnp.where(low, sr, t2r)
        hfi[pl.ds(r0, 2 * bsz), :] = jnp.where(low, si, t2i)
        return step(t2r, t2i, ar, ai, xr, xi)

    lax.fori_loop(0, tiles, fwd_tile, (zero, zero))

    def bwd_tile(i, carry):
        ar, ai = jnp.broadcast_to(abr[...], shape), jnp.broadcast_to(abi[...], shape)
        r0 = pl.multiple_of(i * 2 * bsz, 2 * bsz)
        xr, xi = xbr[pl.ds(r0, 2 * bsz), :], xbi[pl.ds(r0, 2 * bsz), :]
        sr, si = pltpu.roll(carry[0], bsz, 0), pltpu.roll(carry[1], bsz, 0)
        tr, ti = step(sr, si, ar, ai, xr, xi)
        t2r, t2i = pltpu.roll(tr, bsz, 0), pltpu.roll(ti, bsz, 0)
        hbr[pl.ds(r0, 2 * bsz), :] = jnp.where(low, t2r, sr)
        hbi[pl.ds(r0, 2 * bsz), :] = jnp.where(low, t2i, si)
        return step(t2r, t2i, ar, ai, xr, xi)

    carry = lax.fori_loop(0, tiles_ctx, lambda j, c: bwd_tile(tiles_ctx - 1 - j, c), (zero, zero))
    lax.fori_loop(0, tiles - tiles_ctx, lambda j, c: bwd_tile(tiles - 1 - j, c), carry)


def _s5_scan(xs, coef, chunks_ctx, chunks, bsz):
    m, lanes = xs[0].shape
    assert 2 * bsz == 8 and chunks % 2 == 0 and chunks_ctx % 2 == 0
    lb = min(256, lanes)
    xspec = pl.BlockSpec((m, lb), lambda j: (0, j))
    cspec = pl.BlockSpec((1, lb), lambda j: (0, j))
    shape = jax.ShapeDtypeStruct((m, lanes), F32)
    return pl.pallas_call(
        functools.partial(_s5_scan_kernel, tiles_ctx=chunks_ctx // 2, tiles=chunks // 2, bsz=bsz),
        grid=(lanes // lb,),
        in_specs=[xspec] * 4 + [cspec] * 4,
        out_specs=[xspec] * 4,
        out_shape=[shape] * 4,
        compiler_params=_cparams(("parallel",)),
        name="s5_scan",
    )(*xs, *coef)


def _s5_readout_kernel(y_ref, hfr, hfi, hbr, hbi, r_ref, o_ref):
    h = jnp.concatenate([hfr[...], hfi[...], hbr[...], hbi[...]], axis=1).astype(BF16)
    o_ref[...] = (y_ref[...] + jnp.dot(h, r_ref[...], preferred_element_type=F32)).astype(o_ref.dtype)


def _s5_readout(y0, hs, r2):
    gp, m, tk2 = y0.shape
    nx = hs[0].shape[1] // gp
    hspec = pl.BlockSpec((m, nx), lambda g: (0, g))
    yspec = pl.BlockSpec((None, m, tk2), lambda g: (g, 0, 0))
    return pl.pallas_call(
        _s5_readout_kernel,
        grid=(gp,),
        in_specs=[yspec, hspec, hspec, hspec, hspec, pl.BlockSpec((None, 4 * nx, tk2), lambda g: (g, 0, 0))],
        out_specs=yspec,
        out_shape=jax.ShapeDtypeStruct((gp, m, tk2), BF16),
        compiler_params=_cparams(("parallel",)),
        name="s5_readout",
    )(y0, *hs, r2)


def _s5_branch(u_ctx, u_lat, p, tables, need_ctx):
    w1, r2, coef = tables
    b, l, w = u_lat.shape
    lc = u_ctx.shape[1]
    t, k = S5_CHUNK, SSM_GROUP
    g = w // k
    nc = (lc + l) // t
    m = nc * b
    ug = _to_groups(jnp.concatenate([u_ctx, u_lat], axis=1))
    y0, *xs = _s5_chunk(ug, w1)
    hs = _s5_scan(xs, coef, lc // t, nc, b)
    y = _from_groups(_s5_readout(y0, hs, r2), b)
    w_glu = p['ssm_w_glu'].astype(BF16)
    out_l = _s5_out(y, lc, u_lat, p['ssm_d'], w_glu)
    out_c = _s5_out(y, 0, u_ctx, p['ssm_d'], w_glu) if need_ctx else None
    return out_l, out_c


def _na_kernel(q_ref, k_ref, v_ref, kc_ref, vc_ref, bias_ref, o_ref, s_scr, p_scr, *, rows, head_dim):
    r = pl.program_id(1)
    kr = NA_WIN_ROWS
    start = jnp.clip(r - kr // 2, 0, rows - kr)
    cls = r - start
    base = pl.multiple_of(start * GRID_W, GRID_W)
    kwin = k_ref[pl.ds(base, kr * GRID_W), :]
    vwin = v_ref[pl.ds(base, kr * GRID_W), :]
    q = q_ref[...]
    kc = kc_ref[...]
    vc = vc_ref[...]
    outs = []
    nt = (((1,), (1,)), ((), ()))
    lane_tile = 2 * head_dim
    nwin = kwin.shape[0]
    first_head = lax.broadcasted_iota(jnp.int32, (q.shape[0], lane_tile), 1) < head_dim
    for h in range(NA_HEADS):
        sl = slice(h // 2 * lane_tile, (h // 2 + 1) * lane_tile)
        qh = jnp.where(first_head == (h % 2 == 0), q[:, sl], jnp.zeros_like(q[:, sl]))
        s_scr[h, :, :nwin] = lax.dot_general(qh, kwin[:, sl], nt, preferred_element_type=F32) + bias_ref[cls, h]
        s_scr[h, :, nwin:] = lax.dot_general(qh, kc[:, sl], nt, preferred_element_type=F32)
    inv = []
    for h in range(NA_HEADS):
        s = s_scr[h]
        pr = jnp.exp(s - jnp.max(s, axis=-1, keepdims=True))
        inv.append(1.0 / jnp.sum(pr, axis=-1, keepdims=True))
        p_scr[h] = pr.astype(BF16)
    for pair in range(NA_HEADS // 2):
        sl = slice(pair * lane_tile, (pair + 1) * lane_tile)
        halves = []
        for half in range(2):
            h = 2 * pair + half
            o = (jnp.dot(p_scr[h, :, :nwin], vwin[:, sl], preferred_element_type=F32)
                 + jnp.dot(p_scr[h, :, nwin:], vc[:, sl], preferred_element_type=F32))
            halves.append(o * inv[h])
        outs.append(jnp.where(first_head, halves[0], halves[1]))
    o_ref[...] = jnp.concatenate(outs, axis=-1).astype(o_ref.dtype)


def _na_bias_table(rpb):
    kr, kw, w = NA_WIN_ROWS, NA_WIN_COLS, GRID_W
    col = np.arange(w)
    cstart = np.clip(col - kw // 2, 0, w - kw)
    kcol = np.arange(w)
    inwin = (kcol[None, :] >= cstart[:, None]) & (kcol[None, :] < cstart[:, None] + kw)
    dc = np.clip(kcol[None, :] - col[:, None] + (kw - 1), 0, 2 * kw - 2)
    rpb = rpb.astype(F32)
    dr = np.arange(kr)[None, :] - np.arange(kr)[:, None] + (kr - 1)
    row_sel = jnp.asarray(dr[:, :, None] == np.arange(2 * kr - 1), F32)
    col_sel = jnp.asarray((dc[:, :, None] == np.arange(2 * kw - 1)) & inwin[:, :, None], F32)
    hp = lax.Precision.HIGHEST
    rows = jnp.einsum('hab,oja->ohjb', rpb, row_sel, precision=hp)
    bias = jnp.einsum('ohjb,cdb->ohcjd', rows, col_sel, precision=hp)
    bias = jnp.where(jnp.asarray(inwin)[None, None, :, None, :], bias, MASK_VALUE)
    return bias.reshape(kr, rpb.shape[0], w, kr * w)


def _na_latent(q, k, v, kc, vc, bias_tab):
    b, l, w = q.shape
    lc = kc.shape[1]
    rows = l // GRID_W
    assert rows >= NA_WIN_ROWS
    nk = NA_WIN_ROWS * GRID_W
    return pl.pallas_call(
        functools.partial(_na_kernel, rows=rows, head_dim=w // NA_HEADS),
        grid=(b, rows),
        in_specs=[pl.BlockSpec((None, GRID_W, w), lambda bi, r: (bi, r, 0)),
                  pl.BlockSpec((None, l, w), lambda bi, r: (bi, 0, 0)),
                  pl.BlockSpec((None, l, w), lambda bi, r: (bi, 0, 0)),
                  pl.BlockSpec((None, lc, w), lambda bi, r: (bi, 0, 0)),
                  pl.BlockSpec((None, lc, w), lambda bi, r: (bi, 0, 0)),
                  pl.BlockSpec((NA_WIN_ROWS, NA_HEADS, GRID_W, nk), lambda bi, r: (0, 0, 0, 0))],
        out_specs=pl.BlockSpec((None, GRID_W, w), lambda bi, r: (bi, r, 0)),
        out_shape=jax.ShapeDtypeStruct((b, l, w), BF16),
        scratch_shapes=[pltpu.VMEM((NA_HEADS, GRID_W, nk + lc), F32), pltpu.VMEM((NA_HEADS, GRID_W, nk + lc), BF16)],
        compiler_params=_cparams(("parallel", "arbitrary"), vmem_mb=56),
        name="na_latent",
    )(q, k, v, kc, vc, bias_tab)


def _na_ctx_kernel(q_ref, k_ref, v_ref, o_ref, *, head_dim):
    q, k, v = q_ref[...], k_ref[...], v_ref[...]
    outs = []
    for h in range(NA_HEADS):
        sl = slice(h * head_dim, (h + 1) * head_dim)
        s = lax.dot_general(q[:, sl], k[:, sl], (((1,), (1,)), ((), ())), preferred_element_type=F32)
        pr = jnp.exp(s - jnp.max(s, axis=-1, keepdims=True))
        den = jnp.sum(pr, axis=-1, keepdims=True)
        outs.append(jnp.dot(pr.astype(BF16), v[:, sl], preferred_element_type=F32) / den)
    o_ref[...] = jnp.concatenate(outs, axis=-1).astype(o_ref.dtype)


def _na_context(q, k, v):
    b, lc, w = q.shape
    spec = pl.BlockSpec((None, lc, w), lambda bi: (bi, 0, 0))
    return pl.pallas_call(
        functools.partial(_na_ctx_kernel, head_dim=w // NA_HEADS),
        grid=(b,),
        in_specs=[spec, spec, spec],
        out_specs=spec,
        out_shape=jax.ShapeDtypeStruct((b, lc, w), BF16),
        compiler_params=_cparams(("parallel",)),
        name="na_context",
    )(q, k, v)


def _hyena_filter(length, p, slow=None):
    n = 2 * length
    pos = np.arange(n)
    if slow is not None:
        pos = pos.reshape(slow, n // slow).T.reshape(-1)
    off = np.where(pos < length, pos, n - pos)
    backward = jnp.asarray(pos > length)[:, None, None]
    live = jnp.asarray(pos != length, F32)[:, None, None]
    tau = jnp.asarray(np.minimum(off, length - 1), F32)[:, None]
    t = tau / (length - 1)
    freqs = jnp.linspace(1e-4, HY_BANDS - 1, HY_BANDS, dtype=F32)
    ang = (2.0 * math.pi / length) * tau * freqs[None, :]
    z = jnp.concatenate([t, jnp.cos(ang), -jnp.sin(ang)], axis=-1)
    fr = p['hy_freq'].astype(F32)
    hp = lax.Precision.HIGHEST
    h = jnp.sin(fr * (jnp.dot(z, p['hy_w1'].astype(F32), precision=hp) + p['hy_b1'].astype(F32)))
    h = jnp.sin(fr * (jnp.dot(h, p['hy_w2'].astype(F32), precision=hp) + p['hy_b2'].astype(F32)))
    h = jnp.sin(fr * (jnp.dot(h, p['hy_w3'].astype(F32), precision=hp) + p['hy_b3'].astype(F32)))
    wdim = p['hy_bias'].shape[-1]
    h = jnp.dot(h, p['hy_w4'].astype(F32), precision=hp).reshape(n, HY_ORDER, 2, wdim)
    decay = jnp.exp(-t * jnp.abs(jnp.linspace(HY_MIN_DECAY, HY_MAX_DECAY, wdim, dtype=F32)))
    filt = jnp.where(backward, h[:, :, 1], h[:, :, 0]) * (decay[:, None, :] * live)
    filt = filt / jnp.sum(jnp.abs(filt), axis=0, keepdims=True)
    return filt.reshape(n, HY_ORDER * wdim)


def _block2(re, im):
    return np.block([[re, -im], [im, re]])


def _fft_consts(length):
    n = 2 * length
    n2 = HY_FAST
    n1 = n // n2
    h1 = n1 // 2
    a1 = -2.0 * np.pi * np.outer(np.arange(n1), np.arange(n1)) / n1
    a2 = -2.0 * np.pi * np.outer(np.arange(n2), np.arange(n2)) / n2
    f1 = _block2(np.cos(a1[:, :h1]), np.sin(a1[:, :h1]))
    f1_real = np.concatenate([np.cos(a1), np.sin(a1)], axis=0)
    f3 = _block2(np.cos(a2), np.sin(a2))
    g3 = _block2(np.cos(a2), -np.sin(a2))
    g1 = _block2(np.cos(a1[:h1, :]) / n, -np.sin(a1[:h1, :]) / n)
    wa = -2.0 * np.pi * np.arange(n2) / n
    wr = np.broadcast_to(np.cos(wa)[:, None], (n2, 128))
    wi = np.broadcast_to(np.sin(wa)[:, None], (n2, 128))
    cast = lambda m: jnp.asarray(m, BF16)
    return dict(n1=n1, h1=h1, f1=cast(f1), f1_real=cast(f1_real), f3=cast(f3), g3=cast(g3), g1=cast(g1),
                wr=jnp.asarray(wr, F32), wi=jnp.asarray(wi, F32))


def _fft_first_stage(a_r, a_i, load_cols, f1_ref, n1, cb):
    slab = HY_FAST + HY_PAD

    def body(i, carry):
        n2 = 2 * i
        z = jnp.concatenate([load_cols(n2), load_cols(n2 + 1)], axis=1)
        r = jnp.dot(f1_ref[...], z, preferred_element_type=F32)
        a_r[pl.ds(n2, n1, stride=slab), :] = r[:n1, :cb]
        a_i[pl.ds(n2, n1, stride=slab), :] = r[n1:, :cb]
        a_r[pl.ds(n2 + 1, n1, stride=slab), :] = r[:n1, cb:]
        a_i[pl.ds(n2 + 1, n1, stride=slab), :] = r[n1:, cb:]
        return carry

    lax.fori_loop(0, HY_FAST // 2, body, 0, unroll=4)


def _fft_slab_loop(a_r, a_i, f3_ref, wr_ref, wi_ref, n1, cb, finish):
    slab = HY_FAST + HY_PAD
    n2 = HY_FAST
    wr, wi = wr_ref[...], wi_ref[...]
    if cb != wr.shape[1]:
        wr, wi = jnp.tile(wr, (1, cb // wr.shape[1])), jnp.tile(wi, (1, cb // wi.shape[1]))

    def body(i, carry):
        tr, ti = carry
        k1 = 2 * i
        r0 = pl.multiple_of(k1 * slab, 8)
        r1 = pl.multiple_of((k1 + 1) * slab, 8)
        tr1, ti1 = tr * wr - ti * wi, tr * wi + ti * wr
        ar0, ai0 = a_r[pl.ds(r0, n2), :], a_i[pl.ds(r0, n2), :]
        ar1, ai1 = a_r[pl.ds(r1, n2), :], a_i[pl.ds(r1, n2), :]
        xr = jnp.concatenate([ar0 * tr - ai0 * ti, ar1 * tr1 - ai1 * ti1], axis=1)
        xi = jnp.concatenate([ar0 * ti + ai0 * tr, ar1 * ti1 + ai1 * tr1], axis=1)
        bc = jnp.dot(f3_ref[...], jnp.concatenate([xr, xi], axis=0).astype(BF16), preferred_element_type=F32)
        finish(k1, (r0, r1), (tr, ti, tr1, ti1), bc[:n2], bc[n2:])
        return tr1 * wr - ti1 * wi, tr1 * wi + ti1 * wr

    one = jnp.ones((n2, cb), F32)
    lax.fori_loop(0, n1 // 2, body, (one, jnp.zeros_like(one)), unroll=2)


def _fft_conv_kernel(y_ref, sr_ref, si_ref, f1_ref, f3_ref, g3_ref, g1_ref, wr_ref, wi_ref, o_ref, a_r, a_i,
                     *, n1):
    h1 = n1 // 2
    n2 = HY_FAST
    cb = o_ref.shape[-1]
    slab = HY_FAST + HY_PAD

    def load_cols(j):
        r = pl.multiple_of(j * h1, h1)
        return jnp.concatenate([y_ref[0, pl.ds(r, h1), :], y_ref[1, pl.ds(r, h1), :]], axis=0)

    _fft_first_stage(a_r, a_i, load_cols, f1_ref, n1, cb)

    def finish(k1, rows, tw, br, bi):
        s0 = pl.multiple_of(k1 * n2, n2)
        s1 = pl.multiple_of((k1 + 1) * n2, n2)
        sr = jnp.concatenate([sr_ref[pl.ds(s0, n2), :], sr_ref[pl.ds(s1, n2), :]], axis=1).astype(F32)
        si = jnp.concatenate([si_ref[pl.ds(s0, n2), :], si_ref[pl.ds(s1, n2), :]], axis=1).astype(F32)
        y = jnp.concatenate([br * sr - bi * si, br * si + bi * sr], axis=0).astype(BF16)
        ac = jnp.dot(g3_ref[...], y, preferred_element_type=F32)
        pr, pi = ac[:n2], ac[n2:]
        for j in range(2):
            tr, ti = tw[2 * j], tw[2 * j + 1]
            prj, pij = pr[:, j * cb:(j + 1) * cb], pi[:, j * cb:(j + 1) * cb]
            a_r[pl.ds(rows[j], n2), :] = prj * tr + pij * ti
            a_i[pl.ds(rows[j], n2), :] = pij * tr - prj * ti

    _fft_slab_loop(a_r, a_i, f3_ref, wr_ref, wi_ref, n1, cb, finish)

    def out_body(i, carry):
        j0 = 2 * i
        cols = []
        for j in (j0, j0 + 1):
            cols.append(jnp.concatenate([a_r[pl.ds(j, n1, stride=slab), :], a_i[pl.ds(j, n1, stride=slab), :]],
                                        axis=0))
        o = jnp.dot(g1_ref[...], jnp.concatenate(cols, axis=1).astype(BF16), preferred_element_type=F32)
        for jj in range(2):
            r = pl.multiple_of((j0 + jj) * h1, h1)
            o_ref[0, pl.ds(r, h1), :] = o[:h1, jj * cb:(jj + 1) * cb].astype(o_ref.dtype)
            o_ref[1, pl.ds(r, h1), :] = o[h1:, jj * cb:(jj + 1) * cb].astype(o_ref.dtype)
        return carry

    lax.fori_loop(0, n2 // 2, out_body, 0, unroll=4)


def _fft_conv(yp, col0, spec_r, spec_i, scol0, consts, cb=128):
    b, l, _ = yp.shape
    n = spec_r.shape[0]
    n1 = consts['n1']
    slab = HY_FAST + HY_PAD
    nblk = consts['nblk']
    y4 = yp.reshape(b // 2, 2, l, yp.shape[-1])
    const = lambda a: pl.BlockSpec(a.shape, lambda j, pi: (0, 0))
    return pl.pallas_call(
        functools.partial(_fft_conv_kernel, n1=n1),
        grid=(nblk, b // 2),
        in_specs=[pl.BlockSpec((None, 2, l, cb), lambda j, pi: (pi, 0, 0, col0 + j)),
                  pl.BlockSpec((n, cb), lambda j, pi: (0, scol0 + j)),
                  pl.BlockSpec((n, cb), lambda j, pi: (0, scol0 + j)),
                  const(consts['f1']), const(consts['f3']), const(consts['g3']), const(consts['g1']),
                  const(consts['wr']), const(consts['wi'])],
        out_specs=pl.BlockSpec((None, 2, l, cb), lambda j, pi: (pi, 0, 0, j)),
        out_shape=jax.ShapeDtypeStruct((b // 2, 2, l, nblk * cb), BF16),
        scratch_shapes=[pltpu.VMEM((n1 * slab, cb), F32), pltpu.VMEM((n1 * slab, cb), F32)],
        compiler_params=_cparams(("arbitrary", "arbitrary"), vmem_mb=58),
        name="fft_conv",
    )(y4, spec_r, spec_i, consts['f1'], consts['f3'], consts['g3'], consts['g1'],
      consts['wr'], consts['wi']).reshape(b, l, nblk * cb)


def _fft_spec_kernel(f_ref, f1_ref, f3_ref, wr_ref, wi_ref, sr_ref, si_ref, a_r, a_i, *, n1):
    n2 = HY_FAST
    cb = sr_ref.shape[-1]

    def load_cols(j):
        return f_ref[pl.ds(pl.multiple_of(j * n1, n1), n1), :]

    _fft_first_stage(a_r, a_i, load_cols, f1_ref, n1, cb)

    def finish(k1, rows, tw, br, bi):
        for j in range(2):
            s = pl.multiple_of((k1 + j) * n2, n2)
            sr_ref[pl.ds(s, n2), :] = br[:, j * cb:(j + 1) * cb].astype(sr_ref.dtype)
            si_ref[pl.ds(s, n2), :] = bi[:, j * cb:(j + 1) * cb].astype(si_ref.dtype)

    _fft_slab_loop(a_r, a_i, f3_ref, wr_ref, wi_ref, n1, cb, finish)


def _fft_spectrum(filt_p, consts, cb=128):
    n, c = filt_p.shape
    n1 = consts['n1']
    slab = HY_FAST + HY_PAD
    const = lambda a: pl.BlockSpec(a.shape, lambda j: (0, 0))
    spec = pl.BlockSpec((n, cb), lambda j: (0, j))
    return pl.pallas_call(
        functools.partial(_fft_spec_kernel, n1=n1),
        grid=(c // cb,),
        in_specs=[spec, const(consts['f1_real']), const(consts['f3']), const(consts['wr']), const(consts['wi'])],
        out_specs=[spec, spec],
        out_shape=[jax.ShapeDtypeStruct((n, c), BF16)] * 2,
        scratch_shapes=[pltpu.VMEM((n1 * slab, cb), F32), pltpu.VMEM((n1 * slab, cb), F32)],
        compiler_params=_cparams(("arbitrary",), vmem_mb=58),
        name="fft_spectrum",
    )(filt_p, consts['f1_real'], consts['f3'], consts['wr'], consts['wi'])


def _dense_conv_kernel(y_ref, sr_ref, si_ref, f_ref, g_ref, o_ref):
    l = y_ref.shape[1]
    n = sr_ref.shape[0]
    bc = jnp.dot(f_ref[...], jnp.concatenate([y_ref[0], y_ref[1]], axis=0), preferred_element_type=F32)
    br, bi = bc[:n], bc[n:]
    sr, si = sr_ref[...], si_ref[...]
    y = jnp.concatenate([br * sr - bi * si, br * si + bi * sr], axis=0).astype(BF16)
    o = jnp.dot(g_ref[...], y, preferred_element_type=F32)
    o_ref[0] = o[:l].astype(o_ref.dtype)
    o_ref[1] = o[l:].astype(o_ref.dtype)


def _dense_conv(y, col0, spec_r, spec_i, scol0, fmat, gmat, nblk, cb=128):
    b, l, _ = y.shape
    n = spec_r.shape[0]
    y4 = y.reshape(b // 2, 2, l, y.shape[-1])
    const = lambda a: pl.BlockSpec(a.shape, lambda j, pi: (0, 0))
    return pl.pallas_call(
        _dense_conv_kernel,
        grid=(nblk, b // 2),
        in_specs=[pl.BlockSpec((None, 2, l, cb), lambda j, pi: (pi, 0, 0, col0 + j)),
                  pl.BlockSpec((n, cb), lambda j, pi: (0, scol0 + j)),
                  pl.BlockSpec((n, cb), lambda j, pi: (0, scol0 + j)),
                  const(fmat), const(gmat)],
        out_specs=pl.BlockSpec((None, 2, l, cb), lambda j, pi: (pi, 0, 0, j)),
        out_shape=jax.ShapeDtypeStruct((b // 2, 2, l, nblk * cb), BF16),
        compiler_params=_cparams(("parallel", "parallel")),
        name="dense_conv",
    )(y4, spec_r, spec_i, fmat, gmat).reshape(b, l, nblk * cb)


def _sconv_kernel(u_ref, w_ref, b_ref, o_ref, *scr, h1):
    u = u_ref[...].astype(F32)
    l = u.shape[0]
    row = lax.broadcasted_iota(jnp.int32, u.shape, 0)
    prev = jnp.where(row == 0, 0.0, pltpu.roll(u, 1, 0))
    nxt = jnp.where(row == l - 1, 0.0, pltpu.roll(u, l - 1, 0))
    z = prev * w_ref[0:1, :] + u * w_ref[1:2, :] + nxt * w_ref[2:3, :] + b_ref[...]
    if h1 is None:
        o_ref[...] = z.astype(o_ref.dtype)
        return
    scr[0][...] = z

    def body(n2, carry):
        o_ref[pl.ds(pl.multiple_of(n2 * h1, h1), h1), :] = scr[0][pl.ds(n2, h1, stride=HY_FAST), :].astype(o_ref.dtype)
        return carry

    lax.fori_loop(0, HY_FAST, body, 0)


def _short_conv(proj, col0, conv_w, conv_b, h1, cb=128):
    b, l, _ = proj.shape
    c = conv_w.shape[1]
    return pl.pallas_call(
        functools.partial(_sconv_kernel, h1=h1),
        scratch_shapes=[] if h1 is None else [pltpu.VMEM((l, cb), F32)],
        grid=(b, c // cb),
        in_specs=[pl.BlockSpec((None, l, cb), lambda bi, j: (bi, 0, col0 + j)),
                  pl.BlockSpec((conv_w.shape[0], cb), lambda bi, j: (0, j)),
                  pl.BlockSpec((1, cb), lambda bi, j: (0, j))],
        out_specs=pl.BlockSpec((None, l, cb), lambda bi, j: (bi, 0, j)),
        out_shape=jax.ShapeDtypeStruct((b, l, c), BF16),
        compiler_params=_cparams(("parallel", "parallel")),
        name="short_conv",
    )(proj, conv_w.astype(F32), conv_b.astype(F32).reshape(1, c))


def _gate_kernel(x_ref, conv_ref, y_ref, bias_ref, o_ref, *scr, h1):
    y = y_ref[...].astype(F32)
    z = x_ref[...].astype(F32) * (conv_ref[...].astype(F32) + y * bias_ref[...])
    if h1 is None:
        o_ref[...] = z.astype(o_ref.dtype)
        return
    scr[0][...] = z

    def body(n1, carry):
        r = pl.multiple_of(n1 * HY_FAST, HY_FAST)
        o_ref[pl.ds(r, HY_FAST), :] = scr[0][pl.ds(n1, HY_FAST, stride=h1), :].astype(o_ref.dtype)
        return carry

    lax.fori_loop(0, h1, body, 0)


def _hy_gate(z, xcol, conv, y, ycol, bias, h1=None, cb=128):
    b, l, w = conv.shape
    return pl.pallas_call(
        functools.partial(_gate_kernel, h1=h1),
        grid=(b, w // cb),
        in_specs=[pl.BlockSpec((None, l, cb), lambda bi, j: (bi, 0, xcol + j)),
                  pl.BlockSpec((None, l, cb), lambda bi, j: (bi, 0, j)),
                  pl.BlockSpec((None, l, cb), lambda bi, j: (bi, 0, ycol + j)),
                  pl.BlockSpec((1, cb), lambda bi, j: (0, j))],
        out_specs=pl.BlockSpec((None, l, cb), lambda bi, j: (bi, 0, j)),
        out_shape=jax.ShapeDtypeStruct((b, l, w), BF16),
        scratch_shapes=[] if h1 is None else [pltpu.VMEM((l, cb), F32)],
        compiler_params=_cparams(("parallel", "parallel")),
        name="hy_gate",
    )(z, conv, y, bias.astype(F32).reshape(1, w))


def _hyena_branch(proj, col0, p, cb=128):
    b, l, _ = proj.shape
    w = p['hy_bias'].shape[-1]
    n = 2 * l
    nblk = w // cb
    if n // HY_FAST >= 16:
        consts = dict(_fft_consts(l), nblk=nblk)
        n1, h1 = consts['n1'], consts['h1']
        spec_r, spec_i = _fft_spectrum(_hyena_filter(l, p, slow=n1).astype(BF16), consts, cb)
        conv = lambda y, ycol, order: _fft_conv(y, ycol * nblk, spec_r, spec_i, order * nblk, consts, cb)
    else:
        h1 = None
        k = np.arange(n)
        ang = -2.0 * np.pi * np.outer(k, k) / n
        fmat = jnp.asarray(_block2(np.cos(ang[:, :l]), np.sin(ang[:, :l])), BF16)
        gmat = jnp.asarray(_block2(np.cos(ang[:l, :]) / n, -np.sin(ang[:l, :]) / n), BF16)
        f_real = jnp.asarray(np.concatenate([np.cos(ang), np.sin(ang)], axis=0), F32)
        spec = _bmm(f_real[None], _hyena_filter(l, p).astype(BF16)[None], name="dense_spectrum")[0]
        spec_r, spec_i = spec[:n], spec[n:]
        conv = lambda y, ycol, order: _dense_conv(y, ycol * nblk, spec_r, spec_i, order * nblk, fmat, gmat,
                                                  nblk, cb)
    z = _short_conv(proj, col0, p['hy_conv_w'], p['hy_conv_b'], h1, cb)
    c0 = conv(z, HY_ORDER, 0)
    y1 = _hy_gate(z, 0, c0, z, HY_ORDER * nblk, p['hy_bias'][0], None, cb)
    c1 = conv(y1, 0, 1)
    return _hy_gate(z, nblk, c1, y1, 0, p['hy_bias'][1], h1, cb)


def _merge_kernel(x_ref, g_ref, sh_ref, sc_ref, g1_ref, br_ref, wg0_ref, wg1_ref, wg2_ref, bg_ref, wb_ref, wo_ref,
                  o_ref, h_scr, acc_scr):
    j = pl.program_id(2)

    @pl.when(j == 0)
    def _():
        h_scr[...] = _norm_mod(x_ref[...], g_ref[...], sh_ref[...], sc_ref[...]).astype(BF16)
        acc_scr[...] = jnp.zeros_like(acc_scr)

    h = h_scr[...]
    mixed = None
    for i, wg_ref in enumerate((wg0_ref, wg1_ref, wg2_ref)):
        gate = jax.nn.sigmoid(jnp.dot(h, wg_ref[...], preferred_element_type=F32) + bg_ref[i])
        term = gate * jnp.dot(br_ref[i], wb_ref[i], preferred_element_type=F32)
        mixed = term if mixed is None else mixed + term
    acc_scr[...] += jnp.dot(mixed.astype(BF16), wo_ref[...], preferred_element_type=F32)

    @pl.when(j == pl.num_programs(2) - 1)
    def _():
        o_ref[...] = x_ref[...] + g1_ref[...] * acc_scr[...]


def _merge(x, g, sh, sc, g1, branches, w_gate_bf, b_gate, w_branch_bf, w_out_bf, tm, tc):
    b, l, d = x.shape
    nbr, w, _ = w_branch_bf.shape
    nb = sh.shape[0]
    nj = d // tc
    mod_map = (lambda bi, i, j: (bi, 0, 0)) if nb > 1 else (lambda bi, i, j: (0, 0, 0))
    br = jnp.stack(branches, axis=1)
    bg = b_gate.reshape(nbr, nj, 1, tc).transpose(1, 0, 2, 3)
    return pl.pallas_call(
        _merge_kernel,
        grid=(b, l // tm, nj),
        in_specs=[pl.BlockSpec((None, tm, d), lambda bi, i, j: (bi, i, 0)),
                  pl.BlockSpec((1, d), lambda bi, i, j: (0, 0)),
                  pl.BlockSpec((None, 1, d), mod_map),
                  pl.BlockSpec((None, 1, d), mod_map),
                  pl.BlockSpec((None, 1, d), mod_map),
                  pl.BlockSpec((None, nbr, tm, w), lambda bi, i, j: (bi, 0, i, 0)),
                  pl.BlockSpec((d, tc), lambda bi, i, j: (0, j)),
                  pl.BlockSpec((d, tc), lambda bi, i, j: (0, nj + j)),
                  pl.BlockSpec((d, tc), lambda bi, i, j: (0, 2 * nj + j)),
                  pl.BlockSpec((None, nbr, 1, tc), lambda bi, i, j: (j, 0, 0, 0)),
                  pl.BlockSpec((nbr, w, tc), lambda bi, i, j: (0, 0, j)),
                  pl.BlockSpec((tc, d), lambda bi, i, j: (j, 0))],
        out_specs=pl.BlockSpec((None, tm, d), lambda bi, i, j: (bi, i, 0)),
        out_shape=jax.ShapeDtypeStruct((b, l, d), F32),
        scratch_shapes=[pltpu.VMEM((tm, d), BF16), pltpu.VMEM((tm, d), F32)],
        compiler_params=_cparams(("parallel", "parallel", "arbitrary"), vmem_mb=56),
        name="merge",
    )(x, g.reshape(1, d), sh, sc, g1, br, w_gate_bf, w_gate_bf, w_gate_bf, bg, w_branch_bf, w_out_bf)


def _router_kernel(x_ref, g_ref, sh_ref, sc_ref, rt_ref, h_ref, aff_ref):
    h = _norm_mod(x_ref[...], g_ref[...], sh_ref[...], sc_ref[...])
    h_ref[...] = h.astype(h_ref.dtype)
    nt = (((1,), (1,)), ((), ()))
    rt = rt_ref[...]
    r_hi = rt.astype(BF16)
    r_lo = (rt - r_hi.astype(F32)).astype(BF16)
    h_hi = h.astype(BF16)
    h_lo = (h - h_hi.astype(F32)).astype(BF16)
    logits = (lax.dot_general(r_hi, h_hi, nt, preferred_element_type=F32)
              + lax.dot_general(r_hi, h_lo, nt, preferred_element_type=F32)
              + lax.dot_general(r_lo, h_hi, nt, preferred_element_type=F32))
    e = jnp.exp(logits - jnp.max(logits, axis=0, keepdims=True))
    aff_ref[...] = e / jnp.sum(e, axis=0, keepdims=True)


def _router(x, g, sh, sc, router_t, tm):
    b, l, d = x.shape
    e = router_t.shape[0]
    nb = sh.shape[0]
    mod_map = (lambda bi, i: (bi, 0, 0)) if nb > 1 else (lambda bi, i: (0, 0, 0))
    return pl.pallas_call(
        _router_kernel,
        grid=(b, l // tm),
        in_specs=[pl.BlockSpec((None, tm, d), lambda bi, i: (bi, i, 0)),
                  pl.BlockSpec((1, d), lambda bi, i: (0, 0)),
                  pl.BlockSpec((None, 1, d), mod_map),
                  pl.BlockSpec((None, 1, d), mod_map),
                  pl.BlockSpec((e, d), lambda bi, i: (0, 0))],
        out_specs=[pl.BlockSpec((None, tm, d), lambda bi, i: (bi, i, 0)),
                   pl.BlockSpec((None, e, tm), lambda bi, i: (bi, 0, i))],
        out_shape=[jax.ShapeDtypeStruct((b, l, d), BF16), jax.ShapeDtypeStruct((b, e, l), F32)],
        compiler_params=_cparams(("parallel", "parallel")),
        name="router",
    )(x, g.reshape(1, d), sh, sc, router_t)


def _ffn_kernel(xs_ref, gate_ref, w1_ref, w3_ref, w2_ref, o_ref, acc_scr):
    f = pl.program_id(2)

    @pl.when(f == 0)
    def _():
        acc_scr[...] = jnp.zeros_like(acc_scr)

    xs = xs_ref[...]
    a = jnp.dot(xs, w1_ref[...].astype(BF16), preferred_element_type=F32)
    gl = jnp.dot(xs, w3_ref[...].astype(BF16), preferred_element_type=F32)
    mid = (a * jax.nn.sigmoid(a) * gl).astype(BF16)
    acc_scr[...] += jnp.dot(mid, w2_ref[...].astype(BF16), preferred_element_type=F32)

    @pl.when(f == pl.num_programs(2) - 1)
    def _():
        o_ref[...] = (acc_scr[...] * gate_ref[...]).astype(o_ref.dtype)


def _expert_ffn(xs, gate, w1, w3, w2, layer, tr, tf):
    e, r, d = xs.shape
    ff = w1.shape[3]
    return pl.pallas_call(
        _ffn_kernel,
        grid=(e, r // tr, ff // tf),
        in_specs=[pl.BlockSpec((None, tr, d), lambda ei, i, f: (ei, i, 0)),
                  pl.BlockSpec((None, tr, 1), lambda ei, i, f: (ei, i, 0)),
                  pl.BlockSpec((None, None, d, tf), lambda ei, i, f: (layer, ei, 0, f)),
                  pl.BlockSpec((None, None, d, tf), lambda ei, i, f: (layer, ei, 0, f)),
                  pl.BlockSpec((None, None, tf, d), lambda ei, i, f: (layer, ei, f, 0))],
        out_specs=pl.BlockSpec((None, tr, d), lambda ei, i, f: (ei, i, 0)),
        out_shape=jax.ShapeDtypeStruct((e, r, d), BF16),
        scratch_shapes=[pltpu.VMEM((tr, d), F32)],
        compiler_params=_cparams(("parallel", "parallel", "arbitrary"), vmem_mb=56),
        name="expert_ffn",
    )(xs, gate, w1, w3, w2)


def _route_kernel(aff_ref, pos_ref, idx_ref, csum_scr, *, cap, sblk):
    aff = aff_ref[...]
    e, l = aff.shape
    bits = pltpu.bitcast(aff, jnp.int32)
    count = lambda m: jnp.sum(jnp.where(m, 1.0, 0.0), axis=1, keepdims=True)

    def bisect(i, thr):
        cand = thr | lax.shift_left(jnp.int32(1), 30 - i)
        return jnp.where(count(bits >= cand) >= cap, cand, thr)

    thr = lax.fori_loop(0, 31, bisect, jnp.zeros((e, 1), jnp.int32))
    lane = lax.broadcasted_iota(jnp.int32, (e, l), 1)

    def cumsum(x):
        k = 1
        while k < l:
            x = x + jnp.where(lane >= k, pltpu.roll(x, k, 1), 0.0)
            k *= 2
        return x

    above = bits > thr
    tie = jnp.where(bits == thr, 1.0, 0.0)
    need = cap - count(above)
    keep = jnp.logical_or(above, jnp.logical_and(tie > 0.0, cumsum(tie) - tie < need))
    csum = cumsum(jnp.where(keep, 1.0, 0.0))
    pos_ref[...] = jnp.where(keep, csum - 1.0, -1.0).astype(jnp.int32)
    csum_scr[...] = csum
    expert_lane = lax.broadcasted_iota(jnp.int32, (sblk, e), 1)
    slots = [(lax.broadcasted_iota(jnp.int32, (sblk, 1), 0) + s0).astype(F32) for s0 in range(0, cap, sblk)]

    def per_expert(ei, accs):
        row = csum_scr[pl.ds(ei, 1), :]
        return tuple(jnp.where(expert_lane == ei, count(row <= s), a) for s, a in zip(slots, accs))

    accs = lax.fori_loop(0, e, per_expert, tuple(jnp.zeros((sblk, e), F32) for _ in slots))
    for i, a in enumerate(accs):
        idx_ref[i * sblk:(i + 1) * sblk, :] = a.astype(jnp.int32)


def _route(aff_t, cap):
    b, e, l = aff_t.shape
    return pl.pallas_call(
        functools.partial(_route_kernel, cap=cap, sblk=min(256, cap)),
        grid=(b,),
        in_specs=[pl.BlockSpec((None, e, l), lambda bi: (bi, 0, 0))],
        out_specs=[pl.BlockSpec((None, e, l), lambda bi: (bi, 0, 0)),
                   pl.BlockSpec((None, cap, e), lambda bi: (bi, 0, 0))],
        out_shape=[jax.ShapeDtypeStruct((b, e, l), jnp.int32), jax.ShapeDtypeStruct((b, cap, e), jnp.int32)],
        scratch_shapes=[pltpu.VMEM((e, l), F32)],
        compiler_params=_cparams(("parallel",)),
        name="moe_route",
    )(aff_t)


def _combine_kernel(st_ref, x_ref, g2_ref, pos_ref, *refs, n_exp, tiles, tn):
    y_refs, o_ref, oh_scr = refs[:n_exp], refs[n_exp], refs[n_exp + 1]
    bi, j = pl.program_id(0), pl.program_id(1)
    tt, win = oh_scr.shape[1], y_refs[0].shape[1]
    lane = lax.broadcasted_iota(jnp.int32, (tt, win), 1)
    main = min(win, tt)
    spill = []
    for e in range(n_exp):
        rel = pos_ref[:, e:e + 1] - st_ref[(bi * n_exp + e) * tiles + j]
        oh_scr[e] = jnp.where(lane == rel, 1.0, 0.0).astype(BF16)
        spill.append(jnp.max(rel) >= main)
    for c in range(o_ref.shape[1] // tn):
        cols = slice(c * tn, (c + 1) * tn)
        acc = jnp.dot(oh_scr[0, :, :main], y_refs[0][0, :main, cols], preferred_element_type=F32)
        for e in range(1, n_exp):
            acc = acc + jnp.dot(oh_scr[e, :, :main], y_refs[e][0, :main, cols], preferred_element_type=F32)
        o_ref[:, cols] = x_ref[:, cols] + g2_ref[:, cols] * acc
    if win > main:
        for e in range(n_exp):
            @pl.when(spill[e])
            def _(e=e):
                tail = jnp.dot(oh_scr[e, :, main:], y_refs[e][0, main:, :], preferred_element_type=F32)
                o_ref[...] += g2_ref[...] * tail


def _moe_combine(x, g2, pos, y, cap, tt):
    b, l, d = x.shape
    e = y.shape[0]
    tiles = l // tt
    win = min(tt + COMBINE_ALIGN, cap)
    sel = (pos >= 0).reshape(b, tiles, tt, e).sum(2)
    first = jnp.cumsum(sel, axis=1) - sel
    start = jnp.minimum(first // COMBINE_ALIGN * COMBINE_ALIGN, cap - win).astype(jnp.int32)
    start = jnp.transpose(start, (0, 2, 1)).reshape(-1)
    nb = g2.shape[0]

    def y_spec(ei):
        def y_map(bi, j, st):
            return ei, pl.multiple_of(bi * cap + st[(bi * e + ei) * tiles + j], COMBINE_ALIGN), 0
        return pl.BlockSpec((pl.Element(1), pl.Element(win), pl.Element(d)), y_map)

    grid_spec = pltpu.PrefetchScalarGridSpec(
        num_scalar_prefetch=1,
        grid=(b, tiles),
        in_specs=[pl.BlockSpec((None, tt, d), lambda bi, j, st: (bi, j, 0)),
                  pl.BlockSpec((None, 1, d), lambda bi, j, st: (bi if nb > 1 else 0, 0, 0)),
                  pl.BlockSpec((None, tt, e), lambda bi, j, st: (bi, j, 0))] + [y_spec(ei) for ei in range(e)],
        out_specs=pl.BlockSpec((None, tt, d), lambda bi, j, st: (bi, j, 0)),
        scratch_shapes=[pltpu.VMEM((e, tt, win), BF16)])
    return pl.pallas_call(
        functools.partial(_combine_kernel, n_exp=e, tiles=tiles, tn=min(256, d)),
        grid_spec=grid_spec,
        out_shape=jax.ShapeDtypeStruct((b, l, d), F32),
        compiler_params=_cparams(("parallel", "parallel"), vmem_mb=56),
        name="moe_combine",
    )(start, x, g2, pos, *([y] * e))


def _ec_moe_update(x, g, sh, sc, g2, p, tm):
    b, l, d = x.shape
    e = p['router'].shape[1]
    cap = max(1, EC_CAPACITY * l // e)
    h, aff_t = _router(x, g, sh, sc, jnp.transpose(p['router']).astype(F32), tm)
    pos_t, idx_ce = _route(aff_t, cap)
    pos = jnp.transpose(pos_t, (0, 2, 1))
    idx_t = jnp.transpose(idx_ce, (2, 0, 1))
    xs = h[jnp.arange(b)[None, :, None], idx_t].reshape(e, b * cap, d)
    gt = jnp.take_along_axis(jnp.transpose(aff_t, (1, 0, 2)), idx_t, axis=2).reshape(e, b * cap, 1)
    tr = min(1024, b * cap)
    ff = p['exp_w1'].shape[3]
    y = _expert_ffn(xs, gt, p['exp_w1'], p['exp_w3'], p['exp_w2'], p['layer'], tr=tr, tf=min(512, ff))
    return _moe_combine(x, g2, pos, y, cap, min(256, l))


def _layer(x, xc, c, c_ctx, p, last):
    b, l, d = x.shape
    lc = xc.shape[1]
    w = p['ssm_d'].shape[0]
    hd = w // NA_HEADS

    cc = jnp.zeros((8, d), F32).at[:b].set(c).at[b].set(c_ctx)
    mod = _ada(cc, p['w_ada'], p['b_ada'], p['layer'])
    m_l = [t[:, None, :] for t in jnp.split(mod[:b], 6, axis=-1)]
    m_c = [t[:, None, :] for t in jnp.split(mod[b:b + 1], 6, axis=-1)]

    w_in = p['w_in'].astype(BF16)
    scale = hd ** -0.5
    head_gain = jnp.stack([jnp.tile(p['na_q_gain'].astype(F32) * scale, NA_HEADS),
                           jnp.tile(p['na_k_gain'].astype(F32), NA_HEADS)])[:, None, :]
    ones_blk = jnp.asarray(np.kron(np.eye(NA_HEADS), np.ones((hd, hd))), BF16)
    tm_l = min(512, l)
    tm_c = min(512, lc)
    proj = _in_proj(x, p['norm1'], m_l[0], m_l[1], w_in, head_gain, ones_blk, min(1024, l))
    projc = _in_proj(xc, p['norm1'], m_c[0], m_c[1], w_in, head_gain, ones_blk, tm_c)

    ssm_l, ssm_c = _s5_branch(projc[..., :w], proj[..., :w], p, _s5_tables(p), need_ctx=not last)

    na_l = _na_latent(proj[..., w:2 * w], proj[..., 2 * w:3 * w], proj[..., 3 * w:4 * w],
                      projc[..., 2 * w:3 * w], projc[..., 3 * w:4 * w], _na_bias_table(p['na_rpb']))
    hy_l = _hyena_branch(proj, 4 * w // 128, p)

    wg = p['w_gate'].astype(BF16)
    wb = p['w_branch'].astype(BF16)
    wo = p['w_out'].astype(BF16)
    x = _merge(x, p['norm1'], m_l[0], m_l[1], m_l[2], (ssm_l, na_l, hy_l), wg, p['b_gate'], wb, wo,
               tm=tm_l, tc=min(512, d))
    x = _ec_moe_update(x, p['norm2'], m_l[3], m_l[4], m_l[5], p, tm_l)
    if last:
        return x, None
    na_c = _na_context(projc[..., w:2 * w], projc[..., 2 * w:3 * w], projc[..., 3 * w:4 * w])
    hy_c = _hyena_branch(projc, 4 * w // 128, p)
    xc = _merge(xc, p['norm1'], m_c[0], m_c[1], m_c[2], (ssm_c, na_c, hy_c), wg, p['b_gate'], wb, wo,
                tm=tm_c, tc=min(512, d))
    xc = _ec_moe_update(xc, p['norm2'], m_c[3], m_c[4], m_c[5], p, tm_c)
    return x, xc


_PARAM_NAMES = ('w_ada', 'b_ada', 'norm1', 'norm2', 'w_in',
                'ssm_lam_re', 'ssm_lam_im', 'ssm_log_step', 'ssm_b_re', 'ssm_b_im', 'ssm_c_re', 'ssm_c_im',
                'ssm_d', 'ssm_w_glu', 'na_q_gain', 'na_k_gain', 'na_rpb',
                'hy_conv_w', 'hy_conv_b', 'hy_w1', 'hy_b1', 'hy_w2', 'hy_b2', 'hy_w3', 'hy_b3', 'hy_w4',
                'hy_freq', 'hy_bias', 'w_gate', 'b_gate', 'w_branch', 'w_out',
                'router', 'exp_w1', 'exp_w3', 'exp_w2')
_STACKED = ('w_ada', 'b_ada', 'exp_w1', 'exp_w3', 'exp_w2')


def kernel(x, c, ctx, c_ctx, w_ada, b_ada, norm1, norm2, w_in, ssm_lam_re, ssm_lam_im, ssm_log_step, ssm_b_re, ssm_b_im, ssm_c_re, ssm_c_im, ssm_d, ssm_w_glu, na_q_gain, na_k_gain, na_rpb, hy_conv_w, hy_conv_b, hy_w1, hy_b1, hy_w2, hy_b2, hy_w3, hy_b3, hy_w4, hy_freq, hy_bias, w_gate, b_gate, w_branch, w_out, router, exp_w1, exp_w3, exp_w2):
    stacked = (w_ada, b_ada, norm1, norm2, w_in, ssm_lam_re, ssm_lam_im, ssm_log_step, ssm_b_re, ssm_b_im,
               ssm_c_re, ssm_c_im, ssm_d, ssm_w_glu, na_q_gain, na_k_gain, na_rpb, hy_conv_w, hy_conv_b,
               hy_w1, hy_b1, hy_w2, hy_b2, hy_w3, hy_b3, hy_w4, hy_freq, hy_bias, w_gate, b_gate, w_branch,
               w_out, router, exp_w1, exp_w3, exp_w2)
    depth = w_ada.shape[0]
    xc = ctx
    for layer in range(depth):
        p = {name: (t if name in _STACKED else t[layer]) for name, t in zip(_PARAM_NAMES, stacked)}
        p['layer'] = layer
        x, xc = _layer(x, xc, c, c_ctx, p, layer == depth - 1)
    return x
```

```python
import functools
import math

import numpy as np
import jax
import jax.numpy as jnp
from jax import lax
from jax.experimental import pallas as pl
from jax.experimental.pallas import tpu as pltpu

BF16 = jnp.bfloat16
F32 = jnp.float32

NORM_EPS = 1e-6
GRID_W = 64
SSM_GROUP = 16
NA_HEADS = 8
NA_WIN_ROWS = 8
NA_WIN_COLS = 16
HY_ORDER = 2
HY_BANDS = 16
HY_MIN_DECAY = math.log(1e-2) / 1.5
HY_MAX_DECAY = math.log(1e-2) / 0.3
EC_CAPACITY = 2
S5_CHUNK = 16
HY_FAST = 128
HY_PAD = 8
MASK_VALUE = -1e30
COMBINE_ALIGN = 16

V7X_VMEM_BYTES = 64 * 1024 * 1024


def _cparams(semantics, vmem_mb=48):
    return pltpu.CompilerParams(dimension_semantics=semantics, vmem_limit_bytes=vmem_mb * 1024 * 1024)


def _norm_mod(x, g, sh, sc):
    ms = jnp.mean(x * x, axis=-1, keepdims=True)
    return (x * lax.rsqrt(ms + NORM_EPS) * g) * (1.0 + sc) + sh


def _ada_kernel(c_ref, w_ref, b_ref, o_ref):
    c = c_ref[...]
    a = (c * jax.nn.sigmoid(c)).astype(BF16)
    o_ref[...] = jnp.dot(a, w_ref[...].astype(BF16), preferred_element_type=F32) + b_ref[...]


def _ada(cc, w, b, layer):
    m, k = cc.shape
    depth, _, n = w.shape
    tn = next(t for t in (1024, 512, 256, 128) if n % t == 0)
    return pl.pallas_call(
        _ada_kernel,
        grid=(n // tn,),
        in_specs=[pl.BlockSpec((m, k), lambda j: (0, 0)),
                  pl.BlockSpec((None, k, tn), lambda j: (layer, 0, j)),
                  pl.BlockSpec((None, 1, tn), lambda j: (layer, 0, j))],
        out_specs=pl.BlockSpec((m, tn), lambda j: (0, j)),
        out_shape=jax.ShapeDtypeStruct((m, n), F32),
        compiler_params=_cparams(("arbitrary",)),
        name="ada",
    )(cc, w, b.reshape(depth, 1, n))


def _in_proj_kernel(x_ref, g_ref, sh_ref, sc_ref, w_ref, hg_ref, ones_ref, o_ref, h_scr, *, head_dim):
    j = pl.program_id(2)

    @pl.when(j == 0)
    def _():
        h_scr[...] = _norm_mod(x_ref[...], g_ref[...], sh_ref[...], sc_ref[...]).astype(BF16)

    acc = jnp.dot(h_scr[...], w_ref[...], preferred_element_type=F32)
    is_qk = jnp.logical_or(j == 1, j == 2)

    @pl.when(is_qk)
    def _():
        yy = acc * acc
        hi = yy.astype(BF16)
        lo = (yy - hi.astype(F32)).astype(BF16)
        ss = (jnp.dot(hi, ones_ref[...], preferred_element_type=F32)
              + jnp.dot(lo, ones_ref[...], preferred_element_type=F32))
        o_ref[...] = (acc * lax.rsqrt(ss * (1.0 / head_dim) + NORM_EPS) * hg_ref[...]).astype(o_ref.dtype)

    @pl.when(jnp.logical_not(is_qk))
    def _():
        o_ref[...] = acc.astype(o_ref.dtype)


def _in_proj(x, g, sh, sc, w_bf, head_gain, ones_blk, tm):
    b, l, d = x.shape
    n = w_bf.shape[1]
    tn = ones_blk.shape[0]
    nb = sh.shape[0]
    mod_map = (lambda bi, i, j: (bi, 0, 0)) if nb > 1 else (lambda bi, i, j: (0, 0, 0))
    return pl.pallas_call(
        functools.partial(_in_proj_kernel, head_dim=tn // NA_HEADS),
        grid=(b, l // tm, n // tn),
        in_specs=[pl.BlockSpec((None, tm, d), lambda bi, i, j: (bi, i, 0)),
                  pl.BlockSpec((1, d), lambda bi, i, j: (0, 0)),
                  pl.BlockSpec((None, 1, d), mod_map),
                  pl.BlockSpec((None, 1, d), mod_map),
                  pl.BlockSpec((d, tn), lambda bi, i, j: (0, j)),
                  pl.BlockSpec((None, 1, tn), lambda bi, i, j: (jnp.clip(j - 1, 0, 1), 0, 0)),
                  pl.BlockSpec((tn, tn), lambda bi, i, j: (0, 0))],
        out_specs=pl.BlockSpec((None, tm, tn), lambda bi, i, j: (bi, i, j)),
        out_shape=jax.ShapeDtypeStruct((b, l, n), BF16),
        scratch_shapes=[pltpu.VMEM((tm, d), BF16)],
        compiler_params=_cparams(("parallel", "parallel", "arbitrary")),
        name="in_proj",
    )(x, g.reshape(1, d), sh, sc, w_bf, head_gain, ones_blk)


def _bmm_kernel(a_ref, w_ref, o_ref):
    o_ref[...] = jnp.dot(a_ref[...].astype(BF16), w_ref[...], preferred_element_type=F32).astype(o_ref.dtype)


def _bmm_add_kernel(a_ref, w_ref, y_ref, o_ref):
    o_ref[...] = (y_ref[...] + jnp.dot(a_ref[...].astype(BF16), w_ref[...],
                                       preferred_element_type=F32)).astype(o_ref.dtype)


def _bmm(a, w, add=None, tm=None, name="bmm"):
    g, m, k = a.shape
    n = w.shape[2]
    tm = m if tm is None else tm
    in_specs = [pl.BlockSpec((None, tm, k), lambda gi, i: (gi, i, 0)),
                pl.BlockSpec((None, k, n), lambda gi, i: (gi, 0, 0))]
    args = [a, w]
    kern = _bmm_kernel
    if add is not None:
        in_specs.append(pl.BlockSpec((None, tm, n), lambda gi, i: (gi, i, 0)))
        args.append(add)
        kern = _bmm_add_kernel
    return pl.pallas_call(
        kern,
        grid=(g, m // tm),
        in_specs=in_specs,
        out_specs=pl.BlockSpec((None, tm, n), lambda gi, i: (gi, i, 0)),
        out_shape=jax.ShapeDtypeStruct((g, m, n), F32),
        compiler_params=_cparams(("parallel", "parallel")),
        name=name,
    )(*args)


def _gelu_tanh(x):
    return 0.5 * x * (1.0 + jnp.tanh(math.sqrt(2.0 / math.pi) * (x + 0.044715 * (x * x * x))))


def _s5_out_kernel(y_ref, u_ref, d_ref, w_ref, o_ref):
    y = y_ref[...].astype(F32) + d_ref[...] * u_ref[...].astype(F32)
    z = _gelu_tanh(y)
    gate = jax.nn.sigmoid(jnp.dot(z.astype(BF16), w_ref[...], preferred_element_type=F32))
    o_ref[...] = (z * gate).astype(o_ref.dtype)


def _s5_out(y, row0, u, d, w_glu_bf):
    b, l, w = u.shape
    tm = math.gcd(math.gcd(l, row0), 512) if row0 else math.gcd(l, 512)
    off = row0 // tm
    return pl.pallas_call(
        _s5_out_kernel,
        grid=(b, l // tm),
        in_specs=[pl.BlockSpec((None, tm, w), lambda bi, i: (bi, off + i, 0)),
                  pl.BlockSpec((None, tm, w), lambda bi, i: (bi, i, 0)),
                  pl.BlockSpec((1, w), lambda bi, i: (0, 0)),
                  pl.BlockSpec((w, w), lambda bi, i: (0, 0))],
        out_specs=pl.BlockSpec((None, tm, w), lambda bi, i: (bi, i, 0)),
        out_shape=jax.ShapeDtypeStruct((b, l, w), BF16),
        compiler_params=_cparams(("parallel", "parallel")),
        name="s5_out",
    )(y, u, d.reshape(1, w), w_glu_bf)


def _s5_tables(p):
    t = S5_CHUNK
    mats, sums, reads, steps = [], [], [], []
    for d in range(2):
        lam_re = p['ssm_lam_re'][d].astype(F32)
        lam_im = p['ssm_lam_im'][d].astype(F32)
        step = jnp.exp(p['ssm_log_step'][d].astype(F32))[:, None]
        mag = jnp.exp(lam_re * step)
        a_re = mag * jnp.cos(lam_im * step)
        a_im = mag * jnp.sin(lam_im * step)
        den = lam_re * lam_re + lam_im * lam_im
        num_re = a_re - 1.0
        coef_re = (num_re * lam_re + a_im * lam_im) / den
        coef_im = (a_im * lam_re - num_re * lam_im) / den
        b_re = p['ssm_b_re'][d].astype(F32)
        b_im = p['ssm_b_im'][d].astype(F32)
        bb_re = coef_re[..., None] * b_re - coef_im[..., None] * b_im
        bb_im = coef_re[..., None] * b_im + coef_im[..., None] * b_re
        c_re = p['ssm_c_re'][d].astype(F32)
        c_im = p['ssm_c_im'][d].astype(F32)
        n = jnp.arange(t + 1, dtype=F32)[:, None, None]
        pm = jnp.exp(lam_re * step * n)
        pw_re = pm * jnp.cos(lam_im * step * n)
        pw_im = pm * jnp.sin(lam_im * step * n)
        ca_re = c_re[None] * pw_re[:, :, None, :] - c_im[None] * pw_im[:, :, None, :]
        ca_im = c_re[None] * pw_im[:, :, None, :] + c_im[None] * pw_re[:, :, None, :]
        ktau = (jnp.einsum('ngop,gpi->ngoi', ca_re[:t], bb_re)
                - jnp.einsum('ngop,gpi->ngoi', ca_im[:t], bb_im))
        s_idx = np.arange(t)[:, None]
        t_idx = np.arange(t)[None, :]
        lag = (t_idx - s_idx) if d == 0 else (s_idx - t_idx)
        lag_sel = jnp.asarray(lag[:, :, None] == np.arange(t), F32)
        blk = jnp.einsum('stn,ngoi->gsito', lag_sel, ktau, precision=lax.Precision.HIGHEST)
        mats.append(blk)
        flip = (lambda a: a[::-1]) if d == 0 else (lambda a: a)
        unflip = (lambda a: a) if d == 0 else (lambda a: a[::-1])
        sp_re, sp_im = flip(pw_re[:t]), flip(pw_im[:t])
        sw_re = sp_re[:, :, :, None] * bb_re[None] - sp_im[:, :, :, None] * bb_im[None]
        sw_im = sp_re[:, :, :, None] * bb_im[None] + sp_im[:, :, :, None] * bb_re[None]
        sums.append((jnp.transpose(sw_re, (1, 0, 3, 2)), jnp.transpose(sw_im, (1, 0, 3, 2))))
        rd_re, rd_im = unflip(ca_re[1:t + 1]), unflip(ca_im[1:t + 1])
        reads.append((jnp.transpose(rd_re, (1, 3, 0, 2)), -jnp.transpose(rd_im, (1, 3, 0, 2))))
        steps.append((pw_re[t], pw_im[t]))
    g, pdim = steps[0][0].shape
    k = SSM_GROUP
    mtot = (mats[0] + mats[1]).reshape(g, t * k, t * k)
    ssum = jnp.concatenate([sums[0][0], sums[0][1], sums[1][0], sums[1][1]], axis=-1).reshape(g, t * k, 4 * pdim)
    r = jnp.concatenate([reads[0][0], reads[0][1], reads[1][0], reads[1][1]], axis=1).reshape(g, 4 * pdim, t * k)
    tk = t * k
    eye = jnp.eye(2, dtype=F32)
    ymat = jnp.einsum('gjsito,jk->gsjitko', mtot.reshape(g // 2, 2, t, k, t, k), eye).reshape(g // 2, 2 * tk, 2 * tk)
    smat = jnp.einsum('gjsiqp,jk->gsjiqkp', ssum.reshape(g // 2, 2, t, k, 4, pdim), eye
                      ).reshape(g // 2, 2 * tk, 8 * pdim)
    w1 = jnp.concatenate([ymat, smat], axis=-1)
    r2 = jnp.einsum('gjqpto,jk->gqjptko', r.reshape(g // 2, 2, 4, pdim, t, k), eye).reshape(g // 2, 8 * pdim, 2 * tk)
    coef = [c.reshape(1, g * pdim) for c in (steps[0][0], steps[0][1], steps[1][0], steps[1][1])]
    return w1.astype(BF16), r2.astype(BF16), coef


LANES = 128


def _to_groups_kernel(u_ref, o_ref, u_scr, o_scr, *, cbk, bsz):
    t, k2 = S5_CHUNK, 2 * SSM_GROUP
    per = LANES // k2
    for b in range(bsz):
        for lt in range(u_scr.shape[0]):
            u_scr[lt] = u_ref[b, :, lt * LANES:(lt + 1) * LANES].astype(F32)
        for lt in range(u_scr.shape[0]):
            xs = [u_scr[lt, pl.ds(s, cbk, stride=t), :] for s in range(t)]
            for q in range(per):
                v = jnp.concatenate([x[:, q * k2:(q + 1) * k2] for x in xs], axis=1)
                for ot in range(o_scr.shape[1]):
                    o_scr[lt * per + q, ot, pl.ds(b, cbk, stride=bsz), :] = v[:, ot * LANES:(ot + 1) * LANES]
    for gp in range(o_scr.shape[0]):
        for ot in range(o_scr.shape[1]):
            o_ref[gp, :, ot * LANES:(ot + 1) * LANES] = o_scr[gp, ot].astype(o_ref.dtype)


def _from_groups_kernel(y_ref, o_ref, y_scr, o_scr, *, cbk, bsz):
    t, k2 = S5_CHUNK, 2 * SSM_GROUP
    per = LANES // k2
    ngp, nlt = y_scr.shape[0], y_scr.shape[1]
    for gp in range(ngp):
        for ot in range(nlt):
            y_scr[gp, ot] = y_ref[gp, :, ot * LANES:(ot + 1) * LANES].astype(F32)
    for b in range(bsz):
        for ot in range(nlt):
            zs = [y_scr[gp, ot, pl.ds(b, cbk, stride=bsz), :] for gp in range(ngp)]
            for q in range(per):
                s = ot * per + q
                v = jnp.concatenate([z[:, q * k2:(q + 1) * k2] for z in zs], axis=1)
                for lt in range(o_scr.shape[0]):
                    o_scr[lt, pl.ds(s, cbk, stride=t), :] = v[:, lt * LANES:(lt + 1) * LANES]
        for lt in range(o_scr.shape[0]):
            o_ref[b, :, lt * LANES:(lt + 1) * LANES] = o_scr[lt].astype(o_ref.dtype)


def _chunk_block(nc, bsz):
    return next(c for c in range(min(64, nc), 0, -1) if nc % c == 0 and (c * bsz) % 16 == 0)


def _to_groups(u):
    bsz, ltot, w = u.shape
    t, k2 = S5_CHUNK, 2 * SSM_GROUP
    nc = ltot // t
    cbk = _chunk_block(nc, bsz)
    ngp = w // k2
    return pl.pallas_call(
        functools.partial(_to_groups_kernel, cbk=cbk, bsz=bsz),
        grid=(nc // cbk,),
        in_specs=[pl.BlockSpec((bsz, cbk * t, w), lambda i: (0, i, 0))],
        out_specs=pl.BlockSpec((ngp, cbk * bsz, t * k2), lambda i: (0, i, 0)),
        out_shape=jax.ShapeDtypeStruct((ngp, nc * bsz, t * k2), BF16),
        scratch_shapes=[pltpu.VMEM((w // LANES, cbk * t, LANES), F32),
                        pltpu.VMEM((ngp, t * k2 // LANES, cbk * bsz, LANES), F32)],
        compiler_params=_cparams(("parallel",)),
        name="s5_to_groups",
    )(u)


def _from_groups(y, bsz):
    ngp, m, tk2 = y.shape
    t, k2 = S5_CHUNK, 2 * SSM_GROUP
    nc = m // bsz
    cbk = _chunk_block(nc, bsz)
    w = ngp * k2
    return pl.pallas_call(
        functools.partial(_from_groups_kernel, cbk=cbk, bsz=bsz),
        grid=(nc // cbk,),
        in_specs=[pl.BlockSpec((ngp, cbk * bsz, tk2), lambda i: (0, i, 0))],
        out_specs=pl.BlockSpec((bsz, cbk * t, w), lambda i: (0, i, 0)),
        out_shape=jax.ShapeDtypeStruct((bsz, nc * t, w), BF16),
        scratch_shapes=[pltpu.VMEM((ngp, tk2 // LANES, cbk * bsz, LANES), F32),
                        pltpu.VMEM((w // LANES, cbk * t, LANES), F32)],
        compiler_params=_cparams(("parallel",)),
        name="s5_from_groups",
    )(y)


def _s5_chunk_kernel(u_ref, w_ref, y_ref, xfr_ref, xfi_ref, xbr_ref, xbi_ref):
    r = jnp.dot(u_ref[...], w_ref[...], preferred_element_type=F32)
    ny = y_ref.shape[-1]
    nx = xfr_ref.shape[-1]
    y_ref[...] = r[:, :ny]
    for i, ref in enumerate((xfr_ref, xfi_ref, xbr_ref, xbi_ref)):
        ref[...] = r[:, ny + i * nx:ny + (i + 1) * nx]


def _s5_chunk(ug, w1):
    gp, m, tk2 = ug.shape
    nx = (w1.shape[2] - tk2) // 4
    xspec = pl.BlockSpec((m, nx), lambda g: (0, g))
    xshape = jax.ShapeDtypeStruct((m, gp * nx), F32)
    return pl.pallas_call(
        _s5_chunk_kernel,
        grid=(gp,),
        in_specs=[pl.BlockSpec((None, m, tk2), lambda g: (g, 0, 0)),
                  pl.BlockSpec((None, tk2, w1.shape[2]), lambda g: (g, 0, 0))],
        out_specs=[pl.BlockSpec((None, m, tk2), lambda g: (g, 0, 0)), xspec, xspec, xspec, xspec],
        out_shape=[jax.ShapeDtypeStruct((gp, m, tk2), F32), xshape, xshape, xshape, xshape],
        compiler_params=_cparams(("parallel",)),
        name="s5_chunk",
    )(ug, w1)


def _s5_scan_kernel(xfr, xfi, xbr, xbi, afr, afi, abr, abi, hfr, hfi, hbr, hbi, *, tiles_ctx, tiles, bsz):
    shape = (2 * bsz, xfr.shape[1])
    low = lax.broadcasted_iota(jnp.int32, shape, 0) < bsz
    zero = jnp.zeros(shape, F32)

    def step(hr, hi, ar, ai, xr, xi):
        return ar * hr - ai * hi + xr, ar * hi + ai * hr + xi

    def fwd_tile(i, carry):
        ar, ai = jnp.broadcast_to(afr[...], shape), jnp.broadcast_to(afi[...], shape)
        r0 = pl.multiple_of(i * 2 * bsz, 2 * bsz)
        xr, xi = xfr[pl.ds(r0, 2 * bsz), :], xfi[pl.ds(r0, 2 * bsz), :]
        sr, si = pltpu.roll(carry[0], bsz, 0), pltpu.roll(carry[1], bsz, 0)
        tr, ti = step(sr, si, ar, ai, xr, xi)
        t2r, t2i = pltpu.roll(tr, bsz, 0), pltpu.roll(ti, bsz, 0)
        hfr[pl.ds(r0, 2 * bsz), :] = jnp.where(low, sr, t2r)
        hfi[pl.ds(r0, 2 * bsz), :] = jnp.where(low, si, t2i)
        return step(t2r, t2i, ar, ai, xr, xi)

    lax.fori_loop(0, tiles, fwd_tile, (zero, zero))

    def bwd_tile(i, carry):
        ar, ai = jnp.broadcast_to(abr[...], shape), jnp.broadcast_to(abi[...], shape)
        r0 = pl.multiple_of(i * 2 * bsz, 2 * bsz)
        xr, xi = xbr[pl.ds(r0, 2 * bsz), :], xbi[pl.ds(r0, 2 * bsz), :]
        sr, si = pltpu.roll(carry[0], bsz, 0), pltpu.roll(carry[1], bsz, 0)
        tr, ti = step(sr, si, ar, ai, xr, xi)
        t2r, t2i = pltpu.roll(tr, bsz, 0), pltpu.roll(ti, bsz, 0)
        hbr[pl.ds(r0, 2 * bsz), :] = jnp.where(low, t2r, sr)
        hbi[pl.ds(r0, 2 * bsz), :] = jnp.where(low, t2i, si)
        return step(t2r, t2i, ar, ai, xr, xi)

    carry = lax.fori_loop(0, tiles_ctx, lambda j, c: bwd_tile(tiles_ctx - 1 - j, c), (zero, zero))
    lax.fori_loop(0, tiles - tiles_ctx, lambda j, c: bwd_tile(tiles - 1 - j, c), carry)


def _s5_scan(xs, coef, chunks_ctx, chunks, bsz):
    m, lanes = xs[0].shape
    assert 2 * bsz == 8 and chunks % 2 == 0 and chunks_ctx % 2 == 0
    lb = min(256, lanes)
    xspec = pl.BlockSpec((m, lb), lambda j: (0, j))
    cspec = pl.BlockSpec((1, lb), lambda j: (0, j))
    shape = jax.ShapeDtypeStruct((m, lanes), F32)
    return pl.pallas_call(
        functools.partial(_s5_scan_kernel, tiles_ctx=chunks_ctx // 2, tiles=chunks // 2, bsz=bsz),
        grid=(lanes // lb,),
        in_specs=[xspec] * 4 + [cspec] * 4,
        out_specs=[xspec] * 4,
        out_shape=[shape] * 4,
        compiler_params=_cparams(("parallel",)),
        name="s5_scan",
    )(*xs, *coef)


def _s5_readout_kernel(y_ref, hfr, hfi, hbr, hbi, r_ref, o_ref):
    h = jnp.concatenate([hfr[...], hfi[...], hbr[...], hbi[...]], axis=1).astype(BF16)
    o_ref[...] = (y_ref[...] + jnp.dot(h, r_ref[...], preferred_element_type=F32)).astype(o_ref.dtype)


def _s5_readout(y0, hs, r2):
    gp, m, tk2 = y0.shape
    nx = hs[0].shape[1] // gp
    hspec = pl.BlockSpec((m, nx), lambda g: (0, g))
    yspec = pl.BlockSpec((None, m, tk2), lambda g: (g, 0, 0))
    return pl.pallas_call(
        _s5_readout_kernel,
        grid=(gp,),
        in_specs=[yspec, hspec, hspec, hspec, hspec, pl.BlockSpec((None, 4 * nx, tk2), lambda g: (g, 0, 0))],
        out_specs=yspec,
        out_shape=jax.ShapeDtypeStruct((gp, m, tk2), BF16),
        compiler_params=_cparams(("parallel",)),
        name="s5_readout",
    )(y0, *hs, r2)


def _s5_branch(u_ctx, u_lat, p, tables, need_ctx):
    w1, r2, coef = tables
    b, l, w = u_lat.shape
    lc = u_ctx.shape[1]
    t, k = S5_CHUNK, SSM_GROUP
    g = w // k
    nc = (lc + l) // t
    m = nc * b
    ug = _to_groups(jnp.concatenate([u_ctx, u_lat], axis=1))
    y0, *xs = _s5_chunk(ug, w1)
    hs = _s5_scan(xs, coef, lc // t, nc, b)
    y = _from_groups(_s5_readout(y0, hs, r2), b)
    w_glu = p['ssm_w_glu'].astype(BF16)
    out_l = _s5_out(y, lc, u_lat, p['ssm_d'], w_glu)
    out_c = _s5_out(y, 0, u_ctx, p['ssm_d'], w_glu) if need_ctx else None
    return out_l, out_c


def _na_kernel(q_ref, k_ref, v_ref, kc_ref, vc_ref, bias_ref, o_ref, s_scr, p_scr, *, rows, head_dim, rps):
    kr = NA_WIN_ROWS
    kc = kc_ref[...]
    vc = vc_ref[...]
    nt = (((1,), (1,)), ((), ()))
    lane_tile = 2 * head_dim
    nwin = kr * GRID_W
    first_head = lax.broadcasted_iota(jnp.int32, (GRID_W, lane_tile), 1) < head_dim
    vwins = []
    for sub in range(rps):
        r = pl.program_id(1) * rps + sub
        start = jnp.clip(r - kr // 2, 0, rows - kr)
        cls = r - start
        base = pl.multiple_of(start * GRID_W, GRID_W)
        kwin = k_ref[pl.ds(base, nwin), :]
        vwins.append(v_ref[pl.ds(base, nwin), :])
        q = q_ref[sub * GRID_W:(sub + 1) * GRID_W, :]
        for h in range(NA_HEADS):
            sl = slice(h // 2 * lane_tile, (h // 2 + 1) * lane_tile)
            qh = jnp.where(first_head == (h % 2 == 0), q[:, sl], jnp.zeros_like(q[:, sl]))
            i = sub * NA_HEADS + h
            s_scr[i, :, :nwin] = lax.dot_general(qh, kwin[:, sl], nt, preferred_element_type=F32) + bias_ref[cls, h]
            s_scr[i, :, nwin:] = lax.dot_general(qh, kc[:, sl], nt, preferred_element_type=F32)
    inv = []
    for i in range(rps * NA_HEADS):
        s = s_scr[i]
        pr = jnp.exp(s - jnp.max(s, axis=-1, keepdims=True))
        inv.append(1.0 / jnp.sum(pr, axis=-1, keepdims=True))
        p_scr[i] = pr.astype(BF16)
    for sub in range(rps):
        outs = []
        for pair in range(NA_HEADS // 2):
            sl = slice(pair * lane_tile, (pair + 1) * lane_tile)
            halves = []
            for half in range(2):
                i = sub * NA_HEADS + 2 * pair + half
                o = (jnp.dot(p_scr[i, :, :nwin], vwins[sub][:, sl], preferred_element_type=F32)
                     + jnp.dot(p_scr[i, :, nwin:], vc[:, sl], preferred_element_type=F32))
                halves.append(o * inv[i])
            outs.append(jnp.where(first_head, halves[0], halves[1]))
        o_ref[sub * GRID_W:(sub + 1) * GRID_W, :] = jnp.concatenate(outs, axis=-1).astype(o_ref.dtype)


def _na_bias_table(rpb):
    kr, kw, w = NA_WIN_ROWS, NA_WIN_COLS, GRID_W
    col = np.arange(w)
    cstart = np.clip(col - kw // 2, 0, w - kw)
    kcol = np.arange(w)
    inwin = (kcol[None, :] >= cstart[:, None]) & (kcol[None, :] < cstart[:, None] + kw)
    dc = np.clip(kcol[None, :] - col[:, None] + (kw - 1), 0, 2 * kw - 2)
    rpb = rpb.astype(F32)
    dr = np.arange(kr)[None, :] - np.arange(kr)[:, None] + (kr - 1)
    row_sel = jnp.asarray(dr[:, :, None] == np.arange(2 * kr - 1), F32)
    col_sel = jnp.asarray((dc[:, :, None] == np.arange(2 * kw - 1)) & inwin[:, :, None], F32)
    hp = lax.Precision.HIGHEST
    rows = jnp.einsum('hab,oja->ohjb', rpb, row_sel, precision=hp)
    bias = jnp.einsum('ohjb,cdb->ohcjd', rows, col_sel, precision=hp)
    bias = jnp.where(jnp.asarray(inwin)[None, None, :, None, :], bias, MASK_VALUE)
    return bias.reshape(kr, rpb.shape[0], w, kr * w)


def _na_latent(q, k, v, kc, vc, bias_tab):
    b, l, w = q.shape
    lc = kc.shape[1]
    rows = l // GRID_W
    assert rows >= NA_WIN_ROWS
    nk = NA_WIN_ROWS * GRID_W
    rps = 2
    assert rows % rps == 0
    return pl.pallas_call(
        functools.partial(_na_kernel, rows=rows, head_dim=w // NA_HEADS, rps=rps),
        grid=(b, rows // rps),
        in_specs=[pl.BlockSpec((None, rps * GRID_W, w), lambda bi, r: (bi, r, 0)),
                  pl.BlockSpec((None, l, w), lambda bi, r: (bi, 0, 0)),
                  pl.BlockSpec((None, l, w), lambda bi, r: (bi, 0, 0)),
                  pl.BlockSpec((None, lc, w), lambda bi, r: (bi, 0, 0)),
                  pl.BlockSpec((None, lc, w), lambda bi, r: (bi, 0, 0)),
                  pl.BlockSpec((NA_WIN_ROWS, NA_HEADS, GRID_W, nk), lambda bi, r: (0, 0, 0, 0))],
        out_specs=pl.BlockSpec((None, rps * GRID_W, w), lambda bi, r: (bi, r, 0)),
        out_shape=jax.ShapeDtypeStruct((b, l, w), BF16),
        scratch_shapes=[pltpu.VMEM((rps * NA_HEADS, GRID_W, nk + lc), F32),
                        pltpu.VMEM((rps * NA_HEADS, GRID_W, nk + lc), BF16)],
        compiler_params=_cparams(("parallel", "arbitrary"), vmem_mb=56),
        name="na_latent",
    )(q, k, v, kc, vc, bias_tab)


def _na_ctx_kernel(q_ref, k_ref, v_ref, o_ref, *, head_dim):
    q, k, v = q_ref[...], k_ref[...], v_ref[...]
    outs = []
    for h in range(NA_HEADS):
        sl = slice(h * head_dim, (h + 1) * head_dim)
        s = lax.dot_general(q[:, sl], k[:, sl], (((1,), (1,)), ((), ())), preferred_element_type=F32)
        pr = jnp.exp(s - jnp.max(s, axis=-1, keepdims=True))
        den = jnp.sum(pr, axis=-1, keepdims=True)
        outs.append(jnp.dot(pr.astype(BF16), v[:, sl], preferred_element_type=F32) / den)
    o_ref[...] = jnp.concatenate(outs, axis=-1).astype(o_ref.dtype)


def _na_context(q, k, v):
    b, lc, w = q.shape
    spec = pl.BlockSpec((None, lc, w), lambda bi: (bi, 0, 0))
    return pl.pallas_call(
        functools.partial(_na_ctx_kernel, head_dim=w // NA_HEADS),
        grid=(b,),
        in_specs=[spec, spec, spec],
        out_specs=spec,
        out_shape=jax.ShapeDtypeStruct((b, lc, w), BF16),
        compiler_params=_cparams(("parallel",)),
        name="na_context",
    )(q, k, v)


def _hyena_filter(length, p, slow=None):
    n = 2 * length
    pos = np.arange(n)
    if slow is not None:
        pos = pos.reshape(slow, n // slow).T.reshape(-1)
    off = np.where(pos < length, pos, n - pos)
    backward = jnp.asarray(pos > length)[:, None, None]
    live = jnp.asarray(pos != length, F32)[:, None, None]
    tau = jnp.asarray(np.minimum(off, length - 1), F32)[:, None]
    t = tau / (length - 1)
    freqs = jnp.linspace(1e-4, HY_BANDS - 1, HY_BANDS, dtype=F32)
    ang = (2.0 * math.pi / length) * tau * freqs[None, :]
    z = jnp.concatenate([t, jnp.cos(ang), -jnp.sin(ang)], axis=-1)
    fr = p['hy_freq'].astype(F32)
    hp = lax.Precision.HIGH
    h = jnp.sin(fr * (jnp.dot(z, p['hy_w1'].astype(F32), precision=hp) + p['hy_b1'].astype(F32)))
    h = jnp.sin(fr * (jnp.dot(h, p['hy_w2'].astype(F32), precision=hp) + p['hy_b2'].astype(F32)))
    h = jnp.sin(fr * (jnp.dot(h, p['hy_w3'].astype(F32), precision=hp) + p['hy_b3'].astype(F32)))
    wdim = p['hy_bias'].shape[-1]
    h = jnp.dot(h, p['hy_w4'].astype(F32), precision=hp).reshape(n, HY_ORDER, 2, wdim)
    decay = jnp.exp(-t * jnp.abs(jnp.linspace(HY_MIN_DECAY, HY_MAX_DECAY, wdim, dtype=F32)))
    filt = jnp.where(backward, h[:, :, 1], h[:, :, 0]) * (decay[:, None, :] * live)
    filt = filt / jnp.sum(jnp.abs(filt), axis=0, keepdims=True)
    return filt.reshape(n, HY_ORDER * wdim)


def _block2(re, im):
    return np.block([[re, -im], [im, re]])


def _fft_consts(length):
    n = 2 * length
    n2 = HY_FAST
    n1 = n // n2
    h1 = n1 // 2
    a1 = -2.0 * np.pi * np.outer(np.arange(n1), np.arange(n1)) / n1
    a2 = -2.0 * np.pi * np.outer(np.arange(n2), np.arange(n2)) / n2
    f1 = _block2(np.cos(a1[:, :h1]), np.sin(a1[:, :h1]))
    f1_real = np.concatenate([np.cos(a1), np.sin(a1)], axis=0)
    f3 = _block2(np.cos(a2), np.sin(a2))
    g3 = _block2(np.cos(a2), -np.sin(a2))
    g1 = _block2(np.cos(a1[:h1, :]) / n, -np.sin(a1[:h1, :]) / n)
    wa = -2.0 * np.pi * np.arange(n2) / n
    wr = np.broadcast_to(np.cos(wa)[:, None], (n2, 128))
    wi = np.broadcast_to(np.sin(wa)[:, None], (n2, 128))
    cast = lambda m: jnp.asarray(m, BF16)
    return dict(n1=n1, h1=h1, f1=cast(f1), f1_real=cast(f1_real), f3=cast(f3), g3=cast(g3), g1=cast(g1),
                wr=jnp.asarray(wr, F32), wi=jnp.asarray(wi, F32))


def _fft_first_stage(a_r, a_i, load_cols, f1_ref, n1, cb):
    slab = HY_FAST + HY_PAD

    def body(i, carry):
        n2 = 2 * i
        z = jnp.concatenate([load_cols(n2), load_cols(n2 + 1)], axis=1)
        r = jnp.dot(f1_ref[...], z, preferred_element_type=F32)
        a_r[pl.ds(n2, n1, stride=slab), :] = r[:n1, :cb]
        a_i[pl.ds(n2, n1, stride=slab), :] = r[n1:, :cb]
        a_r[pl.ds(n2 + 1, n1, stride=slab), :] = r[:n1, cb:]
        a_i[pl.ds(n2 + 1, n1, stride=slab), :] = r[n1:, cb:]
        return carry

    lax.fori_loop(0, HY_FAST // 2, body, 0, unroll=4)


def _fft_slab_loop(a_r, a_i, f3_ref, wr_ref, wi_ref, n1, cb, finish):
    slab = HY_FAST + HY_PAD
    n2 = HY_FAST
    wr, wi = wr_ref[...], wi_ref[...]
    if cb != wr.shape[1]:
        wr, wi = jnp.tile(wr, (1, cb // wr.shape[1])), jnp.tile(wi, (1, cb // wi.shape[1]))

    def body(i, carry):
        tr, ti = carry
        k1 = 2 * i
        r0 = pl.multiple_of(k1 * slab, 8)
        r1 = pl.multiple_of((k1 + 1) * slab, 8)
        tr1, ti1 = tr * wr - ti * wi, tr * wi + ti * wr
        ar0, ai0 = a_r[pl.ds(r0, n2), :], a_i[pl.ds(r0, n2), :]
        ar1, ai1 = a_r[pl.ds(r1, n2), :], a_i[pl.ds(r1, n2), :]
        xr = jnp.concatenate([ar0 * tr - ai0 * ti, ar1 * tr1 - ai1 * ti1], axis=1)
        xi = jnp.concatenate([ar0 * ti + ai0 * tr, ar1 * ti1 + ai1 * tr1], axis=1)
        bc = jnp.dot(f3_ref[...], jnp.concatenate([xr, xi], axis=0).astype(BF16), preferred_element_type=F32)
        finish(k1, (r0, r1), (tr, ti, tr1, ti1), bc[:n2], bc[n2:])
        return tr1 * wr - ti1 * wi, tr1 * wi + ti1 * wr

    one = jnp.ones((n2, cb), F32)
    lax.fori_loop(0, n1 // 2, body, (one, jnp.zeros_like(one)), unroll=2)


def _fft_conv_kernel(y_ref, sr_ref, si_ref, f1_ref, f3_ref, g3_ref, g1_ref, wr_ref, wi_ref, o_ref, a_r, a_i,
                     *, n1):
    h1 = n1 // 2
    n2 = HY_FAST
    cb = o_ref.shape[-1]
    slab = HY_FAST + HY_PAD

    def load_cols(j):
        r = pl.multiple_of(j * h1, h1)
        return jnp.concatenate([y_ref[0, pl.ds(r, h1), :], y_ref[1, pl.ds(r, h1), :]], axis=0)

    _fft_first_stage(a_r, a_i, load_cols, f1_ref, n1, cb)

    def finish(k1, rows, tw, br, bi):
        s0 = pl.multiple_of(k1 * n2, n2)
        s1 = pl.multiple_of((k1 + 1) * n2, n2)
        sr = jnp.concatenate([sr_ref[pl.ds(s0, n2), :], sr_ref[pl.ds(s1, n2), :]], axis=1).astype(F32)
        si = jnp.concatenate([si_ref[pl.ds(s0, n2), :], si_ref[pl.ds(s1, n2), :]], axis=1).astype(F32)
        y = jnp.concatenate([br * sr - bi * si, br * si + bi * sr], axis=0).astype(BF16)
        ac = jnp.dot(g3_ref[...], y, preferred_element_type=F32)
        pr, pi = ac[:n2], ac[n2:]
        for j in range(2):
            tr, ti = tw[2 * j], tw[2 * j + 1]
            prj, pij = pr[:, j * cb:(j + 1) * cb], pi[:, j * cb:(j + 1) * cb]
            a_r[pl.ds(rows[j], n2), :] = prj * tr + pij * ti
            a_i[pl.ds(rows[j], n2), :] = pij * tr - prj * ti

    _fft_slab_loop(a_r, a_i, f3_ref, wr_ref, wi_ref, n1, cb, finish)

    def out_body(i, carry):
        j0 = 2 * i
        cols = []
        for j in (j0, j0 + 1):
            cols.append(jnp.concatenate([a_r[pl.ds(j, n1, stride=slab), :], a_i[pl.ds(j, n1, stride=slab), :]],
                                        axis=0))
        o = jnp.dot(g1_ref[...], jnp.concatenate(cols, axis=1).astype(BF16), preferred_element_type=F32)
        for jj in range(2):
            r = pl.multiple_of((j0 + jj) * h1, h1)
            o_ref[0, pl.ds(r, h1), :] = o[:h1, jj * cb:(jj + 1) * cb].astype(o_ref.dtype)
            o_ref[1, pl.ds(r, h1), :] = o[h1:, jj * cb:(jj + 1) * cb].astype(o_ref.dtype)
        return carry

    lax.fori_loop(0, n2 // 2, out_body, 0, unroll=4)


def _fft_conv(yp, col0, spec_r, spec_i, scol0, consts, cb=128):
    b, l, _ = yp.shape
    n = spec_r.shape[0]
    n1 = consts['n1']
    slab = HY_FAST + HY_PAD
    nblk = consts['nblk']
    y4 = yp.reshape(b // 2, 2, l, yp.shape[-1])
    const = lambda a: pl.BlockSpec(a.shape, lambda j, pi: (0, 0))
    return pl.pallas_call(
        functools.partial(_fft_conv_kernel, n1=n1),
        grid=(nblk, b // 2),
        in_specs=[pl.BlockSpec((None, 2, l, cb), lambda j, pi: (pi, 0, 0, col0 + j)),
                  pl.BlockSpec((n, cb), lambda j, pi: (0, scol0 + j)),
                  pl.BlockSpec((n, cb), lambda j, pi: (0, scol0 + j)),
                  const(consts['f1']), const(consts['f3']), const(consts['g3']), const(consts['g1']),
                  const(consts['wr']), const(consts['wi'])],
        out_specs=pl.BlockSpec((None, 2, l, cb), lambda j, pi: (pi, 0, 0, j)),
        out_shape=jax.ShapeDtypeStruct((b // 2, 2, l, nblk * cb), BF16),
        scratch_shapes=[pltpu.VMEM((n1 * slab, cb), F32), pltpu.VMEM((n1 * slab, cb), F32)],
        compiler_params=_cparams(("arbitrary", "arbitrary"), vmem_mb=58),
        name="fft_conv",
    )(y4, spec_r, spec_i, consts['f1'], consts['f3'], consts['g3'], consts['g1'],
      consts['wr'], consts['wi']).reshape(b, l, nblk * cb)


def _fft_spec_kernel(f_ref, f1_ref, f3_ref, wr_ref, wi_ref, sr_ref, si_ref, a_r, a_i, *, n1):
    n2 = HY_FAST
    cb = sr_ref.shape[-1]

    def load_cols(j):
        return f_ref[pl.ds(pl.multiple_of(j * n1, n1), n1), :]

    _fft_first_stage(a_r, a_i, load_cols, f1_ref, n1, cb)

    def finish(k1, rows, tw, br, bi):
        for j in range(2):
            s = pl.multiple_of((k1 + j) * n2, n2)
            sr_ref[pl.ds(s, n2), :] = br[:, j * cb:(j + 1) * cb].astype(sr_ref.dtype)
            si_ref[pl.ds(s, n2), :] = bi[:, j * cb:(j + 1) * cb].astype(si_ref.dtype)

    _fft_slab_loop(a_r, a_i, f3_ref, wr_ref, wi_ref, n1, cb, finish)


def _fft_spectrum(filt_p, consts, cb=128):
    n, c = filt_p.shape
    n1 = consts['n1']
    slab = HY_FAST + HY_PAD
    const = lambda a: pl.BlockSpec(a.shape, lambda j: (0, 0))
    spec = pl.BlockSpec((n, cb), lambda j: (0, j))
    return pl.pallas_call(
        functools.partial(_fft_spec_kernel, n1=n1),
        grid=(c // cb,),
        in_specs=[spec, const(consts['f1_real']), const(consts['f3']), const(consts['wr']), const(consts['wi'])],
        out_specs=[spec, spec],
        out_shape=[jax.ShapeDtypeStruct((n, c), BF16)] * 2,
        scratch_shapes=[pltpu.VMEM((n1 * slab, cb), F32), pltpu.VMEM((n1 * slab, cb), F32)],
        compiler_params=_cparams(("arbitrary",), vmem_mb=58),
        name="fft_spectrum",
    )(filt_p, consts['f1_real'], consts['f3'], consts['wr'], consts['wi'])


def _dense_conv_kernel(y_ref, sr_ref, si_ref, f_ref, g_ref, o_ref):
    l = y_ref.shape[1]
    n = sr_ref.shape[0]
    bc = jnp.dot(f_ref[...], jnp.concatenate([y_ref[0], y_ref[1]], axis=0), preferred_element_type=F32)
    br, bi = bc[:n], bc[n:]
    sr, si = sr_ref[...], si_ref[...]
    y = jnp.concatenate([br * sr - bi * si, br * si + bi * sr], axis=0).astype(BF16)
    o = jnp.dot(g_ref[...], y, preferred_element_type=F32)
    o_ref[0] = o[:l].astype(o_ref.dtype)
    o_ref[1] = o[l:].astype(o_ref.dtype)


def _dense_conv(y, col0, spec_r, spec_i, scol0, fmat, gmat, nblk, cb=128):
    b, l, _ = y.shape
    n = spec_r.shape[0]
    y4 = y.reshape(b // 2, 2, l, y.shape[-1])
    const = lambda a: pl.BlockSpec(a.shape, lambda j, pi: (0, 0))
    return pl.pallas_call(
        _dense_conv_kernel,
        grid=(nblk, b // 2),
        in_specs=[pl.BlockSpec((None, 2, l, cb), lambda j, pi: (pi, 0, 0, col0 + j)),
                  pl.BlockSpec((n, cb), lambda j, pi: (0, scol0 + j)),
                  pl.BlockSpec((n, cb), lambda j, pi: (0, scol0 + j)),
                  const(fmat), const(gmat)],
        out_specs=pl.BlockSpec((None, 2, l, cb), lambda j, pi: (pi, 0, 0, j)),
        out_shape=jax.ShapeDtypeStruct((b // 2, 2, l, nblk * cb), BF16),
        compiler_params=_cparams(("parallel", "parallel")),
        name="dense_conv",
    )(y4, spec_r, spec_i, fmat, gmat).reshape(b, l, nblk * cb)


def _sconv_kernel(u_ref, w_ref, b_ref, o_ref, *scr, h1):
    l = u_ref.shape[0]
    if h1 is None:
        u = u_ref[...].astype(F32)
        row = lax.broadcasted_iota(jnp.int32, u.shape, 0)
        prev = jnp.where(row == 0, 0.0, pltpu.roll(u, 1, 0))
        nxt = jnp.where(row == l - 1, 0.0, pltpu.roll(u, l - 1, 0))
        z = prev * w_ref[0:1, :] + u * w_ref[1:2, :] + nxt * w_ref[2:3, :] + b_ref[...]
        o_ref[...] = z.astype(o_ref.dtype)
        return
    buf = scr[0]
    zeros = jnp.zeros((HY_FAST, buf.shape[1]), F32)
    buf[0:HY_FAST, :] = zeros
    buf[HY_FAST + l:HY_FAST + l + 8, :] = zeros[:8]
    buf[HY_FAST:HY_FAST + l, :] = u_ref[...].astype(F32)
    w0, w1, w2, bias = w_ref[0:1, :], w_ref[1:2, :], w_ref[2:3, :], b_ref[...]

    def body(n2, carry):
        taps = [buf[pl.ds(HY_FAST - 1 + n2 + k, h1, stride=HY_FAST), :] for k in range(3)]
        z = taps[0] * w0 + taps[1] * w1 + taps[2] * w2 + bias
        o_ref[pl.ds(pl.multiple_of(n2 * h1, h1), h1), :] = z.astype(o_ref.dtype)
        return carry

    lax.fori_loop(0, HY_FAST, body, 0, unroll=4)


def _short_conv(proj, col0, conv_w, conv_b, h1, cb=128):
    b, l, _ = proj.shape
    c = conv_w.shape[1]
    return pl.pallas_call(
        functools.partial(_sconv_kernel, h1=h1),
        scratch_shapes=[] if h1 is None else [pltpu.VMEM((l + HY_FAST + 8, cb), F32)],
        grid=(b, c // cb),
        in_specs=[pl.BlockSpec((None, l, cb), lambda bi, j: (bi, 0, col0 + j)),
                  pl.BlockSpec((conv_w.shape[0], cb), lambda bi, j: (0, j)),
                  pl.BlockSpec((1, cb), lambda bi, j: (0, j))],
        out_specs=pl.BlockSpec((None, l, cb), lambda bi, j: (bi, 0, j)),
        out_shape=jax.ShapeDtypeStruct((b, l, c), BF16),
        compiler_params=_cparams(("parallel", "parallel")),
        name="short_conv",
    )(proj, conv_w.astype(F32), conv_b.astype(F32).reshape(1, c))


def _gate_kernel(x_ref, conv_ref, y_ref, bias_ref, o_ref, *scr, h1):
    y = y_ref[...].astype(F32)
    z = x_ref[...].astype(F32) * (conv_ref[...].astype(F32) + y * bias_ref[...])
    if h1 is None:
        o_ref[...] = z.astype(o_ref.dtype)
        return
    scr[0][...] = z

    def body(n1, carry):
        r = pl.multiple_of(n1 * HY_FAST, HY_FAST)
        o_ref[pl.ds(r, HY_FAST), :] = scr[0][pl.ds(n1, HY_FAST, stride=h1), :].astype(o_ref.dtype)
        return carry

    lax.fori_loop(0, h1, body, 0)


def _hy_gate(z, xcol, conv, y, ycol, bias, h1=None, cb=128):
    b, l, w = conv.shape
    return pl.pallas_call(
        functools.partial(_gate_kernel, h1=h1),
        grid=(b, w // cb),
        in_specs=[pl.BlockSpec((None, l, cb), lambda bi, j: (bi, 0, xcol + j)),
                  pl.BlockSpec((None, l, cb), lambda bi, j: (bi, 0, j)),
                  pl.BlockSpec((None, l, cb), lambda bi, j: (bi, 0, ycol + j)),
                  pl.BlockSpec((1, cb), lambda bi, j: (0, j))],
        out_specs=pl.BlockSpec((None, l, cb), lambda bi, j: (bi, 0, j)),
        out_shape=jax.ShapeDtypeStruct((b, l, w), BF16),
        scratch_shapes=[] if h1 is None else [pltpu.VMEM((l, cb), F32)],
        compiler_params=_cparams(("parallel", "parallel")),
        name="hy_gate",
    )(z, conv, y, bias.astype(F32).reshape(1, w))


def _hyena_branch(proj, col0, p, cb=128):
    b, l, _ = proj.shape
    w = p['hy_bias'].shape[-1]
    n = 2 * l
    nblk = w // cb
    if n // HY_FAST >= 16:
        consts = dict(_fft_consts(l), nblk=nblk)
        n1, h1 = consts['n1'], consts['h1']
        spec_r, spec_i = _fft_spectrum(_hyena_filter(l, p, slow=n1).astype(BF16), consts, cb)
        conv = lambda y, ycol, order: _fft_conv(y, ycol * nblk, spec_r, spec_i, order * nblk, consts, cb)
    else:
        h1 = None
        k = np.arange(n)
        ang = -2.0 * np.pi * np.outer(k, k) / n
        fmat = jnp.asarray(_block2(np.cos(ang[:, :l]), np.sin(ang[:, :l])), BF16)
        gmat = jnp.asarray(_block2(np.cos(ang[:l, :]) / n, -np.sin(ang[:l, :]) / n), BF16)
        f_real = jnp.asarray(np.concatenate([np.cos(ang), np.sin(ang)], axis=0), F32)
        spec = _bmm(f_real[None], _hyena_filter(l, p).astype(BF16)[None], name="dense_spectrum")[0]
        spec_r, spec_i = spec[:n], spec[n:]
        conv = lambda y, ycol, order: _dense_conv(y, ycol * nblk, spec_r, spec_i, order * nblk, fmat, gmat,
                                                  nblk, cb)
    z = _short_conv(proj, col0, p['hy_conv_w'], p['hy_conv_b'], h1, cb)
    c0 = conv(z, HY_ORDER, 0)
    y1 = _hy_gate(z, 0, c0, z, HY_ORDER * nblk, p['hy_bias'][0], None, cb)
    c1 = conv(y1, 0, 1)
    return _hy_gate(z, nblk, c1, y1, 0, p['hy_bias'][1], h1, cb)


def _merge_kernel(x_ref, g_ref, sh_ref, sc_ref, g1_ref, br_ref, wg0_ref, wg1_ref, wg2_ref, bg_ref, wb_ref, wo_ref,
                  o_ref, h_scr, acc_scr):
    j = pl.program_id(2)

    @pl.when(j == 0)
    def _():
        h_scr[...] = _norm_mod(x_ref[...], g_ref[...], sh_ref[...], sc_ref[...]).astype(BF16)
        acc_scr[...] = jnp.zeros_like(acc_scr)

    h = h_scr[...]
    mixed = None
    for i, wg_ref in enumerate((wg0_ref, wg1_ref, wg2_ref)):
        gate = jax.nn.sigmoid(jnp.dot(h, wg_ref[...], preferred_element_type=F32) + bg_ref[i])
        term = gate * jnp.dot(br_ref[i], wb_ref[i], preferred_element_type=F32)
        mixed = term if mixed is None else mixed + term
    acc_scr[...] += jnp.dot(mixed.astype(BF16), wo_ref[...], preferred_element_type=F32)

    @pl.when(j == pl.num_programs(2) - 1)
    def _():
        o_ref[...] = x_ref[...] + g1_ref[...] * acc_scr[...]


def _merge(x, g, sh, sc, g1, branches, w_gate_bf, b_gate, w_branch_bf, w_out_bf, tm, tc):
    b, l, d = x.shape
    nbr, w, _ = w_branch_bf.shape
    nb = sh.shape[0]
    nj = d // tc
    mod_map = (lambda bi, i, j: (bi, 0, 0)) if nb > 1 else (lambda bi, i, j: (0, 0, 0))
    br = jnp.stack(branches, axis=1)
    bg = b_gate.reshape(nbr, nj, 1, tc).transpose(1, 0, 2, 3)
    return pl.pallas_call(
        _merge_kernel,
        grid=(b, l // tm, nj),
        in_specs=[pl.BlockSpec((None, tm, d), lambda bi, i, j: (bi, i, 0)),
                  pl.BlockSpec((1, d), lambda bi, i, j: (0, 0)),
                  pl.BlockSpec((None, 1, d), mod_map),
                  pl.BlockSpec((None, 1, d), mod_map),
                  pl.BlockSpec((None, 1, d), mod_map),
                  pl.BlockSpec((None, nbr, tm, w), lambda bi, i, j: (bi, 0, i, 0)),
                  pl.BlockSpec((d, tc), lambda bi, i, j: (0, j)),
                  pl.BlockSpec((d, tc), lambda bi, i, j: (0, nj + j)),
                  pl.BlockSpec((d, tc), lambda bi, i, j: (0, 2 * nj + j)),
                  pl.BlockSpec((None, nbr, 1, tc), lambda bi, i, j: (j, 0, 0, 0)),
                  pl.BlockSpec((nbr, w, tc), lambda bi, i, j: (0, 0, j)),
                  pl.BlockSpec((tc, d), lambda bi, i, j: (j, 0))],
        out_specs=pl.BlockSpec((None, tm, d), lambda bi, i, j: (bi, i, 0)),
        out_shape=jax.ShapeDtypeStruct((b, l, d), F32),
        scratch_shapes=[pltpu.VMEM((tm, d), BF16), pltpu.VMEM((tm, d), F32)],
        compiler_params=_cparams(("parallel", "parallel", "arbitrary"), vmem_mb=56),
        name="merge",
    )(x, g.reshape(1, d), sh, sc, g1, br, w_gate_bf, w_gate_bf, w_gate_bf, bg, w_branch_bf, w_out_bf)


def _router_kernel(x_ref, g_ref, sh_ref, sc_ref, rt_ref, h_ref, aff_ref):
    h = _norm_mod(x_ref[...], g_ref[...], sh_ref[...], sc_ref[...])
    h_ref[...] = h.astype(h_ref.dtype)
    nt = (((1,), (1,)), ((), ()))
    rt = rt_ref[...]
    r_hi = rt.astype(BF16)
    r_lo = (rt - r_hi.astype(F32)).astype(BF16)
    h_hi = h.astype(BF16)
    h_lo = (h - h_hi.astype(F32)).astype(BF16)
    logits = (lax.dot_general(r_hi, h_hi, nt, preferred_element_type=F32)
              + lax.dot_general(r_hi, h_lo, nt, preferred_element_type=F32)
              + lax.dot_general(r_lo, h_hi, nt, preferred_element_type=F32))
    e = jnp.exp(logits - jnp.max(logits, axis=0, keepdims=True))
    aff_ref[...] = e / jnp.sum(e, axis=0, keepdims=True)


def _router(x, g, sh, sc, router_t, tm):
    b, l, d = x.shape
    e = router_t.shape[0]
    nb = sh.shape[0]
    mod_map = (lambda bi, i: (bi, 0, 0)) if nb > 1 else (lambda bi, i: (0, 0, 0))
    return pl.pallas_call(
        _router_kernel,
        grid=(b, l // tm),
        in_specs=[pl.BlockSpec((None, tm, d), lambda bi, i: (bi, i, 0)),
                  pl.BlockSpec((1, d), lambda bi, i: (0, 0)),
                  pl.BlockSpec((None, 1, d), mod_map),
                  pl.BlockSpec((None, 1, d), mod_map),
                  pl.BlockSpec((e, d), lambda bi, i: (0, 0))],
        out_specs=[pl.BlockSpec((None, tm, d), lambda bi, i: (bi, i, 0)),
                   pl.BlockSpec((None, e, tm), lambda bi, i: (bi, 0, i))],
        out_shape=[jax.ShapeDtypeStruct((b, l, d), BF16), jax.ShapeDtypeStruct((b, e, l), F32)],
        compiler_params=_cparams(("parallel", "parallel")),
        name="router",
    )(x, g.reshape(1, d), sh, sc, router_t)


def _ffn_kernel(xs_ref, gate_ref, w1_ref, w3_ref, w2_ref, o_ref, acc_scr):
    f = pl.program_id(2)

    @pl.when(f == 0)
    def _():
        acc_scr[...] = jnp.zeros_like(acc_scr)

    xs = xs_ref[...]
    a = jnp.dot(xs, w1_ref[...].astype(BF16), preferred_element_type=F32)
    gl = jnp.dot(xs, w3_ref[...].astype(BF16), preferred_element_type=F32)
    mid = (a * jax.nn.sigmoid(a) * gl).astype(BF16)
    acc_scr[...] += jnp.dot(mid, w2_ref[...].astype(BF16), preferred_element_type=F32)

    @pl.when(f == pl.num_programs(2) - 1)
    def _():
        o_ref[...] = (acc_scr[...] * gate_ref[...]).astype(o_ref.dtype)


def _expert_ffn(xs, gate, w1, w3, w2, layer, tr, tf):
    e, r, d = xs.shape
    ff = w1.shape[3]
    return pl.pallas_call(
        _ffn_kernel,
        grid=(e, r // tr, ff // tf),
        in_specs=[pl.BlockSpec((None, tr, d), lambda ei, i, f: (ei, i, 0)),
                  pl.BlockSpec((None, tr, 1), lambda ei, i, f: (ei, i, 0)),
                  pl.BlockSpec((None, None, d, tf), lambda ei, i, f: (layer, ei, 0, f)),
                  pl.BlockSpec((None, None, d, tf), lambda ei, i, f: (layer, ei, 0, f)),
                  pl.BlockSpec((None, None, tf, d), lambda ei, i, f: (layer, ei, f, 0))],
        out_specs=pl.BlockSpec((None, tr, d), lambda ei, i, f: (ei, i, 0)),
        out_shape=jax.ShapeDtypeStruct((e, r, d), BF16),
        scratch_shapes=[pltpu.VMEM((tr, d), F32)],
        compiler_params=_cparams(("parallel", "parallel", "arbitrary"), vmem_mb=56),
        name="expert_ffn",
    )(xs, gate, w1, w3, w2)


def _route_kernel(aff_ref, pos_ref, idx_ref, csum_scr, *, cap, sblk):
    aff = aff_ref[...]
    e, l = aff.shape
    bits = pltpu.bitcast(aff, jnp.int32)
    count = lambda m: jnp.sum(jnp.where(m, 1.0, 0.0), axis=1, keepdims=True)

    def bisect(i, thr):
        cand = thr | lax.shift_left(jnp.int32(1), 30 - i)
        return jnp.where(count(bits >= cand) >= cap, cand, thr)

    thr = lax.fori_loop(0, 31, bisect, jnp.zeros((e, 1), jnp.int32))
    lane = lax.broadcasted_iota(jnp.int32, (e, l), 1)

    def cumsum(x):
        k = 1
        while k < l:
            x = x + jnp.where(lane >= k, pltpu.roll(x, k, 1), 0.0)
            k *= 2
        return x

    above = bits > thr
    tie = jnp.where(bits == thr, 1.0, 0.0)
    need = cap - count(above)
    keep = jnp.logical_or(above, jnp.logical_and(tie > 0.0, cumsum(tie) - tie < need))
    csum = cumsum(jnp.where(keep, 1.0, 0.0))
    pos_ref[...] = jnp.where(keep, csum - 1.0, -1.0).astype(jnp.int32)
    csum_scr[...] = csum
    expert_lane = lax.broadcasted_iota(jnp.int32, (sblk, e), 1)
    slots = [(lax.broadcasted_iota(jnp.int32, (sblk, 1), 0) + s0).astype(F32) for s0 in range(0, cap, sblk)]

    def per_expert(ei, accs):
        row = csum_scr[pl.ds(ei, 1), :]
        return tuple(jnp.where(expert_lane == ei, count(row <= s), a) for s, a in zip(slots, accs))

    accs = lax.fori_loop(0, e, per_expert, tuple(jnp.zeros((sblk, e), F32) for _ in slots))
    for i, a in enumerate(accs):
        idx_ref[i * sblk:(i + 1) * sblk, :] = a.astype(jnp.int32)


def _route(aff_t, cap):
    b, e, l = aff_t.shape
    return pl.pallas_call(
        functools.partial(_route_kernel, cap=cap, sblk=min(256, cap)),
        grid=(b,),
        in_specs=[pl.BlockSpec((None, e, l), lambda bi: (bi, 0, 0))],
        out_specs=[pl.BlockSpec((None, e, l), lambda bi: (bi, 0, 0)),
                   pl.BlockSpec((None, cap, e), lambda bi: (bi, 0, 0))],
        out_shape=[jax.ShapeDtypeStruct((b, e, l), jnp.int32), jax.ShapeDtypeStruct((b, cap, e), jnp.int32)],
        scratch_shapes=[pltpu.VMEM((e, l), F32)],
        compiler_params=_cparams(("parallel",)),
        name="moe_route",
    )(aff_t)


def _combine_kernel(st_ref, x_ref, g2_ref, pos_ref, *refs, n_exp, tiles, tn):
    y_refs, o_ref, oh_scr = refs[:n_exp], refs[n_exp], refs[n_exp + 1]
    bi, j = pl.program_id(0), pl.program_id(1)
    tt, win = oh_scr.shape[1], y_refs[0].shape[1]
    lane = lax.broadcasted_iota(jnp.int32, (tt, win), 1)
    main = min(win, tt)
    spill = []
    for e in range(n_exp):
        rel = pos_ref[:, e:e + 1] - st_ref[(bi * n_exp + e) * tiles + j]
        oh_scr[e] = jnp.where(lane == rel, 1.0, 0.0).astype(BF16)
        spill.append(jnp.max(rel) >= main)
    for c in range(o_ref.shape[1] // tn):
        cols = slice(c * tn, (c + 1) * tn)
        acc = jnp.dot(oh_scr[0, :, :main], y_refs[0][0, :main, cols], preferred_element_type=F32)
        for e in range(1, n_exp):
            acc = acc + jnp.dot(oh_scr[e, :, :main], y_refs[e][0, :main, cols], preferred_element_type=F32)
        o_ref[:, cols] = x_ref[:, cols] + g2_ref[:, cols] * acc
    if win > main:
        for e in range(n_exp):
            @pl.when(spill[e])
            def _(e=e):
                tail = jnp.dot(oh_scr[e, :, main:], y_refs[e][0, main:, :], preferred_element_type=F32)
                o_ref[...] += g2_ref[...] * tail


def _moe_combine(x, g2, pos, y, cap, tt):
    b, l, d = x.shape
    e = y.shape[0]
    tiles = l // tt
    win = min(tt + COMBINE_ALIGN, cap)
    sel = (pos >= 0).reshape(b, tiles, tt, e).sum(2)
    first = jnp.cumsum(sel, axis=1) - sel
    start = jnp.minimum(first // COMBINE_ALIGN * COMBINE_ALIGN, cap - win).astype(jnp.int32)
    start = jnp.transpose(start, (0, 2, 1)).reshape(-1)
    nb = g2.shape[0]

    def y_spec(ei):
        def y_map(bi, j, st):
            return ei, pl.multiple_of(bi * cap + st[(bi * e + ei) * tiles + j], COMBINE_ALIGN), 0
        return pl.BlockSpec((pl.Element(1), pl.Element(win), pl.Element(d)), y_map)

    grid_spec = pltpu.PrefetchScalarGridSpec(
        num_scalar_prefetch=1,
        grid=(b, tiles),
        in_specs=[pl.BlockSpec((None, tt, d), lambda bi, j, st: (bi, j, 0)),
                  pl.BlockSpec((None, 1, d), lambda bi, j, st: (bi if nb > 1 else 0, 0, 0)),
                  pl.BlockSpec((None, tt, e), lambda bi, j, st: (bi, j, 0))] + [y_spec(ei) for ei in range(e)],
        out_specs=pl.BlockSpec((None, tt, d), lambda bi, j, st: (bi, j, 0)),
        scratch_shapes=[pltpu.VMEM((e, tt, win), BF16)])
    return pl.pallas_call(
        functools.partial(_combine_kernel, n_exp=e, tiles=tiles, tn=min(256, d)),
        grid_spec=grid_spec,
        out_shape=jax.ShapeDtypeStruct((b, l, d), F32),
        compiler_params=_cparams(("parallel", "parallel"), vmem_mb=56),
        name="moe_combine",
    )(start, x, g2, pos, *([y] * e))


def _ec_moe_update(x, g, sh, sc, g2, p, tm):
    b, l, d = x.shape
    e = p['router'].shape[1]
    cap = max(1, EC_CAPACITY * l // e)
    h, aff_t = _router(x, g, sh, sc, jnp.transpose(p['router']).astype(F32), tm)
    pos_t, idx_ce = _route(aff_t, cap)
    pos = jnp.transpose(pos_t, (0, 2, 1))
    idx_t = jnp.transpose(idx_ce, (2, 0, 1))
    flat = (idx_t + (jnp.arange(b, dtype=jnp.int32) * l)[None, :, None]).reshape(-1)
    xs = jnp.take(h.reshape(b * l, d), flat, axis=0, mode="clip").reshape(e, b * cap, d)
    gt = jnp.take_along_axis(jnp.transpose(aff_t, (1, 0, 2)), idx_t, axis=2).reshape(e, b * cap, 1)
    tr = min(1024, b * cap)
    ff = p['exp_w1'].shape[3]
    y = _expert_ffn(xs, gt, p['exp_w1'], p['exp_w3'], p['exp_w2'], p['layer'], tr=tr, tf=min(512, ff))
    return _moe_combine(x, g2, pos, y, cap, min(256, l))


def _layer(x, xc, c, c_ctx, p, last):
    b, l, d = x.shape
    lc = xc.shape[1]
    w = p['ssm_d'].shape[0]
    hd = w // NA_HEADS

    cc = jnp.zeros((8, d), F32).at[:b].set(c).at[b].set(c_ctx)
    mod = _ada(cc, p['w_ada'], p['b_ada'], p['layer'])
    m_l = [t[:, None, :] for t in jnp.split(mod[:b], 6, axis=-1)]
    m_c = [t[:, None, :] for t in jnp.split(mod[b:b + 1], 6, axis=-1)]

    w_in = p['w_in'].astype(BF16)
    scale = hd ** -0.5
    head_gain = jnp.stack([jnp.tile(p['na_q_gain'].astype(F32) * scale, NA_HEADS),
                           jnp.tile(p['na_k_gain'].astype(F32), NA_HEADS)])[:, None, :]
    ones_blk = jnp.asarray(np.kron(np.eye(NA_HEADS), np.ones((hd, hd))), BF16)
    tm_l = min(512, l)
    tm_c = min(512, lc)
    proj = _in_proj(x, p['norm1'], m_l[0], m_l[1], w_in, head_gain, ones_blk, min(1024, l))
    projc = _in_proj(xc, p['norm1'], m_c[0], m_c[1], w_in, head_gain, ones_blk, tm_c)

    ssm_l, ssm_c = _s5_branch(projc[..., :w], proj[..., :w], p, _s5_tables(p), need_ctx=not last)

    na_l = _na_latent(proj[..., w:2 * w], proj[..., 2 * w:3 * w], proj[..., 3 * w:4 * w],
                      projc[..., 2 * w:3 * w], projc[..., 3 * w:4 * w], _na_bias_table(p['na_rpb']))
    hy_l = _hyena_branch(proj, 4 * w // 128, p)

    wg = p['w_gate'].astype(BF16)
    wb = p['w_branch'].astype(BF16)
    wo = p['w_out'].astype(BF16)
    x = _merge(x, p['norm1'], m_l[0], m_l[1], m_l[2], (ssm_l, na_l, hy_l), wg, p['b_gate'], wb, wo,
               tm=tm_l, tc=min(512, d))
    x = _ec_moe_update(x, p['norm2'], m_l[3], m_l[4], m_l[5], p, tm_l)
    if last:
        return x, None
    na_c = _na_context(projc[..., w:2 * w], projc[..., 2 * w:3 * w], projc[..., 3 * w:4 * w])
    hy_c = _hyena_branch(projc, 4 * w // 128, p)
    xc = _merge(xc, p['norm1'], m_c[0], m_c[1], m_c[2], (ssm_c, na_c, hy_c), wg, p['b_gate'], wb, wo,
                tm=tm_c, tc=min(512, d))
    xc = _ec_moe_update(xc, p['norm2'], m_c[3], m_c[4], m_c[5], p, tm_c)
    return x, xc


_PARAM_NAMES = ('w_ada', 'b_ada', 'norm1', 'norm2', 'w_in',
                'ssm_lam_re', 'ssm_lam_im', 'ssm_log_step', 'ssm_b_re', 'ssm_b_im', 'ssm_c_re', 'ssm_c_im',
                'ssm_d', 'ssm_w_glu', 'na_q_gain', 'na_k_gain', 'na_rpb',
                'hy_conv_w', 'hy_conv_b', 'hy_w1', 'hy_b1', 'hy_w2', 'hy_b2', 'hy_w3', 'hy_b3', 'hy_w4',
                'hy_freq', 'hy_bias', 'w_gate', 'b_gate', 'w_branch', 'w_out',
                'router', 'exp_w1', 'exp_w3', 'exp_w2')
_STACKED = ('w_ada', 'b_ada', 'exp_w1', 'exp_w3', 'exp_w2')


def kernel(x, c, ctx, c_ctx, w_ada, b_ada, norm1, norm2, w_in, ssm_lam_re, ssm_lam_im, ssm_log_step, ssm_b_re, ssm_b_im, ssm_c_re, ssm_c_im, ssm_d, ssm_w_glu, na_q_gain, na_k_gain, na_rpb, hy_conv_w, hy_conv_b, hy_w1, hy_b1, hy_w2, hy_b2, hy_w3, hy_b3, hy_w4, hy_freq, hy_bias, w_gate, b_gate, w_branch, w_out, router, exp_w1, exp_w3, exp_w2):
    stacked = (w_ada, b_ada, norm1, norm2, w_in, ssm_lam_re, ssm_lam_im, ssm_log_step, ssm_b_re, ssm_b_im,
               ssm_c_re, ssm_c_im, ssm_d, ssm_w_glu, na_q_gain, na_k_gain, na_rpb, hy_conv_w, hy_conv_b,
               hy_w1, hy_b1, hy_w2, hy_b2, hy_w3, hy_b3, hy_w4, hy_freq, hy_bias, w_gate, b_gate, w_branch,
               w_out, router, exp_w1, exp_w3, exp_w2)
    depth = w_ada.shape[0]
    xc = ctx
    for layer in range(depth):
        p = {name: (t if name in _STACKED else t[layer]) for name, t in zip(_PARAM_NAMES, stacked)}
        p['layer'] = layer
        x, xc = _layer(x, xc, c, c_ctx, p, layer == depth - 1)
    return x
```

```python
import functools
import math

import numpy as np
import jax
import jax.numpy as jnp
from jax import lax
from jax.experimental import pallas as pl
from jax.experimental.pallas import tpu as pltpu

BF16 = jnp.bfloat16
F32 = jnp.float32

NORM_EPS = 1e-6
GRID_W = 64
SSM_GROUP = 16
NA_HEADS = 8
NA_WIN_ROWS = 8
NA_WIN_COLS = 16
HY_ORDER = 2
HY_BANDS = 16
HY_MIN_DECAY = math.log(1e-2) / 1.5
HY_MAX_DECAY = math.log(1e-2) / 0.3
EC_CAPACITY = 2
S5_CHUNK = 16
HY_FAST = 128
HY_PAD = 8
MASK_VALUE = -1e30
COMBINE_ALIGN = 16

V7X_VMEM_BYTES = 64 * 1024 * 1024


def _cparams(semantics, vmem_mb=48):
    return pltpu.CompilerParams(dimension_semantics=semantics, vmem_limit_bytes=vmem_mb * 1024 * 1024)


def _norm_mod(x, g, sh, sc):
    ms = jnp.mean(x * x, axis=-1, keepdims=True)
    return (x * lax.rsqrt(ms + NORM_EPS) * g) * (1.0 + sc) + sh


def _ada_kernel(c_ref, w_ref, b_ref, o_ref):
    c = c_ref[...]
    a = (c * jax.nn.sigmoid(c)).astype(BF16)
    o_ref[...] = jnp.dot(a, w_ref[...].astype(BF16), preferred_element_type=F32) + b_ref[...]


def _ada(cc, w, b, layer):
    m, k = cc.shape
    depth, _, n = w.shape
    tn = next(t for t in (1024, 512, 256, 128) if n % t == 0)
    return pl.pallas_call(
        _ada_kernel,
        grid=(n // tn,),
        in_specs=[pl.BlockSpec((m, k), lambda j: (0, 0)),
                  pl.BlockSpec((None, k, tn), lambda j: (layer, 0, j)),
                  pl.BlockSpec((None, 1, tn), lambda j: (layer, 0, j))],
        out_specs=pl.BlockSpec((m, tn), lambda j: (0, j)),
        out_shape=jax.ShapeDtypeStruct((m, n), F32),
        compiler_params=_cparams(("arbitrary",)),
        name="ada",
    )(cc, w, b.reshape(depth, 1, n))


def _in_proj_kernel(x_ref, g_ref, sh_ref, sc_ref, w_ref, hg_ref, ones_ref, o_ref, h_scr, *, head_dim):
    j = pl.program_id(2)

    @pl.when(j == 0)
    def _():
        h_scr[...] = _norm_mod(x_ref[...], g_ref[...], sh_ref[...], sc_ref[...]).astype(BF16)

    acc = jnp.dot(h_scr[...], w_ref[...], preferred_element_type=F32)
    is_qk = jnp.logical_or(j == 1, j == 2)

    @pl.when(is_qk)
    def _():
        yy = acc * acc
        hi = yy.astype(BF16)
        lo = (yy - hi.astype(F32)).astype(BF16)
        ss = (jnp.dot(hi, ones_ref[...], preferred_element_type=F32)
              + jnp.dot(lo, ones_ref[...], preferred_element_type=F32))
        o_ref[...] = (acc * lax.rsqrt(ss * (1.0 / head_dim) + NORM_EPS) * hg_ref[...]).astype(o_ref.dtype)

    @pl.when(jnp.logical_not(is_qk))
    def _():
        o_ref[...] = acc.astype(o_ref.dtype)


def _in_proj(x, g, sh, sc, w_bf, head_gain, ones_blk, tm):
    b, l, d = x.shape
    n = w_bf.shape[1]
    tn = ones_blk.shape[0]
    nb = sh.shape[0]
    mod_map = (lambda bi, i, j: (bi, 0, 0)) if nb > 1 else (lambda bi, i, j: (0, 0, 0))
    return pl.pallas_call(
        functools.partial(_in_proj_kernel, head_dim=tn // NA_HEADS),
        grid=(b, l // tm, n // tn),
        in_specs=[pl.BlockSpec((None, tm, d), lambda bi, i, j: (bi, i, 0)),
                  pl.BlockSpec((1, d), lambda bi, i, j: (0, 0)),
                  pl.BlockSpec((None, 1, d), mod_map),
                  pl.BlockSpec((None, 1, d), mod_map),
                  pl.BlockSpec((d, tn), lambda bi, i, j: (0, j)),
                  pl.BlockSpec((None, 1, tn), lambda bi, i, j: (jnp.clip(j - 1, 0, 1), 0, 0)),
                  pl.BlockSpec((tn, tn), lambda bi, i, j: (0, 0))],
        out_specs=pl.BlockSpec((None, tm, tn), lambda bi, i, j: (bi, i, j)),
        out_shape=jax.ShapeDtypeStruct((b, l, n), BF16),
        scratch_shapes=[pltpu.VMEM((tm, d), BF16)],
        compiler_params=_cparams(("parallel", "parallel", "arbitrary")),
        name="in_proj",
    )(x, g.reshape(1, d), sh, sc, w_bf, head_gain, ones_blk)


def _bmm_kernel(a_ref, w_ref, o_ref):
    o_ref[...] = jnp.dot(a_ref[...].astype(BF16), w_ref[...], preferred_element_type=F32).astype(o_ref.dtype)


def _bmm_add_kernel(a_ref, w_ref, y_ref, o_ref):
    o_ref[...] = (y_ref[...] + jnp.dot(a_ref[...].astype(BF16), w_ref[...],
                                       preferred_element_type=F32)).astype(o_ref.dtype)


def _bmm(a, w, add=None, tm=None, name="bmm"):
    g, m, k = a.shape
    n = w.shape[2]
    tm = m if tm is None else tm
    in_specs = [pl.BlockSpec((None, tm, k), lambda gi, i: (gi, i, 0)),
                pl.BlockSpec((None, k, n), lambda gi, i: (gi, 0, 0))]
    args = [a, w]
    kern = _bmm_kernel
    if add is not None:
        in_specs.append(pl.BlockSpec((None, tm, n), lambda gi, i: (gi, i, 0)))
        args.append(add)
        kern = _bmm_add_kernel
    return pl.pallas_call(
        kern,
        grid=(g, m // tm),
        in_specs=in_specs,
        out_specs=pl.BlockSpec((None, tm, n), lambda gi, i: (gi, i, 0)),
        out_shape=jax.ShapeDtypeStruct((g, m, n), F32),
        compiler_params=_cparams(("parallel", "parallel")),
        name=name,
    )(*args)


def _gelu_tanh(x):
    return 0.5 * x * (1.0 + jnp.tanh(math.sqrt(2.0 / math.pi) * (x + 0.044715 * (x * x * x))))


def _s5_out_kernel(y_ref, u_ref, d_ref, w_ref, o_ref):
    y = y_ref[...].astype(F32) + d_ref[...] * u_ref[...].astype(F32)
    z = _gelu_tanh(y)
    gate = jax.nn.sigmoid(jnp.dot(z.astype(BF16), w_ref[...], preferred_element_type=F32))
    o_ref[...] = (z * gate).astype(o_ref.dtype)


def _s5_out(y, row0, u, d, w_glu_bf):
    b, l, w = u.shape
    tm = math.gcd(math.gcd(l, row0), 512) if row0 else math.gcd(l, 512)
    off = row0 // tm
    return pl.pallas_call(
        _s5_out_kernel,
        grid=(b, l // tm),
        in_specs=[pl.BlockSpec((None, tm, w), lambda bi, i: (bi, off + i, 0)),
                  pl.BlockSpec((None, tm, w), lambda bi, i: (bi, i, 0)),
                  pl.BlockSpec((1, w), lambda bi, i: (0, 0)),
                  pl.BlockSpec((w, w), lambda bi, i: (0, 0))],
        out_specs=pl.BlockSpec((None, tm, w), lambda bi, i: (bi, i, 0)),
        out_shape=jax.ShapeDtypeStruct((b, l, w), BF16),
        compiler_params=_cparams(("parallel", "parallel")),
        name="s5_out",
    )(y, u, d.reshape(1, w), w_glu_bf)


def _s5_tables(p):
    t = S5_CHUNK
    mats, sums, reads, steps = [], [], [], []
    for d in range(2):
        lam_re = p['ssm_lam_re'][d].astype(F32)
        lam_im = p['ssm_lam_im'][d].astype(F32)
        step = jnp.exp(p['ssm_log_step'][d].astype(F32))[:, None]
        mag = jnp.exp(lam_re * step)
        a_re = mag * jnp.cos(lam_im * step)
        a_im = mag * jnp.sin(lam_im * step)
        den = lam_re * lam_re + lam_im * lam_im
        num_re = a_re - 1.0
        coef_re = (num_re * lam_re + a_im * lam_im) / den
        coef_im = (a_im * lam_re - num_re * lam_im) / den
        b_re = p['ssm_b_re'][d].astype(F32)
        b_im = p['ssm_b_im'][d].astype(F32)
        bb_re = coef_re[..., None] * b_re - coef_im[..., None] * b_im
        bb_im = coef_re[..., None] * b_im + coef_im[..., None] * b_re
        c_re = p['ssm_c_re'][d].astype(F32)
        c_im = p['ssm_c_im'][d].astype(F32)
        n = jnp.arange(t + 1, dtype=F32)[:, None, None]
        pm = jnp.exp(lam_re * step * n)
        pw_re = pm * jnp.cos(lam_im * step * n)
        pw_im = pm * jnp.sin(lam_im * step * n)
        ca_re = c_re[None] * pw_re[:, :, None, :] - c_im[None] * pw_im[:, :, None, :]
        ca_im = c_re[None] * pw_im[:, :, None, :] + c_im[None] * pw_re[:, :, None, :]
        ktau = (jnp.einsum('ngop,gpi->ngoi', ca_re[:t], bb_re)
                - jnp.einsum('ngop,gpi->ngoi', ca_im[:t], bb_im))
        s_idx = np.arange(t)[:, None]
        t_idx = np.arange(t)[None, :]
        lag = (t_idx - s_idx) if d == 0 else (s_idx - t_idx)
        lag_sel = jnp.asarray(lag[:, :, None] == np.arange(t), F32)
        blk = jnp.einsum('stn,ngoi->gsito', lag_sel, ktau, precision=lax.Precision.HIGHEST)
        mats.append(blk)
        flip = (lambda a: a[::-1]) if d == 0 else (lambda a: a)
        unflip = (lambda a: a) if d == 0 else (lambda a: a[::-1])
        sp_re, sp_im = flip(pw_re[:t]), flip(pw_im[:t])
        sw_re = sp_re[:, :, :, None] * bb_re[None] - sp_im[:, :, :, None] * bb_im[None]
        sw_im = sp_re[:, :, :, None] * bb_im[None] + sp_im[:, :, :, None] * bb_re[None]
        sums.append((jnp.transpose(sw_re, (1, 0, 3, 2)), jnp.transpose(sw_im, (1, 0, 3, 2))))
        rd_re, rd_im = unflip(ca_re[1:t + 1]), unflip(ca_im[1:t + 1])
        reads.append((jnp.transpose(rd_re, (1, 3, 0, 2)), -jnp.transpose(rd_im, (1, 3, 0, 2))))
        steps.append((pw_re[t], pw_im[t]))
    g, pdim = steps[0][0].shape
    k = SSM_GROUP
    mtot = (mats[0] + mats[1]).reshape(g, t * k, t * k)
    ssum = jnp.concatenate([sums[0][0], sums[0][1], sums[1][0], sums[1][1]], axis=-1).reshape(g, t * k, 4 * pdim)
    r = jnp.concatenate([reads[0][0], reads[0][1], reads[1][0], reads[1][1]], axis=1).reshape(g, 4 * pdim, t * k)
    tk = t * k
    eye = jnp.eye(2, dtype=F32)
    ymat = jnp.einsum('gjsito,jk->gsjitko', mtot.reshape(g // 2, 2, t, k, t, k), eye).reshape(g // 2, 2 * tk, 2 * tk)
    smat = jnp.einsum('gjsiqp,jk->gsjiqkp', ssum.reshape(g // 2, 2, t, k, 4, pdim), eye
                      ).reshape(g // 2, 2 * tk, 8 * pdim)
    w1 = jnp.concatenate([ymat, smat], axis=-1)
    r2 = jnp.einsum('gjqpto,jk->gqjptko', r.reshape(g // 2, 2, 4, pdim, t, k), eye).reshape(g // 2, 8 * pdim, 2 * tk)
    coef = [c.reshape(1, g * pdim) for c in (steps[0][0], steps[0][1], steps[1][0], steps[1][1])]
    return w1.astype(BF16), r2.astype(BF16), coef


LANES = 128


def _to_groups_kernel(u_ref, o_ref, u_scr, o_scr, *, cbk, bsz):
    t, k2 = S5_CHUNK, 2 * SSM_GROUP
    per = LANES // k2
    for b in range(bsz):
        for lt in range(u_scr.shape[0]):
            u_scr[lt] = u_ref[b, :, lt * LANES:(lt + 1) * LANES].astype(F32)
        for lt in range(u_scr.shape[0]):
            xs = [u_scr[lt, pl.ds(s, cbk, stride=t), :] for s in range(t)]
            for q in range(per):
                v = jnp.concatenate([x[:, q * k2:(q + 1) * k2] for x in xs], axis=1)
                for ot in range(o_scr.shape[1]):
                    o_scr[lt * per + q, ot, pl.ds(b, cbk, stride=bsz), :] = v[:, ot * LANES:(ot + 1) * LANES]
    for gp in range(o_scr.shape[0]):
        for ot in range(o_scr.shape[1]):
            o_ref[gp, :, ot * LANES:(ot + 1) * LANES] = o_scr[gp, ot].astype(o_ref.dtype)


def _from_groups_kernel(y_ref, o_ref, y_scr, o_scr, *, cbk, bsz):
    t, k2 = S5_CHUNK, 2 * SSM_GROUP
    per = LANES // k2
    ngp, nlt = y_scr.shape[0], y_scr.shape[1]
    for gp in range(ngp):
        for ot in range(nlt):
            y_scr[gp, ot] = y_ref[gp, :, ot * LANES:(ot + 1) * LANES].astype(F32)
    for b in range(bsz):
        for ot in range(nlt):
            zs = [y_scr[gp, ot, pl.ds(b, cbk, stride=bsz), :] for gp in range(ngp)]
            for q in range(per):
                s = ot * per + q
                v = jnp.concatenate([z[:, q * k2:(q + 1) * k2] for z in zs], axis=1)
                for lt in range(o_scr.shape[0]):
                    o_scr[lt, pl.ds(s, cbk, stride=t), :] = v[:, lt * LANES:(lt + 1) * LANES]
        for lt in range(o_scr.shape[0]):
            o_ref[b, :, lt * LANES:(lt + 1) * LANES] = o_scr[lt].astype(o_ref.dtype)


def _chunk_block(nc, bsz):
    return next(c for c in range(min(64, nc), 0, -1) if nc % c == 0 and (c * bsz) % 16 == 0)


def _to_groups(u):
    bsz, ltot, w = u.shape
    t, k2 = S5_CHUNK, 2 * SSM_GROUP
    nc = ltot // t
    cbk = _chunk_block(nc, bsz)
    ngp = w // k2
    return pl.pallas_call(
        functools.partial(_to_groups_kernel, cbk=cbk, bsz=bsz),
        grid=(nc // cbk,),
        in_specs=[pl.BlockSpec((bsz, cbk * t, w), lambda i: (0, i, 0))],
        out_specs=pl.BlockSpec((ngp, cbk * bsz, t * k2), lambda i: (0, i, 0)),
        out_shape=jax.ShapeDtypeStruct((ngp, nc * bsz, t * k2), BF16),
        scratch_shapes=[pltpu.VMEM((w // LANES, cbk * t, LANES), F32),
                        pltpu.VMEM((ngp, t * k2 // LANES, cbk * bsz, LANES), F32)],
        compiler_params=_cparams(("parallel",)),
        name="s5_to_groups",
    )(u)


def _from_groups(y, bsz):
    ngp, m, tk2 = y.shape
    t, k2 = S5_CHUNK, 2 * SSM_GROUP
    nc = m // bsz
    cbk = _chunk_block(nc, bsz)
    w = ngp * k2
    return pl.pallas_call(
        functools.partial(_from_groups_kernel, cbk=cbk, bsz=bsz),
        grid=(nc // cbk,),
        in_specs=[pl.BlockSpec((ngp, cbk * bsz, tk2), lambda i: (0, i, 0))],
        out_specs=pl.BlockSpec((bsz, cbk * t, w), lambda i: (0, i, 0)),
        out_shape=jax.ShapeDtypeStruct((bsz, nc * t, w), BF16),
        scratch_shapes=[pltpu.VMEM((ngp, tk2 // LANES, cbk * bsz, LANES), F32),
                        pltpu.VMEM((w // LANES, cbk * t, LANES), F32)],
        compiler_params=_cparams(("parallel",)),
        name="s5_from_groups",
    )(y)


def _s5_chunk_kernel(u_ref, w_ref, y_ref, xfr_ref, xfi_ref, xbr_ref, xbi_ref):
    r = jnp.dot(u_ref[...], w_ref[...], preferred_element_type=F32)
    ny = y_ref.shape[-1]
    nx = xfr_ref.shape[-1]
    y_ref[...] = r[:, :ny]
    for i, ref in enumerate((xfr_ref, xfi_ref, xbr_ref, xbi_ref)):
        ref[...] = r[:, ny + i * nx:ny + (i + 1) * nx]


def _s5_chunk(ug, w1):
    gp, m, tk2 = ug.shape
    nx = (w1.shape[2] - tk2) // 4
    xspec = pl.BlockSpec((m, nx), lambda g: (0, g))
    xshape = jax.ShapeDtypeStruct((m, gp * nx), F32)
    return pl.pallas_call(
        _s5_chunk_kernel,
        grid=(gp,),
        in_specs=[pl.BlockSpec((None, m, tk2), lambda g: (g, 0, 0)),
                  pl.BlockSpec((None, tk2, w1.shape[2]), lambda g: (g, 0, 0))],
        out_specs=[pl.BlockSpec((None, m, tk2), lambda g: (g, 0, 0)), xspec, xspec, xspec, xspec],
        out_shape=[jax.ShapeDtypeStruct((gp, m, tk2), F32), xshape, xshape, xshape, xshape],
        compiler_params=_cparams(("parallel",)),
        name="s5_chunk",
    )(ug, w1)


def _s5_scan_kernel(xfr, xfi, xbr, xbi, afr, afi, abr, abi, hfr, hfi, hbr, hbi, *, tiles_ctx, tiles, bsz):
    shape = (2 * bsz, xfr.shape[1])
    low = lax.broadcasted_iota(jnp.int32, shape, 0) < bsz
    zero = jnp.zeros(shape, F32)

    def step(hr, hi, ar, ai, xr, xi):
        return ar * hr - ai * hi + xr, ar * hi + ai * hr + xi

    def fwd_tile(i, carry):
        ar, ai = jnp.broadcast_to(afr[...], shape), jnp.broadcast_to(afi[...], shape)
        r0 = pl.multiple_of(i * 2 * bsz, 2 * bsz)
        xr, xi = xfr[pl.ds(r0, 2 * bsz), :], xfi[pl.ds(r0, 2 * bsz), :]
        sr, si = pltpu.roll(carry[0], bsz, 0), pltpu.roll(carry[1], bsz, 0)
        tr, ti = step(sr, si, ar, ai, xr, xi)
        t2r, t2i = pltpu.roll(tr, bsz, 0), pltpu.roll(ti, bsz, 0)
        hfr[pl.ds(r0, 2 * bsz), :] = jnp.where(low, sr, t2r)
        hfi[pl.ds(r0, 2 * bsz), :] = jnp.where(low, si, t2i)
        return step(t2r, t2i, ar, ai, xr, xi)

    lax.fori_loop(0, tiles, fwd_tile, (zero, zero))

    def bwd_tile(i, carry):
        ar, ai = jnp.broadcast_to(abr[...], shape), jnp.broadcast_to(abi[...], shape)
        r0 = pl.multiple_of(i * 2 * bsz, 2 * bsz)
        xr, xi = xbr[pl.ds(r0, 2 * bsz), :], xbi[pl.ds(r0, 2 * bsz), :]
        sr, si = pltpu.roll(carry[0], bsz, 0), pltpu.roll(carry[1], bsz, 0)
        tr, ti = step(sr, si, ar, ai, xr, xi)
        t2r, t2i = pltpu.roll(tr, bsz, 0), pltpu.roll(ti, bsz, 0)
        hbr[pl.ds(r0, 2 * bsz), :] = jnp.where(low, t2r, sr)
        hbi[pl.ds(r0, 2 * bsz), :] = jnp.where(low, t2i, si)
        return step(t2r, t2i, ar, ai, xr, xi)

    carry = lax.fori_loop(0, tiles_ctx, lambda j, c: bwd_tile(tiles_ctx - 1 - j, c), (zero, zero))
    lax.fori_loop(0, tiles - tiles_ctx, lambda j, c: bwd_tile(tiles - 1 - j, c), carry)


def _s5_scan(xs, coef, chunks_ctx, chunks, bsz):
    m, lanes = xs[0].shape
    assert 2 * bsz == 8 and chunks % 2 == 0 and chunks_ctx % 2 == 0
    lb = min(256, lanes)
    xspec = pl.BlockSpec((m, lb), lambda j: (0, j))
    cspec = pl.BlockSpec((1, lb), lambda j: (0, j))
    shape = jax.ShapeDtypeStruct((m, lanes), F32)
    return pl.pallas_call(
        functools.partial(_s5_scan_kernel, tiles_ctx=chunks_ctx // 2, tiles=chunks // 2, bsz=bsz),
        grid=(lanes // lb,),
        in_specs=[xspec] * 4 + [cspec] * 4,
        out_specs=[xspec] * 4,
        out_shape=[shape] * 4,
        compiler_params=_cparams(("parallel",)),
        name="s5_scan",
    )(*xs, *coef)


def _s5_readout_kernel(y_ref, hfr, hfi, hbr, hbi, r_ref, o_ref):
    h = jnp.concatenate([hfr[...], hfi[...], hbr[...], hbi[...]], axis=1).astype(BF16)
    o_ref[...] = (y_ref[...] + jnp.dot(h, r_ref[...], preferred_element_type=F32)).astype(o_ref.dtype)


def _s5_readout(y0, hs, r2):
    gp, m, tk2 = y0.shape
    nx = hs[0].shape[1] // gp
    hspec = pl.BlockSpec((m, nx), lambda g: (0, g))
    yspec = pl.BlockSpec((None, m, tk2), lambda g: (g, 0, 0))
    return pl.pallas_call(
        _s5_readout_kernel,
        grid=(gp,),
        in_specs=[yspec, hspec, hspec, hspec, hspec, pl.BlockSpec((None, 4 * nx, tk2), lambda g: (g, 0, 0))],
        out_specs=yspec,
        out_shape=jax.ShapeDtypeStruct((gp, m, tk2), BF16),
        compiler_params=_cparams(("parallel",)),
        name="s5_readout",
    )(y0, *hs, r2)


def _s5_branch(u_ctx, u_lat, p, tables, need_ctx):
    w1, r2, coef = tables
    b, l, w = u_lat.shape
    lc = u_ctx.shape[1]
    t, k = S5_CHUNK, SSM_GROUP
    g = w // k
    nc = (lc + l) // t
    m = nc * b
    ug = _to_groups(jnp.concatenate([u_ctx, u_lat], axis=1))
    y0, *xs = _s5_chunk(ug, w1)
    hs = _s5_scan(xs, coef, lc // t, nc, b)
    y = _from_groups(_s5_readout(y0, hs, r2), b)
    w_glu = p['ssm_w_glu'].astype(BF16)
    out_l = _s5_out(y, lc, u_lat, p['ssm_d'], w_glu)
    out_c = _s5_out(y, 0, u_ctx, p['ssm_d'], w_glu) if need_ctx else None
    return out_l, out_c


def _na_kernel(q_ref, k_ref, v_ref, kc_ref, vc_ref, bias_ref, o_ref, s_scr, p_scr, *, rows, head_dim, rps):
    kr = NA_WIN_ROWS
    kc = kc_ref[...]
    vc = vc_ref[...]
    nt = (((1,), (1,)), ((), ()))
    lane_tile = 2 * head_dim
    nwin = kr * GRID_W
    first_head = lax.broadcasted_iota(jnp.int32, (GRID_W, lane_tile), 1) < head_dim
    vwins = []
    for sub in range(rps):
        r = pl.program_id(1) * rps + sub
        start = jnp.clip(r - kr // 2, 0, rows - kr)
        cls = r - start
        base = pl.multiple_of(start * GRID_W, GRID_W)
        kwin = k_ref[pl.ds(base, nwin), :]
        vwins.append(v_ref[pl.ds(base, nwin), :])
        q = q_ref[sub * GRID_W:(sub + 1) * GRID_W, :]
        for h in range(NA_HEADS):
            sl = slice(h // 2 * lane_tile, (h // 2 + 1) * lane_tile)
            qh = jnp.where(first_head == (h % 2 == 0), q[:, sl], jnp.zeros_like(q[:, sl]))
            i = sub * NA_HEADS + h
            s_scr[i, :, :nwin] = lax.dot_general(qh, kwin[:, sl], nt, preferred_element_type=F32) + bias_ref[cls, h]
            s_scr[i, :, nwin:] = lax.dot_general(qh, kc[:, sl], nt, preferred_element_type=F32)
    inv = []
    for i in range(rps * NA_HEADS):
        s = s_scr[i]
        pr = jnp.exp(s - jnp.max(s, axis=-1, keepdims=True))
        inv.append(1.0 / jnp.sum(pr, axis=-1, keepdims=True))
        p_scr[i] = pr.astype(BF16)
    for sub in range(rps):
        outs = []
        for pair in range(NA_HEADS // 2):
            sl = slice(pair * lane_tile, (pair + 1) * lane_tile)
            halves = []
            for half in range(2):
                i = sub * NA_HEADS + 2 * pair + half
                o = (jnp.dot(p_scr[i, :, :nwin], vwins[sub][:, sl], preferred_element_type=F32)
                     + jnp.dot(p_scr[i, :, nwin:], vc[:, sl], preferred_element_type=F32))
                halves.append(o * inv[i])
            outs.append(jnp.where(first_head, halves[0], halves[1]))
        o_ref[sub * GRID_W:(sub + 1) * GRID_W, :] = jnp.concatenate(outs, axis=-1).astype(o_ref.dtype)


def _na_bias_table(rpb):
    kr, kw, w = NA_WIN_ROWS, NA_WIN_COLS, GRID_W
    col = np.arange(w)
    cstart = np.clip(col - kw // 2, 0, w - kw)
    kcol = np.arange(w)
    inwin = (kcol[None, :] >= cstart[:, None]) & (kcol[None, :] < cstart[:, None] + kw)
    dc = np.clip(kcol[None, :] - col[:, None] + (kw - 1), 0, 2 * kw - 2)
    rpb = rpb.astype(F32)
    dr = np.arange(kr)[None, :] - np.arange(kr)[:, None] + (kr - 1)
    row_sel = jnp.asarray(dr[:, :, None] == np.arange(2 * kr - 1), F32)
    col_sel = jnp.asarray((dc[:, :, None] == np.arange(2 * kw - 1)) & inwin[:, :, None], F32)
    hp = lax.Precision.HIGHEST
    rows = jnp.einsum('hab,oja->ohjb', rpb, row_sel, precision=hp)
    bias = jnp.einsum('ohjb,cdb->ohcjd', rows, col_sel, precision=hp)
    bias = jnp.where(jnp.asarray(inwin)[None, None, :, None, :], bias, MASK_VALUE)
    return bias.reshape(kr, rpb.shape[0], w, kr * w)


def _na_latent(q, k, v, kc, vc, bias_tab):
    b, l, w = q.shape
    lc = kc.shape[1]
    rows = l // GRID_W
    assert rows >= NA_WIN_ROWS
    nk = NA_WIN_ROWS * GRID_W
    rps = 2
    assert rows % rps == 0
    return pl.pallas_call(
        functools.partial(_na_kernel, rows=rows, head_dim=w // NA_HEADS, rps=rps),
        grid=(b, rows // rps),
        in_specs=[pl.BlockSpec((None, rps * GRID_W, w), lambda bi, r: (bi, r, 0)),
                  pl.BlockSpec((None, l, w), lambda bi, r: (bi, 0, 0)),
                  pl.BlockSpec((None, l, w), lambda bi, r: (bi, 0, 0)),
                  pl.BlockSpec((None, lc, w), lambda bi, r: (bi, 0, 0)),
                  pl.BlockSpec((None, lc, w), lambda bi, r: (bi, 0, 0)),
                  pl.BlockSpec((NA_WIN_ROWS, NA_HEADS, GRID_W, nk), lambda bi, r: (0, 0, 0, 0))],
        out_specs=pl.BlockSpec((None, rps * GRID_W, w), lambda bi, r: (bi, r, 0)),
        out_shape=jax.ShapeDtypeStruct((b, l, w), BF16),
        scratch_shapes=[pltpu.VMEM((rps * NA_HEADS, GRID_W, nk + lc), F32),
                        pltpu.VMEM((rps * NA_HEADS, GRID_W, nk + lc), BF16)],
        compiler_params=_cparams(("parallel", "arbitrary"), vmem_mb=56),
        name="na_latent",
    )(q, k, v, kc, vc, bias_tab)


def _na_ctx_kernel(q_ref, k_ref, v_ref, o_ref, *, head_dim):
    q, k, v = q_ref[...], k_ref[...], v_ref[...]
    outs = []
    for h in range(NA_HEADS):
        sl = slice(h * head_dim, (h + 1) * head_dim)
        s = lax.dot_general(q[:, sl], k[:, sl], (((1,), (1,)), ((), ())), preferred_element_type=F32)
        pr = jnp.exp(s - jnp.max(s, axis=-1, keepdims=True))
        den = jnp.sum(pr, axis=-1, keepdims=True)
        outs.append(jnp.dot(pr.astype(BF16), v[:, sl], preferred_element_type=F32) / den)
    o_ref[...] = jnp.concatenate(outs, axis=-1).astype(o_ref.dtype)


def _na_context(q, k, v):
    b, lc, w = q.shape
    spec = pl.BlockSpec((None, lc, w), lambda bi: (bi, 0, 0))
    return pl.pallas_call(
        functools.partial(_na_ctx_kernel, head_dim=w // NA_HEADS),
        grid=(b,),
        in_specs=[spec, spec, spec],
        out_specs=spec,
        out_shape=jax.ShapeDtypeStruct((b, lc, w), BF16),
        compiler_params=_cparams(("parallel",)),
        name="na_context",
    )(q, k, v)


def _hyena_filter(length, p, slow=None):
    n = 2 * length
    pos = np.arange(n)
    if slow is not None:
        pos = pos.reshape(slow, n // slow).T.reshape(-1)
    off = np.where(pos < length, pos, n - pos)
    backward = jnp.asarray(pos > length)[:, None, None]
    live = jnp.asarray(pos != length, F32)[:, None, None]
    tau = jnp.asarray(np.minimum(off, length - 1), F32)[:, None]
    t = tau / (length - 1)
    freqs = jnp.linspace(1e-4, HY_BANDS - 1, HY_BANDS, dtype=F32)
    ang = (2.0 * math.pi / length) * tau * freqs[None, :]
    z = jnp.concatenate([t, jnp.cos(ang), -jnp.sin(ang)], axis=-1)
    fr = p['hy_freq'].astype(F32)
    hp = lax.Precision.HIGH
    h = jnp.sin(fr * (jnp.dot(z, p['hy_w1'].astype(F32), precision=hp) + p['hy_b1'].astype(F32)))
    h = jnp.sin(fr * (jnp.dot(h, p['hy_w2'].astype(F32), precision=hp) + p['hy_b2'].astype(F32)))
    h = jnp.sin(fr * (jnp.dot(h, p['hy_w3'].astype(F32), precision=hp) + p['hy_b3'].astype(F32)))
    wdim = p['hy_bias'].shape[-1]
    h = jnp.dot(h, p['hy_w4'].astype(F32), precision=hp).reshape(n, HY_ORDER, 2, wdim)
    decay = jnp.exp(-t * jnp.abs(jnp.linspace(HY_MIN_DECAY, HY_MAX_DECAY, wdim, dtype=F32)))
    filt = jnp.where(backward, h[:, :, 1], h[:, :, 0]) * (decay[:, None, :] * live)
    filt = filt / jnp.sum(jnp.abs(filt), axis=0, keepdims=True)
    return filt.reshape(n, HY_ORDER * wdim)


def _block2(re, im):
    return np.block([[re, -im], [im, re]])


def _fft_consts(length):
    n = 2 * length
    n2 = HY_FAST
    n1 = n // n2
    h1 = n1 // 2
    a1 = -2.0 * np.pi * np.outer(np.arange(n1), np.arange(n1)) / n1
    a2 = -2.0 * np.pi * np.outer(np.arange(n2), np.arange(n2)) / n2
    f1 = _block2(np.cos(a1[:, :h1]), np.sin(a1[:, :h1]))
    f1_real = np.concatenate([np.cos(a1), np.sin(a1)], axis=0)
    f3 = _block2(np.cos(a2), np.sin(a2))
    g3 = _block2(np.cos(a2), -np.sin(a2))
    g1 = _block2(np.cos(a1[:h1, :]) / n, -np.sin(a1[:h1, :]) / n)
    wa = -2.0 * np.pi * np.arange(n2) / n
    wr = np.broadcast_to(np.cos(wa)[:, None], (n2, 128))
    wi = np.broadcast_to(np.sin(wa)[:, None], (n2, 128))
    cast = lambda m: jnp.asarray(m, BF16)
    return dict(n1=n1, h1=h1, f1=cast(f1), f1_real=cast(f1_real), f3=cast(f3), g3=cast(g3), g1=cast(g1),
                wr=jnp.asarray(wr, F32), wi=jnp.asarray(wi, F32))


def _fft_first_stage(a_r, a_i, load_cols, f1_ref, n1, cb):
    slab = HY_FAST + HY_PAD

    def body(i, carry):
        n2 = 2 * i
        z = jnp.concatenate([load_cols(n2), load_cols(n2 + 1)], axis=1)
        r = jnp.dot(f1_ref[...], z, preferred_element_type=F32)
        a_r[pl.ds(n2, n1, stride=slab), :] = r[:n1, :cb]
        a_i[pl.ds(n2, n1, stride=slab), :] = r[n1:, :cb]
        a_r[pl.ds(n2 + 1, n1, stride=slab), :] = r[:n1, cb:]
        a_i[pl.ds(n2 + 1, n1, stride=slab), :] = r[n1:, cb:]
        return carry

    lax.fori_loop(0, HY_FAST // 2, body, 0, unroll=4)


def _fft_slab_loop(a_r, a_i, f3_ref, wr_ref, wi_ref, n1, cb, finish):
    slab = HY_FAST + HY_PAD
    n2 = HY_FAST
    wr, wi = wr_ref[...], wi_ref[...]
    if cb != wr.shape[1]:
        wr, wi = jnp.tile(wr, (1, cb // wr.shape[1])), jnp.tile(wi, (1, cb // wi.shape[1]))

    def body(i, carry):
        tr, ti = carry
        k1 = 2 * i
        r0 = pl.multiple_of(k1 * slab, 8)
        r1 = pl.multiple_of((k1 + 1) * slab, 8)
        tr1, ti1 = tr * wr - ti * wi, tr * wi + ti * wr
        ar0, ai0 = a_r[pl.ds(r0, n2), :], a_i[pl.ds(r0, n2), :]
        ar1, ai1 = a_r[pl.ds(r1, n2), :], a_i[pl.ds(r1, n2), :]
        xr = jnp.concatenate([ar0 * tr - ai0 * ti, ar1 * tr1 - ai1 * ti1], axis=1)
        xi = jnp.concatenate([ar0 * ti + ai0 * tr, ar1 * ti1 + ai1 * tr1], axis=1)
        bc = jnp.dot(f3_ref[...], jnp.concatenate([xr, xi], axis=0).astype(BF16), preferred_element_type=F32)
        finish(k1, (r0, r1), (tr, ti, tr1, ti1), bc[:n2], bc[n2:])
        return tr1 * wr - ti1 * wi, tr1 * wi + ti1 * wr

    one = jnp.ones((n2, cb), F32)
    lax.fori_loop(0, n1 // 2, body, (one, jnp.zeros_like(one)), unroll=2)


def _fft_conv_kernel(y_ref, sr_ref, si_ref, f1_ref, f3_ref, g3_ref, g1_ref, wr_ref, wi_ref, o_ref, a_r, a_i,
                     *, n1):
    h1 = n1 // 2
    n2 = HY_FAST
    cb = o_ref.shape[-1]
    slab = HY_FAST + HY_PAD

    def load_cols(j):
        r = pl.multiple_of(j * h1, h1)
        return jnp.concatenate([y_ref[0, pl.ds(r, h1), :], y_ref[1, pl.ds(r, h1), :]], axis=0)

    _fft_first_stage(a_r, a_i, load_cols, f1_ref, n1, cb)

    def finish(k1, rows, tw, br, bi):
        s0 = pl.multiple_of(k1 * n2, n2)
        s1 = pl.multiple_of((k1 + 1) * n2, n2)
        sr = jnp.concatenate([sr_ref[pl.ds(s0, n2), :], sr_ref[pl.ds(s1, n2), :]], axis=1).astype(F32)
        si = jnp.concatenate([si_ref[pl.ds(s0, n2), :], si_ref[pl.ds(s1, n2), :]], axis=1).astype(F32)
        y = jnp.concatenate([br * sr - bi * si, br * si + bi * sr], axis=0).astype(BF16)
        ac = jnp.dot(g3_ref[...], y, preferred_element_type=F32)
        pr, pi = ac[:n2], ac[n2:]
        for j in range(2):
            tr, ti = tw[2 * j], tw[2 * j + 1]
            prj, pij = pr[:, j * cb:(j + 1) * cb], pi[:, j * cb:(j + 1) * cb]
            a_r[pl.ds(rows[j], n2), :] = prj * tr + pij * ti
            a_i[pl.ds(rows[j], n2), :] = pij * tr - prj * ti

    _fft_slab_loop(a_r, a_i, f3_ref, wr_ref, wi_ref, n1, cb, finish)

    def out_body(i, carry):
        j0 = 2 * i
        cols = []
        for j in (j0, j0 + 1):
            cols.append(jnp.concatenate([a_r[pl.ds(j, n1, stride=slab), :], a_i[pl.ds(j, n1, stride=slab), :]],
                                        axis=0))
        o = jnp.dot(g1_ref[...], jnp.concatenate(cols, axis=1).astype(BF16), preferred_element_type=F32)
        for jj in range(2):
            r = pl.multiple_of((j0 + jj) * h1, h1)
            o_ref[0, pl.ds(r, h1), :] = o[:h1, jj * cb:(jj + 1) * cb].astype(o_ref.dtype)
            o_ref[1, pl.ds(r, h1), :] = o[h1:, jj * cb:(jj + 1) * cb].astype(o_ref.dtype)
        return carry

    lax.fori_loop(0, n2 // 2, out_body, 0, unroll=4)


def _fft_conv(yp, col0, spec_r, spec_i, scol0, consts, cb=128):
    b, l, _ = yp.shape
    n = spec_r.shape[0]
    n1 = consts['n1']
    slab = HY_FAST + HY_PAD
    nblk = consts['nblk']
    y4 = yp.reshape(b // 2, 2, l, yp.shape[-1])
    const = lambda a: pl.BlockSpec(a.shape, lambda j, pi: (0, 0))
    return pl.pallas_call(
        functools.partial(_fft_conv_kernel, n1=n1),
        grid=(nblk, b // 2),
        in_specs=[pl.BlockSpec((None, 2, l, cb), lambda j, pi: (pi, 0, 0, col0 + j)),
                  pl.BlockSpec((n, cb), lambda j, pi: (0, scol0 + j)),
                  pl.BlockSpec((n, cb), lambda j, pi: (0, scol0 + j)),
                  const(consts['f1']), const(consts['f3']), const(consts['g3']), const(consts['g1']),
                  const(consts['wr']), const(consts['wi'])],
        out_specs=pl.BlockSpec((None, 2, l, cb), lambda j, pi: (pi, 0, 0, j)),
        out_shape=jax.ShapeDtypeStruct((b // 2, 2, l, nblk * cb), BF16),
        scratch_shapes=[pltpu.VMEM((n1 * slab, cb), F32), pltpu.VMEM((n1 * slab, cb), F32)],
        compiler_params=_cparams(("arbitrary", "arbitrary"), vmem_mb=58),
        name="fft_conv",
    )(y4, spec_r, spec_i, consts['f1'], consts['f3'], consts['g3'], consts['g1'],
      consts['wr'], consts['wi']).reshape(b, l, nblk * cb)


def _fft_spec_kernel(f_ref, f1_ref, f3_ref, wr_ref, wi_ref, sr_ref, si_ref, a_r, a_i, *, n1):
    n2 = HY_FAST
    cb = sr_ref.shape[-1]

    def load_cols(j):
        return f_ref[pl.ds(pl.multiple_of(j * n1, n1), n1), :]

    _fft_first_stage(a_r, a_i, load_cols, f1_ref, n1, cb)

    def finish(k1, rows, tw, br, bi):
        for j in range(2):
            s = pl.multiple_of((k1 + j) * n2, n2)
            sr_ref[pl.ds(s, n2), :] = br[:, j * cb:(j + 1) * cb].astype(sr_ref.dtype)
            si_ref[pl.ds(s, n2), :] = bi[:, j * cb:(j + 1) * cb].astype(si_ref.dtype)

    _fft_slab_loop(a_r, a_i, f3_ref, wr_ref, wi_ref, n1, cb, finish)


def _fft_spectrum(filt_p, consts, cb=128):
    n, c = filt_p.shape
    n1 = consts['n1']
    slab = HY_FAST + HY_PAD
    const = lambda a: pl.BlockSpec(a.shape, lambda j: (0, 0))
    spec = pl.BlockSpec((n, cb), lambda j: (0, j))
    return pl.pallas_call(
        functools.partial(_fft_spec_kernel, n1=n1),
        grid=(c // cb,),
        in_specs=[spec, const(consts['f1_real']), const(consts['f3']), const(consts['wr']), const(consts['wi'])],
        out_specs=[spec, spec],
        out_shape=[jax.ShapeDtypeStruct((n, c), BF16)] * 2,
        scratch_shapes=[pltpu.VMEM((n1 * slab, cb), F32), pltpu.VMEM((n1 * slab, cb), F32)],
        compiler_params=_cparams(("arbitrary",), vmem_mb=58),
        name="fft_spectrum",
    )(filt_p, consts['f1_real'], consts['f3'], consts['wr'], consts['wi'])


def _dense_conv_kernel(y_ref, sr_ref, si_ref, f_ref, g_ref, o_ref):
    l = y_ref.shape[1]
    n = sr_ref.shape[0]
    bc = jnp.dot(f_ref[...], jnp.concatenate([y_ref[0], y_ref[1]], axis=0), preferred_element_type=F32)
    br, bi = bc[:n], bc[n:]
    sr, si = sr_ref[...], si_ref[...]
    y = jnp.concatenate([br * sr - bi * si, br * si + bi * sr], axis=0).astype(BF16)
    o = jnp.dot(g_ref[...], y, preferred_element_type=F32)
    o_ref[0] = o[:l].astype(o_ref.dtype)
    o_ref[1] = o[l:].astype(o_ref.dtype)


def _dense_conv(y, col0, spec_r, spec_i, scol0, fmat, gmat, nblk, cb=128):
    b, l, _ = y.shape
    n = spec_r.shape[0]
    y4 = y.reshape(b // 2, 2, l, y.shape[-1])
    const = lambda a: pl.BlockSpec(a.shape, lambda j, pi: (0, 0))
    return pl.pallas_call(
        _dense_conv_kernel,
        grid=(nblk, b // 2),
        in_specs=[pl.BlockSpec((None, 2, l, cb), lambda j, pi: (pi, 0, 0, col0 + j)),
                  pl.BlockSpec((n, cb), lambda j, pi: (0, scol0 + j)),
                  pl.BlockSpec((n, cb), lambda j, pi: (0, scol0 + j)),
                  const(fmat), const(gmat)],
        out_specs=pl.BlockSpec((None, 2, l, cb), lambda j, pi: (pi, 0, 0, j)),
        out_shape=jax.ShapeDtypeStruct((b // 2, 2, l, nblk * cb), BF16),
        compiler_params=_cparams(("parallel", "parallel")),
        name="dense_conv",
    )(y4, spec_r, spec_i, fmat, gmat).reshape(b, l, nblk * cb)


def _sconv_kernel(u_ref, w_ref, b_ref, o_ref, *scr, h1):
    l = u_ref.shape[0]
    if h1 is None:
        u = u_ref[...].astype(F32)
        row = lax.broadcasted_iota(jnp.int32, u.shape, 0)
        prev = jnp.where(row == 0, 0.0, pltpu.roll(u, 1, 0))
        nxt = jnp.where(row == l - 1, 0.0, pltpu.roll(u, l - 1, 0))
        z = prev * w_ref[0:1, :] + u * w_ref[1:2, :] + nxt * w_ref[2:3, :] + b_ref[...]
        o_ref[...] = z.astype(o_ref.dtype)
        return
    buf = scr[0]
    pitch = HY_FAST + HY_PAD
    zero_row = jnp.zeros((1, buf.shape[1]), F32)
    buf[pitch - 1:pitch, :] = zero_row
    for n1 in range(h1):
        base = pitch * (n1 + 1)
        blk = u_ref[n1 * HY_FAST:(n1 + 1) * HY_FAST, :].astype(F32)
        buf[base:base + HY_FAST, :] = blk
        buf[base + pitch - 1:base + pitch, :] = blk[HY_FAST - 1:HY_FAST, :]
        buf[base - HY_PAD:base - HY_PAD + 1, :] = blk[0:1, :]
    buf[pitch * h1 + HY_FAST:pitch * h1 + HY_FAST + 1, :] = zero_row
    w0, w1, w2, bias = w_ref[0:1, :], w_ref[1:2, :], w_ref[2:3, :], b_ref[...]

    def body(n2, carry):
        taps = [buf[pl.ds(pitch - 1 + n2 + k, h1, stride=pitch), :] for k in range(3)]
        z = taps[0] * w0 + taps[1] * w1 + taps[2] * w2 + bias
        o_ref[pl.ds(pl.multiple_of(n2 * h1, h1), h1), :] = z.astype(o_ref.dtype)
        return carry

    lax.fori_loop(0, HY_FAST, body, 0, unroll=4)


def _short_conv(proj, col0, conv_w, conv_b, h1, cb=128):
    b, l, _ = proj.shape
    c = conv_w.shape[1]
    return pl.pallas_call(
        functools.partial(_sconv_kernel, h1=h1),
        scratch_shapes=[] if h1 is None else [pltpu.VMEM(((HY_FAST + HY_PAD) * (h1 + 1), cb), F32)],
        grid=(b, c // cb),
        in_specs=[pl.BlockSpec((None, l, cb), lambda bi, j: (bi, 0, col0 + j)),
                  pl.BlockSpec((conv_w.shape[0], cb), lambda bi, j: (0, j)),
                  pl.BlockSpec((1, cb), lambda bi, j: (0, j))],
        out_specs=pl.BlockSpec((None, l, cb), lambda bi, j: (bi, 0, j)),
        out_shape=jax.ShapeDtypeStruct((b, l, c), BF16),
        compiler_params=_cparams(("parallel", "parallel")),
        name="short_conv",
    )(proj, conv_w.astype(F32), conv_b.astype(F32).reshape(1, c))


def _gate_kernel(x_ref, conv_ref, y_ref, bias_ref, o_ref, *scr, h1):
    y = y_ref[...].astype(F32)
    z = x_ref[...].astype(F32) * (conv_ref[...].astype(F32) + y * bias_ref[...])
    if h1 is None:
        o_ref[...] = z.astype(o_ref.dtype)
        return
    scr[0][...] = z

    def body(n1, carry):
        r = pl.multiple_of(n1 * HY_FAST, HY_FAST)
        o_ref[pl.ds(r, HY_FAST), :] = scr[0][pl.ds(n1, HY_FAST, stride=h1), :].astype(o_ref.dtype)
        return carry

    lax.fori_loop(0, h1, body, 0)


def _hy_gate(z, xcol, conv, y, ycol, bias, h1=None, cb=128):
    b, l, w = conv.shape
    return pl.pallas_call(
        functools.partial(_gate_kernel, h1=h1),
        grid=(b, w // cb),
        in_specs=[pl.BlockSpec((None, l, cb), lambda bi, j: (bi, 0, xcol + j)),
                  pl.BlockSpec((None, l, cb), lambda bi, j: (bi, 0, j)),
                  pl.BlockSpec((None, l, cb), lambda bi, j: (bi, 0, ycol + j)),
                  pl.BlockSpec((1, cb), lambda bi, j: (0, j))],
        out_specs=pl.BlockSpec((None, l, cb), lambda bi, j: (bi, 0, j)),
        out_shape=jax.ShapeDtypeStruct((b, l, w), BF16),
        scratch_shapes=[] if h1 is None else [pltpu.VMEM((l, cb), F32)],
        compiler_params=_cparams(("parallel", "parallel")),
        name="hy_gate",
    )(z, conv, y, bias.astype(F32).reshape(1, w))


def _hyena_branch(proj, col0, p, cb=128):
    b, l, _ = proj.shape
    w = p['hy_bias'].shape[-1]
    n = 2 * l
    nblk = w // cb
    if n // HY_FAST >= 16:
        consts = dict(_fft_consts(l), nblk=nblk)
        n1, h1 = consts['n1'], consts['h1']
        spec_r, spec_i = _fft_spectrum(_hyena_filter(l, p, slow=n1).astype(BF16), consts, cb)
        conv = lambda y, ycol, order: _fft_conv(y, ycol * nblk, spec_r, spec_i, order * nblk, consts, cb)
    else:
        h1 = None
        k = np.arange(n)
        ang = -2.0 * np.pi * np.outer(k, k) / n
        fmat = jnp.asarray(_block2(np.cos(ang[:, :l]), np.sin(ang[:, :l])), BF16)
        gmat = jnp.asarray(_block2(np.cos(ang[:l, :]) / n, -np.sin(ang[:l, :]) / n), BF16)
        f_real = jnp.asarray(np.concatenate([np.cos(ang), np.sin(ang)], axis=0), F32)
        spec = _bmm(f_real[None], _hyena_filter(l, p).astype(BF16)[None], name="dense_spectrum")[0]
        spec_r, spec_i = spec[:n], spec[n:]
        conv = lambda y, ycol, order: _dense_conv(y, ycol * nblk, spec_r, spec_i, order * nblk, fmat, gmat,
                                                  nblk, cb)
    z = _short_conv(proj, col0, p['hy_conv_w'], p['hy_conv_b'], h1, cb)
    c0 = conv(z, HY_ORDER, 0)
    y1 = _hy_gate(z, 0, c0, z, HY_ORDER * nblk, p['hy_bias'][0], None, cb)
    c1 = conv(y1, 0, 1)
    return _hy_gate(z, nblk, c1, y1, 0, p['hy_bias'][1], h1, cb)


def _merge_kernel(x_ref, g_ref, sh_ref, sc_ref, g1_ref, br_ref, wg0_ref, wg1_ref, wg2_ref, bg_ref, wb_ref, wo_ref,
                  o_ref, h_scr, acc_scr):
    j = pl.program_id(2)

    @pl.when(j == 0)
    def _():
        h_scr[...] = _norm_mod(x_ref[...], g_ref[...], sh_ref[...], sc_ref[...]).astype(BF16)
        acc_scr[...] = jnp.zeros_like(acc_scr)

    h = h_scr[...]
    mixed = None
    for i, wg_ref in enumerate((wg0_ref, wg1_ref, wg2_ref)):
        gate = jax.nn.sigmoid(jnp.dot(h, wg_ref[...], preferred_element_type=F32) + bg_ref[i])
        term = gate * jnp.dot(br_ref[i], wb_ref[i], preferred_element_type=F32)
        mixed = term if mixed is None else mixed + term
    acc_scr[...] += jnp.dot(mixed.astype(BF16), wo_ref[...], preferred_element_type=F32)

    @pl.when(j == pl.num_programs(2) - 1)
    def _():
        o_ref[...] = x_ref[...] + g1_ref[...] * acc_scr[...]


def _merge(x, g, sh, sc, g1, branches, w_gate_bf, b_gate, w_branch_bf, w_out_bf, tm, tc):
    b, l, d = x.shape
    nbr, w, _ = w_branch_bf.shape
    nb = sh.shape[0]
    nj = d // tc
    mod_map = (lambda bi, i, j: (bi, 0, 0)) if nb > 1 else (lambda bi, i, j: (0, 0, 0))
    br = jnp.stack(branches, axis=1)
    bg = b_gate.reshape(nbr, nj, 1, tc).transpose(1, 0, 2, 3)
    return pl.pallas_call(
        _merge_kernel,
        grid=(b, l // tm, nj),
        in_specs=[pl.BlockSpec((None, tm, d), lambda bi, i, j: (bi, i, 0)),
                  pl.BlockSpec((1, d), lambda bi, i, j: (0, 0)),
                  pl.BlockSpec((None, 1, d), mod_map),
                  pl.BlockSpec((None, 1, d), mod_map),
                  pl.BlockSpec((None, 1, d), mod_map),
                  pl.BlockSpec((None, nbr, tm, w), lambda bi, i, j: (bi, 0, i, 0)),
                  pl.BlockSpec((d, tc), lambda bi, i, j: (0, j)),
                  pl.BlockSpec((d, tc), lambda bi, i, j: (0, nj + j)),
                  pl.BlockSpec((d, tc), lambda bi, i, j: (0, 2 * nj + j)),
                  pl.BlockSpec((None, nbr, 1, tc), lambda bi, i, j: (j, 0, 0, 0)),
                  pl.BlockSpec((nbr, w, tc), lambda bi, i, j: (0, 0, j)),
                  pl.BlockSpec((tc, d), lambda bi, i, j: (j, 0))],
        out_specs=pl.BlockSpec((None, tm, d), lambda bi, i, j: (bi, i, 0)),
        out_shape=jax.ShapeDtypeStruct((b, l, d), F32),
        scratch_shapes=[pltpu.VMEM((tm, d), BF16), pltpu.VMEM((tm, d), F32)],
        compiler_params=_cparams(("parallel", "parallel", "arbitrary"), vmem_mb=56),
        name="merge",
    )(x, g.reshape(1, d), sh, sc, g1, br, w_gate_bf, w_gate_bf, w_gate_bf, bg, w_branch_bf, w_out_bf)


def _router_kernel(x_ref, g_ref, sh_ref, sc_ref, rt_ref, h_ref, aff_ref):
    h = _norm_mod(x_ref[...], g_ref[...], sh_ref[...], sc_ref[...])
    h_ref[...] = h.astype(h_ref.dtype)
    nt = (((1,), (1,)), ((), ()))
    rt = rt_ref[...]
    r_hi = rt.astype(BF16)
    r_lo = (rt - r_hi.astype(F32)).astype(BF16)
    h_hi = h.astype(BF16)
    h_lo = (h - h_hi.astype(F32)).astype(BF16)
    logits = (lax.dot_general(r_hi, h_hi, nt, preferred_element_type=F32)
              + lax.dot_general(r_hi, h_lo, nt, preferred_element_type=F32)
              + lax.dot_general(r_lo, h_hi, nt, preferred_element_type=F32))
    e = jnp.exp(logits - jnp.max(logits, axis=0, keepdims=True))
    aff_ref[...] = e / jnp.sum(e, axis=0, keepdims=True)


def _router(x, g, sh, sc, router_t, tm):
    b, l, d = x.shape
    e = router_t.shape[0]
    nb = sh.shape[0]
    mod_map = (lambda bi, i: (bi, 0, 0)) if nb > 1 else (lambda bi, i: (0, 0, 0))
    return pl.pallas_call(
        _router_kernel,
        grid=(b, l // tm),
        in_specs=[pl.BlockSpec((None, tm, d), lambda bi, i: (bi, i, 0)),
                  pl.BlockSpec((1, d), lambda bi, i: (0, 0)),
                  pl.BlockSpec((None, 1, d), mod_map),
                  pl.BlockSpec((None, 1, d), mod_map),
                  pl.BlockSpec((e, d), lambda bi, i: (0, 0))],
        out_specs=[pl.BlockSpec((None, tm, d), lambda bi, i: (bi, i, 0)),
                   pl.BlockSpec((None, e, tm), lambda bi, i: (bi, 0, i))],
        out_shape=[jax.ShapeDtypeStruct((b, l, d), BF16), jax.ShapeDtypeStruct((b, e, l), F32)],
        compiler_params=_cparams(("parallel", "parallel")),
        name="router",
    )(x, g.reshape(1, d), sh, sc, router_t)


def _ffn_kernel(xs_ref, gate_ref, w1_ref, w3_ref, w2_ref, o_ref, acc_scr):
    f = pl.program_id(2)

    @pl.when(f == 0)
    def _():
        acc_scr[...] = jnp.zeros_like(acc_scr)

    xs = xs_ref[...]
    a = jnp.dot(xs, w1_ref[...].astype(BF16), preferred_element_type=F32)
    gl = jnp.dot(xs, w3_ref[...].astype(BF16), preferred_element_type=F32)
    mid = (a * jax.nn.sigmoid(a) * gl).astype(BF16)
    acc_scr[...] += jnp.dot(mid, w2_ref[...].astype(BF16), preferred_element_type=F32)

    @pl.when(f == pl.num_programs(2) - 1)
    def _():
        o_ref[...] = (acc_scr[...] * gate_ref[...]).astype(o_ref.dtype)


def _expert_ffn(xs, gate, w1, w3, w2, layer, tr, tf):
    e, r, d = xs.shape
    ff = w1.shape[3]
    return pl.pallas_call(
        _ffn_kernel,
        grid=(e, r // tr, ff // tf),
        in_specs=[pl.BlockSpec((None, tr, d), lambda ei, i, f: (ei, i, 0)),
                  pl.BlockSpec((None, tr, 1), lambda ei, i, f: (ei, i, 0)),
                  pl.BlockSpec((None, None, d, tf), lambda ei, i, f: (layer, ei, 0, f)),
                  pl.BlockSpec((None, None, d, tf), lambda ei, i, f: (layer, ei, 0, f)),
                  pl.BlockSpec((None, None, tf, d), lambda ei, i, f: (layer, ei, f, 0))],
        out_specs=pl.BlockSpec((None, tr, d), lambda ei, i, f: (ei, i, 0)),
        out_shape=jax.ShapeDtypeStruct((e, r, d), BF16),
        scratch_shapes=[pltpu.VMEM((tr, d), F32)],
        compiler_params=_cparams(("parallel", "parallel", "arbitrary"), vmem_mb=56),
        name="expert_ffn",
    )(xs, gate, w1, w3, w2)


def _route_kernel(aff_ref, pos_ref, idx_ref, csum_scr, *, cap, sblk):
    aff = aff_ref[...]
    e, l = aff.shape
    bits = pltpu.bitcast(aff, jnp.int32)
    count = lambda m: jnp.sum(jnp.where(m, 1.0, 0.0), axis=1, keepdims=True)

    def bisect(i, thr):
        cand = thr | lax.shift_left(jnp.int32(1), 30 - i)
        return jnp.where(count(bits >= cand) >= cap, cand, thr)

    thr = lax.fori_loop(0, 31, bisect, jnp.zeros((e, 1), jnp.int32))
    lane = lax.broadcasted_iota(jnp.int32, (e, l), 1)

    def cumsum(x):
        k = 1
        while k < l:
            x = x + jnp.where(lane >= k, pltpu.roll(x, k, 1), 0.0)
            k *= 2
        return x

    above = bits > thr
    tie = jnp.where(bits == thr, 1.0, 0.0)
    need = cap - count(above)
    keep = jnp.logical_or(above, jnp.logical_and(tie > 0.0, cumsum(tie) - tie < need))
    csum = cumsum(jnp.where(keep, 1.0, 0.0))
    slot1 = jnp.where(keep, csum, 0.0)
    hi = jnp.floor(slot1 * (1.0 / 32.0))
    lo = slot1 - 32.0 * hi
    eye = jnp.where(lax.broadcasted_iota(jnp.int32, (e, e), 0) == lax.broadcasted_iota(jnp.int32, (e, e), 1),
                    1.0, 0.0).astype(BF16)
    tn = (((0,), (0,)), ((), ()))
    pos_ref[...] = (32.0 * lax.dot_general(hi.astype(BF16), eye, tn, preferred_element_type=F32)
                    + lax.dot_general(lo.astype(BF16), eye, tn, preferred_element_type=F32) - 1.0).astype(jnp.int32)
    csum_scr[...] = csum
    expert_lane = lax.broadcasted_iota(jnp.int32, (sblk, e), 1)
    slots = [(lax.broadcasted_iota(jnp.int32, (sblk, 1), 0) + s0).astype(F32) for s0 in range(0, cap, sblk)]

    def per_expert(ei, accs):
        row = csum_scr[pl.ds(ei, 1), :]
        return tuple(jnp.where(expert_lane == ei, count(row <= s), a) for s, a in zip(slots, accs))

    accs = lax.fori_loop(0, e, per_expert, tuple(jnp.zeros((sblk, e), F32) for _ in slots))
    for i, a in enumerate(accs):
        idx_ref[i * sblk:(i + 1) * sblk, :] = a.astype(jnp.int32)


def _route(aff_t, cap):
    b, e, l = aff_t.shape
    return pl.pallas_call(
        functools.partial(_route_kernel, cap=cap, sblk=min(256, cap)),
        grid=(b,),
        in_specs=[pl.BlockSpec((None, e, l), lambda bi: (bi, 0, 0))],
        out_specs=[pl.BlockSpec((None, l, e), lambda bi: (bi, 0, 0)),
                   pl.BlockSpec((None, cap, e), lambda bi: (bi, 0, 0))],
        out_shape=[jax.ShapeDtypeStruct((b, l, e), jnp.int32), jax.ShapeDtypeStruct((b, cap, e), jnp.int32)],
        scratch_shapes=[pltpu.VMEM((e, l), F32)],
        compiler_params=_cparams(("parallel",)),
        name="moe_route",
    )(aff_t)


def _combine_kernel(st_ref, x_ref, g2_ref, pos_ref, *refs, n_exp, tiles, tn):
    y_refs, o_ref, oh_scr = refs[:n_exp], refs[n_exp], refs[n_exp + 1]
    bi, j = pl.program_id(0), pl.program_id(1)
    tt, win = oh_scr.shape[1], y_refs[0].shape[1]
    lane = lax.broadcasted_iota(jnp.int32, (tt, win), 1)
    main = min(win, tt)
    spill = []
    for e in range(n_exp):
        rel = pos_ref[:, e:e + 1] - st_ref[(bi * n_exp + e) * tiles + j]
        oh_scr[e] = jnp.where(lane == rel, 1.0, 0.0).astype(BF16)
        spill.append(jnp.max(rel) >= main)
    for c in range(o_ref.shape[1] // tn):
        cols = slice(c * tn, (c + 1) * tn)
        acc = jnp.dot(oh_scr[0, :, :main], y_refs[0][0, :main, cols], preferred_element_type=F32)
        for e in range(1, n_exp):
            acc = acc + jnp.dot(oh_scr[e, :, :main], y_refs[e][0, :main, cols], preferred_element_type=F32)
        o_ref[:, cols] = x_ref[:, cols] + g2_ref[:, cols] * acc
    if win > main:
        for e in range(n_exp):
            @pl.when(spill[e])
            def _(e=e):
                tail = jnp.dot(oh_scr[e, :, main:], y_refs[e][0, main:, :], preferred_element_type=F32)
                o_ref[...] += g2_ref[...] * tail


def _moe_combine(x, g2, pos, y, cap, tt):
    b, l, d = x.shape
    e = y.shape[0]
    tiles = l // tt
    win = min(tt + COMBINE_ALIGN, cap)
    sel = (pos >= 0).reshape(b, tiles, tt, e).sum(2)
    first = jnp.cumsum(sel, axis=1) - sel
    start = jnp.minimum(first // COMBINE_ALIGN * COMBINE_ALIGN, cap - win).astype(jnp.int32)
    start = jnp.transpose(start, (0, 2, 1)).reshape(-1)
    nb = g2.shape[0]

    def y_spec(ei):
        def y_map(bi, j, st):
            return ei, pl.multiple_of(bi * cap + st[(bi * e + ei) * tiles + j], COMBINE_ALIGN), 0
        return pl.BlockSpec((pl.Element(1), pl.Element(win), pl.Element(d)), y_map)

    grid_spec = pltpu.PrefetchScalarGridSpec(
        num_scalar_prefetch=1,
        grid=(b, tiles),
        in_specs=[pl.BlockSpec((None, tt, d), lambda bi, j, st: (bi, j, 0)),
                  pl.BlockSpec((None, 1, d), lambda bi, j, st: (bi if nb > 1 else 0, 0, 0)),
                  pl.BlockSpec((None, tt, e), lambda bi, j, st: (bi, j, 0))] + [y_spec(ei) for ei in range(e)],
        out_specs=pl.BlockSpec((None, tt, d), lambda bi, j, st: (bi, j, 0)),
        scratch_shapes=[pltpu.VMEM((e, tt, win), BF16)])
    return pl.pallas_call(
        functools.partial(_combine_kernel, n_exp=e, tiles=tiles, tn=min(256, d)),
        grid_spec=grid_spec,
        out_shape=jax.ShapeDtypeStruct((b, l, d), F32),
        compiler_params=_cparams(("parallel", "parallel"), vmem_mb=56),
        name="moe_combine",
    )(start, x, g2, pos, *([y] * e))


def _ec_moe_update(x, g, sh, sc, g2, p, tm):
    b, l, d = x.shape
    e = p['router'].shape[1]
    cap = max(1, EC_CAPACITY * l // e)
    h, aff_t = _router(x, g, sh, sc, jnp.transpose(p['router']).astype(F32), tm)
    pos_t, idx_ce = _route(aff_t, cap)
    pos = pos_t
    idx_t = jnp.transpose(idx_ce, (2, 0, 1))
    flat = (idx_t + (jnp.arange(b, dtype=jnp.int32) * l)[None, :, None]).reshape(-1)
    xs = jnp.take(h.reshape(b * l, d), flat, axis=0, mode="clip").reshape(e, b * cap, d)
    gt = jnp.take_along_axis(jnp.transpose(aff_t, (1, 0, 2)), idx_t, axis=2).reshape(e, b * cap, 1)
    tr = min(1024, b * cap)
    ff = p['exp_w1'].shape[3]
    y = _expert_ffn(xs, gt, p['exp_w1'], p['exp_w3'], p['exp_w2'], p['layer'], tr=tr, tf=min(512, ff))
    return _moe_combine(x, g2, pos, y, cap, min(256, l))


def _layer(x, xc, c, c_ctx, p, last):
    b, l, d = x.shape
    lc = xc.shape[1]
    w = p['ssm_d'].shape[0]
    hd = w // NA_HEADS

    cc = jnp.zeros((8, d), F32).at[:b].set(c).at[b].set(c_ctx)
    mod = _ada(cc, p['w_ada'], p['b_ada'], p['layer'])
    m_l = [t[:, None, :] for t in jnp.split(mod[:b], 6, axis=-1)]
    m_c = [t[:, None, :] for t in jnp.split(mod[b:b + 1], 6, axis=-1)]

    w_in = p['w_in'].astype(BF16)
    scale = hd ** -0.5
    head_gain = jnp.stack([jnp.tile(p['na_q_gain'].astype(F32) * scale, NA_HEADS),
                           jnp.tile(p['na_k_gain'].astype(F32), NA_HEADS)])[:, None, :]
    ones_blk = jnp.asarray(np.kron(np.eye(NA_HEADS), np.ones((hd, hd))), BF16)
    tm_l = min(512, l)
    tm_c = min(512, lc)
    proj = _in_proj(x, p['norm1'], m_l[0], m_l[1], w_in, head_gain, ones_blk, min(1024, l))
    projc = _in_proj(xc, p['norm1'], m_c[0], m_c[1], w_in, head_gain, ones_blk, tm_c)

    ssm_l, ssm_c = _s5_branch(projc[..., :w], proj[..., :w], p, _s5_tables(p), need_ctx=not last)

    na_l = _na_latent(proj[..., w:2 * w], proj[..., 2 * w:3 * w], proj[..., 3 * w:4 * w],
                      projc[..., 2 * w:3 * w], projc[..., 3 * w:4 * w], _na_bias_table(p['na_rpb']))
    hy_l = _hyena_branch(proj, 4 * w // 128, p)

    wg = p['w_gate'].astype(BF16)
    wb = p['w_branch'].astype(BF16)
    wo = p['w_out'].astype(BF16)
    x = _merge(x, p['norm1'], m_l[0], m_l[1], m_l[2], (ssm_l, na_l, hy_l), wg, p['b_gate'], wb, wo,
               tm=tm_l, tc=min(512, d))
    x = _ec_moe_update(x, p['norm2'], m_l[3], m_l[4], m_l[5], p, tm_l)
    if last:
        return x, None
    na_c = _na_context(projc[..., w:2 * w], projc[..., 2 * w:3 * w], projc[..., 3 * w:4 * w])
    hy_c = _hyena_branch(projc, 4 * w // 128, p)
    xc = _merge(xc, p['norm1'], m_c[0], m_c[1], m_c[2], (ssm_c, na_c, hy_c), wg, p['b_gate'], wb, wo,
                tm=tm_c, tc=min(512, d))
    xc = _ec_moe_update(xc, p['norm2'], m_c[3], m_c[4], m_c[5], p, tm_c)
    return x, xc


_PARAM_NAMES = ('w_ada', 'b_ada', 'norm1', 'norm2', 'w_in',
                'ssm_lam_re', 'ssm_lam_im', 'ssm_log_step', 'ssm_b_re', 'ssm_b_im', 'ssm_c_re', 'ssm_c_im',
                'ssm_d', 'ssm_w_glu', 'na_q_gain', 'na_k_gain', 'na_rpb',
                'hy_conv_w', 'hy_conv_b', 'hy_w1', 'hy_b1', 'hy_w2', 'hy_b2', 'hy_w3', 'hy_b3', 'hy_w4',
                'hy_freq', 'hy_bias', 'w_gate', 'b_gate', 'w_branch', 'w_out',
                'router', 'exp_w1', 'exp_w3', 'exp_w2')
_STACKED = ('w_ada', 'b_ada', 'exp_w1', 'exp_w3', 'exp_w2')


def kernel(x, c, ctx, c_ctx, w_ada, b_ada, norm1, norm2, w_in, ssm_lam_re, ssm_lam_im, ssm_log_step, ssm_b_re, ssm_b_im, ssm_c_re, ssm_c_im, ssm_d, ssm_w_glu, na_q_gain, na_k_gain, na_rpb, hy_conv_w, hy_conv_b, hy_w1, hy_b1, hy_w2, hy_b2, hy_w3, hy_b3, hy_w4, hy_freq, hy_bias, w_gate, b_gate, w_branch, w_out, router, exp_w1, exp_w3, exp_w2):
    stacked = (w_ada, b_ada, norm1, norm2, w_in, ssm_lam_re, ssm_lam_im, ssm_log_step, ssm_b_re, ssm_b_im,
               ssm_c_re, ssm_c_im, ssm_d, ssm_w_glu, na_q_gain, na_k_gain, na_rpb, hy_conv_w, hy_conv_b,
               hy_w1, hy_b1, hy_w2, hy_b2, hy_w3, hy_b3, hy_w4, hy_freq, hy_bias, w_gate, b_gate, w_branch,
               w_out, router, exp_w1, exp_w3, exp_w2)
    depth = w_ada.shape[0]
    xc = ctx
    for layer in range(depth):
        p = {name: (t if name in _STACKED else t[layer]) for name, t in zip(_PARAM_NAMES, stacked)}
        p['layer'] = layer
        x, xc = _layer(x, xc, c, c_ctx, p, layer == depth - 1)
    return x
```

```python
import functools
import math

import numpy as np
import jax
import jax.numpy as jnp
from jax import lax
from jax.experimental import pallas as pl
from jax.experimental.pallas import tpu as pltpu

BF16 = jnp.bfloat16
F32 = jnp.float32

NORM_EPS = 1e-6
GRID_W = 64
SSM_GROUP = 16
NA_HEADS = 8
NA_WIN_ROWS = 8
NA_WIN_COLS = 16
HY_ORDER = 2
HY_BANDS = 16
HY_MIN_DECAY = math.log(1e-2) / 1.5
HY_MAX_DECAY = math.log(1e-2) / 0.3
EC_CAPACITY = 2
S5_CHUNK = 16
HY_FAST = 128
HY_PAD = 8
MASK_VALUE = -1e30
COMBINE_ALIGN = 16

V7X_VMEM_BYTES = 64 * 1024 * 1024


def _cparams(semantics, vmem_mb=48):
    return pltpu.CompilerParams(dimension_semantics=semantics, vmem_limit_bytes=vmem_mb * 1024 * 1024)


def _norm_mod(x, g, sh, sc):
    ms = jnp.mean(x * x, axis=-1, keepdims=True)
    return (x * lax.rsqrt(ms + NORM_EPS) * g) * (1.0 + sc) + sh


def _ada_kernel(c_ref, w_ref, b_ref, o_ref):
    c = c_ref[...]
    a = (c * jax.nn.sigmoid(c)).astype(BF16)
    o_ref[...] = jnp.dot(a, w_ref[...].astype(BF16), preferred_element_type=F32) + b_ref[...]


def _ada(cc, w, b, layer):
    m, k = cc.shape
    depth, _, n = w.shape
    tn = next(t for t in (1024, 512, 256, 128) if n % t == 0)
    return pl.pallas_call(
        _ada_kernel,
        grid=(n // tn,),
        in_specs=[pl.BlockSpec((m, k), lambda j: (0, 0)),
                  pl.BlockSpec((None, k, tn), lambda j: (layer, 0, j)),
                  pl.BlockSpec((None, 1, tn), lambda j: (layer, 0, j))],
        out_specs=pl.BlockSpec((m, tn), lambda j: (0, j)),
        out_shape=jax.ShapeDtypeStruct((m, n), F32),
        compiler_params=_cparams(("arbitrary",)),
        name="ada",
    )(cc, w, b.reshape(depth, 1, n))


def _in_proj_kernel(x_ref, g_ref, sh_ref, sc_ref, w_ref, hg_ref, ones_ref, o_ref, h_scr, *, head_dim):
    j = pl.program_id(2)

    @pl.when(j == 0)
    def _():
        h_scr[...] = _norm_mod(x_ref[...], g_ref[...], sh_ref[...], sc_ref[...]).astype(BF16)

    acc = jnp.dot(h_scr[...], w_ref[...], preferred_element_type=F32)
    is_qk = jnp.logical_or(j == 1, j == 2)

    @pl.when(is_qk)
    def _():
        yy = acc * acc
        hi = yy.astype(BF16)
        lo = (yy - hi.astype(F32)).astype(BF16)
        ss = (jnp.dot(hi, ones_ref[...], preferred_element_type=F32)
              + jnp.dot(lo, ones_ref[...], preferred_element_type=F32))
        o_ref[...] = (acc * lax.rsqrt(ss * (1.0 / head_dim) + NORM_EPS) * hg_ref[...]).astype(o_ref.dtype)

    @pl.when(jnp.logical_not(is_qk))
    def _():
        o_ref[...] = acc.astype(o_ref.dtype)


def _in_proj(x, g, sh, sc, w_bf, head_gain, ones_blk, tm):
    b, l, d = x.shape
    n = w_bf.shape[1]
    tn = ones_blk.shape[0]
    nb = sh.shape[0]
    mod_map = (lambda bi, i, j: (bi, 0, 0)) if nb > 1 else (lambda bi, i, j: (0, 0, 0))
    return pl.pallas_call(
        functools.partial(_in_proj_kernel, head_dim=tn // NA_HEADS),
        grid=(b, l // tm, n // tn),
        in_specs=[pl.BlockSpec((None, tm, d), lambda bi, i, j: (bi, i, 0)),
                  pl.BlockSpec((1, d), lambda bi, i, j: (0, 0)),
                  pl.BlockSpec((None, 1, d), mod_map),
                  pl.BlockSpec((None, 1, d), mod_map),
                  pl.BlockSpec((d, tn), lambda bi, i, j: (0, j)),
                  pl.BlockSpec((None, 1, tn), lambda bi, i, j: (jnp.clip(j - 1, 0, 1), 0, 0)),
                  pl.BlockSpec((tn, tn), lambda bi, i, j: (0, 0))],
        out_specs=pl.BlockSpec((None, tm, tn), lambda bi, i, j: (bi, i, j)),
        out_shape=jax.ShapeDtypeStruct((b, l, n), BF16),
        scratch_shapes=[pltpu.VMEM((tm, d), BF16)],
        compiler_params=_cparams(("parallel", "parallel", "arbitrary")),
        name="in_proj",
    )(x, g.reshape(1, d), sh, sc, w_bf, head_gain, ones_blk)


def _bmm_kernel(a_ref, w_ref, o_ref):
    o_ref[...] = jnp.dot(a_ref[...].astype(BF16), w_ref[...], preferred_element_type=F32).astype(o_ref.dtype)


def _bmm_add_kernel(a_ref, w_ref, y_ref, o_ref):
    o_ref[...] = (y_ref[...] + jnp.dot(a_ref[...].astype(BF16), w_ref[...],
                                       preferred_element_type=F32)).astype(o_ref.dtype)


def _bmm(a, w, add=None, tm=None, name="bmm"):
    g, m, k = a.shape
    n = w.shape[2]
    tm = m if tm is None else tm
    in_specs = [pl.BlockSpec((None, tm, k), lambda gi, i: (gi, i, 0)),
                pl.BlockSpec((None, k, n), lambda gi, i: (gi, 0, 0))]
    args = [a, w]
    kern = _bmm_kernel
    if add is not None:
        in_specs.append(pl.BlockSpec((None, tm, n), lambda gi, i: (gi, i, 0)))
        args.append(add)
        kern = _bmm_add_kernel
    return pl.pallas_call(
        kern,
        grid=(g, m // tm),
        in_specs=in_specs,
        out_specs=pl.BlockSpec((None, tm, n), lambda gi, i: (gi, i, 0)),
        out_shape=jax.ShapeDtypeStruct((g, m, n), F32),
        compiler_params=_cparams(("parallel", "parallel")),
        name=name,
    )(*args)


def _gelu_tanh(x):
    return 0.5 * x * (1.0 + jnp.tanh(math.sqrt(2.0 / math.pi) * (x + 0.044715 * (x * x * x))))


def _s5_out_kernel(y_ref, u_ref, d_ref, w_ref, o_ref):
    y = y_ref[...].astype(F32) + d_ref[...] * u_ref[...].astype(F32)
    z = _gelu_tanh(y)
    gate = jax.nn.sigmoid(jnp.dot(z.astype(BF16), w_ref[...], preferred_element_type=F32))
    o_ref[...] = (z * gate).astype(o_ref.dtype)


def _s5_out(y, row0, u, d, w_glu_bf):
    b, l, w = u.shape
    tm = math.gcd(math.gcd(l, row0), 512) if row0 else math.gcd(l, 512)
    off = row0 // tm
    return pl.pallas_call(
        _s5_out_kernel,
        grid=(b, l // tm),
        in_specs=[pl.BlockSpec((None, tm, w), lambda bi, i: (bi, off + i, 0)),
                  pl.BlockSpec((None, tm, w), lambda bi, i: (bi, i, 0)),
                  pl.BlockSpec((1, w), lambda bi, i: (0, 0)),
                  pl.BlockSpec((w, w), lambda bi, i: (0, 0))],
        out_specs=pl.BlockSpec((None, tm, w), lambda bi, i: (bi, i, 0)),
        out_shape=jax.ShapeDtypeStruct((b, l, w), BF16),
        compiler_params=_cparams(("parallel", "parallel")),
        name="s5_out",
    )(y, u, d.reshape(1, w), w_glu_bf)


def _s5_tables(p):
    t = S5_CHUNK
    mats, sums, reads, steps = [], [], [], []
    for d in range(2):
        lam_re = p['ssm_lam_re'][d].astype(F32)
        lam_im = p['ssm_lam_im'][d].astype(F32)
        step = jnp.exp(p['ssm_log_step'][d].astype(F32))[:, None]
        mag = jnp.exp(lam_re * step)
        a_re = mag * jnp.cos(lam_im * step)
        a_im = mag * jnp.sin(lam_im * step)
        den = lam_re * lam_re + lam_im * lam_im
        num_re = a_re - 1.0
        coef_re = (num_re * lam_re + a_im * lam_im) / den
        coef_im = (a_im * lam_re - num_re * lam_im) / den
        b_re = p['ssm_b_re'][d].astype(F32)
        b_im = p['ssm_b_im'][d].astype(F32)
        bb_re = coef_re[..., None] * b_re - coef_im[..., None] * b_im
        bb_im = coef_re[..., None] * b_im + coef_im[..., None] * b_re
        c_re = p['ssm_c_re'][d].astype(F32)
        c_im = p['ssm_c_im'][d].astype(F32)
        n = jnp.arange(t + 1, dtype=F32)[:, None, None]
        pm = jnp.exp(lam_re * step * n)
        pw_re = pm * jnp.cos(lam_im * step * n)
        pw_im = pm * jnp.sin(lam_im * step * n)
        ca_re = c_re[None] * pw_re[:, :, None, :] - c_im[None] * pw_im[:, :, None, :]
        ca_im = c_re[None] * pw_im[:, :, None, :] + c_im[None] * pw_re[:, :, None, :]
        ktau = (jnp.einsum('ngop,gpi->ngoi', ca_re[:t], bb_re)
                - jnp.einsum('ngop,gpi->ngoi', ca_im[:t], bb_im))
        s_idx = np.arange(t)[:, None]
        t_idx = np.arange(t)[None, :]
        lag = (t_idx - s_idx) if d == 0 else (s_idx - t_idx)
        lag_sel = jnp.asarray(lag[:, :, None] == np.arange(t), F32)
        blk = jnp.einsum('stn,ngoi->gsito', lag_sel, ktau, precision=lax.Precision.HIGHEST)
        mats.append(blk)
        flip = (lambda a: a[::-1]) if d == 0 else (lambda a: a)
        unflip = (lambda a: a) if d == 0 else (lambda a: a[::-1])
        sp_re, sp_im = flip(pw_re[:t]), flip(pw_im[:t])
        sw_re = sp_re[:, :, :, None] * bb_re[None] - sp_im[:, :, :, None] * bb_im[None]
        sw_im = sp_re[:, :, :, None] * bb_im[None] + sp_im[:, :, :, None] * bb_re[None]
        sums.append((jnp.transpose(sw_re, (1, 0, 3, 2)), jnp.transpose(sw_im, (1, 0, 3, 2))))
        rd_re, rd_im = unflip(ca_re[1:t + 1]), unflip(ca_im[1:t + 1])
        reads.append((jnp.transpose(rd_re, (1, 3, 0, 2)), -jnp.transpose(rd_im, (1, 3, 0, 2))))
        steps.append((pw_re[t], pw_im[t]))
    g, pdim = steps[0][0].shape
    k = SSM_GROUP
    mtot = (mats[0] + mats[1]).reshape(g, t * k, t * k)
    ssum = jnp.concatenate([sums[0][0], sums[0][1], sums[1][0], sums[1][1]], axis=-1).reshape(g, t * k, 4 * pdim)
    r = jnp.concatenate([reads[0][0], reads[0][1], reads[1][0], reads[1][1]], axis=1).reshape(g, 4 * pdim, t * k)
    tk = t * k
    eye = jnp.eye(2, dtype=F32)
    mt6 = jnp.transpose(mtot.reshape(g // 2, 2, t, k, t, k), (0, 2, 1, 3, 4, 5))
    ymat = (mt6[:, :, :, :, :, None, :] * eye[None, None, :, None, None, :, None]).reshape(g // 2, 2 * tk, 2 * tk)
    ss6 = jnp.transpose(ssum.reshape(g // 2, 2, t, k, 4, pdim), (0, 2, 1, 3, 4, 5))
    smat = (ss6[:, :, :, :, :, None, :] * eye[None, None, :, None, None, :, None]).reshape(g // 2, 2 * tk, 8 * pdim)
    w1 = jnp.concatenate([ymat, smat], axis=-1)
    r6 = jnp.transpose(r.reshape(g // 2, 2, 4, pdim, t, k), (0, 2, 1, 3, 4, 5))
    r2 = (r6[:, :, :, :, :, None, :] * eye[None, None, :, None, None, :, None]).reshape(g // 2, 8 * pdim, 2 * tk)
    coef = [c.reshape(1, g * pdim) for c in (steps[0][0], steps[0][1], steps[1][0], steps[1][1])]
    return w1.astype(BF16), r2.astype(BF16), coef


LANES = 128


def _to_groups_kernel(u_ref, o_ref, u_scr, o_scr, *, cbk, bsz):
    t, k2 = S5_CHUNK, 2 * SSM_GROUP
    per = LANES // k2
    for b in range(bsz):
        for lt in range(u_scr.shape[0]):
            u_scr[lt] = u_ref[b, :, lt * LANES:(lt + 1) * LANES].astype(F32)
        for lt in range(u_scr.shape[0]):
            xs = [u_scr[lt, pl.ds(s, cbk, stride=t), :] for s in range(t)]
            for q in range(per):
                v = jnp.concatenate([x[:, q * k2:(q + 1) * k2] for x in xs], axis=1)
                for ot in range(o_scr.shape[1]):
                    o_scr[lt * per + q, ot, pl.ds(b, cbk, stride=bsz), :] = v[:, ot * LANES:(ot + 1) * LANES]
    for gp in range(o_scr.shape[0]):
        for ot in range(o_scr.shape[1]):
            o_ref[gp, :, ot * LANES:(ot + 1) * LANES] = o_scr[gp, ot].astype(o_ref.dtype)


def _from_groups_kernel(y_ref, o_ref, y_scr, o_scr, *, cbk, bsz):
    t, k2 = S5_CHUNK, 2 * SSM_GROUP
    per = LANES // k2
    ngp, nlt = y_scr.shape[0], y_scr.shape[1]
    for gp in range(ngp):
        for ot in range(nlt):
            y_scr[gp, ot] = y_ref[gp, :, ot * LANES:(ot + 1) * LANES].astype(F32)
    for b in range(bsz):
        for ot in range(nlt):
            zs = [y_scr[gp, ot, pl.ds(b, cbk, stride=bsz), :] for gp in range(ngp)]
            for q in range(per):
                s = ot * per + q
                v = jnp.concatenate([z[:, q * k2:(q + 1) * k2] for z in zs], axis=1)
                for lt in range(o_scr.shape[0]):
                    o_scr[lt, pl.ds(s, cbk, stride=t), :] = v[:, lt * LANES:(lt + 1) * LANES]
        for lt in range(o_scr.shape[0]):
            o_ref[b, :, lt * LANES:(lt + 1) * LANES] = o_scr[lt].astype(o_ref.dtype)


def _chunk_block(nc, bsz):
    return next(c for c in range(min(64, nc), 0, -1) if nc % c == 0 and (c * bsz) % 16 == 0)


def _to_groups(u):
    bsz, ltot, w = u.shape
    t, k2 = S5_CHUNK, 2 * SSM_GROUP
    nc = ltot // t
    cbk = _chunk_block(nc, bsz)
    ngp = w // k2
    return pl.pallas_call(
        functools.partial(_to_groups_kernel, cbk=cbk, bsz=bsz),
        grid=(nc // cbk,),
        in_specs=[pl.BlockSpec((bsz, cbk * t, w), lambda i: (0, i, 0))],
        out_specs=pl.BlockSpec((ngp, cbk * bsz, t * k2), lambda i: (0, i, 0)),
        out_shape=jax.ShapeDtypeStruct((ngp, nc * bsz, t * k2), BF16),
        scratch_shapes=[pltpu.VMEM((w // LANES, cbk * t, LANES), F32),
                        pltpu.VMEM((ngp, t * k2 // LANES, cbk * bsz, LANES), F32)],
        compiler_params=_cparams(("parallel",)),
        name="s5_to_groups",
    )(u)


def _from_groups(y, bsz):
    ngp, m, tk2 = y.shape
    t, k2 = S5_CHUNK, 2 * SSM_GROUP
    nc = m // bsz
    cbk = _chunk_block(nc, bsz)
    w = ngp * k2
    return pl.pallas_call(
        functools.partial(_from_groups_kernel, cbk=cbk, bsz=bsz),
        grid=(nc // cbk,),
        in_specs=[pl.BlockSpec((ngp, cbk * bsz, tk2), lambda i: (0, i, 0))],
        out_specs=pl.BlockSpec((bsz, cbk * t, w), lambda i: (0, i, 0)),
        out_shape=jax.ShapeDtypeStruct((bsz, nc * t, w), BF16),
        scratch_shapes=[pltpu.VMEM((ngp, tk2 // LANES, cbk * bsz, LANES), F32),
                        pltpu.VMEM((w // LANES, cbk * t, LANES), F32)],
        compiler_params=_cparams(("parallel",)),
        name="s5_from_groups",
    )(y)


def _s5_chunk_kernel(u_ref, w_ref, y_ref, xfr_ref, xfi_ref, xbr_ref, xbi_ref):
    r = jnp.dot(u_ref[...], w_ref[...], preferred_element_type=F32)
    ny = y_ref.shape[-1]
    nx = xfr_ref.shape[-1]
    y_ref[...] = r[:, :ny]
    for i, ref in enumerate((xfr_ref, xfi_ref, xbr_ref, xbi_ref)):
        ref[...] = r[:, ny + i * nx:ny + (i + 1) * nx]


def _s5_chunk(ug, w1):
    gp, m, tk2 = ug.shape
    nx = (w1.shape[2] - tk2) // 4
    xspec = pl.BlockSpec((m, nx), lambda g: (0, g))
    xshape = jax.ShapeDtypeStruct((m, gp * nx), F32)
    return pl.pallas_call(
        _s5_chunk_kernel,
        grid=(gp,),
        in_specs=[pl.BlockSpec((None, m, tk2), lambda g: (g, 0, 0)),
                  pl.BlockSpec((None, tk2, w1.shape[2]), lambda g: (g, 0, 0))],
        out_specs=[pl.BlockSpec((None, m, tk2), lambda g: (g, 0, 0)), xspec, xspec, xspec, xspec],
        out_shape=[jax.ShapeDtypeStruct((gp, m, tk2), F32), xshape, xshape, xshape, xshape],
        compiler_params=_cparams(("parallel",)),
        name="s5_chunk",
    )(ug, w1)


def _s5_scan_kernel(xfr, xfi, xbr, xbi, afr, afi, abr, abi, hfr, hfi, hbr, hbi, *, tiles_ctx, tiles, bsz):
    shape = (2 * bsz, xfr.shape[1])
    low = lax.broadcasted_iota(jnp.int32, shape, 0) < bsz
    zero = jnp.zeros(shape, F32)

    def step(hr, hi, ar, ai, xr, xi):
        return ar * hr - ai * hi + xr, ar * hi + ai * hr + xi

    def fwd_tile(i, carry):
        ar, ai = jnp.broadcast_to(afr[...], shape), jnp.broadcast_to(afi[...], shape)
        r0 = pl.multiple_of(i * 2 * bsz, 2 * bsz)
        xr, xi = xfr[pl.ds(r0, 2 * bsz), :], xfi[pl.ds(r0, 2 * bsz), :]
        sr, si = pltpu.roll(carry[0], bsz, 0), pltpu.roll(carry[1], bsz, 0)
        tr, ti = step(sr, si, ar, ai, xr, xi)
        t2r, t2i = pltpu.roll(tr, bsz, 0), pltpu.roll(ti, bsz, 0)
        hfr[pl.ds(r0, 2 * bsz), :] = jnp.where(low, sr, t2r)
        hfi[pl.ds(r0, 2 * bsz), :] = jnp.where(low, si, t2i)
        return step(t2r, t2i, ar, ai, xr, xi)

    lax.fori_loop(0, tiles, fwd_tile, (zero, zero))

    def bwd_tile(i, carry):
        ar, ai = jnp.broadcast_to(abr[...], shape), jnp.broadcast_to(abi[...], shape)
        r0 = pl.multiple_of(i * 2 * bsz, 2 * bsz)
        xr, xi = xbr[pl.ds(r0, 2 * bsz), :], xbi[pl.ds(r0, 2 * bsz), :]
        sr, si = pltpu.roll(carry[0], bsz, 0), pltpu.roll(carry[1], bsz, 0)
        tr, ti = step(sr, si, ar, ai, xr, xi)
        t2r, t2i = pltpu.roll(tr, bsz, 0), pltpu.roll(ti, bsz, 0)
        hbr[pl.ds(r0, 2 * bsz), :] = jnp.where(low, t2r, sr)
        hbi[pl.ds(r0, 2 * bsz), :] = jnp.where(low, t2i, si)
        return step(t2r, t2i, ar, ai, xr, xi)

    carry = lax.fori_loop(0, tiles_ctx, lambda j, c: bwd_tile(tiles_ctx - 1 - j, c), (zero, zero))
    lax.fori_loop(0, tiles - tiles_ctx, lambda j, c: bwd_tile(tiles - 1 - j, c), carry)


def _s5_scan(xs, coef, chunks_ctx, chunks, bsz):
    m, lanes = xs[0].shape
    assert 2 * bsz == 8 and chunks % 2 == 0 and chunks_ctx % 2 == 0
    lb = min(256, lanes)
    xspec = pl.BlockSpec((m, lb), lambda j: (0, j))
    cspec = pl.BlockSpec((1, lb), lambda j: (0, j))
    shape = jax.ShapeDtypeStruct((m, lanes), F32)
    return pl.pallas_call(
        functools.partial(_s5_scan_kernel, tiles_ctx=chunks_ctx // 2, tiles=chunks // 2, bsz=bsz),
        grid=(lanes // lb,),
        in_specs=[xspec] * 4 + [cspec] * 4,
        out_specs=[xspec] * 4,
        out_shape=[shape] * 4,
        compiler_params=_cparams(("parallel",)),
        name="s5_scan",
    )(*xs, *coef)


def _s5_readout_kernel(y_ref, hfr, hfi, hbr, hbi, r_ref, o_ref):
    h = jnp.concatenate([hfr[...], hfi[...], hbr[...], hbi[...]], axis=1).astype(BF16)
    o_ref[...] = (y_ref[...] + jnp.dot(h, r_ref[...], preferred_element_type=F32)).astype(o_ref.dtype)


def _s5_readout(y0, hs, r2):
    gp, m, tk2 = y0.shape
    nx = hs[0].shape[1] // gp
    hspec = pl.BlockSpec((m, nx), lambda g: (0, g))
    yspec = pl.BlockSpec((None, m, tk2), lambda g: (g, 0, 0))
    return pl.pallas_call(
        _s5_readout_kernel,
        grid=(gp,),
        in_specs=[yspec, hspec, hspec, hspec, hspec, pl.BlockSpec((None, 4 * nx, tk2), lambda g: (g, 0, 0))],
        out_specs=yspec,
        out_shape=jax.ShapeDtypeStruct((gp, m, tk2), BF16),
        compiler_params=_cparams(("parallel",)),
        name="s5_readout",
    )(y0, *hs, r2)


def _s5_branch(u_ctx, u_lat, p, tables, need_ctx):
    w1, r2, coef = tables
    b, l, w = u_lat.shape
    lc = u_ctx.shape[1]
    t, k = S5_CHUNK, SSM_GROUP
    g = w // k
    nc = (lc + l) // t
    m = nc * b
    ug = _to_groups(jnp.concatenate([u_ctx, u_lat], axis=1))
    y0, *xs = _s5_chunk(ug, w1)
    hs = _s5_scan(xs, coef, lc // t, nc, b)
    y = _from_groups(_s5_readout(y0, hs, r2), b)
    w_glu = p['ssm_w_glu'].astype(BF16)
    out_l = _s5_out(y, lc, u_lat, p['ssm_d'], w_glu)
    out_c = _s5_out(y, 0, u_ctx, p['ssm_d'], w_glu) if need_ctx else None
    return out_l, out_c


def _na_kernel(q_ref, k_ref, v_ref, kc_ref, vc_ref, bias_ref, o_ref, s_scr, p_scr, *, rows, head_dim, rps):
    kr = NA_WIN_ROWS
    kc = kc_ref[...]
    vc = vc_ref[...]
    nt = (((1,), (1,)), ((), ()))
    lane_tile = 2 * head_dim
    nwin = kr * GRID_W
    first_head = lax.broadcasted_iota(jnp.int32, (GRID_W, lane_tile), 1) < head_dim
    vwins = []
    for sub in range(rps):
        r = pl.program_id(1) * rps + sub
        start = jnp.clip(r - kr // 2, 0, rows - kr)
        cls = r - start
        base = pl.multiple_of(start * GRID_W, GRID_W)
        kwin = k_ref[pl.ds(base, nwin), :]
        vwins.append(v_ref[pl.ds(base, nwin), :])
        q = q_ref[sub * GRID_W:(sub + 1) * GRID_W, :]
        for h in range(NA_HEADS):
            sl = slice(h // 2 * lane_tile, (h // 2 + 1) * lane_tile)
            qh = jnp.where(first_head == (h % 2 == 0), q[:, sl], jnp.zeros_like(q[:, sl]))
            i = sub * NA_HEADS + h
            s_scr[i, :, :nwin] = lax.dot_general(qh, kwin[:, sl], nt, preferred_element_type=F32) + bias_ref[cls, h]
            s_scr[i, :, nwin:] = lax.dot_general(qh, kc[:, sl], nt, preferred_element_type=F32)
    inv = []
    for i in range(rps * NA_HEADS):
        s = s_scr[i]
        pr = jnp.exp(s - jnp.max(s, axis=-1, keepdims=True))
        inv.append(1.0 / jnp.sum(pr, axis=-1, keepdims=True))
        p_scr[i] = pr.astype(BF16)
    for sub in range(rps):
        outs = []
        for pair in range(NA_HEADS // 2):
            sl = slice(pair * lane_tile, (pair + 1) * lane_tile)
            halves = []
            for half in range(2):
                i = sub * NA_HEADS + 2 * pair + half
                o = (jnp.dot(p_scr[i, :, :nwin], vwins[sub][:, sl], preferred_element_type=F32)
                     + jnp.dot(p_scr[i, :, nwin:], vc[:, sl], preferred_element_type=F32))
                halves.append(o * inv[i])
            outs.append(jnp.where(first_head, halves[0], halves[1]))
        o_ref[sub * GRID_W:(sub + 1) * GRID_W, :] = jnp.concatenate(outs, axis=-1).astype(o_ref.dtype)


def _na_bias_table(rpb):
    kr, kw, w = NA_WIN_ROWS, NA_WIN_COLS, GRID_W
    col = np.arange(w)
    cstart = np.clip(col - kw // 2, 0, w - kw)
    kcol = np.arange(w)
    inwin = (kcol[None, :] >= cstart[:, None]) & (kcol[None, :] < cstart[:, None] + kw)
    dc = np.clip(kcol[None, :] - col[:, None] + (kw - 1), 0, 2 * kw - 2)
    rpb = rpb.astype(F32)
    dr = np.arange(kr)[None, :] - np.arange(kr)[:, None] + (kr - 1)
    row_sel = jnp.asarray(dr[:, :, None] == np.arange(2 * kr - 1), F32)
    col_sel = jnp.asarray((dc[:, :, None] == np.arange(2 * kw - 1)) & inwin[:, :, None], F32)
    hp = lax.Precision.HIGHEST
    rows = jnp.einsum('hab,oja->ohjb', rpb, row_sel, precision=hp)
    bias = jnp.einsum('ohjb,cdb->ohcjd', rows, col_sel, precision=hp)
    bias = jnp.where(jnp.asarray(inwin)[None, None, :, None, :], bias, MASK_VALUE)
    return bias.reshape(kr, rpb.shape[0], w, kr * w)


def _na_latent(q, k, v, kc, vc, bias_tab):
    b, l, w = q.shape
    lc = kc.shape[1]
    rows = l // GRID_W
    assert rows >= NA_WIN_ROWS
    nk = NA_WIN_ROWS * GRID_W
    rps = 2
    assert rows % rps == 0
    return pl.pallas_call(
        functools.partial(_na_kernel, rows=rows, head_dim=w // NA_HEADS, rps=rps),
        grid=(b, rows // rps),
        in_specs=[pl.BlockSpec((None, rps * GRID_W, w), lambda bi, r: (bi, r, 0)),
                  pl.BlockSpec((None, l, w), lambda bi, r: (bi, 0, 0)),
                  pl.BlockSpec((None, l, w), lambda bi, r: (bi, 0, 0)),
                  pl.BlockSpec((None, lc, w), lambda bi, r: (bi, 0, 0)),
                  pl.BlockSpec((None, lc, w), lambda bi, r: (bi, 0, 0)),
                  pl.BlockSpec((NA_WIN_ROWS, NA_HEADS, GRID_W, nk), lambda bi, r: (0, 0, 0, 0))],
        out_specs=pl.BlockSpec((None, rps * GRID_W, w), lambda bi, r: (bi, r, 0)),
        out_shape=jax.ShapeDtypeStruct((b, l, w), BF16),
        scratch_shapes=[pltpu.VMEM((rps * NA_HEADS, GRID_W, nk + lc), F32),
                        pltpu.VMEM((rps * NA_HEADS, GRID_W, nk + lc), BF16)],
        compiler_params=_cparams(("parallel", "arbitrary"), vmem_mb=56),
        name="na_latent",
    )(q, k, v, kc, vc, bias_tab)


def _na_ctx_kernel(q_ref, k_ref, v_ref, o_ref, *, head_dim):
    q, k, v = q_ref[...], k_ref[...], v_ref[...]
    outs = []
    for h in range(NA_HEADS):
        sl = slice(h * head_dim, (h + 1) * head_dim)
        s = lax.dot_general(q[:, sl], k[:, sl], (((1,), (1,)), ((), ())), preferred_element_type=F32)
        pr = jnp.exp(s - jnp.max(s, axis=-1, keepdims=True))
        den = jnp.sum(pr, axis=-1, keepdims=True)
        outs.append(jnp.dot(pr.astype(BF16), v[:, sl], preferred_element_type=F32) / den)
    o_ref[...] = jnp.concatenate(outs, axis=-1).astype(o_ref.dtype)


def _na_context(q, k, v):
    b, lc, w = q.shape
    spec = pl.BlockSpec((None, lc, w), lambda bi: (bi, 0, 0))
    return pl.pallas_call(
        functools.partial(_na_ctx_kernel, head_dim=w // NA_HEADS),
        grid=(b,),
        in_specs=[spec, spec, spec],
        out_specs=spec,
        out_shape=jax.ShapeDtypeStruct((b, lc, w), BF16),
        compiler_params=_cparams(("parallel",)),
        name="na_context",
    )(q, k, v)


def _hyena_filter(length, p, slow=None):
    n = 2 * length
    pos = np.arange(n)
    if slow is not None:
        pos = pos.reshape(slow, n // slow).T.reshape(-1)
    off = np.where(pos < length, pos, n - pos)
    backward = jnp.asarray(pos > length)[:, None, None]
    live = jnp.asarray(pos != length, F32)[:, None, None]
    tau = jnp.asarray(np.minimum(off, length - 1), F32)[:, None]
    t = tau / (length - 1)
    freqs = jnp.linspace(1e-4, HY_BANDS - 1, HY_BANDS, dtype=F32)
    ang = (2.0 * math.pi / length) * tau * freqs[None, :]
    z = jnp.concatenate([t, jnp.cos(ang), -jnp.sin(ang)], axis=-1)
    fr = p['hy_freq'].astype(F32)
    hp = lax.Precision.HIGH
    h = jnp.sin(fr * (jnp.dot(z, p['hy_w1'].astype(F32), precision=hp) + p['hy_b1'].astype(F32)))
    h = jnp.sin(fr * (jnp.dot(h, p['hy_w2'].astype(F32), precision=hp) + p['hy_b2'].astype(F32)))
    h = jnp.sin(fr * (jnp.dot(h, p['hy_w3'].astype(F32), precision=hp) + p['hy_b3'].astype(F32)))
    wdim = p['hy_bias'].shape[-1]
    h = jnp.dot(h, p['hy_w4'].astype(F32), precision=hp).reshape(n, HY_ORDER, 2, wdim)
    decay = jnp.exp(-t * jnp.abs(jnp.linspace(HY_MIN_DECAY, HY_MAX_DECAY, wdim, dtype=F32)))
    filt = jnp.where(backward, h[:, :, 1], h[:, :, 0]) * (decay[:, None, :] * live)
    filt = filt / jnp.sum(jnp.abs(filt), axis=0, keepdims=True)
    return filt.reshape(n, HY_ORDER * wdim)


def _block2(re, im):
    return np.block([[re, -im], [im, re]])


def _fft_consts(length):
    n = 2 * length
    n2 = HY_FAST
    n1 = n // n2
    h1 = n1 // 2
    a1 = -2.0 * np.pi * np.outer(np.arange(n1), np.arange(n1)) / n1
    a2 = -2.0 * np.pi * np.outer(np.arange(n2), np.arange(n2)) / n2
    f1 = _block2(np.cos(a1[:, :h1]), np.sin(a1[:, :h1]))
    f1_real = np.concatenate([np.cos(a1), np.sin(a1)], axis=0)
    f3 = _block2(np.cos(a2), np.sin(a2))
    g3 = _block2(np.cos(a2), -np.sin(a2))
    g1 = _block2(np.cos(a1[:h1, :]) / n, -np.sin(a1[:h1, :]) / n)
    wa = -2.0 * np.pi * np.arange(n2) / n
    wr = np.broadcast_to(np.cos(wa)[:, None], (n2, 128))
    wi = np.broadcast_to(np.sin(wa)[:, None], (n2, 128))
    cast = lambda m: jnp.asarray(m, BF16)
    return dict(n1=n1, h1=h1, f1=cast(f1), f1_real=cast(f1_real), f3=cast(f3), g3=cast(g3), g1=cast(g1),
                wr=jnp.asarray(wr, F32), wi=jnp.asarray(wi, F32))


def _fft_first_stage(a_r, a_i, load_cols, f1_ref, n1, cb):
    slab = HY_FAST + HY_PAD

    def body(i, carry):
        n2 = 2 * i
        z = jnp.concatenate([load_cols(n2), load_cols(n2 + 1)], axis=1)
        r = jnp.dot(f1_ref[...], z, preferred_element_type=F32)
        a_r[pl.ds(n2, n1, stride=slab), :] = r[:n1, :cb]
        a_i[pl.ds(n2, n1, stride=slab), :] = r[n1:, :cb]
        a_r[pl.ds(n2 + 1, n1, stride=slab), :] = r[:n1, cb:]
        a_i[pl.ds(n2 + 1, n1, stride=slab), :] = r[n1:, cb:]
        return carry

    lax.fori_loop(0, HY_FAST // 2, body, 0, unroll=4)


def _fft_slab_loop(a_r, a_i, f3_ref, wr_ref, wi_ref, n1, cb, finish):
    slab = HY_FAST + HY_PAD
    n2 = HY_FAST
    wr, wi = wr_ref[...], wi_ref[...]
    if cb != wr.shape[1]:
        wr, wi = jnp.tile(wr, (1, cb // wr.shape[1])), jnp.tile(wi, (1, cb // wi.shape[1]))

    def body(i, carry):
        tr, ti = carry
        k1 = 2 * i
        r0 = pl.multiple_of(k1 * slab, 8)
        r1 = pl.multiple_of((k1 + 1) * slab, 8)
        tr1, ti1 = tr * wr - ti * wi, tr * wi + ti * wr
        ar0, ai0 = a_r[pl.ds(r0, n2), :], a_i[pl.ds(r0, n2), :]
        ar1, ai1 = a_r[pl.ds(r1, n2), :], a_i[pl.ds(r1, n2), :]
        xr = jnp.concatenate([ar0 * tr - ai0 * ti, ar1 * tr1 - ai1 * ti1], axis=1)
        xi = jnp.concatenate([ar0 * ti + ai0 * tr, ar1 * ti1 + ai1 * tr1], axis=1)
        bc = jnp.dot(f3_ref[...], jnp.concatenate([xr, xi], axis=0).astype(BF16), preferred_element_type=F32)
        finish(k1, (r0, r1), (tr, ti, tr1, ti1), bc[:n2], bc[n2:])
        return tr1 * wr - ti1 * wi, tr1 * wi + ti1 * wr

    one = jnp.ones((n2, cb), F32)
    lax.fori_loop(0, n1 // 2, body, (one, jnp.zeros_like(one)), unroll=2)


def _fft_conv_kernel(y_ref, sr_ref, si_ref, f1_ref, f3_ref, g3_ref, g1_ref, wr_ref, wi_ref, o_ref, a_r, a_i,
                     *, n1):
    h1 = n1 // 2
    n2 = HY_FAST
    cb = o_ref.shape[-1]
    slab = HY_FAST + HY_PAD

    def load_cols(j):
        r = pl.multiple_of(j * h1, h1)
        return jnp.concatenate([y_ref[0, pl.ds(r, h1), :], y_ref[1, pl.ds(r, h1), :]], axis=0)

    _fft_first_stage(a_r, a_i, load_cols, f1_ref, n1, cb)

    def finish(k1, rows, tw, br, bi):
        s0 = pl.multiple_of(k1 * n2, n2)
        s1 = pl.multiple_of((k1 + 1) * n2, n2)
        sr = jnp.concatenate([sr_ref[pl.ds(s0, n2), :], sr_ref[pl.ds(s1, n2), :]], axis=1).astype(F32)
        si = jnp.concatenate([si_ref[pl.ds(s0, n2), :], si_ref[pl.ds(s1, n2), :]], axis=1).astype(F32)
        y = jnp.concatenate([br * sr - bi * si, br * si + bi * sr], axis=0).astype(BF16)
        ac = jnp.dot(g3_ref[...], y, preferred_element_type=F32)
        pr, pi = ac[:n2], ac[n2:]
        for j in range(2):
            tr, ti = tw[2 * j], tw[2 * j + 1]
            prj, pij = pr[:, j * cb:(j + 1) * cb], pi[:, j * cb:(j + 1) * cb]
            a_r[pl.ds(rows[j], n2), :] = prj * tr + pij * ti
            a_i[pl.ds(rows[j], n2), :] = pij * tr - prj * ti

    _fft_slab_loop(a_r, a_i, f3_ref, wr_ref, wi_ref, n1, cb, finish)

    def out_body(i, carry):
        j0 = 2 * i
        cols = []
        for j in (j0, j0 + 1):
            cols.append(jnp.concatenate([a_r[pl.ds(j, n1, stride=slab), :], a_i[pl.ds(j, n1, stride=slab), :]],
                                        axis=0))
        o = jnp.dot(g1_ref[...], jnp.concatenate(cols, axis=1).astype(BF16), preferred_element_type=F32)
        for jj in range(2):
            r = pl.multiple_of((j0 + jj) * h1, h1)
            o_ref[0, pl.ds(r, h1), :] = o[:h1, jj * cb:(jj + 1) * cb].astype(o_ref.dtype)
            o_ref[1, pl.ds(r, h1), :] = o[h1:, jj * cb:(jj + 1) * cb].astype(o_ref.dtype)
        return carry

    lax.fori_loop(0, n2 // 2, out_body, 0, unroll=4)


def _fft_conv(yp, col0, spec_r, spec_i, scol0, consts, cb=128):
    b, l, _ = yp.shape
    n = spec_r.shape[0]
    n1 = consts['n1']
    slab = HY_FAST + HY_PAD
    nblk = consts['nblk']
    y4 = yp.reshape(b // 2, 2, l, yp.shape[-1])
    const = lambda a: pl.BlockSpec(a.shape, lambda j, pi: (0, 0))
    return pl.pallas_call(
        functools.partial(_fft_conv_kernel, n1=n1),
        grid=(nblk, b // 2),
        in_specs=[pl.BlockSpec((None, 2, l, cb), lambda j, pi: (pi, 0, 0, col0 + j)),
                  pl.BlockSpec((n, cb), lambda j, pi: (0, scol0 + j)),
                  pl.BlockSpec((n, cb), lambda j, pi: (0, scol0 + j)),
                  const(consts['f1']), const(consts['f3']), const(consts['g3']), const(consts['g1']),
                  const(consts['wr']), const(consts['wi'])],
        out_specs=pl.BlockSpec((None, 2, l, cb), lambda j, pi: (pi, 0, 0, j)),
        out_shape=jax.ShapeDtypeStruct((b // 2, 2, l, nblk * cb), BF16),
        scratch_shapes=[pltpu.VMEM((n1 * slab, cb), F32), pltpu.VMEM((n1 * slab, cb), F32)],
        compiler_params=_cparams(("arbitrary", "arbitrary"), vmem_mb=58),
        name="fft_conv",
    )(y4, spec_r, spec_i, consts['f1'], consts['f3'], consts['g3'], consts['g1'],
      consts['wr'], consts['wi']).reshape(b, l, nblk * cb)


def _fft_spec_kernel(f_ref, f1_ref, f3_ref, wr_ref, wi_ref, sr_ref, si_ref, a_r, a_i, *, n1):
    n2 = HY_FAST
    cb = sr_ref.shape[-1]

    def load_cols(j):
        return f_ref[pl.ds(pl.multiple_of(j * n1, n1), n1), :]

    _fft_first_stage(a_r, a_i, load_cols, f1_ref, n1, cb)

    def finish(k1, rows, tw, br, bi):
        for j in range(2):
            s = pl.multiple_of((k1 + j) * n2, n2)
            sr_ref[pl.ds(s, n2), :] = br[:, j * cb:(j + 1) * cb].astype(sr_ref.dtype)
            si_ref[pl.ds(s, n2), :] = bi[:, j * cb:(j + 1) * cb].astype(si_ref.dtype)

    _fft_slab_loop(a_r, a_i, f3_ref, wr_ref, wi_ref, n1, cb, finish)


def _fft_spectrum(filt_p, consts, cb=128):
    n, c = filt_p.shape
    n1 = consts['n1']
    slab = HY_FAST + HY_PAD
    const = lambda a: pl.BlockSpec(a.shape, lambda j: (0, 0))
    spec = pl.BlockSpec((n, cb), lambda j: (0, j))
    return pl.pallas_call(
        functools.partial(_fft_spec_kernel, n1=n1),
        grid=(c // cb,),
        in_specs=[spec, const(consts['f1_real']), const(consts['f3']), const(consts['wr']), const(consts['wi'])],
        out_specs=[spec, spec],
        out_shape=[jax.ShapeDtypeStruct((n, c), BF16)] * 2,
        scratch_shapes=[pltpu.VMEM((n1 * slab, cb), F32), pltpu.VMEM((n1 * slab, cb), F32)],
        compiler_params=_cparams(("arbitrary",), vmem_mb=58),
        name="fft_spectrum",
    )(filt_p, consts['f1_real'], consts['f3'], consts['wr'], consts['wi'])


def _dense_conv_kernel(y_ref, sr_ref, si_ref, f_ref, g_ref, o_ref):
    l = y_ref.shape[1]
    n = sr_ref.shape[0]
    bc = jnp.dot(f_ref[...], jnp.concatenate([y_ref[0], y_ref[1]], axis=0), preferred_element_type=F32)
    br, bi = bc[:n], bc[n:]
    sr, si = sr_ref[...], si_ref[...]
    y = jnp.concatenate([br * sr - bi * si, br * si + bi * sr], axis=0).astype(BF16)
    o = jnp.dot(g_ref[...], y, preferred_element_type=F32)
    o_ref[0] = o[:l].astype(o_ref.dtype)
    o_ref[1] = o[l:].astype(o_ref.dtype)


def _dense_conv(y, col0, spec_r, spec_i, scol0, fmat, gmat, nblk, cb=128):
    b, l, _ = y.shape
    n = spec_r.shape[0]
    y4 = y.reshape(b // 2, 2, l, y.shape[-1])
    const = lambda a: pl.BlockSpec(a.shape, lambda j, pi: (0, 0))
    return pl.pallas_call(
        _dense_conv_kernel,
        grid=(nblk, b // 2),
        in_specs=[pl.BlockSpec((None, 2, l, cb), lambda j, pi: (pi, 0, 0, col0 + j)),
                  pl.BlockSpec((n, cb), lambda j, pi: (0, scol0 + j)),
                  pl.BlockSpec((n, cb), lambda j, pi: (0, scol0 + j)),
                  const(fmat), const(gmat)],
        out_specs=pl.BlockSpec((None, 2, l, cb), lambda j, pi: (pi, 0, 0, j)),
        out_shape=jax.ShapeDtypeStruct((b // 2, 2, l, nblk * cb), BF16),
        compiler_params=_cparams(("parallel", "parallel")),
        name="dense_conv",
    )(y4, spec_r, spec_i, fmat, gmat).reshape(b, l, nblk * cb)


def _sconv_kernel(u_ref, w_ref, b_ref, o_ref, *scr, h1):
    l = u_ref.shape[0]
    if h1 is None:
        u = u_ref[...].astype(F32)
        row = lax.broadcasted_iota(jnp.int32, u.shape, 0)
        prev = jnp.where(row == 0, 0.0, pltpu.roll(u, 1, 0))
        nxt = jnp.where(row == l - 1, 0.0, pltpu.roll(u, l - 1, 0))
        z = prev * w_ref[0:1, :] + u * w_ref[1:2, :] + nxt * w_ref[2:3, :] + b_ref[...]
        o_ref[...] = z.astype(o_ref.dtype)
        return
    buf = scr[0]
    pitch = HY_FAST + HY_PAD
    zero_row = jnp.zeros((1, buf.shape[1]), F32)
    buf[pitch - 1:pitch, :] = zero_row
    for n1 in range(h1):
        base = pitch * (n1 + 1)
        blk = u_ref[n1 * HY_FAST:(n1 + 1) * HY_FAST, :].astype(F32)
        buf[base:base + HY_FAST, :] = blk
        buf[base + pitch - 1:base + pitch, :] = blk[HY_FAST - 1:HY_FAST, :]
        buf[base - HY_PAD:base - HY_PAD + 1, :] = blk[0:1, :]
    buf[pitch * h1 + HY_FAST:pitch * h1 + HY_FAST + 1, :] = zero_row
    w0, w1, w2, bias = w_ref[0:1, :], w_ref[1:2, :], w_ref[2:3, :], b_ref[...]

    def body(n2, carry):
        taps = [buf[pl.ds(pitch - 1 + n2 + k, h1, stride=pitch), :] for k in range(3)]
        z = taps[0] * w0 + taps[1] * w1 + taps[2] * w2 + bias
        o_ref[pl.ds(pl.multiple_of(n2 * h1, h1), h1), :] = z.astype(o_ref.dtype)
        return carry

    lax.fori_loop(0, HY_FAST, body, 0, unroll=4)


def _short_conv(proj, col0, conv_w, conv_b, h1, cb=128):
    b, l, _ = proj.shape
    c = conv_w.shape[1]
    return pl.pallas_call(
        functools.partial(_sconv_kernel, h1=h1),
        scratch_shapes=[] if h1 is None else [pltpu.VMEM(((HY_FAST + HY_PAD) * (h1 + 1), cb), F32)],
        grid=(b, c // cb),
        in_specs=[pl.BlockSpec((None, l, cb), lambda bi, j: (bi, 0, col0 + j)),
                  pl.BlockSpec((conv_w.shape[0], cb), lambda bi, j: (0, j)),
                  pl.BlockSpec((1, cb), lambda bi, j: (0, j))],
        out_specs=pl.BlockSpec((None, l, cb), lambda bi, j: (bi, 0, j)),
        out_shape=jax.ShapeDtypeStruct((b, l, c), BF16),
        compiler_params=_cparams(("parallel", "parallel")),
        name="short_conv",
    )(proj, conv_w.astype(F32), conv_b.astype(F32).reshape(1, c))


def _gate_kernel(x_ref, conv_ref, y_ref, bias_ref, o_ref, *scr, h1):
    y = y_ref[...].astype(F32)
    z = x_ref[...].astype(F32) * (conv_ref[...].astype(F32) + y * bias_ref[...])
    if h1 is None:
        o_ref[...] = z.astype(o_ref.dtype)
        return
    scr[0][...] = z

    def body(n1, carry):
        r = pl.multiple_of(n1 * HY_FAST, HY_FAST)
        o_ref[pl.ds(r, HY_FAST), :] = scr[0][pl.ds(n1, HY_FAST, stride=h1), :].astype(o_ref.dtype)
        return carry

    lax.fori_loop(0, h1, body, 0)


def _hy_gate(z, xcol, conv, y, ycol, bias, h1=None, cb=128):
    b, l, w = conv.shape
    return pl.pallas_call(
        functools.partial(_gate_kernel, h1=h1),
        grid=(b, w // cb),
        in_specs=[pl.BlockSpec((None, l, cb), lambda bi, j: (bi, 0, xcol + j)),
                  pl.BlockSpec((None, l, cb), lambda bi, j: (bi, 0, j)),
                  pl.BlockSpec((None, l, cb), lambda bi, j: (bi, 0, ycol + j)),
                  pl.BlockSpec((1, cb), lambda bi, j: (0, j))],
        out_specs=pl.BlockSpec((None, l, cb), lambda bi, j: (bi, 0, j)),
        out_shape=jax.ShapeDtypeStruct((b, l, w), BF16),
        scratch_shapes=[] if h1 is None else [pltpu.VMEM((l, cb), F32)],
        compiler_params=_cparams(("parallel", "parallel")),
        name="hy_gate",
    )(z, conv, y, bias.astype(F32).reshape(1, w))


def _hyena_branch(proj, col0, p, cb=128):
    b, l, _ = proj.shape
    w = p['hy_bias'].shape[-1]
    n = 2 * l
    nblk = w // cb
    if n // HY_FAST >= 16:
        consts = dict(_fft_consts(l), nblk=nblk)
        n1, h1 = consts['n1'], consts['h1']
        spec_r, spec_i = _fft_spectrum(_hyena_filter(l, p, slow=n1).astype(BF16), consts, cb)
        conv = lambda y, ycol, order: _fft_conv(y, ycol * nblk, spec_r, spec_i, order * nblk, consts, cb)
    else:
        h1 = None
        k = np.arange(n)
        ang = -2.0 * np.pi * np.outer(k, k) / n
        fmat = jnp.asarray(_block2(np.cos(ang[:, :l]), np.sin(ang[:, :l])), BF16)
        gmat = jnp.asarray(_block2(np.cos(ang[:l, :]) / n, -np.sin(ang[:l, :]) / n), BF16)
        f_real = jnp.asarray(np.concatenate([np.cos(ang), np.sin(ang)], axis=0), F32)
        spec = _bmm(f_real[None], _hyena_filter(l, p).astype(BF16)[None], name="dense_spectrum")[0]
        spec_r, spec_i = spec[:n], spec[n:]
        conv = lambda y, ycol, order: _dense_conv(y, ycol * nblk, spec_r, spec_i, order * nblk, fmat, gmat,
                                                  nblk, cb)
    z = _short_conv(proj, col0, p['hy_conv_w'], p['hy_conv_b'], h1, cb)
    c0 = conv(z, HY_ORDER, 0)
    y1 = _hy_gate(z, 0, c0, z, HY_ORDER * nblk, p['hy_bias'][0], None, cb)
    c1 = conv(y1, 0, 1)
    return _hy_gate(z, nblk, c1, y1, 0, p['hy_bias'][1], h1, cb)


def _merge_kernel(x_ref, g_ref, sh_ref, sc_ref, g1_ref, br_ref, wg0_ref, wg1_ref, wg2_ref, bg_ref, wb_ref, wo_ref,
                  o_ref, h_scr, acc_scr):
    j = pl.program_id(2)

    @pl.when(j == 0)
    def _():
        h_scr[...] = _norm_mod(x_ref[...], g_ref[...], sh_ref[...], sc_ref[...]).astype(BF16)
        acc_scr[...] = jnp.zeros_like(acc_scr)

    h = h_scr[...]
    mixed = None
    for i, wg_ref in enumerate((wg0_ref, wg1_ref, wg2_ref)):
        gate = jax.nn.sigmoid(jnp.dot(h, wg_ref[...], preferred_element_type=F32) + bg_ref[i])
        term = gate * jnp.dot(br_ref[i], wb_ref[i], preferred_element_type=F32)
        mixed = term if mixed is None else mixed + term
    acc_scr[...] += jnp.dot(mixed.astype(BF16), wo_ref[...], preferred_element_type=F32)

    @pl.when(j == pl.num_programs(2) - 1)
    def _():
        o_ref[...] = x_ref[...] + g1_ref[...] * acc_scr[...]


def _merge(x, g, sh, sc, g1, branches, w_gate_bf, b_gate, w_branch_bf, w_out_bf, tm, tc):
    b, l, d = x.shape
    nbr, w, _ = w_branch_bf.shape
    nb = sh.shape[0]
    nj = d // tc
    mod_map = (lambda bi, i, j: (bi, 0, 0)) if nb > 1 else (lambda bi, i, j: (0, 0, 0))
    br = jnp.stack(branches, axis=1)
    bg = b_gate.reshape(nbr, nj, 1, tc).transpose(1, 0, 2, 3)
    return pl.pallas_call(
        _merge_kernel,
        grid=(b, l // tm, nj),
        in_specs=[pl.BlockSpec((None, tm, d), lambda bi, i, j: (bi, i, 0)),
                  pl.BlockSpec((1, d), lambda bi, i, j: (0, 0)),
                  pl.BlockSpec((None, 1, d), mod_map),
                  pl.BlockSpec((None, 1, d), mod_map),
                  pl.BlockSpec((None, 1, d), mod_map),
                  pl.BlockSpec((None, nbr, tm, w), lambda bi, i, j: (bi, 0, i, 0)),
                  pl.BlockSpec((d, tc), lambda bi, i, j: (0, j)),
                  pl.BlockSpec((d, tc), lambda bi, i, j: (0, nj + j)),
                  pl.BlockSpec((d, tc), lambda bi, i, j: (0, 2 * nj + j)),
                  pl.BlockSpec((None, nbr, 1, tc), lambda bi, i, j: (j, 0, 0, 0)),
                  pl.BlockSpec((nbr, w, tc), lambda bi, i, j: (0, 0, j)),
                  pl.BlockSpec((tc, d), lambda bi, i, j: (j, 0))],
        out_specs=pl.BlockSpec((None, tm, d), lambda bi, i, j: (bi, i, 0)),
        out_shape=jax.ShapeDtypeStruct((b, l, d), F32),
        scratch_shapes=[pltpu.VMEM((tm, d), BF16), pltpu.VMEM((tm, d), F32)],
        compiler_params=_cparams(("parallel", "parallel", "arbitrary"), vmem_mb=56),
        name="merge",
    )(x, g.reshape(1, d), sh, sc, g1, br, w_gate_bf, w_gate_bf, w_gate_bf, bg, w_branch_bf, w_out_bf)


def _router_kernel(x_ref, g_ref, sh_ref, sc_ref, rt_ref, h_ref, aff_ref):
    h = _norm_mod(x_ref[...], g_ref[...], sh_ref[...], sc_ref[...])
    h_ref[...] = h.astype(h_ref.dtype)
    nt = (((1,), (1,)), ((), ()))
    rt = rt_ref[...]
    r_hi = rt.astype(BF16)
    r_lo = (rt - r_hi.astype(F32)).astype(BF16)
    h_hi = h.astype(BF16)
    h_lo = (h - h_hi.astype(F32)).astype(BF16)
    logits = (lax.dot_general(r_hi, h_hi, nt, preferred_element_type=F32)
              + lax.dot_general(r_hi, h_lo, nt, preferred_element_type=F32)
              + lax.dot_general(r_lo, h_hi, nt, preferred_element_type=F32))
    e = jnp.exp(logits - jnp.max(logits, axis=0, keepdims=True))
    aff_ref[...] = e / jnp.sum(e, axis=0, keepdims=True)


def _router(x, g, sh, sc, router_t, tm):
    b, l, d = x.shape
    e = router_t.shape[0]
    nb = sh.shape[0]
    mod_map = (lambda bi, i: (bi, 0, 0)) if nb > 1 else (lambda bi, i: (0, 0, 0))
    return pl.pallas_call(
        _router_kernel,
        grid=(b, l // tm),
        in_specs=[pl.BlockSpec((None, tm, d), lambda bi, i: (bi, i, 0)),
                  pl.BlockSpec((1, d), lambda bi, i: (0, 0)),
                  pl.BlockSpec((None, 1, d), mod_map),
                  pl.BlockSpec((None, 1, d), mod_map),
                  pl.BlockSpec((e, d), lambda bi, i: (0, 0))],
        out_specs=[pl.BlockSpec((None, tm, d), lambda bi, i: (bi, i, 0)),
                   pl.BlockSpec((None, e, tm), lambda bi, i: (bi, 0, i))],
        out_shape=[jax.ShapeDtypeStruct((b, l, d), BF16), jax.ShapeDtypeStruct((b, e, l), F32)],
        compiler_params=_cparams(("parallel", "parallel")),
        name="router",
    )(x, g.reshape(1, d), sh, sc, router_t)


def _ffn_kernel(xs_ref, gate_ref, w1_ref, w3_ref, w2_ref, o_ref, acc_scr):
    f = pl.program_id(2)

    @pl.when(f == 0)
    def _():
        acc_scr[...] = jnp.zeros_like(acc_scr)

    xs = xs_ref[...]
    a = jnp.dot(xs, w1_ref[...].astype(BF16), preferred_element_type=F32)
    gl = jnp.dot(xs, w3_ref[...].astype(BF16), preferred_element_type=F32)
    mid = (a * jax.nn.sigmoid(a) * gl).astype(BF16)
    acc_scr[...] += jnp.dot(mid, w2_ref[...].astype(BF16), preferred_element_type=F32)

    @pl.when(f == pl.num_programs(2) - 1)
    def _():
        o_ref[...] = (acc_scr[...] * gate_ref[...]).astype(o_ref.dtype)


def _expert_ffn(xs, gate, w1, w3, w2, layer, tr, tf):
    e, r, d = xs.shape
    ff = w1.shape[3]
    return pl.pallas_call(
        _ffn_kernel,
        grid=(e, r // tr, ff // tf),
        in_specs=[pl.BlockSpec((None, tr, d), lambda ei, i, f: (ei, i, 0)),
                  pl.BlockSpec((None, tr, 1), lambda ei, i, f: (ei, i, 0)),
                  pl.BlockSpec((None, None, d, tf), lambda ei, i, f: (layer, ei, 0, f)),
                  pl.BlockSpec((None, None, d, tf), lambda ei, i, f: (layer, ei, 0, f)),
                  pl.BlockSpec((None, None, tf, d), lambda ei, i, f: (layer, ei, f, 0))],
        out_specs=pl.BlockSpec((None, tr, d), lambda ei, i, f: (ei, i, 0)),
        out_shape=jax.ShapeDtypeStruct((e, r, d), BF16),
        scratch_shapes=[pltpu.VMEM((tr, d), F32)],
        compiler_params=_cparams(("parallel", "parallel", "arbitrary"), vmem_mb=56),
        name="expert_ffn",
    )(xs, gate, w1, w3, w2)


def _route_kernel(aff_ref, pos_ref, idx_ref, csum_scr, *, cap, sblk):
    aff = aff_ref[...]
    e, l = aff.shape
    bits = pltpu.bitcast(aff, jnp.int32)
    count = lambda m: jnp.sum(jnp.where(m, 1.0, 0.0), axis=1, keepdims=True)

    def bisect(i, thr):
        cand = thr | lax.shift_left(jnp.int32(1), 30 - i)
        return jnp.where(count(bits >= cand) >= cap, cand, thr)

    thr = lax.fori_loop(0, 31, bisect, jnp.zeros((e, 1), jnp.int32))
    lane = lax.broadcasted_iota(jnp.int32, (e, l), 1)

    def cumsum(x):
        k = 1
        while k < l:
            x = x + jnp.where(lane >= k, pltpu.roll(x, k, 1), 0.0)
            k *= 2
        return x

    above = bits > thr
    tie = jnp.where(bits == thr, 1.0, 0.0)
    need = cap - count(above)
    keep = jnp.logical_or(above, jnp.logical_and(tie > 0.0, cumsum(tie) - tie < need))
    csum = cumsum(jnp.where(keep, 1.0, 0.0))
    slot1 = jnp.where(keep, csum, 0.0)
    hi = jnp.floor(slot1 * (1.0 / 32.0))
    lo = slot1 - 32.0 * hi
    eye = jnp.where(lax.broadcasted_iota(jnp.int32, (e, e), 0) == lax.broadcasted_iota(jnp.int32, (e, e), 1),
                    1.0, 0.0).astype(BF16)
    tn = (((0,), (0,)), ((), ()))
    pos_ref[...] = (32.0 * lax.dot_general(hi.astype(BF16), eye, tn, preferred_element_type=F32)
                    + lax.dot_general(lo.astype(BF16), eye, tn, preferred_element_type=F32) - 1.0).astype(jnp.int32)
    csum_scr[...] = csum
    expert_lane = lax.broadcasted_iota(jnp.int32, (sblk, e), 1)
    slots = [(lax.broadcasted_iota(jnp.int32, (sblk, 1), 0) + s0).astype(F32) for s0 in range(0, cap, sblk)]

    def per_expert(ei, accs):
        row = csum_scr[pl.ds(ei, 1), :]
        return tuple(jnp.where(expert_lane == ei, count(row <= s), a) for s, a in zip(slots, accs))

    accs = lax.fori_loop(0, e, per_expert, tuple(jnp.zeros((sblk, e), F32) for _ in slots))
    for i, a in enumerate(accs):
        idx_ref[i * sblk:(i + 1) * sblk, :] = a.astype(jnp.int32)


def _route(aff_t, cap):
    b, e, l = aff_t.shape
    return pl.pallas_call(
        functools.partial(_route_kernel, cap=cap, sblk=min(256, cap)),
        grid=(b,),
        in_specs=[pl.BlockSpec((None, e, l), lambda bi: (bi, 0, 0))],
        out_specs=[pl.BlockSpec((None, l, e), lambda bi: (bi, 0, 0)),
                   pl.BlockSpec((None, cap, e), lambda bi: (bi, 0, 0))],
        out_shape=[jax.ShapeDtypeStruct((b, l, e), jnp.int32), jax.ShapeDtypeStruct((b, cap, e), jnp.int32)],
        scratch_shapes=[pltpu.VMEM((e, l), F32)],
        compiler_params=_cparams(("parallel",)),
        name="moe_route",
    )(aff_t)


def _combine_kernel(st_ref, x_ref, g2_ref, pos_ref, *refs, n_exp, tiles, tn):
    y_refs, o_ref, oh_scr = refs[:n_exp], refs[n_exp], refs[n_exp + 1]
    bi, j = pl.program_id(0), pl.program_id(1)
    tt, win = oh_scr.shape[1], y_refs[0].shape[1]
    lane = lax.broadcasted_iota(jnp.int32, (tt, win), 1)
    main = min(win, tt)
    spill = []
    for e in range(n_exp):
        rel = pos_ref[:, e:e + 1] - st_ref[(bi * n_exp + e) * tiles + j]
        oh_scr[e] = jnp.where(lane == rel, 1.0, 0.0).astype(BF16)
        spill.append(jnp.max(rel) >= main)
    for c in range(o_ref.shape[1] // tn):
        cols = slice(c * tn, (c + 1) * tn)
        acc = jnp.dot(oh_scr[0, :, :main], y_refs[0][0, :main, cols], preferred_element_type=F32)
        for e in range(1, n_exp):
            acc = acc + jnp.dot(oh_scr[e, :, :main], y_refs[e][0, :main, cols], preferred_element_type=F32)
        o_ref[:, cols] = x_ref[:, cols] + g2_ref[:, cols] * acc
    if win > main:
        for e in range(n_exp):
            @pl.when(spill[e])
            def _(e=e):
                tail = jnp.dot(oh_scr[e, :, main:], y_refs[e][0, main:, :], preferred_element_type=F32)
                o_ref[...] += g2_ref[...] * tail


def _moe_combine(x, g2, pos, y, cap, tt):
    b, l, d = x.shape
    e = y.shape[0]
    tiles = l // tt
    win = min(tt + COMBINE_ALIGN, cap)
    sel = (pos >= 0).reshape(b, tiles, tt, e).sum(2)
    first = jnp.cumsum(sel, axis=1) - sel
    start = jnp.minimum(first // COMBINE_ALIGN * COMBINE_ALIGN, cap - win).astype(jnp.int32)
    start = jnp.transpose(start, (0, 2, 1)).reshape(-1)
    nb = g2.shape[0]

    def y_spec(ei):
        def y_map(bi, j, st):
            return ei, pl.multiple_of(bi * cap + st[(bi * e + ei) * tiles + j], COMBINE_ALIGN), 0
        return pl.BlockSpec((pl.Element(1), pl.Element(win), pl.Element(d)), y_map)

    grid_spec = pltpu.PrefetchScalarGridSpec(
        num_scalar_prefetch=1,
        grid=(b, tiles),
        in_specs=[pl.BlockSpec((None, tt, d), lambda bi, j, st: (bi, j, 0)),
                  pl.BlockSpec((None, 1, d), lambda bi, j, st: (bi if nb > 1 else 0, 0, 0)),
                  pl.BlockSpec((None, tt, e), lambda bi, j, st: (bi, j, 0))] + [y_spec(ei) for ei in range(e)],
        out_specs=pl.BlockSpec((None, tt, d), lambda bi, j, st: (bi, j, 0)),
        scratch_shapes=[pltpu.VMEM((e, tt, win), BF16)])
    return pl.pallas_call(
        functools.partial(_combine_kernel, n_exp=e, tiles=tiles, tn=min(256, d)),
        grid_spec=grid_spec,
        out_shape=jax.ShapeDtypeStruct((b, l, d), F32),
        compiler_params=_cparams(("parallel", "parallel"), vmem_mb=56),
        name="moe_combine",
    )(start, x, g2, pos, *([y] * e))


def _ec_moe_update(x, g, sh, sc, g2, p, tm):
    b, l, d = x.shape
    e = p['router'].shape[1]
    cap = max(1, EC_CAPACITY * l // e)
    h, aff_t = _router(x, g, sh, sc, jnp.transpose(p['router']).astype(F32), tm)
    pos_t, idx_ce = _route(aff_t, cap)
    pos = pos_t
    idx_t = jnp.transpose(idx_ce, (2, 0, 1))
    flat = (idx_t + (jnp.arange(b, dtype=jnp.int32) * l)[None, :, None]).reshape(-1)
    xs = jnp.take(h.reshape(b * l, d), flat, axis=0, mode="clip").reshape(e, b * cap, d)
    gt = jnp.take_along_axis(jnp.transpose(aff_t, (1, 0, 2)), idx_t, axis=2).reshape(e, b * cap, 1)
    tr = min(1024, b * cap)
    ff = p['exp_w1'].shape[3]
    y = _expert_ffn(xs, gt, p['exp_w1'], p['exp_w3'], p['exp_w2'], p['layer'], tr=tr, tf=min(512, ff))
    return _moe_combine(x, g2, pos, y, cap, min(256, l))


def _layer(x, xc, c, c_ctx, p, last):
    b, l, d = x.shape
    lc = xc.shape[1]
    w = p['ssm_d'].shape[0]
    hd = w // NA_HEADS

    cc = jnp.zeros((8, d), F32).at[:b].set(c).at[b].set(c_ctx)
    mod = _ada(cc, p['w_ada'], p['b_ada'], p['layer'])
    m_l = [t[:, None, :] for t in jnp.split(mod[:b], 6, axis=-1)]
    m_c = [t[:, None, :] for t in jnp.split(mod[b:b + 1], 6, axis=-1)]

    w_in = p['w_in'].astype(BF16)
    scale = hd ** -0.5
    head_gain = jnp.stack([jnp.tile(p['na_q_gain'].astype(F32) * scale, NA_HEADS),
                           jnp.tile(p['na_k_gain'].astype(F32), NA_HEADS)])[:, None, :]
    ones_blk = jnp.asarray(np.kron(np.eye(NA_HEADS), np.ones((hd, hd))), BF16)
    tm_l = min(512, l)
    tm_c = min(512, lc)
    proj = _in_proj(x, p['norm1'], m_l[0], m_l[1], w_in, head_gain, ones_blk, min(1024, l))
    projc = _in_proj(xc, p['norm1'], m_c[0], m_c[1], w_in, head_gain, ones_blk, tm_c)

    ssm_l, ssm_c = _s5_branch(projc[..., :w], proj[..., :w], p, _s5_tables(p), need_ctx=not last)

    na_l = _na_latent(proj[..., w:2 * w], proj[..., 2 * w:3 * w], proj[..., 3 * w:4 * w],
                      projc[..., 2 * w:3 * w], projc[..., 3 * w:4 * w], _na_bias_table(p['na_rpb']))
    hy_l = _hyena_branch(proj, 4 * w // 128, p)

    wg = p['w_gate'].astype(BF16)
    wb = p['w_branch'].astype(BF16)
    wo = p['w_out'].astype(BF16)
    x = _merge(x, p['norm1'], m_l[0], m_l[1], m_l[2], (ssm_l, na_l, hy_l), wg, p['b_gate'], wb, wo,
               tm=tm_l, tc=min(512, d))
    x = _ec_moe_update(x, p['norm2'], m_l[3], m_l[4], m_l[5], p, tm_l)
    if last:
        return x, None
    na_c = _na_context(projc[..., w:2 * w], projc[..., 2 * w:3 * w], projc[..., 3 * w:4 * w])
    hy_c = _hyena_branch(projc, 4 * w // 128, p)
    xc = _merge(xc, p['norm1'], m_c[0], m_c[1], m_c[2], (ssm_c, na_c, hy_c), wg, p['b_gate'], wb, wo,
                tm=tm_c, tc=min(512, d))
    xc = _ec_moe_update(xc, p['norm2'], m_c[3], m_c[4], m_c[5], p, tm_c)
    return x, xc


_PARAM_NAMES = ('w_ada', 'b_ada', 'norm1', 'norm2', 'w_in',
                'ssm_lam_re', 'ssm_lam_im', 'ssm_log_step', 'ssm_b_re', 'ssm_b_im', 'ssm_c_re', 'ssm_c_im',
                'ssm_d', 'ssm_w_glu', 'na_q_gain', 'na_k_gain', 'na_rpb',
                'hy_conv_w', 'hy_conv_b', 'hy_w1', 'hy_b1', 'hy_w2', 'hy_b2', 'hy_w3', 'hy_b3', 'hy_w4',
                'hy_freq', 'hy_bias', 'w_gate', 'b_gate', 'w_branch', 'w_out',
                'router', 'exp_w1', 'exp_w3', 'exp_w2')
_STACKED = ('w_ada', 'b_ada', 'exp_w1', 'exp_w3', 'exp_w2')


def kernel(x, c, ctx, c_ctx, w_ada, b_ada, norm1, norm2, w_in, ssm_lam_re, ssm_lam_im, ssm_log_step, ssm_b_re, ssm_b_im, ssm_c_re, ssm_c_im, ssm_d, ssm_w_glu, na_q_gain, na_k_gain, na_rpb, hy_conv_w, hy_conv_b, hy_w1, hy_b1, hy_w2, hy_b2, hy_w3, hy_b3, hy_w4, hy_freq, hy_bias, w_gate, b_gate, w_branch, w_out, router, exp_w1, exp_w3, exp_w2):
    stacked = (w_ada, b_ada, norm1, norm2, w_in, ssm_lam_re, ssm_lam_im, ssm_log_step, ssm_b_re, ssm_b_im,
               ssm_c_re, ssm_c_im, ssm_d, ssm_w_glu, na_q_gain, na_k_gain, na_rpb, hy_conv_w, hy_conv_b,
               hy_w1, hy_b1, hy_w2, hy_b2, hy_w3, hy_b3, hy_w4, hy_freq, hy_bias, w_gate, b_gate, w_branch,
               w_out, router, exp_w1, exp_w3, exp_w2)
    depth = w_ada.shape[0]
    xc = ctx
    for layer in range(depth):
        p = {name: (t if name in _STACKED else t[layer]) for name, t in zip(_PARAM_NAMES, stacked)}
        p['layer'] = layer
        x, xc = _layer(x, xc, c, c_ctx, p, layer == depth - 1)
    return x
```

```python
import functools
import math

import numpy as np
import jax
import jax.numpy as jnp
from jax import lax
from jax.experimental import pallas as pl
from jax.experimental.pallas import tpu as pltpu

BF16 = jnp.bfloat16
F32 = jnp.float32

NORM_EPS = 1e-6
GRID_W = 64
SSM_GROUP = 16
NA_HEADS = 8
NA_WIN_ROWS = 8
NA_WIN_COLS = 16
HY_ORDER = 2
HY_BANDS = 16
HY_MIN_DECAY = math.log(1e-2) / 1.5
HY_MAX_DECAY = math.log(1e-2) / 0.3
EC_CAPACITY = 2
S5_CHUNK = 16
HY_FAST = 128
HY_PAD = 8
MASK_VALUE = -1e30
COMBINE_ALIGN = 16

V7X_VMEM_BYTES = 64 * 1024 * 1024


def _cparams(semantics, vmem_mb=48):
    return pltpu.CompilerParams(dimension_semantics=semantics, vmem_limit_bytes=vmem_mb * 1024 * 1024)


def _norm_mod(x, g, sh, sc):
    ms = jnp.mean(x * x, axis=-1, keepdims=True)
    return (x * lax.rsqrt(ms + NORM_EPS) * g) * (1.0 + sc) + sh


def _ada_kernel(c_ref, w_ref, b_ref, o_ref):
    c = c_ref[...]
    a = (c * jax.nn.sigmoid(c)).astype(BF16)
    o_ref[...] = jnp.dot(a, w_ref[...].astype(BF16), preferred_element_type=F32) + b_ref[...]


def _ada(cc, w, b, layer):
    m, k = cc.shape
    depth, _, n = w.shape
    tn = next(t for t in (1024, 512, 256, 128) if n % t == 0)
    return pl.pallas_call(
        _ada_kernel,
        grid=(n // tn,),
        in_specs=[pl.BlockSpec((m, k), lambda j: (0, 0)),
                  pl.BlockSpec((None, k, tn), lambda j: (layer, 0, j)),
                  pl.BlockSpec((None, 1, tn), lambda j: (layer, 0, j))],
        out_specs=pl.BlockSpec((m, tn), lambda j: (0, j)),
        out_shape=jax.ShapeDtypeStruct((m, n), F32),
        compiler_params=_cparams(("arbitrary",)),
        name="ada",
    )(cc, w, b.reshape(depth, 1, n))


def _in_proj_kernel(x_ref, g_ref, sh_ref, sc_ref, w_ref, hg_ref, ones_ref, o_ref, h_scr, *, head_dim):
    j = pl.program_id(2)

    @pl.when(j == 0)
    def _():
        h_scr[...] = _norm_mod(x_ref[...], g_ref[...], sh_ref[...], sc_ref[...]).astype(BF16)

    acc = jnp.dot(h_scr[...], w_ref[...], preferred_element_type=F32)
    is_qk = jnp.logical_or(j == 1, j == 2)

    @pl.when(is_qk)
    def _():
        yy = acc * acc
        hi = yy.astype(BF16)
        lo = (yy - hi.astype(F32)).astype(BF16)
        ss = (jnp.dot(hi, ones_ref[...], preferred_element_type=F32)
              + jnp.dot(lo, ones_ref[...], preferred_element_type=F32))
        o_ref[...] = (acc * lax.rsqrt(ss * (1.0 / head_dim) + NORM_EPS) * hg_ref[...]).astype(o_ref.dtype)

    @pl.when(jnp.logical_not(is_qk))
    def _():
        o_ref[...] = acc.astype(o_ref.dtype)


def _in_proj(x, g, sh, sc, w_bf, head_gain, ones_blk, tm):
    b, l, d = x.shape
    n = w_bf.shape[1]
    tn = ones_blk.shape[0]
    nb = sh.shape[0]
    mod_map = (lambda bi, i, j: (bi, 0, 0)) if nb > 1 else (lambda bi, i, j: (0, 0, 0))
    return pl.pallas_call(
        functools.partial(_in_proj_kernel, head_dim=tn // NA_HEADS),
        grid=(b, l // tm, n // tn),
        in_specs=[pl.BlockSpec((None, tm, d), lambda bi, i, j: (bi, i, 0)),
                  pl.BlockSpec((1, d), lambda bi, i, j: (0, 0)),
                  pl.BlockSpec((None, 1, d), mod_map),
                  pl.BlockSpec((None, 1, d), mod_map),
                  pl.BlockSpec((d, tn), lambda bi, i, j: (0, j)),
                  pl.BlockSpec((None, 1, tn), lambda bi, i, j: (jnp.clip(j - 1, 0, 1), 0, 0)),
                  pl.BlockSpec((tn, tn), lambda bi, i, j: (0, 0))],
        out_specs=pl.BlockSpec((None, tm, tn), lambda bi, i, j: (bi, i, j)),
        out_shape=jax.ShapeDtypeStruct((b, l, n), BF16),
        scratch_shapes=[pltpu.VMEM((tm, d), BF16)],
        compiler_params=_cparams(("parallel", "parallel", "arbitrary")),
        name="in_proj",
    )(x, g.reshape(1, d), sh, sc, w_bf, head_gain, ones_blk)


def _bmm_kernel(a_ref, w_ref, o_ref):
    o_ref[...] = jnp.dot(a_ref[...].astype(BF16), w_ref[...], preferred_element_type=F32).astype(o_ref.dtype)


def _bmm_add_kernel(a_ref, w_ref, y_ref, o_ref):
    o_ref[...] = (y_ref[...] + jnp.dot(a_ref[...].astype(BF16), w_ref[...],
                                       preferred_element_type=F32)).astype(o_ref.dtype)


def _bmm(a, w, add=None, tm=None, name="bmm"):
    g, m, k = a.shape
    n = w.shape[2]
    tm = m if tm is None else tm
    in_specs = [pl.BlockSpec((None, tm, k), lambda gi, i: (gi, i, 0)),
                pl.BlockSpec((None, k, n), lambda gi, i: (gi, 0, 0))]
    args = [a, w]
    kern = _bmm_kernel
    if add is not None:
        in_specs.append(pl.BlockSpec((None, tm, n), lambda gi, i: (gi, i, 0)))
        args.append(add)
        kern = _bmm_add_kernel
    return pl.pallas_call(
        kern,
        grid=(g, m // tm),
        in_specs=in_specs,
        out_specs=pl.BlockSpec((None, tm, n), lambda gi, i: (gi, i, 0)),
        out_shape=jax.ShapeDtypeStruct((g, m, n), F32),
        compiler_params=_cparams(("parallel", "parallel")),
        name=name,
    )(*args)


def _gelu_tanh(x):
    return 0.5 * x * (1.0 + jnp.tanh(math.sqrt(2.0 / math.pi) * (x + 0.044715 * (x * x * x))))


def _s5_out_kernel(y_ref, u_ref, d_ref, w_ref, o_ref):
    y = y_ref[...].astype(F32) + d_ref[...] * u_ref[...].astype(F32)
    z = _gelu_tanh(y)
    gate = jax.nn.sigmoid(jnp.dot(z.astype(BF16), w_ref[...], preferred_element_type=F32))
    o_ref[...] = (z * gate).astype(o_ref.dtype)


def _s5_out(y, row0, u, d, w_glu_bf):
    b, l, w = u.shape
    tm = math.gcd(math.gcd(l, row0), 512) if row0 else math.gcd(l, 512)
    off = row0 // tm
    return pl.pallas_call(
        _s5_out_kernel,
        grid=(b, l // tm),
        in_specs=[pl.BlockSpec((None, tm, w), lambda bi, i: (bi, off + i, 0)),
                  pl.BlockSpec((None, tm, w), lambda bi, i: (bi, i, 0)),
                  pl.BlockSpec((1, w), lambda bi, i: (0, 0)),
                  pl.BlockSpec((w, w), lambda bi, i: (0, 0))],
        out_specs=pl.BlockSpec((None, tm, w), lambda bi, i: (bi, i, 0)),
        out_shape=jax.ShapeDtypeStruct((b, l, w), BF16),
        compiler_params=_cparams(("parallel", "parallel")),
        name="s5_out",
    )(y, u, d.reshape(1, w), w_glu_bf)


def _s5_tables(p):
    t = S5_CHUNK
    mats, sums, reads, steps = [], [], [], []
    for d in range(2):
        lam_re = p['ssm_lam_re'][d].astype(F32)
        lam_im = p['ssm_lam_im'][d].astype(F32)
        step = jnp.exp(p['ssm_log_step'][d].astype(F32))[:, None]
        mag = jnp.exp(lam_re * step)
        a_re = mag * jnp.cos(lam_im * step)
        a_im = mag * jnp.sin(lam_im * step)
        den = lam_re * lam_re + lam_im * lam_im
        num_re = a_re - 1.0
        coef_re = (num_re * lam_re + a_im * lam_im) / den
        coef_im = (a_im * lam_re - num_re * lam_im) / den
        b_re = p['ssm_b_re'][d].astype(F32)
        b_im = p['ssm_b_im'][d].astype(F32)
        bb_re = coef_re[..., None] * b_re - coef_im[..., None] * b_im
        bb_im = coef_re[..., None] * b_im + coef_im[..., None] * b_re
        c_re = p['ssm_c_re'][d].astype(F32)
        c_im = p['ssm_c_im'][d].astype(F32)
        n = jnp.arange(t + 1, dtype=F32)[:, None, None]
        pm = jnp.exp(lam_re * step * n)
        pw_re = pm * jnp.cos(lam_im * step * n)
        pw_im = pm * jnp.sin(lam_im * step * n)
        ca_re = c_re[None] * pw_re[:, :, None, :] - c_im[None] * pw_im[:, :, None, :]
        ca_im = c_re[None] * pw_im[:, :, None, :] + c_im[None] * pw_re[:, :, None, :]
        ktau = (jnp.einsum('ngop,gpi->ngoi', ca_re[:t], bb_re)
                - jnp.einsum('ngop,gpi->ngoi', ca_im[:t], bb_im))
        s_idx = np.arange(t)[:, None]
        t_idx = np.arange(t)[None, :]
        lag = (t_idx - s_idx) if d == 0 else (s_idx - t_idx)
        lag_sel = jnp.asarray(lag[:, :, None] == np.arange(t), F32)
        blk = jnp.einsum('stn,ngoi->gsito', lag_sel, ktau, precision=lax.Precision.HIGHEST)
        mats.append(blk)
        flip = (lambda a: a[::-1]) if d == 0 else (lambda a: a)
        unflip = (lambda a: a) if d == 0 else (lambda a: a[::-1])
        sp_re, sp_im = flip(pw_re[:t]), flip(pw_im[:t])
        sw_re = sp_re[:, :, :, None] * bb_re[None] - sp_im[:, :, :, None] * bb_im[None]
        sw_im = sp_re[:, :, :, None] * bb_im[None] + sp_im[:, :, :, None] * bb_re[None]
        sums.append((jnp.transpose(sw_re, (1, 0, 3, 2)), jnp.transpose(sw_im, (1, 0, 3, 2))))
        rd_re, rd_im = unflip(ca_re[1:t + 1]), unflip(ca_im[1:t + 1])
        reads.append((jnp.transpose(rd_re, (1, 3, 0, 2)), -jnp.transpose(rd_im, (1, 3, 0, 2))))
        steps.append((pw_re[t], pw_im[t]))
    g, pdim = steps[0][0].shape
    k = SSM_GROUP
    mtot = (mats[0] + mats[1]).reshape(g, t * k, t * k)
    ssum = jnp.concatenate([sums[0][0], sums[0][1], sums[1][0], sums[1][1]], axis=-1).reshape(g, t * k, 4 * pdim)
    r = jnp.concatenate([reads[0][0], reads[0][1], reads[1][0], reads[1][1]], axis=1).reshape(g, 4 * pdim, t * k)
    tk = t * k
    eye = jnp.eye(2, dtype=F32)
    ymat = jnp.einsum('gjsito,jk->gsjitko', mtot.reshape(g // 2, 2, t, k, t, k), eye).reshape(g // 2, 2 * tk, 2 * tk)
    smat = jnp.einsum('gjsiqp,jk->gsjiqkp', ssum.reshape(g // 2, 2, t, k, 4, pdim), eye
                      ).reshape(g // 2, 2 * tk, 8 * pdim)
    w1 = jnp.concatenate([ymat, smat], axis=-1)
    r2 = jnp.einsum('gjqpto,jk->gqjptko', r.reshape(g // 2, 2, 4, pdim, t, k), eye).reshape(g // 2, 8 * pdim, 2 * tk)
    coef = [c.reshape(1, g * pdim) for c in (steps[0][0], steps[0][1], steps[1][0], steps[1][1])]
    return w1.astype(BF16), r2.astype(BF16), coef


LANES = 128


def _to_groups_kernel(u_ref, o_ref, u_scr, o_scr, *, cbk, bsz):
    t, k2 = S5_CHUNK, 2 * SSM_GROUP
    per = LANES // k2
    for b in range(bsz):
        for lt in range(u_scr.shape[0]):
            u_scr[lt] = u_ref[b, :, lt * LANES:(lt + 1) * LANES].astype(F32)
        for lt in range(u_scr.shape[0]):
            xs = [u_scr[lt, pl.ds(s, cbk, stride=t), :] for s in range(t)]
            for q in range(per):
                v = jnp.concatenate([x[:, q * k2:(q + 1) * k2] for x in xs], axis=1)
                for ot in range(o_scr.shape[1]):
                    o_scr[lt * per + q, ot, pl.ds(b, cbk, stride=bsz), :] = v[:, ot * LANES:(ot + 1) * LANES]
    for gp in range(o_scr.shape[0]):
        for ot in range(o_scr.shape[1]):
            o_ref[gp, :, ot * LANES:(ot + 1) * LANES] = o_scr[gp, ot].astype(o_ref.dtype)


def _from_groups_kernel(y_ref, o_ref, y_scr, o_scr, *, cbk, bsz):
    t, k2 = S5_CHUNK, 2 * SSM_GROUP
    per = LANES // k2
    ngp, nlt = y_scr.shape[0], y_scr.shape[1]
    for gp in range(ngp):
        for ot in range(nlt):
            y_scr[gp, ot] = y_ref[gp, :, ot * LANES:(ot + 1) * LANES].astype(F32)
    for b in range(bsz):
        for ot in range(nlt):
            zs = [y_scr[gp, ot, pl.ds(b, cbk, stride=bsz), :] for gp in range(ngp)]
            for q in range(per):
                s = ot * per + q
                v = jnp.concatenate([z[:, q * k2:(q + 1) * k2] for z in zs], axis=1)
                for lt in range(o_scr.shape[0]):
                    o_scr[lt, pl.ds(s, cbk, stride=t), :] = v[:, lt * LANES:(lt + 1) * LANES]
        for lt in range(o_scr.shape[0]):
            o_ref[b, :, lt * LANES:(lt + 1) * LANES] = o_scr[lt].astype(o_ref.dtype)


def _chunk_block(nc, bsz):
    return next(c for c in range(min(64, nc), 0, -1) if nc % c == 0 and (c * bsz) % 16 == 0)


def _to_groups(u):
    bsz, ltot, w = u.shape
    t, k2 = S5_CHUNK, 2 * SSM_GROUP
    nc = ltot // t
    cbk = _chunk_block(nc, bsz)
    ngp = w // k2
    return pl.pallas_call(
        functools.partial(_to_groups_kernel, cbk=cbk, bsz=bsz),
        grid=(nc // cbk,),
        in_specs=[pl.BlockSpec((bsz, cbk * t, w), lambda i: (0, i, 0))],
        out_specs=pl.BlockSpec((ngp, cbk * bsz, t * k2), lambda i: (0, i, 0)),
        out_shape=jax.ShapeDtypeStruct((ngp, nc * bsz, t * k2), BF16),
        scratch_shapes=[pltpu.VMEM((w // LANES, cbk * t, LANES), F32),
                        pltpu.VMEM((ngp, t * k2 // LANES, cbk * bsz, LANES), F32)],
        compiler_params=_cparams(("parallel",)),
        name="s5_to_groups",
    )(u)


def _from_groups(y, bsz):
    ngp, m, tk2 = y.shape
    t, k2 = S5_CHUNK, 2 * SSM_GROUP
    nc = m // bsz
    cbk = _chunk_block(nc, bsz)
    w = ngp * k2
    return pl.pallas_call(
        functools.partial(_from_groups_kernel, cbk=cbk, bsz=bsz),
        grid=(nc // cbk,),
        in_specs=[pl.BlockSpec((ngp, cbk * bsz, tk2), lambda i: (0, i, 0))],
        out_specs=pl.BlockSpec((bsz, cbk * t, w), lambda i: (0, i, 0)),
        out_shape=jax.ShapeDtypeStruct((bsz, nc * t, w), BF16),
        scratch_shapes=[pltpu.VMEM((ngp, tk2 // LANES, cbk * bsz, LANES), F32),
                        pltpu.VMEM((w // LANES, cbk * t, LANES), F32)],
        compiler_params=_cparams(("parallel",)),
        name="s5_from_groups",
    )(y)


def _s5_chunk_kernel(u_ref, w_ref, y_ref, xfr_ref, xfi_ref, xbr_ref, xbi_ref):
    r = jnp.dot(u_ref[...], w_ref[...], preferred_element_type=F32)
    ny = y_ref.shape[-1]
    nx = xfr_ref.shape[-1]
    y_ref[...] = r[:, :ny]
    for i, ref in enumerate((xfr_ref, xfi_ref, xbr_ref, xbi_ref)):
        ref[...] = r[:, ny + i * nx:ny + (i + 1) * nx]


def _s5_chunk(ug, w1):
    gp, m, tk2 = ug.shape
    nx = (w1.shape[2] - tk2) // 4
    xspec = pl.BlockSpec((m, nx), lambda g: (0, g))
    xshape = jax.ShapeDtypeStruct((m, gp * nx), F32)
    return pl.pallas_call(
        _s5_chunk_kernel,
        grid=(gp,),
        in_specs=[pl.BlockSpec((None, m, tk2), lambda g: (g, 0, 0)),
                  pl.BlockSpec((None, tk2, w1.shape[2]), lambda g: (g, 0, 0))],
        out_specs=[pl.BlockSpec((None, m, tk2), lambda g: (g, 0, 0)), xspec, xspec, xspec, xspec],
        out_shape=[jax.ShapeDtypeStruct((gp, m, tk2), F32), xshape, xshape, xshape, xshape],
        compiler_params=_cparams(("parallel",)),
        name="s5_chunk",
    )(ug, w1)


def _s5_scan_kernel(xfr, xfi, xbr, xbi, afr, afi, abr, abi, hfr, hfi, hbr, hbi, *, tiles_ctx, tiles, bsz):
    shape = (2 * bsz, xfr.shape[1])
    low = lax.broadcasted_iota(jnp.int32, shape, 0) < bsz
    zero = jnp.zeros(shape, F32)

    def step(hr, hi, ar, ai, xr, xi):
        return ar * hr - ai * hi + xr, ar * hi + ai * hr + xi

    def fwd_tile(i, carry):
        ar, ai = jnp.broadcast_to(afr[...], shape), jnp.broadcast_to(afi[...], shape)
        r0 = pl.multiple_of(i * 2 * bsz, 2 * bsz)
        xr, xi = xfr[pl.ds(r0, 2 * bsz), :], xfi[pl.ds(r0, 2 * bsz), :]
        sr, si = pltpu.roll(carry[0], bsz, 0), pltpu.roll(carry[1], bsz, 0)
        tr, ti = step(sr, si, ar, ai, xr, xi)
        t2r, t2i = pltpu.roll(tr, bsz, 0), pltpu.roll(ti, bsz, 0)
        hfr[pl.ds(r0, 2 * bsz), :] = jnp.where(low, sr, t2r)
        hfi[pl.ds(r0, 2 * bsz), :] = jnp.where(low, si, t2i)
        return step(t2r, t2i, ar, ai, xr, xi)

    lax.fori_loop(0, tiles, fwd_tile, (zero, zero))

    def bwd_tile(i, carry):
        ar, ai = jnp.broadcast_to(abr[...], shape), jnp.broadcast_to(abi[...], shape)
        r0 = pl.multiple_of(i * 2 * bsz, 2 * bsz)
        xr, xi = xbr[pl.ds(r0, 2 * bsz), :], xbi[pl.ds(r0, 2 * bsz), :]
        sr, si = pltpu.roll(carry[0], bsz, 0), pltpu.roll(carry[1], bsz, 0)
        tr, ti = step(sr, si, ar, ai, xr, xi)
        t2r, t2i = pltpu.roll(tr, bsz, 0), pltpu.roll(ti, bsz, 0)
        hbr[pl.ds(r0, 2 * bsz), :] = jnp.where(low, t2r, sr)
        hbi[pl.ds(r0, 2 * bsz), :] = jnp.where(low, t2i, si)
        return step(t2r, t2i, ar, ai, xr, xi)

    carry = lax.fori_loop(0, tiles_ctx, lambda j, c: bwd_tile(tiles_ctx - 1 - j, c), (zero, zero))
    lax.fori_loop(0, tiles - tiles_ctx, lambda j, c: bwd_tile(tiles - 1 - j, c), carry)


def _s5_scan(xs, coef, chunks_ctx, chunks, bsz):
    m, lanes = xs[0].shape
    assert 2 * bsz == 8 and chunks % 2 == 0 and chunks_ctx % 2 == 0
    lb = min(256, lanes)
    xspec = pl.BlockSpec((m, lb), lambda j: (0, j))
    cspec = pl.BlockSpec((1, lb), lambda j: (0, j))
    shape = jax.ShapeDtypeStruct((m, lanes), F32)
    return pl.pallas_call(
        functools.partial(_s5_scan_kernel, tiles_ctx=chunks_ctx // 2, tiles=chunks // 2, bsz=bsz),
        grid=(lanes // lb,),
        in_specs=[xspec] * 4 + [cspec] * 4,
        out_specs=[xspec] * 4,
        out_shape=[shape] * 4,
        compiler_params=_cparams(("parallel",)),
        name="s5_scan",
    )(*xs, *coef)


def _s5_readout_kernel(y_ref, hfr, hfi, hbr, hbi, r_ref, o_ref):
    h = jnp.concatenate([hfr[...], hfi[...], hbr[...], hbi[...]], axis=1).astype(BF16)
    o_ref[...] = (y_ref[...] + jnp.dot(h, r_ref[...], preferred_element_type=F32)).astype(o_ref.dtype)


def _s5_readout(y0, hs, r2):
    gp, m, tk2 = y0.shape
    nx = hs[0].shape[1] // gp
    hspec = pl.BlockSpec((m, nx), lambda g: (0, g))
    yspec = pl.BlockSpec((None, m, tk2), lambda g: (g, 0, 0))
    return pl.pallas_call(
        _s5_readout_kernel,
        grid=(gp,),
        in_specs=[yspec, hspec, hspec, hspec, hspec, pl.BlockSpec((None, 4 * nx, tk2), lambda g: (g, 0, 0))],
        out_specs=yspec,
        out_shape=jax.ShapeDtypeStruct((gp, m, tk2), BF16),
        compiler_params=_cparams(("parallel",)),
        name="s5_readout",
    )(y0, *hs, r2)


def _s5_branch(u_ctx, u_lat, p, tables, need_ctx):
    w1, r2, coef = tables
    b, l, w = u_lat.shape
    lc = u_ctx.shape[1]
    t, k = S5_CHUNK, SSM_GROUP
    g = w // k
    nc = (lc + l) // t
    m = nc * b
    ug = _to_groups(jnp.concatenate([u_ctx, u_lat], axis=1))
    y0, *xs = _s5_chunk(ug, w1)
    hs = _s5_scan(xs, coef, lc // t, nc, b)
    y = _from_groups(_s5_readout(y0, hs, r2), b)
    w_glu = p['ssm_w_glu'].astype(BF16)
    out_l = _s5_out(y, lc, u_lat, p['ssm_d'], w_glu)
    out_c = _s5_out(y, 0, u_ctx, p['ssm_d'], w_glu) if need_ctx else None
    return out_l, out_c


def _na_kernel(q_ref, k_ref, v_ref, kc_ref, vc_ref, bias_ref, o_ref, s_scr, p_scr, *, rows, head_dim, rps):
    kr = NA_WIN_ROWS
    kc = kc_ref[...]
    vc = vc_ref[...]
    nt = (((1,), (1,)), ((), ()))
    lane_tile = 2 * head_dim
    nwin = kr * GRID_W
    first_head = lax.broadcasted_iota(jnp.int32, (GRID_W, lane_tile), 1) < head_dim
    vwins = []
    for sub in range(rps):
        r = pl.program_id(1) * rps + sub
        start = jnp.clip(r - kr // 2, 0, rows - kr)
        cls = r - start
        base = pl.multiple_of(start * GRID_W, GRID_W)
        kwin = k_ref[pl.ds(base, nwin), :]
        vwins.append(v_ref[pl.ds(base, nwin), :])
        q = q_ref[sub * GRID_W:(sub + 1) * GRID_W, :]
        for h in range(NA_HEADS):
            sl = slice(h // 2 * lane_tile, (h // 2 + 1) * lane_tile)
            qh = jnp.where(first_head == (h % 2 == 0), q[:, sl], jnp.zeros_like(q[:, sl]))
            i = sub * NA_HEADS + h
            s_scr[i, :, :nwin] = lax.dot_general(qh, kwin[:, sl], nt, preferred_element_type=F32) + bias_ref[cls, h]
            s_scr[i, :, nwin:] = lax.dot_general(qh, kc[:, sl], nt, preferred_element_type=F32)
    inv = []
    for i in range(rps * NA_HEADS):
        s = s_scr[i]
        pr = jnp.exp(s - jnp.max(s, axis=-1, keepdims=True))
        inv.append(1.0 / jnp.sum(pr, axis=-1, keepdims=True))
        p_scr[i] = pr.astype(BF16)
    for sub in range(rps):
        outs = []
        for pair in range(NA_HEADS // 2):
            sl = slice(pair * lane_tile, (pair + 1) * lane_tile)
            halves = []
            for half in range(2):
                i = sub * NA_HEADS + 2 * pair + half
                o = (jnp.dot(p_scr[i, :, :nwin], vwins[sub][:, sl], preferred_element_type=F32)
                     + jnp.dot(p_scr[i, :, nwin:], vc[:, sl], preferred_element_type=F32))
                halves.append(o * inv[i])
            outs.append(jnp.where(first_head, halves[0], halves[1]))
        o_ref[sub * GRID_W:(sub + 1) * GRID_W, :] = jnp.concatenate(outs, axis=-1).astype(o_ref.dtype)


def _na_bias_table(rpb):
    kr, kw, w = NA_WIN_ROWS, NA_WIN_COLS, GRID_W
    col = np.arange(w)
    cstart = np.clip(col - kw // 2, 0, w - kw)
    kcol = np.arange(w)
    inwin = (kcol[None, :] >= cstart[:, None]) & (kcol[None, :] < cstart[:, None] + kw)
    dc = np.clip(kcol[None, :] - col[:, None] + (kw - 1), 0, 2 * kw - 2)
    rpb = rpb.astype(F32)
    dr = np.arange(kr)[None, :] - np.arange(kr)[:, None] + (kr - 1)
    row_sel = jnp.asarray(dr[:, :, None] == np.arange(2 * kr - 1), F32)
    col_sel = jnp.asarray((dc[:, :, None] == np.arange(2 * kw - 1)) & inwin[:, :, None], F32)
    hp = lax.Precision.HIGHEST
    rows = jnp.einsum('hab,oja->ohjb', rpb, row_sel, precision=hp)
    bias = jnp.einsum('ohjb,cdb->ohcjd', rows, col_sel, precision=hp)
    bias = jnp.where(jnp.asarray(inwin)[None, None, :, None, :], bias, MASK_VALUE)
    return bias.reshape(kr, rpb.shape[0], w, kr * w)


def _na_latent(q, k, v, kc, vc, bias_tab):
    b, l, w = q.shape
    lc = kc.shape[1]
    rows = l // GRID_W
    assert rows >= NA_WIN_ROWS
    nk = NA_WIN_ROWS * GRID_W
    rps = 4 if rows % 4 == 0 else 2
    assert rows % rps == 0
    return pl.pallas_call(
        functools.partial(_na_kernel, rows=rows, head_dim=w // NA_HEADS, rps=rps),
        grid=(b, rows // rps),
        in_specs=[pl.BlockSpec((None, rps * GRID_W, w), lambda bi, r: (bi, r, 0)),
                  pl.BlockSpec((None, l, w), lambda bi, r: (bi, 0, 0)),
                  pl.BlockSpec((None, l, w), lambda bi, r: (bi, 0, 0)),
                  pl.BlockSpec((None, lc, w), lambda bi, r: (bi, 0, 0)),
                  pl.BlockSpec((None, lc, w), lambda bi, r: (bi, 0, 0)),
                  pl.BlockSpec((NA_WIN_ROWS, NA_HEADS, GRID_W, nk), lambda bi, r: (0, 0, 0, 0))],
        out_specs=pl.BlockSpec((None, rps * GRID_W, w), lambda bi, r: (bi, r, 0)),
        out_shape=jax.ShapeDtypeStruct((b, l, w), BF16),
        scratch_shapes=[pltpu.VMEM((rps * NA_HEADS, GRID_W, nk + lc), F32),
                        pltpu.VMEM((rps * NA_HEADS, GRID_W, nk + lc), BF16)],
        compiler_params=_cparams(("parallel", "arbitrary"), vmem_mb=56),
        name="na_latent",
    )(q, k, v, kc, vc, bias_tab)


def _na_ctx_kernel(q_ref, k_ref, v_ref, o_ref, *, head_dim):
    q, k, v = q_ref[...], k_ref[...], v_ref[...]
    outs = []
    for h in range(NA_HEADS):
        sl = slice(h * head_dim, (h + 1) * head_dim)
        s = lax.dot_general(q[:, sl], k[:, sl], (((1,), (1,)), ((), ())), preferred_element_type=F32)
        pr = jnp.exp(s - jnp.max(s, axis=-1, keepdims=True))
        den = jnp.sum(pr, axis=-1, keepdims=True)
        outs.append(jnp.dot(pr.astype(BF16), v[:, sl], preferred_element_type=F32) / den)
    o_ref[...] = jnp.concatenate(outs, axis=-1).astype(o_ref.dtype)


def _na_context(q, k, v):
    b, lc, w = q.shape
    spec = pl.BlockSpec((None, lc, w), lambda bi: (bi, 0, 0))
    return pl.pallas_call(
        functools.partial(_na_ctx_kernel, head_dim=w // NA_HEADS),
        grid=(b,),
        in_specs=[spec, spec, spec],
        out_specs=spec,
        out_shape=jax.ShapeDtypeStruct((b, lc, w), BF16),
        compiler_params=_cparams(("parallel",)),
        name="na_context",
    )(q, k, v)


def _hyena_filter(length, p, slow=None):
    n = 2 * length
    pos = np.arange(n)
    if slow is not None:
        pos = pos.reshape(slow, n // slow).T.reshape(-1)
    off = np.where(pos < length, pos, n - pos)
    backward = jnp.asarray(pos > length)[:, None, None]
    live = jnp.asarray(pos != length, F32)[:, None, None]
    tau = jnp.asarray(np.minimum(off, length - 1), F32)[:, None]
    t = tau / (length - 1)
    freqs = jnp.linspace(1e-4, HY_BANDS - 1, HY_BANDS, dtype=F32)
    ang = (2.0 * math.pi / length) * tau * freqs[None, :]
    z = jnp.concatenate([t, jnp.cos(ang), -jnp.sin(ang)], axis=-1)
    fr = p['hy_freq'].astype(F32)
    hp = lax.Precision.HIGH
    h = jnp.sin(fr * (jnp.dot(z, p['hy_w1'].astype(F32), precision=hp) + p['hy_b1'].astype(F32)))
    h = jnp.sin(fr * (jnp.dot(h, p['hy_w2'].astype(F32), precision=hp) + p['hy_b2'].astype(F32)))
    h = jnp.sin(fr * (jnp.dot(h, p['hy_w3'].astype(F32), precision=hp) + p['hy_b3'].astype(F32)))
    wdim = p['hy_bias'].shape[-1]
    h = jnp.dot(h, p['hy_w4'].astype(F32), precision=hp).reshape(n, HY_ORDER, 2, wdim)
    decay = jnp.exp(-t * jnp.abs(jnp.linspace(HY_MIN_DECAY, HY_MAX_DECAY, wdim, dtype=F32)))
    filt = jnp.where(backward, h[:, :, 1], h[:, :, 0]) * (decay[:, None, :] * live)
    filt = filt / jnp.sum(jnp.abs(filt), axis=0, keepdims=True)
    return filt.reshape(n, HY_ORDER * wdim)


def _block2(re, im):
    return np.block([[re, -im], [im, re]])


def _fft_consts(length):
    n = 2 * length
    n2 = HY_FAST
    n1 = n // n2
    h1 = n1 // 2
    a1 = -2.0 * np.pi * np.outer(np.arange(n1), np.arange(n1)) / n1
    a2 = -2.0 * np.pi * np.outer(np.arange(n2), np.arange(n2)) / n2
    f1 = _block2(np.cos(a1[:, :h1]), np.sin(a1[:, :h1]))
    f1_real = np.concatenate([np.cos(a1), np.sin(a1)], axis=0)
    f3 = _block2(np.cos(a2), np.sin(a2))
    g3 = _block2(np.cos(a2), -np.sin(a2))
    g1 = _block2(np.cos(a1[:h1, :]) / n, -np.sin(a1[:h1, :]) / n)
    wa = -2.0 * np.pi * np.arange(n2) / n
    wr = np.broadcast_to(np.cos(wa)[:, None], (n2, 128))
    wi = np.broadcast_to(np.sin(wa)[:, None], (n2, 128))
    cast = lambda m: jnp.asarray(m, BF16)
    return dict(n1=n1, h1=h1, f1=cast(f1), f1_real=cast(f1_real), f3=cast(f3), g3=cast(g3), g1=cast(g1),
                wr=jnp.asarray(wr, F32), wi=jnp.asarray(wi, F32))


def _fft_first_stage(a_r, a_i, load_cols, f1_ref, n1, cb):
    slab = HY_FAST + HY_PAD

    def body(i, carry):
        n2 = 2 * i
        z = jnp.concatenate([load_cols(n2), load_cols(n2 + 1)], axis=1)
        r = jnp.dot(f1_ref[...], z, preferred_element_type=F32)
        a_r[pl.ds(n2, n1, stride=slab), :] = r[:n1, :cb]
        a_i[pl.ds(n2, n1, stride=slab), :] = r[n1:, :cb]
        a_r[pl.ds(n2 + 1, n1, stride=slab), :] = r[:n1, cb:]
        a_i[pl.ds(n2 + 1, n1, stride=slab), :] = r[n1:, cb:]
        return carry

    lax.fori_loop(0, HY_FAST // 2, body, 0, unroll=4)


def _fft_slab_loop(a_r, a_i, f3_ref, wr_ref, wi_ref, n1, cb, finish):
    slab = HY_FAST + HY_PAD
    n2 = HY_FAST
    wr, wi = wr_ref[...], wi_ref[...]
    if cb != wr.shape[1]:
        wr, wi = jnp.tile(wr, (1, cb // wr.shape[1])), jnp.tile(wi, (1, cb // wi.shape[1]))

    def body(i, carry):
        tr, ti = carry
        k1 = 2 * i
        r0 = pl.multiple_of(k1 * slab, 8)
        r1 = pl.multiple_of((k1 + 1) * slab, 8)
        tr1, ti1 = tr * wr - ti * wi, tr * wi + ti * wr
        ar0, ai0 = a_r[pl.ds(r0, n2), :], a_i[pl.ds(r0, n2), :]
        ar1, ai1 = a_r[pl.ds(r1, n2), :], a_i[pl.ds(r1, n2), :]
        xr = jnp.concatenate([ar0 * tr - ai0 * ti, ar1 * tr1 - ai1 * ti1], axis=1)
        xi = jnp.concatenate([ar0 * ti + ai0 * tr, ar1 * ti1 + ai1 * tr1], axis=1)
        bc = jnp.dot(f3_ref[...], jnp.concatenate([xr, xi], axis=0).astype(BF16), preferred_element_type=F32)
        finish(k1, (r0, r1), (tr, ti, tr1, ti1), bc[:n2], bc[n2:])
        return tr1 * wr - ti1 * wi, tr1 * wi + ti1 * wr

    one = jnp.ones((n2, cb), F32)
    lax.fori_loop(0, n1 // 2, body, (one, jnp.zeros_like(one)), unroll=2)


def _fft_conv_kernel(y_ref, sr_ref, si_ref, f1_ref, f3_ref, g3_ref, g1_ref, wr_ref, wi_ref, o_ref, a_r, a_i,
                     *, n1):
    h1 = n1 // 2
    n2 = HY_FAST
    cb = o_ref.shape[-1]
    slab = HY_FAST + HY_PAD

    def load_cols(j):
        r = pl.multiple_of(j * h1, h1)
        return jnp.concatenate([y_ref[0, pl.ds(r, h1), :], y_ref[1, pl.ds(r, h1), :]], axis=0)

    _fft_first_stage(a_r, a_i, load_cols, f1_ref, n1, cb)

    def finish(k1, rows, tw, br, bi):
        s0 = pl.multiple_of(k1 * n2, n2)
        s1 = pl.multiple_of((k1 + 1) * n2, n2)
        sr = jnp.concatenate([sr_ref[pl.ds(s0, n2), :], sr_ref[pl.ds(s1, n2), :]], axis=1).astype(F32)
        si = jnp.concatenate([si_ref[pl.ds(s0, n2), :], si_ref[pl.ds(s1, n2), :]], axis=1).astype(F32)
        y = jnp.concatenate([br * sr - bi * si, br * si + bi * sr], axis=0).astype(BF16)
        ac = jnp.dot(g3_ref[...], y, preferred_element_type=F32)
        pr, pi = ac[:n2], ac[n2:]
        for j in range(2):
            tr, ti = tw[2 * j], tw[2 * j + 1]
            prj, pij = pr[:, j * cb:(j + 1) * cb], pi[:, j * cb:(j + 1) * cb]
            a_r[pl.ds(rows[j], n2), :] = prj * tr + pij * ti
            a_i[pl.ds(rows[j], n2), :] = pij * tr - prj * ti

    _fft_slab_loop(a_r, a_i, f3_ref, wr_ref, wi_ref, n1, cb, finish)

    def out_body(i, carry):
        j0 = 2 * i
        cols = []
        for j in (j0, j0 + 1):
            cols.append(jnp.concatenate([a_r[pl.ds(j, n1, stride=slab), :], a_i[pl.ds(j, n1, stride=slab), :]],
                                        axis=0))
        o = jnp.dot(g1_ref[...], jnp.concatenate(cols, axis=1).astype(BF16), preferred_element_type=F32)
        for jj in range(2):
            r = pl.multiple_of((j0 + jj) * h1, h1)
            o_ref[0, pl.ds(r, h1), :] = o[:h1, jj * cb:(jj + 1) * cb].astype(o_ref.dtype)
            o_ref[1, pl.ds(r, h1), :] = o[h1:, jj * cb:(jj + 1) * cb].astype(o_ref.dtype)
        return carry

    lax.fori_loop(0, n2 // 2, out_body, 0, unroll=4)


def _fft_conv(yp, col0, spec_r, spec_i, scol0, consts, cb=128):
    b, l, _ = yp.shape
    n = spec_r.shape[0]
    n1 = consts['n1']
    slab = HY_FAST + HY_PAD
    nblk = consts['nblk']
    y4 = yp.reshape(b // 2, 2, l, yp.shape[-1])
    const = lambda a: pl.BlockSpec(a.shape, lambda j, pi: (0, 0))
    return pl.pallas_call(
        functools.partial(_fft_conv_kernel, n1=n1),
        grid=(nblk, b // 2),
        in_specs=[pl.BlockSpec((None, 2, l, cb), lambda j, pi: (pi, 0, 0, col0 + j)),
                  pl.BlockSpec((n, cb), lambda j, pi: (0, scol0 + j)),
                  pl.BlockSpec((n, cb), lambda j, pi: (0, scol0 + j)),
                  const(consts['f1']), const(consts['f3']), const(consts['g3']), const(consts['g1']),
                  const(consts['wr']), const(consts['wi'])],
        out_specs=pl.BlockSpec((None, 2, l, cb), lambda j, pi: (pi, 0, 0, j)),
        out_shape=jax.ShapeDtypeStruct((b // 2, 2, l, nblk * cb), BF16),
        scratch_shapes=[pltpu.VMEM((n1 * slab, cb), F32), pltpu.VMEM((n1 * slab, cb), F32)],
        compiler_params=_cparams(("arbitrary", "arbitrary"), vmem_mb=58),
        name="fft_conv",
    )(y4, spec_r, spec_i, consts['f1'], consts['f3'], consts['g3'], consts['g1'],
      consts['wr'], consts['wi']).reshape(b, l, nblk * cb)


def _fft_spec_kernel(f_ref, f1_ref, f3_ref, wr_ref, wi_ref, sr_ref, si_ref, a_r, a_i, *, n1):
    n2 = HY_FAST
    cb = sr_ref.shape[-1]

    def load_cols(j):
        return f_ref[pl.ds(pl.multiple_of(j * n1, n1), n1), :]

    _fft_first_stage(a_r, a_i, load_cols, f1_ref, n1, cb)

    def finish(k1, rows, tw, br, bi):
        for j in range(2):
            s = pl.multiple_of((k1 + j) * n2, n2)
            sr_ref[pl.ds(s, n2), :] = br[:, j * cb:(j + 1) * cb].astype(sr_ref.dtype)
            si_ref[pl.ds(s, n2), :] = bi[:, j * cb:(j + 1) * cb].astype(si_ref.dtype)

    _fft_slab_loop(a_r, a_i, f3_ref, wr_ref, wi_ref, n1, cb, finish)


def _fft_spectrum(filt_p, consts, cb=128):
    n, c = filt_p.shape
    n1 = consts['n1']
    slab = HY_FAST + HY_PAD
    const = lambda a: pl.BlockSpec(a.shape, lambda j: (0, 0))
    spec = pl.BlockSpec((n, cb), lambda j: (0, j))
    return pl.pallas_call(
        functools.partial(_fft_spec_kernel, n1=n1),
        grid=(c // cb,),
        in_specs=[spec, const(consts['f1_real']), const(consts['f3']), const(consts['wr']), const(consts['wi'])],
        out_specs=[spec, spec],
        out_shape=[jax.ShapeDtypeStruct((n, c), BF16)] * 2,
        scratch_shapes=[pltpu.VMEM((n1 * slab, cb), F32), pltpu.VMEM((n1 * slab, cb), F32)],
        compiler_params=_cparams(("arbitrary",), vmem_mb=58),
        name="fft_spectrum",
    )(filt_p, consts['f1_real'], consts['f3'], consts['wr'], consts['wi'])


def _dense_conv_kernel(y_ref, sr_ref, si_ref, f_ref, g_ref, o_ref):
    l = y_ref.shape[1]
    n = sr_ref.shape[0]
    bc = jnp.dot(f_ref[...], jnp.concatenate([y_ref[0], y_ref[1]], axis=0), preferred_element_type=F32)
    br, bi = bc[:n], bc[n:]
    sr, si = sr_ref[...], si_ref[...]
    y = jnp.concatenate([br * sr - bi * si, br * si + bi * sr], axis=0).astype(BF16)
    o = jnp.dot(g_ref[...], y, preferred_element_type=F32)
    o_ref[0] = o[:l].astype(o_ref.dtype)
    o_ref[1] = o[l:].astype(o_ref.dtype)


def _dense_conv(y, col0, spec_r, spec_i, scol0, fmat, gmat, nblk, cb=128):
    b, l, _ = y.shape
    n = spec_r.shape[0]
    y4 = y.reshape(b // 2, 2, l, y.shape[-1])
    const = lambda a: pl.BlockSpec(a.shape, lambda j, pi: (0, 0))
    return pl.pallas_call(
        _dense_conv_kernel,
        grid=(nblk, b // 2),
        in_specs=[pl.BlockSpec((None, 2, l, cb), lambda j, pi: (pi, 0, 0, col0 + j)),
                  pl.BlockSpec((n, cb), lambda j, pi: (0, scol0 + j)),
                  pl.BlockSpec((n, cb), lambda j, pi: (0, scol0 + j)),
                  const(fmat), const(gmat)],
        out_specs=pl.BlockSpec((None, 2, l, cb), lambda j, pi: (pi, 0, 0, j)),
        out_shape=jax.ShapeDtypeStruct((b // 2, 2, l, nblk * cb), BF16),
        compiler_params=_cparams(("parallel", "parallel")),
        name="dense_conv",
    )(y4, spec_r, spec_i, fmat, gmat).reshape(b, l, nblk * cb)


def _sconv_kernel(u_ref, w_ref, b_ref, o_ref, *scr, h1):
    l = u_ref.shape[0]
    if h1 is None:
        u = u_ref[...].astype(F32)
        row = lax.broadcasted_iota(jnp.int32, u.shape, 0)
        prev = jnp.where(row == 0, 0.0, pltpu.roll(u, 1, 0))
        nxt = jnp.where(row == l - 1, 0.0, pltpu.roll(u, l - 1, 0))
        z = prev * w_ref[0:1, :] + u * w_ref[1:2, :] + nxt * w_ref[2:3, :] + b_ref[...]
        o_ref[...] = z.astype(o_ref.dtype)
        return
    buf = scr[0]
    pitch = HY_FAST + HY_PAD
    zero_row = jnp.zeros((1, buf.shape[1]), F32)
    buf[pitch - 1:pitch, :] = zero_row
    for n1 in range(h1):
        base = pitch * (n1 + 1)
        blk = u_ref[n1 * HY_FAST:(n1 + 1) * HY_FAST, :].astype(F32)
        buf[base:base + HY_FAST, :] = blk
        buf[base + pitch - 1:base + pitch, :] = blk[HY_FAST - 1:HY_FAST, :]
        buf[base - HY_PAD:base - HY_PAD + 1, :] = blk[0:1, :]
    buf[pitch * h1 + HY_FAST:pitch * h1 + HY_FAST + 1, :] = zero_row
    w0, w1, w2, bias = w_ref[0:1, :], w_ref[1:2, :], w_ref[2:3, :], b_ref[...]

    def body(n2, carry):
        taps = [buf[pl.ds(pitch - 1 + n2 + k, h1, stride=pitch), :] for k in range(3)]
        z = taps[0] * w0 + taps[1] * w1 + taps[2] * w2 + bias
        o_ref[pl.ds(pl.multiple_of(n2 * h1, h1), h1), :] = z.astype(o_ref.dtype)
        return carry

    lax.fori_loop(0, HY_FAST, body, 0, unroll=4)


def _short_conv(proj, col0, conv_w, conv_b, h1, cb=128):
    b, l, _ = proj.shape
    c = conv_w.shape[1]
    return pl.pallas_call(
        functools.partial(_sconv_kernel, h1=h1),
        scratch_shapes=[] if h1 is None else [pltpu.VMEM(((HY_FAST + HY_PAD) * (h1 + 1), cb), F32)],
        grid=(b, c // cb),
        in_specs=[pl.BlockSpec((None, l, cb), lambda bi, j: (bi, 0, col0 + j)),
                  pl.BlockSpec((conv_w.shape[0], cb), lambda bi, j: (0, j)),
                  pl.BlockSpec((1, cb), lambda bi, j: (0, j))],
        out_specs=pl.BlockSpec((None, l, cb), lambda bi, j: (bi, 0, j)),
        out_shape=jax.ShapeDtypeStruct((b, l, c), BF16),
        compiler_params=_cparams(("parallel", "parallel")),
        name="short_conv",
    )(proj, conv_w.astype(F32), conv_b.astype(F32).reshape(1, c))


def _gate_kernel(x_ref, conv_ref, y_ref, bias_ref, o_ref, *scr, h1):
    y = y_ref[...].astype(F32)
    z = x_ref[...].astype(F32) * (conv_ref[...].astype(F32) + y * bias_ref[...])
    if h1 is None:
        o_ref[...] = z.astype(o_ref.dtype)
        return
    pitch = h1 + HY_PAD
    for n2 in range(HY_FAST):
        scr[0][n2 * pitch:n2 * pitch + h1, :] = z[n2 * h1:(n2 + 1) * h1, :]

    def body(n1, carry):
        r = pl.multiple_of(n1 * HY_FAST, HY_FAST)
        o_ref[pl.ds(r, HY_FAST), :] = scr[0][pl.ds(n1, HY_FAST, stride=pitch), :].astype(o_ref.dtype)
        return carry

    lax.fori_loop(0, h1, body, 0, unroll=2)


def _hy_gate(z, xcol, conv, y, ycol, bias, h1=None, cb=128):
    b, l, w = conv.shape
    return pl.pallas_call(
        functools.partial(_gate_kernel, h1=h1),
        grid=(b, w // cb),
        in_specs=[pl.BlockSpec((None, l, cb), lambda bi, j: (bi, 0, xcol + j)),
                  pl.BlockSpec((None, l, cb), lambda bi, j: (bi, 0, j)),
                  pl.BlockSpec((None, l, cb), lambda bi, j: (bi, 0, ycol + j)),
                  pl.BlockSpec((1, cb), lambda bi, j: (0, j))],
        out_specs=pl.BlockSpec((None, l, cb), lambda bi, j: (bi, 0, j)),
        out_shape=jax.ShapeDtypeStruct((b, l, w), BF16),
        scratch_shapes=[] if h1 is None else [pltpu.VMEM((HY_FAST * (h1 + HY_PAD), cb), F32)],
        compiler_params=_cparams(("parallel", "parallel")),
        name="hy_gate",
    )(z, conv, y, bias.astype(F32).reshape(1, w))


def _hyena_branch(proj, col0, p, cb=128):
    b, l, _ = proj.shape
    w = p['hy_bias'].shape[-1]
    n = 2 * l
    nblk = w // cb
    if n // HY_FAST >= 16:
        consts = dict(_fft_consts(l), nblk=nblk)
        n1, h1 = consts['n1'], consts['h1']
        spec_r, spec_i = _fft_spectrum(_hyena_filter(l, p, slow=n1).astype(BF16), consts, cb)
        conv = lambda y, ycol, order: _fft_conv(y, ycol * nblk, spec_r, spec_i, order * nblk, consts, cb)
    else:
        h1 = None
        k = np.arange(n)
        ang = -2.0 * np.pi * np.outer(k, k) / n
        fmat = jnp.asarray(_block2(np.cos(ang[:, :l]), np.sin(ang[:, :l])), BF16)
        gmat = jnp.asarray(_block2(np.cos(ang[:l, :]) / n, -np.sin(ang[:l, :]) / n), BF16)
        f_real = jnp.asarray(np.concatenate([np.cos(ang), np.sin(ang)], axis=0), F32)
        spec = _bmm(f_real[None], _hyena_filter(l, p).astype(BF16)[None], name="dense_spectrum")[0]
        spec_r, spec_i = spec[:n], spec[n:]
        conv = lambda y, ycol, order: _dense_conv(y, ycol * nblk, spec_r, spec_i, order * nblk, fmat, gmat,
                                                  nblk, cb)
    z = _short_conv(proj, col0, p['hy_conv_w'], p['hy_conv_b'], h1, cb)
    c0 = conv(z, HY_ORDER, 0)
    y1 = _hy_gate(z, 0, c0, z, HY_ORDER * nblk, p['hy_bias'][0], None, cb)
    c1 = conv(y1, 0, 1)
    return _hy_gate(z, nblk, c1, y1, 0, p['hy_bias'][1], h1, cb)


def _merge_kernel(x_ref, g_ref, sh_ref, sc_ref, g1_ref, br_ref, wg0_ref, wg1_ref, wg2_ref, bg_ref, wb_ref, wo_ref,
                  o_ref, h_scr, acc_scr):
    j = pl.program_id(2)

    @pl.when(j == 0)
    def _():
        h_scr[...] = _norm_mod(x_ref[...], g_ref[...], sh_ref[...], sc_ref[...]).astype(BF16)
        acc_scr[...] = jnp.zeros_like(acc_scr)

    h = h_scr[...]
    mixed = None
    for i, wg_ref in enumerate((wg0_ref, wg1_ref, wg2_ref)):
        gate = jax.nn.sigmoid(jnp.dot(h, wg_ref[...], preferred_element_type=F32) + bg_ref[i])
        term = gate * jnp.dot(br_ref[i], wb_ref[i], preferred_element_type=F32)
        mixed = term if mixed is None else mixed + term
    acc_scr[...] += jnp.dot(mixed.astype(BF16), wo_ref[...], preferred_element_type=F32)

    @pl.when(j == pl.num_programs(2) - 1)
    def _():
        o_ref[...] = x_ref[...] + g1_ref[...] * acc_scr[...]


def _merge(x, g, sh, sc, g1, branches, w_gate_bf, b_gate, w_branch_bf, w_out_bf, tm, tc):
    b, l, d = x.shape
    nbr, w, _ = w_branch_bf.shape
    nb = sh.shape[0]
    nj = d // tc
    mod_map = (lambda bi, i, j: (bi, 0, 0)) if nb > 1 else (lambda bi, i, j: (0, 0, 0))
    br = jnp.stack(branches, axis=1)
    bg = b_gate.reshape(nbr, nj, 1, tc).transpose(1, 0, 2, 3)
    return pl.pallas_call(
        _merge_kernel,
        grid=(b, l // tm, nj),
        in_specs=[pl.BlockSpec((None, tm, d), lambda bi, i, j: (bi, i, 0)),
                  pl.BlockSpec((1, d), lambda bi, i, j: (0, 0)),
                  pl.BlockSpec((None, 1, d), mod_map),
                  pl.BlockSpec((None, 1, d), mod_map),
                  pl.BlockSpec((None, 1, d), mod_map),
                  pl.BlockSpec((None, nbr, tm, w), lambda bi, i, j: (bi, 0, i, 0)),
                  pl.BlockSpec((d, tc), lambda bi, i, j: (0, j)),
                  pl.BlockSpec((d, tc), lambda bi, i, j: (0, nj + j)),
                  pl.BlockSpec((d, tc), lambda bi, i, j: (0, 2 * nj + j)),
                  pl.BlockSpec((None, nbr, 1, tc), lambda bi, i, j: (j, 0, 0, 0)),
                  pl.BlockSpec((nbr, w, tc), lambda bi, i, j: (0, 0, j)),
                  pl.BlockSpec((tc, d), lambda bi, i, j: (j, 0))],
        out_specs=pl.BlockSpec((None, tm, d), lambda bi, i, j: (bi, i, 0)),
        out_shape=jax.ShapeDtypeStruct((b, l, d), F32),
        scratch_shapes=[pltpu.VMEM((tm, d), BF16), pltpu.VMEM((tm, d), F32)],
        compiler_params=_cparams(("parallel", "parallel", "arbitrary"), vmem_mb=56),
        name="merge",
    )(x, g.reshape(1, d), sh, sc, g1, br, w_gate_bf, w_gate_bf, w_gate_bf, bg, w_branch_bf, w_out_bf)


def _router_kernel(x_ref, g_ref, sh_ref, sc_ref, rt_ref, h_ref, aff_ref):
    h = _norm_mod(x_ref[...], g_ref[...], sh_ref[...], sc_ref[...])
    h_ref[...] = h.astype(h_ref.dtype)
    nt = (((1,), (1,)), ((), ()))
    rt = rt_ref[...]
    r_hi = rt.astype(BF16)
    r_lo = (rt - r_hi.astype(F32)).astype(BF16)
    h_hi = h.astype(BF16)
    h_lo = (h - h_hi.astype(F32)).astype(BF16)
    logits = (lax.dot_general(r_hi, h_hi, nt, preferred_element_type=F32)
              + lax.dot_general(r_hi, h_lo, nt, preferred_element_type=F32)
              + lax.dot_general(r_lo, h_hi, nt, preferred_element_type=F32))
    e = jnp.exp(logits - jnp.max(logits, axis=0, keepdims=True))
    aff_ref[...] = e / jnp.sum(e, axis=0, keepdims=True)


def _router(x, g, sh, sc, router_t, tm):
    b, l, d = x.shape
    e = router_t.shape[0]
    nb = sh.shape[0]
    mod_map = (lambda bi, i: (bi, 0, 0)) if nb > 1 else (lambda bi, i: (0, 0, 0))
    return pl.pallas_call(
        _router_kernel,
        grid=(b, l // tm),
        in_specs=[pl.BlockSpec((None, tm, d), lambda bi, i: (bi, i, 0)),
                  pl.BlockSpec((1, d), lambda bi, i: (0, 0)),
                  pl.BlockSpec((None, 1, d), mod_map),
                  pl.BlockSpec((None, 1, d), mod_map),
                  pl.BlockSpec((e, d), lambda bi, i: (0, 0))],
        out_specs=[pl.BlockSpec((None, tm, d), lambda bi, i: (bi, i, 0)),
                   pl.BlockSpec((None, e, tm), lambda bi, i: (bi, 0, i))],
        out_shape=[jax.ShapeDtypeStruct((b, l, d), BF16), jax.ShapeDtypeStruct((b, e, l), F32)],
        compiler_params=_cparams(("parallel", "parallel")),
        name="router",
    )(x, g.reshape(1, d), sh, sc, router_t)


def _ffn_kernel(xs_ref, gate_ref, w1_ref, w3_ref, w2_ref, o_ref, acc_scr):
    f = pl.program_id(2)

    @pl.when(f == 0)
    def _():
        acc_scr[...] = jnp.zeros_like(acc_scr)

    xs = xs_ref[...]
    a = jnp.dot(xs, w1_ref[...].astype(BF16), preferred_element_type=F32)
    gl = jnp.dot(xs, w3_ref[...].astype(BF16), preferred_element_type=F32)
    mid = (a * jax.nn.sigmoid(a) * gl).astype(BF16)
    acc_scr[...] += jnp.dot(mid, w2_ref[...].astype(BF16), preferred_element_type=F32)

    @pl.when(f == pl.num_programs(2) - 1)
    def _():
        o_ref[...] = (acc_scr[...] * gate_ref[...]).astype(o_ref.dtype)


def _expert_ffn(xs, gate, w1, w3, w2, layer, tr, tf):
    e, r, d = xs.shape
    ff = w1.shape[3]
    return pl.pallas_call(
        _ffn_kernel,
        grid=(e, r // tr, ff // tf),
        in_specs=[pl.BlockSpec((None, tr, d), lambda ei, i, f: (ei, i, 0)),
                  pl.BlockSpec((None, tr, 1), lambda ei, i, f: (ei, i, 0)),
                  pl.BlockSpec((None, None, d, tf), lambda ei, i, f: (layer, ei, 0, f)),
                  pl.BlockSpec((None, None, d, tf), lambda ei, i, f: (layer, ei, 0, f)),
                  pl.BlockSpec((None, None, tf, d), lambda ei, i, f: (layer, ei, f, 0))],
        out_specs=pl.BlockSpec((None, tr, d), lambda ei, i, f: (ei, i, 0)),
        out_shape=jax.ShapeDtypeStruct((e, r, d), BF16),
        scratch_shapes=[pltpu.VMEM((tr, d), F32)],
        compiler_params=_cparams(("parallel", "parallel", "arbitrary"), vmem_mb=56),
        name="expert_ffn",
    )(xs, gate, w1, w3, w2)


def _route_kernel(aff_ref, pos_ref, idx_ref, csum_scr, *, cap, sblk):
    aff = aff_ref[...]
    e, l = aff.shape
    bits = pltpu.bitcast(aff, jnp.int32)
    count = lambda m: jnp.sum(jnp.where(m, 1.0, 0.0), axis=1, keepdims=True)

    def bisect(i, thr):
        cand = thr | lax.shift_left(jnp.int32(1), 30 - i)
        return jnp.where(count(bits >= cand) >= cap, cand, thr)

    thr = lax.fori_loop(0, 31, bisect, jnp.zeros((e, 1), jnp.int32))
    lane = lax.broadcasted_iota(jnp.int32, (e, l), 1)

    def cumsum(x):
        k = 1
        while k < l:
            x = x + jnp.where(lane >= k, pltpu.roll(x, k, 1), 0.0)
            k *= 2
        return x

    above = bits > thr
    tie = jnp.where(bits == thr, 1.0, 0.0)
    need = cap - count(above)
    keep = jnp.logical_or(above, jnp.logical_and(tie > 0.0, cumsum(tie) - tie < need))
    csum = cumsum(jnp.where(keep, 1.0, 0.0))
    slot1 = jnp.where(keep, csum, 0.0)
    hi = jnp.floor(slot1 * (1.0 / 32.0))
    lo = slot1 - 32.0 * hi
    eye = jnp.where(lax.broadcasted_iota(jnp.int32, (e, e), 0) == lax.broadcasted_iota(jnp.int32, (e, e), 1),
                    1.0, 0.0).astype(BF16)
    tn = (((0,), (0,)), ((), ()))
    pos_ref[...] = (32.0 * lax.dot_general(hi.astype(BF16), eye, tn, preferred_element_type=F32)
                    + lax.dot_general(lo.astype(BF16), eye, tn, preferred_element_type=F32) - 1.0).astype(jnp.int32)
    csum_scr[...] = csum
    expert_lane = lax.broadcasted_iota(jnp.int32, (sblk, e), 1)
    slots = [(lax.broadcasted_iota(jnp.int32, (sblk, 1), 0) + s0).astype(F32) for s0 in range(0, cap, sblk)]

    def per_expert(ei, accs):
        row = csum_scr[pl.ds(ei, 1), :]
        return tuple(jnp.where(expert_lane == ei, count(row <= s), a) for s, a in zip(slots, accs))

    accs = lax.fori_loop(0, e, per_expert, tuple(jnp.zeros((sblk, e), F32) for _ in slots))
    for i, a in enumerate(accs):
        idx_ref[i * sblk:(i + 1) * sblk, :] = a.astype(jnp.int32)


def _route(aff_t, cap):
    b, e, l = aff_t.shape
    return pl.pallas_call(
        functools.partial(_route_kernel, cap=cap, sblk=min(256, cap)),
        grid=(b,),
        in_specs=[pl.BlockSpec((None, e, l), lambda bi: (bi, 0, 0))],
        out_specs=[pl.BlockSpec((None, l, e), lambda bi: (bi, 0, 0)),
                   pl.BlockSpec((None, cap, e), lambda bi: (bi, 0, 0))],
        out_shape=[jax.ShapeDtypeStruct((b, l, e), jnp.int32), jax.ShapeDtypeStruct((b, cap, e), jnp.int32)],
        scratch_shapes=[pltpu.VMEM((e, l), F32)],
        compiler_params=_cparams(("parallel",)),
        name="moe_route",
    )(aff_t)


def _combine_kernel(st_ref, x_ref, g2_ref, pos_ref, *refs, n_exp, tiles, tn):
    y_refs, o_ref, oh_scr = refs[:n_exp], refs[n_exp], refs[n_exp + 1]
    bi, j = pl.program_id(0), pl.program_id(1)
    tt, win = oh_scr.shape[1], y_refs[0].shape[1]
    lane = lax.broadcasted_iota(jnp.int32, (tt, win), 1)
    main = min(win, tt)
    spill = []
    for e in range(n_exp):
        rel = pos_ref[:, e:e + 1] - st_ref[(bi * n_exp + e) * tiles + j]
        oh_scr[e] = jnp.where(lane == rel, 1.0, 0.0).astype(BF16)
        spill.append(jnp.max(rel) >= main)
    for c in range(o_ref.shape[1] // tn):
        cols = slice(c * tn, (c + 1) * tn)
        acc = jnp.dot(oh_scr[0, :, :main], y_refs[0][0, :main, cols], preferred_element_type=F32)
        for e in range(1, n_exp):
            acc = acc + jnp.dot(oh_scr[e, :, :main], y_refs[e][0, :main, cols], preferred_element_type=F32)
        o_ref[:, cols] = x_ref[:, cols] + g2_ref[:, cols] * acc
    if win > main:
        for e in range(n_exp):
            @pl.when(spill[e])
            def _(e=e):
                tail = jnp.dot(oh_scr[e, :, main:], y_refs[e][0, main:, :], preferred_element_type=F32)
                o_ref[...] += g2_ref[...] * tail


def _moe_combine(x, g2, pos, y, cap, tt):
    b, l, d = x.shape
    e = y.shape[0]
    tiles = l // tt
    win = min(tt + COMBINE_ALIGN, cap)
    sel = (pos >= 0).reshape(b, tiles, tt, e).sum(2)
    first = jnp.cumsum(sel, axis=1) - sel
    start = jnp.minimum(first // COMBINE_ALIGN * COMBINE_ALIGN, cap - win).astype(jnp.int32)
    start = jnp.transpose(start, (0, 2, 1)).reshape(-1)
    nb = g2.shape[0]

    def y_spec(ei):
        def y_map(bi, j, st):
            return ei, pl.multiple_of(bi * cap + st[(bi * e + ei) * tiles + j], COMBINE_ALIGN), 0
        return pl.BlockSpec((pl.Element(1), pl.Element(win), pl.Element(d)), y_map)

    grid_spec = pltpu.PrefetchScalarGridSpec(
        num_scalar_prefetch=1,
        grid=(b, tiles),
        in_specs=[pl.BlockSpec((None, tt, d), lambda bi, j, st: (bi, j, 0)),
                  pl.BlockSpec((None, 1, d), lambda bi, j, st: (bi if nb > 1 else 0, 0, 0)),
                  pl.BlockSpec((None, tt, e), lambda bi, j, st: (bi, j, 0))] + [y_spec(ei) for ei in range(e)],
        out_specs=pl.BlockSpec((None, tt, d), lambda bi, j, st: (bi, j, 0)),
        scratch_shapes=[pltpu.VMEM((e, tt, win), BF16)])
    return pl.pallas_call(
        functools.partial(_combine_kernel, n_exp=e, tiles=tiles, tn=min(256, d)),
        grid_spec=grid_spec,
        out_shape=jax.ShapeDtypeStruct((b, l, d), F32),
        compiler_params=_cparams(("parallel", "parallel"), vmem_mb=56),
        name="moe_combine",
    )(start, x, g2, pos, *([y] * e))


def _ec_moe_update(x, g, sh, sc, g2, p, tm):
    b, l, d = x.shape
    e = p['router'].shape[1]
    cap = max(1, EC_CAPACITY * l // e)
    h, aff_t = _router(x, g, sh, sc, jnp.transpose(p['router']).astype(F32), tm)
    pos_t, idx_ce = _route(aff_t, cap)
    pos = pos_t
    idx_t = jnp.transpose(idx_ce, (2, 0, 1))
    flat = (idx_t + (jnp.arange(b, dtype=jnp.int32) * l)[None, :, None]).reshape(-1)
    xs = jnp.take(h.reshape(b * l, d), flat, axis=0, mode="clip").reshape(e, b * cap, d)
    gt = jnp.take_along_axis(jnp.transpose(aff_t, (1, 0, 2)), idx_t, axis=2).reshape(e, b * cap, 1)
    tr = min(1024, b * cap)
    ff = p['exp_w1'].shape[3]
    y = _expert_ffn(xs, gt, p['exp_w1'], p['exp_w3'], p['exp_w2'], p['layer'], tr=tr, tf=min(512, ff))
    return _moe_combine(x, g2, pos, y, cap, min(256, l))


def _layer(x, xc, c, c_ctx, p, last):
    b, l, d = x.shape
    lc = xc.shape[1]
    w = p['ssm_d'].shape[0]
    hd = w // NA_HEADS

    cc = jnp.zeros((8, d), F32).at[:b].set(c).at[b].set(c_ctx)
    mod = _ada(cc, p['w_ada'], p['b_ada'], p['layer'])
    m_l = [t[:, None, :] for t in jnp.split(mod[:b], 6, axis=-1)]
    m_c = [t[:, None, :] for t in jnp.split(mod[b:b + 1], 6, axis=-1)]

    w_in = p['w_in'].astype(BF16)
    scale = hd ** -0.5
    head_gain = jnp.stack([jnp.tile(p['na_q_gain'].astype(F32) * scale, NA_HEADS),
                           jnp.tile(p['na_k_gain'].astype(F32), NA_HEADS)])[:, None, :]
    ones_blk = jnp.asarray(np.kron(np.eye(NA_HEADS), np.ones((hd, hd))), BF16)
    tm_l = min(512, l)
    tm_c = min(512, lc)
    proj = _in_proj(x, p['norm1'], m_l[0], m_l[1], w_in, head_gain, ones_blk, min(1024, l))
    projc = _in_proj(xc, p['norm1'], m_c[0], m_c[1], w_in, head_gain, ones_blk, tm_c)

    ssm_l, ssm_c = _s5_branch(projc[..., :w], proj[..., :w], p, _s5_tables(p), need_ctx=not last)

    na_l = _na_latent(proj[..., w:2 * w], proj[..., 2 * w:3 * w], proj[..., 3 * w:4 * w],
                      projc[..., 2 * w:3 * w], projc[..., 3 * w:4 * w], _na_bias_table(p['na_rpb']))
    hy_l = _hyena_branch(proj, 4 * w // 128, p)

    wg = p['w_gate'].astype(BF16)
    wb = p['w_branch'].astype(BF16)
    wo = p['w_out'].astype(BF16)
    x = _merge(x, p['norm1'], m_l[0], m_l[1], m_l[2], (ssm_l, na_l, hy_l), wg, p['b_gate'], wb, wo,
               tm=tm_l, tc=min(512, d))
    x = _ec_moe_update(x, p['norm2'], m_l[3], m_l[4], m_l[5], p, tm_l)
    if last:
        return x, None
    na_c = _na_context(projc[..., w:2 * w], projc[..., 2 * w:3 * w], projc[..., 3 * w:4 * w])
    hy_c = _hyena_branch(projc, 4 * w // 128, p)
    xc = _merge(xc, p['norm1'], m_c[0], m_c[1], m_c[2], (ssm_c, na_c, hy_c), wg, p['b_gate'], wb, wo,
                tm=tm_c, tc=min(512, d))
    xc = _ec_moe_update(xc, p['norm2'], m_c[3], m_c[4], m_c[5], p, tm_c)
    return x, xc


_PARAM_NAMES = ('w_ada', 'b_ada', 'norm1', 'norm2', 'w_in',
                'ssm_lam_re', 'ssm_lam_im', 'ssm_log_step', 'ssm_b_re', 'ssm_b_im', 'ssm_c_re', 'ssm_c_im',
                'ssm_d', 'ssm_w_glu', 'na_q_gain', 'na_k_gain', 'na_rpb',
                'hy_conv_w', 'hy_conv_b', 'hy_w1', 'hy_b1', 'hy_w2', 'hy_b2', 'hy_w3', 'hy_b3', 'hy_w4',
                'hy_freq', 'hy_bias', 'w_gate', 'b_gate', 'w_branch', 'w_out',
                'router', 'exp_w1', 'exp_w3', 'exp_w2')
_STACKED = ('w_ada', 'b_ada', 'exp_w1', 'exp_w3', 'exp_w2')


def kernel(x, c, ctx, c_ctx, w_ada, b_ada, norm1, norm2, w_in, ssm_lam_re, ssm_lam_im, ssm_log_step, ssm_b_re, ssm_b_im, ssm_c_re, ssm_c_im, ssm_d, ssm_w_glu, na_q_gain, na_k_gain, na_rpb, hy_conv_w, hy_conv_b, hy_w1, hy_b1, hy_w2, hy_b2, hy_w3, hy_b3, hy_w4, hy_freq, hy_bias, w_gate, b_gate, w_branch, w_out, router, exp_w1, exp_w3, exp_w2):
    stacked = (w_ada, b_ada, norm1, norm2, w_in, ssm_lam_re, ssm_lam_im, ssm_log_step, ssm_b_re, ssm_b_im,
               ssm_c_re, ssm_c_im, ssm_d, ssm_w_glu, na_q_gain, na_k_gain, na_rpb, hy_conv_w, hy_conv_b,
               hy_w1, hy_b1, hy_w2, hy_b2, hy_w3, hy_b3, hy_w4, hy_freq, hy_bias, w_gate, b_gate, w_branch,
               w_out, router, exp_w1, exp_w3, exp_w2)
    depth = w_ada.shape[0]
    xc = ctx
    for layer in range(depth):
        p = {name: (t if name in _STACKED else t[layer]) for name, t in zip(_PARAM_NAMES, stacked)}
        p['layer'] = layer
        x, xc = _layer(x, xc, c, c_ctx, p, layer == depth - 1)
    return x
```
